```python
import jax, jax.numpy as jnp
from jax import lax
import numpy as np

D_MODEL = 2048
BATCH = 8
SEQ = 4096
DEPTH = 4

GRID_W = 64
CTX_LEN = 256

N_HEADS = 4
D_KEY = D_MODEL // 2
D_VAL = D_MODEL
HEAD_K = D_KEY // N_HEADS
HEAD_V = D_VAL // N_HEADS
GATE_RANK = 16
GATE_NORM = 16.0
CHUNK = 64

POOL_WINDOWS = (2, 4, 8, 16)
D_POOL = D_MODEL // 2
POOL_GROUP = D_POOL // len(POOL_WINDOWS)

D_FF = ((8 * D_MODEL + 3 * 256 - 1) // (3 * 256)) * 256

N_BRANCH = 2
N_MOD = 6
DEEPNORM_ALPHA = (2.0 * DEPTH) ** 0.25
DEEPNORM_BETA = (8.0 * DEPTH) ** -0.25
LN_EPS = 1e-5
RMS_EPS = 1e-6

PROJ_WIDTHS = (D_KEY, D_KEY, D_VAL, D_VAL, 2 * GATE_RANK, D_POOL, N_BRANCH * D_MODEL)
D_PROJ = sum(PROJ_WIDTHS)
SPLIT_POINTS = tuple(int(s) for s in np.cumsum(PROJ_WIDTHS)[:-1])

kernel_name = "hybrid_gla_pool_diffusion_trunk"


def _layer_norm(x, gain, bias):
    xf = x.astype(jnp.float32)
    mu = jnp.mean(xf, axis=-1, keepdims=True)
    var = jnp.mean(jnp.square(xf - mu), axis=-1, keepdims=True)
    y = (xf - mu) * lax.rsqrt(var + LN_EPS)
    return (y * gain + bias).astype(x.dtype)


def _modulation(cond, w_ada, b_ada):
    return jnp.split(jax.nn.silu(cond) @ w_ada + b_ada, N_MOD, axis=-1)


def _heads(t, head_dim):
    b, l, _ = t.shape
    return t.reshape(b, l, -1, head_dim).transpose(0, 2, 1, 3)


def _split_proj(proj, w_decay_up, b_decay_up):
    q, k, v, g, a_lr, p, bg = jnp.split(proj, SPLIT_POINTS, axis=-1)
    q = _heads(q * HEAD_K ** -0.5, HEAD_K)
    k = _heads(k, HEAD_K)
    v = _heads(v, HEAD_V)
    log_decay = []
    for d in range(2):
        z = a_lr[..., d * GATE_RANK:(d + 1) * GATE_RANK] @ w_decay_up[d] + b_decay_up[d]
        log_decay.append(_heads(jax.nn.log_sigmoid(z.astype(jnp.float32)) / GATE_NORM, HEAD_K))
    return q, k, v, log_decay[0], log_decay[1], g, p, bg


def _gla_chunked(q, k, v, log_a, s0):
    b_, h_, l_, _ = q.shape
    n = l_ // CHUNK

    def chunks(t):
        return t.reshape(b_, h_, n, CHUNK, t.shape[-1]).astype(jnp.float32)

    q, k, v, log_a = chunks(q), chunks(k), chunks(v), chunks(log_a)
    cum = jnp.cumsum(log_a, axis=3)
    ref = cum[:, :, :, CHUNK // 2 - 1:CHUNK // 2]
    q_in = q * jnp.exp(cum - ref)
    k_in = k * jnp.exp(ref - cum)
    scores = jnp.einsum('bhnid,bhnjd->bhnij', q_in, k_in)
    mask = jnp.tril(jnp.ones((CHUNK, CHUNK), dtype=bool))
    scores = jnp.where(mask, scores, 0.0)
    o_intra = jnp.einsum('bhnij,bhnjv->bhniv', scores, v)
    last = cum[:, :, :, -1:]
    q_inter = q * jnp.exp(cum)
    k_state = k * jnp.exp(last - cum)
    decay_chunk = jnp.exp(last[:, :, :, 0, :])
    xs = (jnp.moveaxis(q_inter, 2, 0), jnp.moveaxis(k_state, 2, 0),
          jnp.moveaxis(v, 2, 0), jnp.moveaxis(decay_chunk, 2, 0))

    def step(state, inp):
        qc, kc, vc, dc = inp
        o = jnp.einsum('bhid,bhdv->bhiv', qc, state)
        state = dc[..., None] * state + jnp.einsum('bhjd,bhjv->bhdv', kc, vc)
        return state, o

    s_final, o_inter = lax.scan(step, s0.astype(jnp.float32), xs)
    o = o_intra + jnp.moveaxis(o_inter, 0, 2)
    return o.reshape(b_, h_, l_, HEAD_V), s_final


def _gla_two_way(q, k, v, la_fwd, la_bwd, s0_fwd, s0_bwd):
    flip = lambda t: jnp.flip(t, axis=2)
    o_f, s_f = _gla_chunked(q, k, v, la_fwd, s0_fwd)
    o_b, s_b = _gla_chunked(flip(q), flip(k), flip(v), flip(la_bwd), s0_bwd)
    return o_f + flip(o_b), s_f, s_b


def _box_mean(t, axis, window):
    n = t.shape[axis]
    lo = window // 2
    hi = window - lo - 1
    cs = jnp.cumsum(t.astype(jnp.float32), axis=axis)
    zero = jnp.zeros_like(lax.slice_in_dim(cs, 0, 1, axis=axis))
    cs = jnp.concatenate([zero, cs], axis=axis)
    idx = jnp.arange(n)
    upper = jnp.minimum(idx + hi + 1, n)
    lower = jnp.maximum(idx - lo, 0)
    total = jnp.take(cs, upper, axis=axis) - jnp.take(cs, lower, axis=axis)
    shape = [1] * t.ndim
    shape[axis] = n
    count = (upper - lower).astype(jnp.float32).reshape(shape)
    return (total / count).astype(t.dtype)


def _pool_branch(p, rows, w_pool_group, pool_scale, w_pool_out):
    outs = []
    for i, w in enumerate(POOL_WINDOWS):
        pg = p[..., i * POOL_GROUP:(i + 1) * POOL_GROUP]
        if rows is None:
            mean = _box_mean(pg, 1, w)
        else:
            b, l, ch = pg.shape
            img = pg.reshape(b, rows, GRID_W, ch)
            mean = _box_mean(_box_mean(img, 2, w), 1, w).reshape(b, l, ch)
        outs.append((mean - pg) @ w_pool_group[i])
    return (jnp.concatenate(outs, axis=-1) * pool_scale) @ w_pool_out


def _gla_output(o, g, gain, w_gla_out):
    of = o.transpose(0, 2, 1, 3)
    of = of * lax.rsqrt(jnp.mean(of * of, axis=-1, keepdims=True) + RMS_EPS)
    b, l = of.shape[:2]
    of = of.reshape(b, l, D_VAL).astype(g.dtype) * gain
    return (of * jax.nn.silu(g)) @ w_gla_out


def _merge(o, g, p, bg, rows, gla_norm_gain, w_gla_out, w_pool_group, pool_scale, w_pool_out, w_out):
    y_gla = _gla_output(o, g, gla_norm_gain, w_gla_out)
    y_pool = _pool_branch(p, rows, w_pool_group, pool_scale, w_pool_out)
    gate_gla, gate_pool = jnp.split(jax.nn.sigmoid(bg), N_BRANCH, axis=-1)
    return (gate_gla * y_gla + gate_pool * y_pool) @ w_out


def _token_mixer(h_lat, h_ctx, rows, ctx_out, w_in, w_decay_up, b_decay_up, gla_norm_gain,
                 w_pool_group, pool_scale, w_gla_out, w_pool_out, w_out):
    q_c, k_c, v_c, laf_c, lab_c, g_c, p_c, bg_c = _split_proj(h_ctx @ w_in, w_decay_up, b_decay_up)
    q_l, k_l, v_l, laf_l, lab_l, g_l, p_l, bg_l = _split_proj(h_lat @ w_in, w_decay_up, b_decay_up)
    s0 = jnp.zeros((h_lat.shape[0], N_HEADS, HEAD_K, HEAD_V), jnp.float32)
    o_c, s_fwd, s_bwd = _gla_two_way(q_c, k_c, v_c, laf_c, lab_c, s0, s0)
    o_l, _, _ = _gla_two_way(q_l, k_l, v_l, laf_l, lab_l, s_fwd, s_bwd)
    y_lat = _merge(o_l, g_l, p_l, bg_l, rows, gla_norm_gain, w_gla_out, w_pool_group,
                   pool_scale, w_pool_out, w_out)
    y_ctx = None
    if ctx_out:
        y_ctx = _merge(o_c, g_c, p_c, bg_c, None, gla_norm_gain, w_gla_out, w_pool_group,
                       pool_scale, w_pool_out, w_out)
    return y_lat, y_ctx


def _swiglu(h, w_ffn_in, w_ffn_out):
    gate, up = jnp.split(h @ w_ffn_in, 2, axis=-1)
    return (jax.nn.silu(gate) * up) @ w_ffn_out


def _fwd_setup_inputs(seed: int = 0) -> dict:
    key = jax.random.key(seed)
    ks = jax.random.split(key, 24)
    f32 = jnp.float32
    nrm = lambda k, shape, s: jax.random.normal(k, shape, f32) * s
    L = DEPTH
    return {
        "x": nrm(ks[0], (BATCH, SEQ, D_MODEL), 1.0),
        "c": nrm(ks[1], (BATCH, D_MODEL), 1.0),
        "ctx": nrm(ks[2], (BATCH, CTX_LEN, D_MODEL), 1.0),
        "c_ctx": nrm(ks[3], (D_MODEL,), 1.0),
        "w_ada": nrm(ks[4], (L, D_MODEL, N_MOD * D_MODEL), 0.5 * D_MODEL ** -0.5),
        "b_ada": nrm(ks[5], (L, N_MOD * D_MODEL), 0.02),
        "w_in": nrm(ks[6], (L, D_MODEL, D_PROJ), D_MODEL ** -0.5),
        "w_decay_up": nrm(ks[7], (L, 2, GATE_RANK, D_KEY), GATE_RANK ** -0.5),
        "b_decay_up": nrm(ks[8], (L, 2, D_KEY), 0.1),
        "gla_norm_gain": 1.0 + nrm(ks[9], (L, D_VAL), 0.1),
        "w_pool_group": nrm(ks[10], (L, len(POOL_WINDOWS), POOL_GROUP, POOL_GROUP), POOL_GROUP ** -0.5),
        "pool_scale": 1.0 + nrm(ks[11], (L, D_POOL), 0.1),
        "w_gla_out": nrm(ks[12], (L, D_VAL, D_MODEL), D_VAL ** -0.5),
        "w_pool_out": nrm(ks[13], (L, D_POOL, D_MODEL), D_POOL ** -0.5),
        "w_out": nrm(ks[14], (L, D_MODEL, D_MODEL), DEEPNORM_BETA * D_MODEL ** -0.5),
        "ln_mix_gain": 1.0 + nrm(ks[15], (L, D_MODEL), 0.1),
        "ln_mix_bias": nrm(ks[16], (L, D_MODEL), 0.02),
        "w_ffn_in": nrm(ks[17], (L, D_MODEL, 2 * D_FF), D_MODEL ** -0.5),
        "w_ffn_out": nrm(ks[18], (L, D_FF, D_MODEL), DEEPNORM_BETA * D_FF ** -0.5),
        "ln_ffn_gain": 1.0 + nrm(ks[19], (L, D_MODEL), 0.1),
        "ln_ffn_bias": nrm(ks[20], (L, D_MODEL), 0.02),
    }


def _fwd_reference(x, c, ctx, c_ctx, w_ada, b_ada, w_in, w_decay_up, b_decay_up, gla_norm_gain,
              w_pool_group, pool_scale, w_gla_out, w_pool_out, w_out, ln_mix_gain, ln_mix_bias,
              w_ffn_in, w_ffn_out, ln_ffn_gain, ln_ffn_bias):
    rows = x.shape[1] // GRID_W
    for layer in range(DEPTH):
        ctx_out = layer < DEPTH - 1
        sh_m, sc_m, gt_m, sh_f, sc_f, gt_f = _modulation(c[:, None, :], w_ada[layer], b_ada[layer])
        csh_m, csc_m, cgt_m, csh_f, csc_f, cgt_f = _modulation(c_ctx, w_ada[layer], b_ada[layer])
        h_lat = x * (1.0 + sc_m) + sh_m
        h_ctx = ctx * (1.0 + csc_m) + csh_m
        mix_lat, mix_ctx = _token_mixer(h_lat, h_ctx, rows, ctx_out, w_in[layer], w_decay_up[layer],
                                        b_decay_up[layer], gla_norm_gain[layer], w_pool_group[layer],
                                        pool_scale[layer], w_gla_out[layer], w_pool_out[layer], w_out[layer])
        x = _layer_norm(DEEPNORM_ALPHA * x + gt_m * mix_lat, ln_mix_gain[layer], ln_mix_bias[layer])
        ffn_lat = _swiglu(x * (1.0 + sc_f) + sh_f, w_ffn_in[layer], w_ffn_out[layer])
        x = _layer_norm(DEEPNORM_ALPHA * x + gt_f * ffn_lat, ln_ffn_gain[layer], ln_ffn_bias[layer])
        if ctx_out:
            ctx = _layer_norm(DEEPNORM_ALPHA * ctx + cgt_m * mix_ctx, ln_mix_gain[layer], ln_mix_bias[layer])
            ffn_ctx = _swiglu(ctx * (1.0 + csc_f) + csh_f, w_ffn_in[layer], w_ffn_out[layer])
            ctx = _layer_norm(DEEPNORM_ALPHA * ctx + cgt_f * ffn_ctx, ln_ffn_gain[layer], ln_ffn_bias[layer])
    return x


import jax as _jax
import jax.numpy as _jnp

TWIN_FORMAT = 'train_step'
FWD_PARAMS = ['x', 'c', 'ctx', 'c_ctx', 'w_ada', 'b_ada', 'w_in', 'w_decay_up', 'b_decay_up', 'gla_norm_gain', 'w_pool_group', 'pool_scale', 'w_gla_out', 'w_pool_out', 'w_out', 'ln_mix_gain', 'ln_mix_bias', 'w_ffn_in', 'w_ffn_out', 'ln_ffn_gain', 'ln_ffn_bias']
TWIN_WEIGHTS = ['c_ctx', 'w_ada', 'b_ada', 'w_in', 'w_decay_up', 'b_decay_up', 'gla_norm_gain', 'w_pool_group', 'pool_scale', 'w_gla_out', 'w_pool_out', 'w_out', 'ln_mix_gain', 'ln_mix_bias', 'w_ffn_in', 'w_ffn_out', 'ln_ffn_gain', 'ln_ffn_bias']
TWIN_DIFF_INPUT = 'x'
TWIN_INPUTS = ['x', 'c', 'ctx', 'c_ctx', 'w_ada', 'b_ada', 'w_in', 'w_decay_up', 'b_decay_up', 'gla_norm_gain', 'w_pool_group', 'pool_scale', 'w_gla_out', 'w_pool_out', 'w_out', 'ln_mix_gain', 'ln_mix_bias', 'w_ffn_in', 'w_ffn_out', 'ln_ffn_gain', 'ln_ffn_bias', 'loss_target', 'm_c_ctx', 'm_w_ada', 'm_b_ada', 'm_w_in', 'm_w_decay_up', 'm_b_decay_up', 'm_gla_norm_gain', 'm_w_pool_group', 'm_pool_scale', 'm_w_gla_out', 'm_w_pool_out', 'm_w_out', 'm_ln_mix_gain', 'm_ln_mix_bias', 'm_w_ffn_in', 'm_w_ffn_out', 'm_ln_ffn_gain', 'm_ln_ffn_bias', 'v_c_ctx', 'v_w_ada', 'v_b_ada', 'v_w_in', 'v_w_decay_up', 'v_b_decay_up', 'v_gla_norm_gain', 'v_w_pool_group', 'v_pool_scale', 'v_w_gla_out', 'v_w_pool_out', 'v_w_out', 'v_ln_mix_gain', 'v_ln_mix_bias', 'v_w_ffn_in', 'v_w_ffn_out', 'v_ln_ffn_gain', 'v_ln_ffn_bias']
TWIN_OUTPUTS = ['loss', 'grad_x', 'grad_c_ctx', 'grad_w_ada', 'grad_b_ada', 'grad_w_in', 'grad_w_decay_up', 'grad_b_decay_up', 'grad_gla_norm_gain', 'grad_w_pool_group', 'grad_pool_scale', 'grad_w_gla_out', 'grad_w_pool_out', 'grad_w_out', 'grad_ln_mix_gain', 'grad_ln_mix_bias', 'grad_w_ffn_in', 'grad_w_ffn_out', 'grad_ln_ffn_gain', 'grad_ln_ffn_bias', 'delta_c_ctx', 'delta_w_ada', 'delta_b_ada', 'delta_w_in', 'delta_w_decay_up', 'delta_b_decay_up', 'delta_gla_norm_gain', 'delta_w_pool_group', 'delta_pool_scale', 'delta_w_gla_out', 'delta_w_pool_out', 'delta_w_out', 'delta_ln_mix_gain', 'delta_ln_mix_bias', 'delta_w_ffn_in', 'delta_w_ffn_out', 'delta_ln_ffn_gain', 'delta_ln_ffn_bias', 'new_m_c_ctx', 'new_m_w_ada', 'new_m_b_ada', 'new_m_w_in', 'new_m_w_decay_up', 'new_m_b_decay_up', 'new_m_gla_norm_gain', 'new_m_w_pool_group', 'new_m_pool_scale', 'new_m_w_gla_out', 'new_m_w_pool_out', 'new_m_w_out', 'new_m_ln_mix_gain', 'new_m_ln_mix_bias', 'new_m_w_ffn_in', 'new_m_w_ffn_out', 'new_m_ln_ffn_gain', 'new_m_ln_ffn_bias', 'new_v_c_ctx', 'new_v_w_ada', 'new_v_b_ada', 'new_v_w_in', 'new_v_w_decay_up', 'new_v_b_decay_up', 'new_v_gla_norm_gain', 'new_v_w_pool_group', 'new_v_pool_scale', 'new_v_w_gla_out', 'new_v_w_pool_out', 'new_v_w_out', 'new_v_ln_mix_gain', 'new_v_ln_mix_bias', 'new_v_w_ffn_in', 'new_v_w_ffn_out', 'new_v_ln_ffn_gain', 'new_v_ln_ffn_bias']
TWIN_LEAF_KINDS = {'loss': 'loss', 'grad_x': 'grad_x', 'grad_c_ctx': 'grad_w', 'grad_w_ada': 'grad_w', 'grad_b_ada': 'grad_w', 'grad_w_in': 'grad_w', 'grad_w_decay_up': 'grad_w', 'grad_b_decay_up': 'grad_w', 'grad_gla_norm_gain': 'grad_w', 'grad_w_pool_group': 'grad_w', 'grad_pool_scale': 'grad_w', 'grad_w_gla_out': 'grad_w', 'grad_w_pool_out': 'grad_w', 'grad_w_out': 'grad_w', 'grad_ln_mix_gain': 'grad_w', 'grad_ln_mix_bias': 'grad_w', 'grad_w_ffn_in': 'grad_w', 'grad_w_ffn_out': 'grad_w', 'grad_ln_ffn_gain': 'grad_w', 'grad_ln_ffn_bias': 'grad_w', 'delta_c_ctx': 'delta_w', 'delta_w_ada': 'delta_w', 'delta_b_ada': 'delta_w', 'delta_w_in': 'delta_w', 'delta_w_decay_up': 'delta_w', 'delta_b_decay_up': 'delta_w', 'delta_gla_norm_gain': 'delta_w', 'delta_w_pool_group': 'delta_w', 'delta_pool_scale': 'delta_w', 'delta_w_gla_out': 'delta_w', 'delta_w_pool_out': 'delta_w', 'delta_w_out': 'delta_w', 'delta_ln_mix_gain': 'delta_w', 'delta_ln_mix_bias': 'delta_w', 'delta_w_ffn_in': 'delta_w', 'delta_w_ffn_out': 'delta_w', 'delta_ln_ffn_gain': 'delta_w', 'delta_ln_ffn_bias': 'delta_w', 'new_m_c_ctx': 'new_m', 'new_m_w_ada': 'new_m', 'new_m_b_ada': 'new_m', 'new_m_w_in': 'new_m', 'new_m_w_decay_up': 'new_m', 'new_m_b_decay_up': 'new_m', 'new_m_gla_norm_gain': 'new_m', 'new_m_w_pool_group': 'new_m', 'new_m_pool_scale': 'new_m', 'new_m_w_gla_out': 'new_m', 'new_m_w_pool_out': 'new_m', 'new_m_w_out': 'new_m', 'new_m_ln_mix_gain': 'new_m', 'new_m_ln_mix_bias': 'new_m', 'new_m_w_ffn_in': 'new_m', 'new_m_w_ffn_out': 'new_m', 'new_m_ln_ffn_gain': 'new_m', 'new_m_ln_ffn_bias': 'new_m', 'new_v_c_ctx': 'new_v', 'new_v_w_ada': 'new_v', 'new_v_b_ada': 'new_v', 'new_v_w_in': 'new_v', 'new_v_w_decay_up': 'new_v', 'new_v_b_decay_up': 'new_v', 'new_v_gla_norm_gain': 'new_v', 'new_v_w_pool_group': 'new_v', 'new_v_pool_scale': 'new_v', 'new_v_w_gla_out': 'new_v', 'new_v_w_pool_out': 'new_v', 'new_v_w_out': 'new_v', 'new_v_ln_mix_gain': 'new_v', 'new_v_ln_mix_bias': 'new_v', 'new_v_w_ffn_in': 'new_v', 'new_v_w_ffn_out': 'new_v', 'new_v_ln_ffn_gain': 'new_v', 'new_v_ln_ffn_bias': 'new_v'}


def _forward(args):
    return _fwd_reference(*[args[k] for k in FWD_PARAMS])


def _output_shape():
    def fwd():
        inp = _fwd_setup_inputs(0)
        return _fwd_reference(*[inp[k] for k in FWD_PARAMS])
    out = _jax.eval_shape(fwd)
    return out.shape, out.dtype

N_MICROBATCH = 1
ADAM_LR = 0.001
ADAM_B1 = 0.9
ADAM_B2 = 0.999
ADAM_EPS = 1e-08
ADAM_WD = 0.01
ADAM_STEP = 10
PER_EXAMPLE_BATCH_AXIS = {'x': 0, 'c': 0, 'ctx': 0, 'loss_target': 0}
SHARED_INPUTS = []
_WEIGHT_DTYPES = {'c_ctx': _jnp.float32, 'w_ada': _jnp.float32, 'b_ada': _jnp.float32, 'w_in': _jnp.float32, 'w_decay_up': _jnp.float32, 'b_decay_up': _jnp.float32, 'gla_norm_gain': _jnp.float32, 'w_pool_group': _jnp.float32, 'pool_scale': _jnp.float32, 'w_gla_out': _jnp.float32, 'w_pool_out': _jnp.float32, 'w_out': _jnp.float32, 'ln_mix_gain': _jnp.float32, 'ln_mix_bias': _jnp.float32, 'w_ffn_in': _jnp.float32, 'w_ffn_out': _jnp.float32, 'ln_ffn_gain': _jnp.float32, 'ln_ffn_bias': _jnp.float32}
MOMENT_SCALE = {'c_ctx': 7.471504e-04, 'w_ada': 6.872207e-03, 'b_ada': 1.234822e-02, 'w_in': 2.865597e-03, 'w_decay_up': 4.338448e-04, 'b_decay_up': 1.134846e-03, 'gla_norm_gain': 2.472035e-03, 'w_pool_group': 5.472796e-03, 'pool_scale': 5.449829e-03, 'w_gla_out': 2.471875e-03, 'w_pool_out': 3.881125e-03, 'w_out': 1.093927e-02, 'ln_mix_gain': 3.432791e+00, 'ln_mix_bias': 3.369966e-01, 'w_ffn_in': 3.006374e-03, 'w_ffn_out': 1.172003e-02, 'ln_ffn_gain': 9.641858e+00, 'ln_ffn_bias': 5.260384e-01}


def _to_microbatches(a, axis):
    t = _jnp.moveaxis(a, axis, 0)
    t = t.reshape((N_MICROBATCH, t.shape[0] // N_MICROBATCH) + t.shape[1:])
    return _jnp.moveaxis(t, 1, axis + 1)


def setup_inputs(seed: int = 0) -> dict:
    inp = _fwd_setup_inputs(seed)
    key = _jax.random.fold_in(_jax.random.key(seed), 7919)
    shape, _ = _output_shape()
    out = dict(inp)
    out["loss_target"] = _jax.random.normal(_jax.random.fold_in(key, 0), shape, _jnp.float32)
    for i, name in enumerate(TWIN_WEIGHTS):
        w = inp[name].astype(_jnp.float32)
        if MOMENT_SCALE is None:
            s = _jnp.sqrt(_jnp.mean(_jnp.square(w)) + 1e-30)
        else:
            s = MOMENT_SCALE[name]
        km, kv = _jax.random.split(_jax.random.fold_in(key, i + 1))
        out[name] = w
        out["m_" + name] = s * _jax.random.normal(km, w.shape, _jnp.float32)
        out["v_" + name] = (s * s) * _jax.random.uniform(kv, w.shape, _jnp.float32, 0.5, 1.5)
    if N_MICROBATCH > 1:
        for name, axis in PER_EXAMPLE_BATCH_AXIS.items():
            out[name] = _to_microbatches(out[name], axis)
    return {'x': out['x'], 'c': out['c'], 'ctx': out['ctx'], 'c_ctx': out['c_ctx'], 'w_ada': out['w_ada'], 'b_ada': out['b_ada'], 'w_in': out['w_in'], 'w_decay_up': out['w_decay_up'], 'b_decay_up': out['b_decay_up'], 'gla_norm_gain': out['gla_norm_gain'], 'w_pool_group': out['w_pool_group'], 'pool_scale': out['pool_scale'], 'w_gla_out': out['w_gla_out'], 'w_pool_out': out['w_pool_out'], 'w_out': out['w_out'], 'ln_mix_gain': out['ln_mix_gain'], 'ln_mix_bias': out['ln_mix_bias'], 'w_ffn_in': out['w_ffn_in'], 'w_ffn_out': out['w_ffn_out'], 'ln_ffn_gain': out['ln_ffn_gain'], 'ln_ffn_bias': out['ln_ffn_bias'], 'loss_target': out['loss_target'], 'm_c_ctx': out['m_c_ctx'], 'm_w_ada': out['m_w_ada'], 'm_b_ada': out['m_b_ada'], 'm_w_in': out['m_w_in'], 'm_w_decay_up': out['m_w_decay_up'], 'm_b_decay_up': out['m_b_decay_up'], 'm_gla_norm_gain': out['m_gla_norm_gain'], 'm_w_pool_group': out['m_w_pool_group'], 'm_pool_scale': out['m_pool_scale'], 'm_w_gla_out': out['m_w_gla_out'], 'm_w_pool_out': out['m_w_pool_out'], 'm_w_out': out['m_w_out'], 'm_ln_mix_gain': out['m_ln_mix_gain'], 'm_ln_mix_bias': out['m_ln_mix_bias'], 'm_w_ffn_in': out['m_w_ffn_in'], 'm_w_ffn_out': out['m_w_ffn_out'], 'm_ln_ffn_gain': out['m_ln_ffn_gain'], 'm_ln_ffn_bias': out['m_ln_ffn_bias'], 'v_c_ctx': out['v_c_ctx'], 'v_w_ada': out['v_w_ada'], 'v_b_ada': out['v_b_ada'], 'v_w_in': out['v_w_in'], 'v_w_decay_up': out['v_w_decay_up'], 'v_b_decay_up': out['v_b_decay_up'], 'v_gla_norm_gain': out['v_gla_norm_gain'], 'v_w_pool_group': out['v_w_pool_group'], 'v_pool_scale': out['v_pool_scale'], 'v_w_gla_out': out['v_w_gla_out'], 'v_w_pool_out': out['v_w_pool_out'], 'v_w_out': out['v_w_out'], 'v_ln_mix_gain': out['v_ln_mix_gain'], 'v_ln_mix_bias': out['v_ln_mix_bias'], 'v_w_ffn_in': out['v_w_ffn_in'], 'v_w_ffn_out': out['v_w_ffn_out'], 'v_ln_ffn_gain': out['v_ln_ffn_gain'], 'v_ln_ffn_bias': out['v_ln_ffn_bias']}


def _loss(weights, diff, rest, loss_target):
    with _jax.named_scope("forward"):
        args = {**rest, TWIN_DIFF_INPUT: diff, **{k: w.astype(_WEIGHT_DTYPES[k]) for k, w in weights.items()}}
        y = _forward(args)
    with _jax.named_scope("loss_head"):
        err = _jnp.square(y.astype(_jnp.float32) - loss_target)
        return 0.5 * _jnp.sum(_jnp.mean(err, axis=-1)) if err.ndim else 0.5 * err


def _adamw(w, g, m, v):
    m = ADAM_B1 * m + (1.0 - ADAM_B1) * g
    v = ADAM_B2 * v + (1.0 - ADAM_B2) * _jnp.square(g)
    m_hat = m / (1.0 - ADAM_B1 ** ADAM_STEP)
    v_hat = v / (1.0 - ADAM_B2 ** ADAM_STEP)
    delta = -ADAM_LR * (m_hat / (_jnp.sqrt(v_hat) + ADAM_EPS) + ADAM_WD * w)
    return delta, m, v


def reference(x, c, ctx, c_ctx, w_ada, b_ada, w_in, w_decay_up, b_decay_up, gla_norm_gain, w_pool_group, pool_scale, w_gla_out, w_pool_out, w_out, ln_mix_gain, ln_mix_bias, w_ffn_in, w_ffn_out, ln_ffn_gain, ln_ffn_bias, loss_target, m_c_ctx, m_w_ada, m_b_ada, m_w_in, m_w_decay_up, m_b_decay_up, m_gla_norm_gain, m_w_pool_group, m_pool_scale, m_w_gla_out, m_w_pool_out, m_w_out, m_ln_mix_gain, m_ln_mix_bias, m_w_ffn_in, m_w_ffn_out, m_ln_ffn_gain, m_ln_ffn_bias, v_c_ctx, v_w_ada, v_b_ada, v_w_in, v_w_decay_up, v_b_decay_up, v_gla_norm_gain, v_w_pool_group, v_pool_scale, v_w_gla_out, v_w_pool_out, v_w_out, v_ln_mix_gain, v_ln_mix_bias, v_w_ffn_in, v_w_ffn_out, v_ln_ffn_gain, v_ln_ffn_bias):
    given = dict(x=x, c=c, ctx=ctx, c_ctx=c_ctx, w_ada=w_ada, b_ada=b_ada, w_in=w_in, w_decay_up=w_decay_up, b_decay_up=b_decay_up, gla_norm_gain=gla_norm_gain, w_pool_group=w_pool_group, pool_scale=pool_scale, w_gla_out=w_gla_out, w_pool_out=w_pool_out, w_out=w_out, ln_mix_gain=ln_mix_gain, ln_mix_bias=ln_mix_bias, w_ffn_in=w_ffn_in, w_ffn_out=w_ffn_out, ln_ffn_gain=ln_ffn_gain, ln_ffn_bias=ln_ffn_bias, loss_target=loss_target, m_c_ctx=m_c_ctx, m_w_ada=m_w_ada, m_b_ada=m_b_ada, m_w_in=m_w_in, m_w_decay_up=m_w_decay_up, m_b_decay_up=m_b_decay_up, m_gla_norm_gain=m_gla_norm_gain, m_w_pool_group=m_w_pool_group, m_pool_scale=m_pool_scale, m_w_gla_out=m_w_gla_out, m_w_pool_out=m_w_pool_out, m_w_out=m_w_out, m_ln_mix_gain=m_ln_mix_gain, m_ln_mix_bias=m_ln_mix_bias, m_w_ffn_in=m_w_ffn_in, m_w_ffn_out=m_w_ffn_out, m_ln_ffn_gain=m_ln_ffn_gain, m_ln_ffn_bias=m_ln_ffn_bias, v_c_ctx=v_c_ctx, v_w_ada=v_w_ada, v_b_ada=v_b_ada, v_w_in=v_w_in, v_w_decay_up=v_w_decay_up, v_b_decay_up=v_b_decay_up, v_gla_norm_gain=v_gla_norm_gain, v_w_pool_group=v_w_pool_group, v_pool_scale=v_pool_scale, v_w_gla_out=v_w_gla_out, v_w_pool_out=v_w_pool_out, v_w_out=v_w_out, v_ln_mix_gain=v_ln_mix_gain, v_ln_mix_bias=v_ln_mix_bias, v_w_ffn_in=v_w_ffn_in, v_w_ffn_out=v_w_ffn_out, v_ln_ffn_gain=v_ln_ffn_gain, v_ln_ffn_bias=v_ln_ffn_bias)
    weights = {n: given[n] for n in TWIN_WEIGHTS}
    shared = {n: given[n] for n in SHARED_INPUTS}
    per_example = {n: given[n] for n in ['x', 'c', 'ctx']}
    grad_fn = _jax.value_and_grad(_loss, argnums=(0, 1))

    def one_microbatch(ex, loss_target):
        ex = dict(ex)
        diff = ex.pop(TWIN_DIFF_INPUT)
        return grad_fn(weights, diff, {**shared, **ex}, loss_target)

    if N_MICROBATCH == 1:
        loss, (grad_w, grad_x) = one_microbatch(per_example, given["loss_target"])
    else:
        def body(carry, xs):
            loss_sum, grad_sum = carry
            l_k, (gw_k, gx_k) = one_microbatch(xs[0], xs[1])
            with _jax.named_scope("update"):
                return (loss_sum + l_k, _jax.tree.map(_jnp.add, grad_sum, gw_k)), gx_k

        init = (_jnp.zeros((), _jnp.float32), _jax.tree.map(_jnp.zeros_like, weights))
        (loss, grad_w), grad_x = _jax.lax.scan(body, init, (per_example, given["loss_target"]))
    with _jax.named_scope("update"):
        delta_w, new_m, new_v = {}, {}, {}
        for n in TWIN_WEIGHTS:
            delta_w[n], new_m[n], new_v[n] = _adamw(weights[n], grad_w[n], given["m_" + n], given["v_" + n])
    return (loss, grad_x, *[grad_w[n] for n in TWIN_WEIGHTS], *[delta_w[n] for n in TWIN_WEIGHTS],
            *[new_m[n] for n in TWIN_WEIGHTS], *[new_v[n] for n in TWIN_WEIGHTS])
```

```python
import functools
import math

import numpy as np
import jax
import jax.numpy as jnp
from jax import lax
from jax.experimental import pallas as pl
from jax.experimental.pallas import tpu as pltpu

F32 = jnp.float32
BF16 = jnp.bfloat16

N_DEV = 8
N_HEADS = 4
GATE_RANK = 16
GATE_NORM = 16.0
CHUNK = 64
GRID_W = 64
POOL_WINDOWS = (2, 4, 8, 16)
N_MOD = 6
LN_EPS = 1e-5
RMS_EPS = 1e-6
ALR_PAD = 128
POOL_TB = 256
POOL_PAD_ROWS = 8
ADAM_LR = 0.001
ADAM_B1 = 0.9
ADAM_B2 = 0.999
ADAM_EPS = 1e-08
ADAM_WD = 0.01
ADAM_STEP = 10
VMEM_LIMIT = 56 * 1024 * 1024
MESH = pl.DeviceIdType.MESH


def _cparams(sem=None):
    return pltpu.CompilerParams(dimension_semantics=sem, vmem_limit_bytes=VMEM_LIMIT)


def _pick(dim, cap, mult):
    best = None
    for d in range(mult, min(dim, cap) + 1, mult):
        if dim % d == 0:
            best = d
    return best if best is not None else dim


def _sig(x):
    return 1.0 / (1.0 + jnp.exp(-x))


def _silu(x):
    return x * _sig(x)


def _dot(a, b):
    return lax.dot_general(a, b, (((1,), (0,)), ((), ())), preferred_element_type=F32)


def _dot_nt(a, b):
    return lax.dot_general(a, b, (((1,), (1,)), ((), ())), preferred_element_type=F32)


def _dot_tn(a, b):
    return lax.dot_general(a, b, (((0,), (0,)), ((), ())), preferred_element_type=F32)


def _split2(x):
    hi = x.astype(BF16)
    lo = (x - hi.astype(F32)).astype(BF16)
    return hi, lo


def _dot2(m_b, x):
    hi, lo = _split2(x)
    return _dot(m_b, hi) + _dot(m_b, lo)


def _dot3(m_b, x):
    h1 = x.astype(BF16)
    r1 = x - h1.astype(F32)
    h2 = r1.astype(BF16)
    h3 = (r1 - h2.astype(F32)).astype(BF16)
    return _dot(m_b, h1) + _dot(m_b, h2) + _dot(m_b, h3)


def _my_pos():
    return lax.axis_index("x"), lax.axis_index("y"), lax.axis_index("c")


def _all_gather(arrs, name):
    n = len(arrs)
    srcs = [a.reshape((a.shape[0], 1) + a.shape[1:]) for a in arrs]
    outs = [jax.ShapeDtypeStruct((a.shape[0], N_DEV) + a.shape[1:], a.dtype) for a in arrs]

    def body(*refs):
        in_refs, out_refs = refs[:n], refs[n:2 * n]
        send_sems, recv_sems, local_sems = refs[2 * n:]
        x, y, c = _my_pos()
        me, sibling = (x, y, c), (x, y, 1 - c)
        chips = [(1 - x, y), (x, 1 - y), (1 - x, 1 - y)]

        def slot(t, pos):
            return out_refs[t].at[:, pl.ds(4 * pos[0] + 2 * pos[1] + pos[2], 1)]

        def copy(t, k, block, to, src=None):
            return pltpu.make_async_remote_copy(
                src_ref=slot(t, block) if src is None else src, dst_ref=slot(t, block),
                send_sem=send_sems.at[t * 7 + k], recv_sem=recv_sems.at[t * 7 + k],
                device_id=to, device_id_type=MESH)

        mine = [pltpu.make_async_copy(in_refs[t], slot(t, me), local_sems.at[t]) for t in range(n)]
        for cp in mine:
            cp.start()
        first = []
        for t in range(n):
            first.append(copy(t, 0, me, sibling, src=in_refs[t]))
            first += [copy(t, 1 + j, me, (*chip, c), src=in_refs[t]) for j, chip in enumerate(chips)]
        for cp in first:
            cp.start()
        passed = []
        for j, chip in enumerate(chips):
            for t in range(n):
                copy(t, 1 + j, (*chip, c), me).wait_recv()
                fwd = copy(t, 4 + j, (*chip, c), sibling)
                fwd.start()
                passed.append(fwd)
        for t in range(n):
            copy(t, 0, sibling, me).wait_recv()
            for j, chip in enumerate(chips):
                copy(t, 4 + j, (*chip, 1 - c), me).wait_recv()
        for cp in first + passed:
            cp.wait_send()
        for cp in mine:
            cp.wait()

    any_spec = pl.BlockSpec(memory_space=pl.ANY)
    res = pl.pallas_call(
        body, name=name, out_shape=outs,
        in_specs=[any_spec] * n, out_specs=[any_spec] * n,
        scratch_shapes=[pltpu.SemaphoreType.DMA((7 * n,)), pltpu.SemaphoreType.DMA((7 * n,)),
                        pltpu.SemaphoreType.DMA((n,))],
        compiler_params=pltpu.CompilerParams(has_side_effects=True),
    )(*srcs)
    return list(res)


def _gather_flat(vecs, name):
    padded = []
    for v in vecs:
        n = v.shape[0]
        padded.append(jnp.pad(v, (0, -n % 128)).reshape(1, -1, 128))
    res = _all_gather(padded, name)
    return [r.reshape(N_DEV, -1)[:, :v.shape[0]] for r, v in zip(res, vecs)]


def _all_to_all(arrs, name):
    n = len(arrs)
    outs = [jax.ShapeDtypeStruct(a.shape, a.dtype) for a in arrs]

    def body(*refs):
        in_refs, out_refs = refs[:n], refs[n:2 * n]
        send_sems, recv_sems, local_sems = refs[2 * n:]
        x, y, c = _my_pos()
        me = 4 * x + 2 * y + c
        copies = []
        mine = []
        for t in range(n):
            cp = pltpu.make_async_copy(in_refs[t].at[pl.ds(me, 1)], out_refs[t].at[pl.ds(me, 1)], local_sems.at[t])
            cp.start()
            mine.append(cp)
            for k in range(1, N_DEV):
                px, py, pc = x ^ ((k >> 2) & 1), y ^ ((k >> 1) & 1), c ^ (k & 1)
                peer = 4 * px + 2 * py + pc
                cp = pltpu.make_async_remote_copy(
                    src_ref=in_refs[t].at[pl.ds(peer, 1)], dst_ref=out_refs[t].at[pl.ds(me, 1)],
                    send_sem=send_sems.at[t * 7 + k - 1], recv_sem=recv_sems.at[t * 7 + k - 1],
                    device_id=(px, py, pc), device_id_type=MESH)
                cp.start()
                copies.append((cp, t, k, peer))
        for cp, t, k, peer in copies:
            pltpu.make_async_remote_copy(
                src_ref=in_refs[t].at[pl.ds(peer, 1)], dst_ref=out_refs[t].at[pl.ds(peer, 1)],
                send_sem=send_sems.at[t * 7 + k - 1], recv_sem=recv_sems.at[t * 7 + k - 1],
                device_id=(x, y, c), device_id_type=MESH).wait_recv()
        for cp, t, k, peer in copies:
            cp.wait_send()
        for cp in mine:
            cp.wait()

    any_spec = pl.BlockSpec(memory_space=pl.ANY)
    res = pl.pallas_call(
        body, name=name, out_shape=outs,
        in_specs=[any_spec] * n, out_specs=[any_spec] * n,
        scratch_shapes=[pltpu.SemaphoreType.DMA((7 * n,)), pltpu.SemaphoreType.DMA((7 * n,)),
                        pltpu.SemaphoreType.DMA((n,))],
        compiler_params=pltpu.CompilerParams(has_side_effects=True),
    )(*arrs)
    return list(res)


def _mm(a, b, mode, out_dtype=F32, name="mm", bias=None):
    if mode == "nn":
        (M, K), (K2, N) = a.shape, b.shape
    elif mode == "nt":
        (M, K), (N, K2) = a.shape, b.shape
    else:
        (K, M), (K2, N) = a.shape, b.shape
    assert K == K2, (a.shape, b.shape, mode)
    tm = _pick(M, 1100, 16) if mode != "tn" else _pick(M, 1024, 128)
    tn = _pick(N, 1024, 128)
    tk = _pick(K, 2176, 128)
    nk = K // tk
    if mode == "nn":
        a_spec = pl.BlockSpec((tm, tk), lambda i, j, k: (i, k))
        b_spec = pl.BlockSpec((tk, tn), lambda i, j, k: (k, j))
        dot = _dot
    elif mode == "nt":
        a_spec = pl.BlockSpec((tm, tk), lambda i, j, k: (i, k))
        b_spec = pl.BlockSpec((tn, tk), lambda i, j, k: (j, k))
        dot = _dot_nt
    else:
        a_spec = pl.BlockSpec((tk, tm), lambda i, j, k: (k, i))
        b_spec = pl.BlockSpec((tk, tn), lambda i, j, k: (k, j))
        dot = _dot_tn
    in_specs = [a_spec, b_spec]
    args = [a, b]
    if bias is not None:
        in_specs.append(pl.BlockSpec((1, tn), lambda i, j, k: (0, j)))
        args.append(bias)

    def body(*refs):
        a_ref, b_ref = refs[0], refs[1]
        bias_ref = refs[2] if bias is not None else None
        o_ref = refs[3] if bias is not None else refs[2]
        p = dot(a_ref[...].astype(BF16), b_ref[...].astype(BF16))

        def finish(acc):
            if bias_ref is not None:
                acc = acc + bias_ref[...]
            o_ref[...] = acc.astype(o_ref.dtype)

        if nk == 1:
            finish(p)
        else:
            acc_ref = refs[-1]
            k = pl.program_id(2)

            @pl.when(k == 0)
            def _():
                acc_ref[...] = p

            @pl.when(k > 0)
            def _():
                acc_ref[...] += p

            @pl.when(k == nk - 1)
            def _():
                finish(acc_ref[...])

    return pl.pallas_call(
        body, name=name, grid=(M // tm, N // tn, nk),
        in_specs=in_specs, out_specs=pl.BlockSpec((tm, tn), lambda i, j, k: (i, j)),
        out_shape=jax.ShapeDtypeStruct((M, N), out_dtype),
        scratch_shapes=[pltpu.VMEM((tm, tn), F32)] if nk > 1 else [],
        compiler_params=_cparams(("parallel", "parallel", "arbitrary")),
    )(*args)


def _row(rb, w, col=0):
    return pl.BlockSpec((rb, w), lambda i: (i, col))


def _modspec(d, sec):
    return pl.BlockSpec((None, 1, d), lambda i: (jnp.minimum(i, 1), 0, sec))


def _vec(w):
    return pl.BlockSpec((1, w), lambda i: (0, 0))


def _acc2(w):
    return pl.BlockSpec((None, 1, w), lambda i: (jnp.minimum(i, 1), 0, 0))


def _accum(ref, val):
    i = pl.program_id(0)

    @pl.when(i <= 1)
    def _():
        ref[...] = val

    @pl.when(i > 1)
    def _():
        ref[...] += val


def _acc_shape(w):
    return jax.ShapeDtypeStruct((2, 1, w), F32)


def _mod_f(x, sc, sh):
    return x * (1.0 + sc) + sh


def _mod_fwd(xa, mod, sec_sc, sec_sh, rb, name):
    T, D = xa.shape

    def body(x_ref, sc_ref, sh_ref, h_ref):
        h_ref[...] = _mod_f(x_ref[...], sc_ref[...], sh_ref[...]).astype(BF16)

    return pl.pallas_call(
        body, name=name, grid=(T // rb,),
        in_specs=[_row(rb, D), _modspec(D, sec_sc), _modspec(D, sec_sh)],
        out_specs=_row(rb, D), out_shape=jax.ShapeDtypeStruct((T, D), BF16),
        compiler_params=_cparams(("parallel",)),
    )(xa, mod, mod)


def _mod_bwd(dxa, dh, xa, mod, sec_sc, sec_sh, rb, name):
    T, D = xa.shape

    def body(dxa_ref, dh_ref, x_ref, sc_ref, sh_ref, dx_ref, dsc_ref, dsh_ref):
        _, vjp = jax.vjp(_mod_f, x_ref[...], sc_ref[...], sh_ref[...])
        dx, dsc, dsh = vjp(dh_ref[...])
        dx_ref[...] = dxa_ref[...] + dx
        _accum(dsc_ref, dsc)
        _accum(dsh_ref, dsh)

    return pl.pallas_call(
        body, name=name, grid=(T // rb,),
        in_specs=[_row(rb, D), _row(rb, D), _row(rb, D), _modspec(D, sec_sc), _modspec(D, sec_sh)],
        out_specs=[_row(rb, D), _acc2(D), _acc2(D)],
        out_shape=[jax.ShapeDtypeStruct((T, D), F32), _acc_shape(D), _acc_shape(D)],
        compiler_params=_cparams(("arbitrary",)),
    )(dxa, dh, xa, mod, mod)


def _ln_f(alpha, x, mix, gt, gain, bias):
    z = alpha * x + gt * mix
    mu = jnp.mean(z, axis=-1, keepdims=True)
    zc = z - mu
    var = jnp.mean(zc * zc, axis=-1, keepdims=True)
    return zc * lax.rsqrt(var + LN_EPS) * gain + bias


def _unit_fwd(alpha, x, mix, mod, sec_gt, gain, bias, next_mod, rb, name):
    T, D = x.shape
    has_mod = next_mod is not None

    def body(*refs):
        if has_mod:
            x_ref, mix_ref, gt_ref, g_ref, b_ref, sc_ref, sh_ref, xo_ref, h_ref = refs
        else:
            x_ref, mix_ref, gt_ref, g_ref, b_ref, xo_ref = refs
        xo = _ln_f(alpha, x_ref[...], mix_ref[...], gt_ref[...], g_ref[...], b_ref[...])
        xo_ref[...] = xo
        if has_mod:
            h_ref[...] = _mod_f(xo, sc_ref[...], sh_ref[...]).astype(BF16)

    in_specs = [_row(rb, D), _row(rb, D), _modspec(D, sec_gt), _vec(D), _vec(D)]
    args = [x, mix, mod, gain, bias]
    out_specs = [_row(rb, D)]
    out_shape = [jax.ShapeDtypeStruct((T, D), F32)]
    if has_mod:
        nm, s_sc, s_sh = next_mod
        in_specs += [_modspec(D, s_sc), _modspec(D, s_sh)]
        args += [nm, nm]
        out_specs.append(_row(rb, D))
        out_shape.append(jax.ShapeDtypeStruct((T, D), BF16))
    res = pl.pallas_call(
        body, name=name, grid=(T // rb,), in_specs=in_specs, out_specs=out_specs, out_shape=out_shape,
        compiler_params=_cparams(("parallel",)),
    )(*args)
    return (res[0], res[1]) if has_mod else (res[0], None)


def _unit_bwd(alpha, dxo, dh, x, mix, mod, sec_gt, gain, bias, next_mod, rb, name):
    T, D = x.shape
    has_mod = next_mod is not None

    def body(*refs):
        if has_mod:
            (dxo_ref, dh_ref, x_ref, mix_ref, gt_ref, g_ref, b_ref, sc_ref, sh_ref,
             dx_ref, dmix_ref, dgt_ref, dg_ref, db_ref, dsc_ref, dsh_ref) = refs
        else:
            (dxo_ref, x_ref, mix_ref, gt_ref, g_ref, b_ref,
             dx_ref, dmix_ref, dgt_ref, dg_ref, db_ref) = refs
        xo, vjp = jax.vjp(functools.partial(_ln_f, alpha), x_ref[...], mix_ref[...], gt_ref[...],
                          g_ref[...], b_ref[...])
        dxo_t = dxo_ref[...]
        if has_mod:
            _, vjp_m = jax.vjp(_mod_f, xo, sc_ref[...], sh_ref[...])
            dxo_m, dsc, dsh = vjp_m(dh_ref[...])
            dxo_t = dxo_t + dxo_m
            _accum(dsc_ref, dsc)
            _accum(dsh_ref, dsh)
        dx, dmix, dgt, dg, db = vjp(dxo_t)
        dx_ref[...] = dx
        dmix_ref[...] = dmix.astype(BF16)
        _accum(dgt_ref, dgt)
        _accum(dg_ref, dg)
        _accum(db_ref, db)

    in_specs = [_row(rb, D)]
    args = [dxo]
    if has_mod:
        in_specs.append(_row(rb, D))
        args.append(dh)
    in_specs += [_row(rb, D), _row(rb, D), _modspec(D, sec_gt), _vec(D), _vec(D)]
    args += [x, mix, mod, gain, bias]
    out_specs = [_row(rb, D), _row(rb, D), _acc2(D), _acc2(D), _acc2(D)]
    out_shape = [jax.ShapeDtypeStruct((T, D), F32), jax.ShapeDtypeStruct((T, D), BF16),
                 _acc_shape(D), _acc_shape(D), _acc_shape(D)]
    if has_mod:
        nm, s_sc, s_sh = next_mod
        in_specs += [_modspec(D, s_sc), _modspec(D, s_sh)]
        args += [nm, nm]
        out_specs += [_acc2(D), _acc2(D)]
        out_shape += [_acc_shape(D), _acc_shape(D)]
    res = pl.pallas_call(
        body, name=name, grid=(T // rb,), in_specs=in_specs, out_specs=out_specs, out_shape=out_shape,
        compiler_params=_cparams(("arbitrary",)),
    )(*args)
    if has_mod:
        return res
    return list(res) + [None, None]


def _log_sigmoid(z):
    return jnp.minimum(z, 0.0) - jnp.log(1.0 + jnp.exp(-jnp.abs(z)))


def _decay_fwd(alr, wdu, bdu, rb, name):
    T = alr.shape[0]
    W = wdu.shape[1]

    def body(a_ref, w_ref, b_ref, la_ref):
        z = _dot(a_ref[...].astype(BF16), w_ref[...]) + b_ref[...]
        la_ref[...] = _log_sigmoid(z) * (1.0 / GATE_NORM)

    return pl.pallas_call(
        body, name=name, grid=(T // rb,),
        in_specs=[_row(rb, ALR_PAD), pl.BlockSpec((ALR_PAD, W), lambda i: (0, 0)), _vec(W)],
        out_specs=_row(rb, W), out_shape=jax.ShapeDtypeStruct((T, W), F32),
        compiler_params=_cparams(("parallel",)),
    )(alr, wdu, bdu)


def _decay_bwd(dla_f, dla_b, alr, wdu, bdu, rb, name):
    T = alr.shape[0]
    W = wdu.shape[1]
    DK = W // 2

    def body(df_ref, db_ref, a_ref, w_ref, b_ref, dalr_ref, gw_ref, gb_ref):
        i = pl.program_id(0)
        ab = a_ref[...].astype(BF16)
        z = _dot(ab, w_ref[...]) + b_ref[...]
        dla = jnp.concatenate([df_ref[...], db_ref[...]], axis=1)
        dz = dla * _sig(-z) * (1.0 / GATE_NORM)
        dzb = dz.astype(BF16)
        dalr_ref[...] = _dot_nt(dzb, w_ref[...]).astype(BF16)
        gw = _dot_tn(ab, dzb)
        gb = jnp.sum(dz, axis=0, keepdims=True)

        @pl.when(i == 0)
        def _():
            gw_ref[...] = gw
            gb_ref[...] = gb

        @pl.when(i > 0)
        def _():
            gw_ref[...] += gw
            gb_ref[...] += gb

    return pl.pallas_call(
        body, name=name, grid=(T // rb,),
        in_specs=[_row(rb, DK), _row(rb, DK), _row(rb, ALR_PAD), pl.BlockSpec((ALR_PAD, W), lambda i: (0, 0)), _vec(W)],
        out_specs=[_row(rb, ALR_PAD), pl.BlockSpec((ALR_PAD, W), lambda i: (0, 0)), _vec(W)],
        out_shape=[jax.ShapeDtypeStruct((T, ALR_PAD), BF16), jax.ShapeDtypeStruct((ALR_PAD, W), F32),
                   jax.ShapeDtypeStruct((1, W), F32)],
        compiler_params=_cparams(("arbitrary",)),
    )(dla_f, dla_b, alr, wdu, bdu)


def _tri(rev):
    m = np.tril(np.ones((CHUNK, CHUNK), np.float32))
    return jnp.asarray(m.T if rev else m, BF16)


def _gla_chunk_common(q_ref, k_ref, v_ref, la_ref, tri_ref, rows, rev, scale_q):
    mid = CHUNK // 2 if rev else CHUNK // 2 - 1
    last_i = 0 if rev else CHUNK - 1
    q = q_ref[rows, :] * scale_q
    k = k_ref[rows, :]
    v = v_ref[rows, :]
    cum = _dot3(tri_ref[...], la_ref[rows, :])
    ref = cum[mid:mid + 1, :]
    last = cum[last_i:last_i + 1, :]
    e_q = jnp.exp(cum - ref)
    e_k = jnp.exp(ref - cum)
    e_c = jnp.exp(cum)
    e_s = jnp.exp(last - cum)
    e_l = jnp.exp(last)
    ri = lax.broadcasted_iota(jnp.int32, (CHUNK, CHUNK), 0)
    ci = lax.broadcasted_iota(jnp.int32, (CHUNK, CHUNK), 1)
    mask = (ci >= ri) if rev else (ci <= ri)
    return q, k, v, e_q, e_k, e_c, e_s, e_l, mask, last_i


def _gla_specs(rb, hk, hv, D, rbmap, la_col0):
    q_col0 = 4 * D // hk
    k_col0 = q_col0 + N_HEADS
    return [
        pl.BlockSpec((rb, hk), lambda h, i: (rbmap(i), q_col0 + h)),
        pl.BlockSpec((rb, hk), lambda h, i: (rbmap(i), k_col0 + h)),
        pl.BlockSpec((rb, hv), lambda h, i: (rbmap(i), h)),
        pl.BlockSpec((rb, hk), lambda h, i: (rbmap(i), la_col0 + h)),
        pl.BlockSpec((CHUNK, CHUNK), lambda h, i: (0, 0)),
    ]


def _gla_fwd(proj, la, rev, rb, D, name):
    T = proj.shape[0]
    nb = T // rb
    ncb = rb // CHUNK
    hk, hv = D // 2 // N_HEADS, D // N_HEADS
    scale_q = float(hk) ** -0.5
    rbmap = (lambda i: jnp.where(i == 0, 0, nb - i)) if rev else (lambda i: i)

    def body(q_ref, k_ref, v_ref, la_ref, tri_ref, o_ref, s_ref, st_ref):
        @pl.when(pl.program_id(1) == 0)
        def _():
            st_ref[...] = jnp.zeros_like(st_ref)

        order = range(ncb - 1, -1, -1) if rev else range(ncb)
        for cc in order:
            rows = slice(cc * CHUNK, (cc + 1) * CHUNK)
            q, k, v, e_q, e_k, e_c, e_s, e_l, mask, _ = _gla_chunk_common(
                q_ref, k_ref, v_ref, la_ref, tri_ref, rows, rev, scale_q)
            vb = v.astype(BF16)
            a = jnp.where(mask, _dot_nt((q * e_q).astype(BF16), (k * e_k).astype(BF16)), 0.0)
            st = st_ref[...]
            s_ref[cc] = st
            o = _dot(a.astype(BF16), vb) + _dot_nt((q * e_c).astype(BF16), st.astype(BF16))
            o_ref[rows, :] = o
            st_ref[...] = st * e_l + _dot_tn(vb, (k * e_s).astype(BF16))

    la_col0 = N_HEADS if rev else 0
    return pl.pallas_call(
        body, name=name, grid=(N_HEADS, nb),
        in_specs=_gla_specs(rb, hk, hv, D, rbmap, la_col0),
        out_specs=[pl.BlockSpec((rb, hv), lambda h, i: (rbmap(i), h)),
                   pl.BlockSpec((None, ncb, hv, hk), lambda h, i: (h, rbmap(i), 0, 0))],
        out_shape=[jax.ShapeDtypeStruct((T, D), F32),
                   jax.ShapeDtypeStruct((N_HEADS, T // CHUNK, hv, hk), F32)],
        scratch_shapes=[pltpu.VMEM((hv, hk), F32)],
        compiler_params=_cparams(("parallel", "arbitrary")),
    )(proj, proj, proj, la, _tri(rev))


def _gla_bwd(proj, la, do, states, rev, rb, D, prev, name):
    T = proj.shape[0]
    nb = T // rb
    ncb = rb // CHUNK
    hk, hv = D // 2 // N_HEADS, D // N_HEADS
    DK = D // 2
    scale_q = float(hk) ** -0.5
    if rev:
        rbmap = lambda i: jnp.where(i == nb - 1, 0, i + 1)
    else:
        rbmap = lambda i: nb - 1 - i
    has_prev = prev is not None
    out_dt = BF16 if has_prev else F32

    def body(*refs):
        q_ref, k_ref, v_ref, la_ref, tri_ref, trit_ref, do_ref, s_ref = refs[:8]
        n_in = 11 if has_prev else 8
        pq_ref, pk_ref, pv_ref = refs[8:11] if has_prev else (None, None, None)
        dq_ref, dk_ref, dv_ref, dla_ref, ds_ref = refs[n_in:]

        @pl.when(pl.program_id(1) == 0)
        def _():
            ds_ref[...] = jnp.zeros_like(ds_ref)

        order = range(ncb) if rev else range(ncb - 1, -1, -1)
        for cc in order:
            rows = slice(cc * CHUNK, (cc + 1) * CHUNK)
            q, k, v, e_q, e_k, e_c, e_s, e_l, mask, last_i = _gla_chunk_common(
                q_ref, k_ref, v_ref, la_ref, tri_ref, rows, rev, scale_q)
            vb = v.astype(BF16)
            qi = (q * e_q).astype(BF16)
            ki = (k * e_k).astype(BF16)
            qc = (q * e_c).astype(BF16)
            ks = (k * e_s).astype(BF16)
            a = jnp.where(mask, _dot_nt(qi, ki), 0.0).astype(BF16)
            st0 = s_ref[cc]
            st0b = st0.astype(BF16)
            dst1 = ds_ref[...]
            dst1b = dst1.astype(BF16)
            dob = do_ref[rows, :].astype(BF16)
            da = jnp.where(mask, _dot_nt(dob, vb), 0.0).astype(BF16)
            dv = _dot_tn(a, dob) + _dot_nt(ks, dst1b)
            dq_inter = _dot(dob, st0b) * e_c
            dk_inter = _dot(vb, dst1b) * e_s
            dq_s = _dot(da, ki) * e_q + dq_inter
            dk = _dot_tn(da, qi) * e_k + dk_inter
            extra = (jnp.sum(k * dk_inter, axis=0, keepdims=True)
                     + e_l * jnp.sum(dst1 * st0, axis=0, keepdims=True))
            rowi = lax.broadcasted_iota(jnp.int32, (CHUNK, hk), 0)
            dcum = q * dq_s - k * dk + jnp.where(rowi == last_i, extra, 0.0)
            dla_ref[rows, :] = _dot3(trit_ref[...], dcum)
            dq = dq_s * scale_q
            if has_prev:
                dq = dq + pq_ref[rows, :]
                dk = dk + pk_ref[rows, :]
                dv = dv + pv_ref[rows, :]
            dq_ref[rows, :] = dq.astype(out_dt)
            dk_ref[rows, :] = dk.astype(out_dt)
            dv_ref[rows, :] = dv.astype(out_dt)
            ds_ref[...] = dst1 * e_l + _dot_tn(dob, qc)

    la_col0 = N_HEADS if rev else 0
    in_specs = _gla_specs(rb, hk, hv, D, rbmap, la_col0)
    in_specs += [pl.BlockSpec((CHUNK, CHUNK), lambda h, i: (0, 0)),
                 pl.BlockSpec((rb, hv), lambda h, i: (rbmap(i), h)),
                 pl.BlockSpec((None, ncb, hv, hk), lambda h, i: (h, rbmap(i), 0, 0))]
    args = [proj, proj, proj, la, _tri(rev), _tri(not rev), do, states]
    hk_spec = pl.BlockSpec((rb, hk), lambda h, i: (rbmap(i), h))
    hv_spec = pl.BlockSpec((rb, hv), lambda h, i: (rbmap(i), h))
    if has_prev:
        in_specs += [hk_spec, hk_spec, hv_spec]
        args += list(prev)
    return pl.pallas_call(
        body, name=name, grid=(N_HEADS, nb), in_specs=in_specs,
        out_specs=[hk_spec, hk_spec, hv_spec, hk_spec],
        out_shape=[jax.ShapeDtypeStruct((T, DK), out_dt), jax.ShapeDtypeStruct((T, DK), out_dt),
                   jax.ShapeDtypeStruct((T, D), out_dt), jax.ShapeDtypeStruct((T, DK), F32)],
        scratch_shapes=[pltpu.VMEM((hv, hk), F32)],
        compiler_params=_cparams(("parallel", "arbitrary")),
    )(*args)


def _glaout_f(of, ob, g, gain):
    o = of + ob
    n = o * lax.rsqrt(jnp.mean(o * o, axis=-1, keepdims=True) + RMS_EPS)
    return n * gain * _silu(g)


def _glaout_fwd(o_f, o_b, proj, gain, rb, name):
    T, D = o_f.shape
    hv = D // N_HEADS

    def body(of_ref, ob_ref, g_ref, gn_ref, u_ref):
        for h in range(N_HEADS):
            cs = slice(h * hv, (h + 1) * hv)
            u_ref[:, cs] = _glaout_f(of_ref[:, cs], ob_ref[:, cs], g_ref[:, cs], gn_ref[:, cs]).astype(BF16)

    return pl.pallas_call(
        body, name=name, grid=(T // rb,),
        in_specs=[_row(rb, D), _row(rb, D), _row(rb, D, 1), _vec(D)],
        out_specs=_row(rb, D), out_shape=jax.ShapeDtypeStruct((T, D), BF16),
        compiler_params=_cparams(("parallel",)),
    )(o_f, o_b, proj, gain)


def _glaout_bwd(du, o_f, o_b, proj, gain, rb, name):
    T, D = o_f.shape
    hv = D // N_HEADS

    def body(du_ref, of_ref, ob_ref, g_ref, gn_ref, do_ref, dg_ref, dgn_ref, tmp_ref):
        for h in range(N_HEADS):
            cs = slice(h * hv, (h + 1) * hv)
            _, vjp = jax.vjp(_glaout_f, of_ref[:, cs], ob_ref[:, cs], g_ref[:, cs], gn_ref[:, cs])
            d_of, _, dg, dgn = vjp(du_ref[:, cs])
            do_ref[:, cs] = d_of
            dg_ref[:, cs] = dg.astype(BF16)
            tmp_ref[:, cs] = dgn
        _accum(dgn_ref, tmp_ref[...])

    return pl.pallas_call(
        body, name=name, grid=(T // rb,),
        in_specs=[_row(rb, D), _row(rb, D), _row(rb, D), _row(rb, D, 1), _vec(D)],
        out_specs=[_row(rb, D), _row(rb, D), _acc2(D)],
        out_shape=[jax.ShapeDtypeStruct((T, D), F32), jax.ShapeDtypeStruct((T, D), BF16), _acc_shape(D)],
        scratch_shapes=[pltpu.VMEM((1, D), F32)],
        compiler_params=_cparams(("arbitrary",)),
    )(du, o_f, o_b, proj, gain)


def _merge_f(bg1, bg2, yg, yp):
    return _sig(bg1) * yg + _sig(bg2) * yp


def _merge_fwd(proj, y_gla, y_pool, rb, name):
    T, D = y_gla.shape

    def body(b1_ref, b2_ref, yg_ref, yp_ref, m_ref):
        m_ref[...] = _merge_f(b1_ref[...], b2_ref[...], yg_ref[...], yp_ref[...]).astype(BF16)

    return pl.pallas_call(
        body, name=name, grid=(T // rb,),
        in_specs=[_row(rb, D, 2), _row(rb, D, 3), _row(rb, D), _row(rb, D)],
        out_specs=_row(rb, D), out_shape=jax.ShapeDtypeStruct((T, D), BF16),
        compiler_params=_cparams(("parallel",)),
    )(proj, proj, y_gla, y_pool)


def _merge_bwd(dm, proj, y_gla, y_pool, rb, name):
    T, D = y_gla.shape

    def body(dm_ref, b1_ref, b2_ref, yg_ref, yp_ref, d1_ref, d2_ref, dyg_ref, dyp_ref):
        _, vjp = jax.vjp(_merge_f, b1_ref[...], b2_ref[...], yg_ref[...], yp_ref[...])
        d1, d2, dyg, dyp = vjp(dm_ref[...])
        d1_ref[...] = d1.astype(BF16)
        d2_ref[...] = d2.astype(BF16)
        dyg_ref[...] = dyg.astype(BF16)
        dyp_ref[...] = dyp.astype(BF16)

    return pl.pallas_call(
        body, name=name, grid=(T // rb,),
        in_specs=[_row(rb, D), _row(rb, D, 2), _row(rb, D, 3), _row(rb, D), _row(rb, D)],
        out_specs=[_row(rb, D)] * 4, out_shape=[jax.ShapeDtypeStruct((T, D), BF16)] * 4,
        compiler_params=_cparams(("parallel",)),
    )(dm, proj, proj, y_gla, y_pool)


def _swiglu_f(gate, up):
    return _silu(gate) * up


def _swiglu_fwd(ff, rb, name):
    T, W2 = ff.shape
    dff = W2 // 2
    tw = _pick(dff, 1024, 128)
    nbw = dff // tw

    def body(g_ref, u_ref, s_ref):
        s_ref[...] = _swiglu_f(g_ref[...], u_ref[...]).astype(BF16)

    return pl.pallas_call(
        body, name=name, grid=(T // rb, nbw),
        in_specs=[pl.BlockSpec((rb, tw), lambda i, j: (i, j)), pl.BlockSpec((rb, tw), lambda i, j: (i, nbw + j))],
        out_specs=pl.BlockSpec((rb, tw), lambda i, j: (i, j)),
        out_shape=jax.ShapeDtypeStruct((T, dff), BF16),
        compiler_params=_cparams(("parallel", "parallel")),
    )(ff, ff)


def _swiglu_bwd(ds, ff, rb, name):
    T, W2 = ff.shape
    dff = W2 // 2
    tw = _pick(dff, 1024, 128)
    nbw = dff // tw

    def body(ds_ref, g_ref, u_ref, d_ref):
        j = pl.program_id(1)
        _, vjp = jax.vjp(_swiglu_f, g_ref[...], u_ref[...])
        dg, du = vjp(ds_ref[...])
        d_ref[...] = jnp.where(j < nbw, dg, du).astype(BF16)

    return pl.pallas_call(
        body, name=name, grid=(T // rb, 2 * nbw),
        in_specs=[pl.BlockSpec((rb, tw), lambda i, j: (i, j % nbw)),
                  pl.BlockSpec((rb, tw), lambda i, j: (i, j % nbw)),
                  pl.BlockSpec((rb, tw), lambda i, j: (i, nbw + j % nbw))],
        out_specs=pl.BlockSpec((rb, tw), lambda i, j: (i, j)),
        out_shape=jax.ShapeDtypeStruct((T, W2), BF16),
        compiler_params=_cparams(("parallel", "parallel")),
    )(ds, ff, ff)


def _pool_consts(ctx_len, seq):
    rows = seq // GRID_W
    reps = POOL_TB // GRID_W
    mw, bc, cw, ch, cc = [], [], [], [], []
    for w in POOL_WINDOWS:
        lo, hi = w // 2, w - w // 2 - 1

        def band(n):
            i = np.arange(n)[:, None]
            j = np.arange(n)[None, :]
            return ((j - i >= -lo) & (j - i <= hi)).astype(np.float32)

        def count(n):
            i = np.arange(n)
            return (np.minimum(i + hi + 1, n) - np.maximum(i - lo, 0)).astype(np.float32)

        mw.append(np.kron(np.eye(reps, dtype=np.float32), band(GRID_W)))
        bc.append(band(ctx_len))
        cw.append(np.tile(count(GRID_W), reps)[:, None])
        ch.append(np.repeat(count(rows), GRID_W)[:, None])
        cc.append(count(ctx_len)[:, None])
    mw, bc = np.stack(mw), np.stack(bc)
    return dict(
        mw=jnp.asarray(mw, BF16), mwt=jnp.asarray(mw.transpose(0, 2, 1), BF16),
        bc=jnp.asarray(bc, BF16), bct=jnp.asarray(bc.transpose(0, 2, 1), BF16),
        cw=jnp.asarray(np.stack(cw)), ch=jnp.asarray(np.stack(ch)), cc=jnp.asarray(np.stack(cc)))


def _gspec(*shape):
    nd = len(shape)
    return pl.BlockSpec((None,) + tuple(shape), lambda g: (g,) + (0,) * nd)


def _pool_fwd(proj, pc, wg, scale, ctx_len, D, name):
    T = proj.shape[0]
    seq = T - ctx_len
    dp = D // 2
    pg = dp // len(POOL_WINDOWS)
    nblk = seq // POOL_TB
    padt = POOL_PAD_ROWS * GRID_W
    p_col0 = 5 * D // pg

    def body(p_ref, mw_ref, bc_ref, cw_ref, ch_ref, cc_ref, wg_ref, sc_ref, pd_ref, y0_ref, r_ref, pad_ref):
        g = pl.program_id(0)

        def tail(rows, mean, x):
            pdb = (mean - x).astype(BF16)
            y0 = _dot(pdb, wg_ref[...])
            pd_ref[rows, :] = pdb
            y0_ref[rows, :] = y0
            r_ref[rows, :] = (y0 * sc_ref[...]).astype(BF16)

        xc = p_ref[0:ctx_len, :]
        tail(slice(0, ctx_len), _dot2(bc_ref[...], xc) / cc_ref[...], xc)

        pad_ref[0:padt, :] = jnp.zeros((padt, pg), F32)
        pad_ref[padt + seq:, :] = jnp.zeros((padt, pg), F32)

        def wpass(b, carry):
            rows = pl.ds(pl.multiple_of(ctx_len + b * POOL_TB, CHUNK), POOL_TB)
            dst = pl.ds(pl.multiple_of(padt + b * POOL_TB, CHUNK), POOL_TB)
            pad_ref[dst, :] = _dot2(mw_ref[...], p_ref[rows, :]) / cw_ref[...]
            return carry

        lax.fori_loop(0, nblk, wpass, 0)

        for gi, w in enumerate(POOL_WINDOWS):
            lo, hi = w // 2, w - w // 2 - 1

            @pl.when(g == gi)
            def _():
                def hpass(b, carry):
                    acc = jnp.zeros((POOL_TB, pg), F32)
                    for d in range(-lo, hi + 1):
                        src = pl.ds(pl.multiple_of(padt + b * POOL_TB + d * GRID_W, CHUNK), POOL_TB)
                        acc = acc + pad_ref[src, :]
                    mean = acc / ch_ref[pl.ds(pl.multiple_of(b * POOL_TB, CHUNK), POOL_TB), :]
                    rows = pl.ds(pl.multiple_of(ctx_len + b * POOL_TB, CHUNK), POOL_TB)
                    tail(rows, mean, p_ref[rows, :])
                    return carry

                lax.fori_loop(0, nblk, hpass, 0)

    col = lambda g: (0, g)
    return pl.pallas_call(
        body, name=name, grid=(len(POOL_WINDOWS),),
        in_specs=[pl.BlockSpec((T, pg), lambda g: (0, p_col0 + g)),
                  _gspec(POOL_TB, POOL_TB), _gspec(ctx_len, ctx_len), _gspec(POOL_TB, 1), _gspec(seq, 1),
                  _gspec(ctx_len, 1), _gspec(pg, pg), pl.BlockSpec((1, pg), col)],
        out_specs=[pl.BlockSpec((T, pg), col)] * 3,
        out_shape=[jax.ShapeDtypeStruct((T, dp), BF16), jax.ShapeDtypeStruct((T, dp), F32),
                   jax.ShapeDtypeStruct((T, dp), BF16)],
        scratch_shapes=[pltpu.VMEM((seq + 2 * padt, pg), F32)],
        compiler_params=_cparams(("arbitrary",)),
    )(proj, pc["mw"], pc["bc"], pc["cw"], pc["ch"], pc["cc"], wg, scale)


def _pool_bwd(dr, y0, pd, pc, wg, scale, ctx_len, D, name):
    T = dr.shape[0]
    seq = T - ctx_len
    dp = D // 2
    ng = len(POOL_WINDOWS)
    pg = dp // ng
    nblk = seq // POOL_TB
    padt = POOL_PAD_ROWS * GRID_W

    def body(dr_ref, y0_ref, pd_ref, mwt_ref, bct_ref, cw_ref, ch_ref, cc_ref, wg_ref, sc_ref,
             dp_ref, dsc_ref, gwg_ref, pad_ref, dpd_ref):
        g = pl.program_id(0)
        dsc_ref[...] = jnp.zeros_like(dsc_ref)
        gwg_ref[...] = jnp.zeros_like(gwg_ref)

        def head(rows):
            drv = dr_ref[rows, :]
            dsc_ref[...] += jnp.sum(drv * y0_ref[rows, :], axis=0, keepdims=True)
            dy0 = (drv * sc_ref[...]).astype(BF16)
            gwg_ref[...] += _dot_tn(pd_ref[rows, :], dy0)
            return _dot_nt(dy0, wg_ref[...])

        crow = slice(0, ctx_len)
        dpd_c = head(crow)
        dp_ref[crow, :] = (_dot2(bct_ref[...], dpd_c / cc_ref[...]) - dpd_c).astype(BF16)

        pad_ref[0:padt, :] = jnp.zeros((padt, pg), F32)
        pad_ref[padt + seq:, :] = jnp.zeros((padt, pg), F32)

        def first(b, carry):
            rows = pl.ds(pl.multiple_of(ctx_len + b * POOL_TB, CHUNK), POOL_TB)
            lrows = pl.ds(pl.multiple_of(b * POOL_TB, CHUNK), POOL_TB)
            dst = pl.ds(pl.multiple_of(padt + b * POOL_TB, CHUNK), POOL_TB)
            dpd = head(rows)
            dpd_ref[lrows, :] = dpd
            pad_ref[dst, :] = dpd / ch_ref[lrows, :]
            return carry

        lax.fori_loop(0, nblk, first, 0)

        for gi, w in enumerate(POOL_WINDOWS):
            lo, hi = w // 2, w - w // 2 - 1

            @pl.when(g == gi)
            def _():
                def second(b, carry):
                    acc = jnp.zeros((POOL_TB, pg), F32)
                    for d in range(-hi, lo + 1):
                        src = pl.ds(pl.multiple_of(padt + b * POOL_TB + d * GRID_W, CHUNK), POOL_TB)
                        acc = acc + pad_ref[src, :]
                    rows = pl.ds(pl.multiple_of(ctx_len + b * POOL_TB, CHUNK), POOL_TB)
                    lrows = pl.ds(pl.multiple_of(b * POOL_TB, CHUNK), POOL_TB)
                    dx = _dot2(mwt_ref[...], acc / cw_ref[...]) - dpd_ref[lrows, :]
                    dp_ref[rows, :] = dx.astype(BF16)
                    return carry

                lax.fori_loop(0, nblk, second, 0)

    col = lambda g: (0, g)
    return pl.pallas_call(
        body, name=name, grid=(ng,),
        in_specs=[pl.BlockSpec((T, pg), col), pl.BlockSpec((T, pg), col), pl.BlockSpec((T, pg), col),
                  _gspec(POOL_TB, POOL_TB), _gspec(ctx_len, ctx_len), _gspec(POOL_TB, 1), _gspec(seq, 1),
                  _gspec(ctx_len, 1), _gspec(pg, pg), pl.BlockSpec((1, pg), col)],
        out_specs=[pl.BlockSpec((T, pg), col), pl.BlockSpec((1, pg), col), _gspec(pg, pg)],
        out_shape=[jax.ShapeDtypeStruct((T, dp), BF16), jax.ShapeDtypeStruct((1, dp), F32),
                   jax.ShapeDtypeStruct((ng, pg, pg), F32)],
        scratch_shapes=[pltpu.VMEM((seq + 2 * padt, pg), F32), pltpu.VMEM((seq, pg), F32)],
        compiler_params=_cparams(("arbitrary",)),
    )(dr, y0, pd, pc["mwt"], pc["bct"], pc["cw"], pc["ch"], pc["cc"], wg, scale)


def _loss_head(x2, target, rb, name):
    T, D = x2.shape

    def body(y_ref, t_ref, dy_ref, l_ref):
        i = pl.program_id(0)

        @pl.when(i == 0)
        def _():
            dy_ref[...] = jnp.zeros_like(dy_ref)
            l_ref[...] = jnp.zeros_like(l_ref)

        @pl.when(i > 0)
        def _():
            e = y_ref[...] - t_ref[...]
            dy_ref[...] = e * (1.0 / D)
            l_ref[...] += 0.5 * jnp.sum(jnp.mean(e * e, axis=-1, keepdims=True), axis=0, keepdims=True)

    return pl.pallas_call(
        body, name=name, grid=(T // rb,),
        in_specs=[_row(rb, D), pl.BlockSpec((rb, D), lambda i: (jnp.maximum(i - 1, 0), 0))],
        out_specs=[_row(rb, D), pl.BlockSpec((8, 128), lambda i: (0, 0))],
        out_shape=[jax.ShapeDtypeStruct((T, D), F32), jax.ShapeDtypeStruct((8, 128), F32)],
        compiler_params=_cparams(("arbitrary",)),
    )(x2, target)


def _sum_lead(x, name):
    S, R, C = x.shape

    def body(x_ref, o_ref):
        acc = x_ref[0]
        for s in range(1, S):
            acc = acc + x_ref[s]
        o_ref[...] = acc

    return pl.pallas_call(
        body, name=name, out_shape=jax.ShapeDtypeStruct((R, C), F32),
        compiler_params=_cparams(),
    )(x)


def _silu_rows(cond, name):
    def body(c_ref, o_ref):
        o_ref[...] = _silu(c_ref[...]).astype(BF16)

    return pl.pallas_call(body, name=name, out_shape=jax.ShapeDtypeStruct(cond.shape, BF16),
                          compiler_params=_cparams())(cond)


def _silu_grad(cond, ds, name):
    def body(c_ref, d_ref, o_ref):
        _, vjp = jax.vjp(_silu, c_ref[...])
        o_ref[...] = vjp(d_ref[...])[0]

    return pl.pallas_call(body, name=name, out_shape=jax.ShapeDtypeStruct(cond.shape, F32),
                          compiler_params=_cparams())(cond, ds)


def _adamw(gs, w, m, v, name):
    S, R, C = gs.shape
    cpad = -(-C // 128) * 128
    rt = _pick(R, max(16, (1 << 20) // (4 * cpad)), 16)
    c1 = 1.0 / (1.0 - ADAM_B1 ** ADAM_STEP)
    c2 = 1.0 / (1.0 - ADAM_B2 ** ADAM_STEP)

    def body(g_ref, w_ref, m_ref, v_ref, go_ref, d_ref, mo_ref, vo_ref):
        g = g_ref[0].astype(F32)
        for s in range(1, S):
            g = g + g_ref[s].astype(F32)
        mn = ADAM_B1 * m_ref[...] + (1.0 - ADAM_B1) * g
        vn = ADAM_B2 * v_ref[...] + (1.0 - ADAM_B2) * (g * g)
        go_ref[...] = g
        mo_ref[...] = mn
        vo_ref[...] = vn
        d_ref[...] = -ADAM_LR * ((mn * c1) / (jnp.sqrt(vn * c2) + ADAM_EPS) + ADAM_WD * w_ref[...])

    blk = pl.BlockSpec((rt, C), lambda i: (i, 0))
    return pl.pallas_call(
        body, name=name, grid=(R // rt,),
        in_specs=[pl.BlockSpec((S, rt, C), lambda i: (0, i, 0)), blk, blk, blk],
        out_specs=[blk] * 4, out_shape=[jax.ShapeDtypeStruct((R, C), F32)] * 4,
        compiler_params=_cparams(("parallel",)),
    )(gs, w, m, v)


def _adamw_nd(gs, w, m, v, name):
    shp = w.shape
    if len(shp) == 1:
        r, c = 1, shp[0]
    else:
        r, c = int(np.prod(shp[:-1])), shp[-1]
    outs = _adamw(gs.reshape(gs.shape[0], r, c), w.reshape(r, c), m.reshape(r, c), v.reshape(r, c), name)
    return [o.reshape(shp) for o in outs]


def kernel(x, c, ctx, c_ctx, w_ada, b_ada, w_in, w_decay_up, b_decay_up, gla_norm_gain, w_pool_group, pool_scale, w_gla_out, w_pool_out, w_out, ln_mix_gain, ln_mix_bias, w_ffn_in, w_ffn_out, ln_ffn_gain, ln_ffn_bias, loss_target, m_c_ctx, m_w_ada, m_b_ada, m_w_in, m_w_decay_up, m_b_decay_up, m_gla_norm_gain, m_w_pool_group, m_pool_scale, m_w_gla_out, m_w_pool_out, m_w_out, m_ln_mix_gain, m_ln_mix_bias, m_w_ffn_in, m_w_ffn_out, m_ln_ffn_gain, m_ln_ffn_bias, v_c_ctx, v_w_ada, v_b_ada, v_w_in, v_w_decay_up, v_b_decay_up, v_gla_norm_gain, v_w_pool_group, v_pool_scale, v_w_gla_out, v_w_pool_out, v_w_out, v_ln_mix_gain, v_ln_mix_bias, v_w_ffn_in, v_w_ffn_out, v_ln_ffn_gain, v_ln_ffn_bias):
    L, D = w_ada.shape[0], w_ada.shape[1]
    seq, ctx_len = x.shape[1], ctx.shape[1]
    T = seq + ctx_len
    rb = ctx_len
    DK = D // 2
    DP = D // 2
    ng = len(POOL_WINDOWS)
    pg = DP // ng
    dff = w_ffn_out.shape[1] * N_DEV
    alpha = (2.0 * L) ** 0.25
    assert seq % rb == 0 and rb % CHUNK == 0 and seq % POOL_TB == 0 and ctx_len % 8 == 0
    xi, yi, ci = _my_pos()
    me = 4 * xi + 2 * yi + ci
    pc = _pool_consts(ctx_len, seq)

    big = [w_in, w_gla_out, w_pool_out, w_out, w_ffn_in, w_ffn_out, w_pool_group]
    g_in, g_go, g_po, g_out, g_fi, g_fo, g_pg = _all_gather([w.astype(BF16) for w in big], "ag_weights")
    cat_cols = lambda g: jnp.swapaxes(g, 1, 2).reshape(g.shape[0], g.shape[2], -1)
    w_in_f = cat_cols(g_in)
    o_q, o_k, o_v, o_g, o_a = 0, DK, 2 * DK, 2 * DK + D, 2 * DK + 2 * D
    o_p, o_bg = o_a + 2 * GATE_RANK, o_a + 2 * GATE_RANK + DP
    w_main = jnp.concatenate([w_in_f[:, :, o_v:o_g], w_in_f[:, :, o_g:o_a], w_in_f[:, :, o_bg:],
                              w_in_f[:, :, o_q:o_k], w_in_f[:, :, o_k:o_v], w_in_f[:, :, o_p:o_bg]], axis=2)
    w_alr = jnp.pad(w_in_f[:, :, o_a:o_p], ((0, 0), (0, 0), (0, ALR_PAD - 2 * GATE_RANK)))
    w_go = g_go.reshape(L, D, D)
    w_po = cat_cols(g_po)
    w_o = g_out.reshape(L, D, D)
    w_fi = cat_cols(g_fi)
    w_fo = g_fo.reshape(L, dff, D)
    w_pgf = jnp.swapaxes(g_pg, 1, 2).reshape(L, ng, pg, pg)

    dku = w_decay_up.shape[-1]
    small_in = jnp.concatenate([c.reshape(-1), w_decay_up.reshape(-1), b_decay_up.reshape(-1)])
    (small_all,) = _gather_flat([small_in], "ag_small")
    c_all = small_all[:, :D]
    n_wdu = L * 2 * GATE_RANK * dku
    wdu_all = small_all[:, D:D + n_wdu].reshape(N_DEV, L, 2, GATE_RANK, dku)
    wdu_full = jnp.transpose(wdu_all, (1, 2, 3, 0, 4)).reshape(L, 2, GATE_RANK, DK)
    bdu_all = small_all[:, D + n_wdu:].reshape(N_DEV, L, 2, dku)
    bdu_full = jnp.transpose(bdu_all, (1, 2, 0, 3)).reshape(L, 1, 2 * DK)
    wdu_bd = jnp.zeros((L, ALR_PAD, 2 * DK), F32)
    wdu_bd = wdu_bd.at[:, :GATE_RANK, :DK].set(wdu_full[:, 0])
    wdu_bd = wdu_bd.at[:, GATE_RANK:2 * GATE_RANK, DK:].set(wdu_full[:, 1]).astype(BF16)

    ncond = 16
    cond = jnp.concatenate([c_all, c_ctx.reshape(1, D), jnp.zeros((ncond - N_DEV - 1, D), F32)], axis=0)
    s_cond = _silu_rows(cond, "silu_cond")
    wsh = w_ada.shape[-1]
    b_ada_mine = lax.dynamic_slice_in_dim(b_ada, me * wsh, wsh, axis=1)
    mod_part = jnp.stack([_mm(s_cond, w_ada[l], "nn", F32, "mod_mm", bias=b_ada_mine[l:l + 1]) for l in range(L)])
    (mod_all,) = _all_gather([mod_part], "ag_mod")
    mod_all = jnp.swapaxes(mod_all, 1, 2).reshape(L, ncond, N_MOD * D)
    mod_lat = lax.dynamic_slice_in_dim(mod_all, me, 1, axis=1)
    mods = jnp.concatenate([mod_all[:, N_DEV:N_DEV + 1], mod_lat], axis=1).reshape(L, 2, 1, N_MOD * D)
    SH_M, SC_M, GT_M, SH_F, SC_F, GT_F = range(N_MOD)

    xa = jnp.concatenate([ctx[0], x[0]], axis=0)
    vec = lambda a, l: a[l].reshape(1, -1)
    saved = []
    h = _mod_fwd(xa, mods[0], SC_M, SH_M, rb, "mod_fwd")
    for l in range(L):
        proj = _mm(h, w_main[l], "nn", F32, "mm_in")
        alr = _mm(h, w_alr[l], "nn", F32, "mm_alr")
        la = _decay_fwd(alr, wdu_bd[l], bdu_full[l], rb, "decay_fwd")
        o_f, s_f = _gla_fwd(proj, la, False, rb, D, "gla_fwd_f")
        o_b, s_b = _gla_fwd(proj, la, True, rb, D, "gla_fwd_b")
        u = _glaout_fwd(o_f, o_b, proj, vec(gla_norm_gain, l), rb, "glaout_fwd")
        y_gla = _mm(u, w_go[l], "nn", F32, "mm_go")
        pd, y0, r = _pool_fwd(proj, pc, w_pgf[l], vec(pool_scale, l), ctx_len, D, "pool_fwd")
        y_pool = _mm(r, w_po[l], "nn", F32, "mm_po")
        m_ = _merge_fwd(proj, y_gla, y_pool, rb, "merge_fwd")
        mix = _mm(m_, w_o[l], "nn", F32, "mm_out")
        x1, h2 = _unit_fwd(alpha, xa, mix, mods[l], GT_M, vec(ln_mix_gain, l), vec(ln_mix_bias, l),
                           (mods[l], SC_F, SH_F), rb, "unit_mix_fwd")
        ff = _mm(h2, w_fi[l], "nn", F32, "mm_fi")
        s_ = _swiglu_fwd(ff, rb, "swiglu_fwd")
        ffn = _mm(s_, w_fo[l], "nn", F32, "mm_fo")
        nxt = (mods[l + 1], SC_M, SH_M) if l + 1 < L else None
        x2, h_next = _unit_fwd(alpha, x1, ffn, mods[l], GT_F, vec(ln_ffn_gain, l), vec(ln_ffn_bias, l),
                               nxt, rb, "unit_ffn_fwd")
        saved.append(dict(xa=xa, h=h, proj=proj, alr=alr, la=la, o_f=o_f, o_b=o_b, s_f=s_f, s_b=s_b, u=u,
                          y_gla=y_gla, pd=pd, y0=y0, r=r, y_pool=y_pool, m=m_, mix=mix, x1=x1, h2=h2, ff=ff,
                          s=s_, ffn=ffn))
        xa, h = x2, h_next

    dxo, loss_part = _loss_head(xa, loss_target[0], rb, "loss_head")
    loss = lax.psum(loss_part[0, 0], ("x", "y", "c"))

    g_big = {k: [None] * L for k in ("w_in", "go", "po", "out", "fi", "fo", "pg")}
    g_small = {k: [None] * L for k in ("gla_gain", "pool_scale", "mix_g", "mix_b", "ffn_g", "ffn_b", "wdu", "bdu")}
    dmods = [None] * L
    dh = None
    sum2 = lambda a: a[0] + a[1]
    for l in range(L - 1, -1, -1):
        sv = saved[l]
        nxt = (mods[l + 1], SC_M, SH_M) if l + 1 < L else None
        dx1, dffn, d_gtf, d_gf, d_bf, d_scm_n, d_shm_n = _unit_bwd(
            alpha, dxo, dh, sv["x1"], sv["ffn"], mods[l], GT_F, vec(ln_ffn_gain, l), vec(ln_ffn_bias, l),
            nxt, rb, "unit_ffn_bwd")
        if nxt is not None:
            dmods[l + 1]["sc_m"], dmods[l + 1]["sh_m"] = d_scm_n, d_shm_n
        dmods[l] = dict(gt_f=d_gtf)
        g_small["ffn_g"][l], g_small["ffn_b"][l] = sum2(d_gf), sum2(d_bf)
        ds = _mm(dffn, w_fo[l], "nt", F32, "mm_fo_dx")
        g_big["fo"][l] = _mm(sv["s"], dffn, "tn", BF16, "mm_fo_dw")
        dff_ = _swiglu_bwd(ds, sv["ff"], rb, "swiglu_bwd")
        dh2 = _mm(dff_, w_fi[l], "nt", F32, "mm_fi_dx")
        g_big["fi"][l] = _mm(sv["h2"], dff_, "tn", BF16, "mm_fi_dw")
        dxa, dmix, d_gtm, d_gm, d_bm, d_scf, d_shf = _unit_bwd(
            alpha, dx1, dh2, sv["xa"], sv["mix"], mods[l], GT_M, vec(ln_mix_gain, l), vec(ln_mix_bias, l),
            (mods[l], SC_F, SH_F), rb, "unit_mix_bwd")
        dmods[l].update(gt_m=d_gtm, sc_f=d_scf, sh_f=d_shf)
        g_small["mix_g"][l], g_small["mix_b"][l] = sum2(d_gm), sum2(d_bm)
        dm = _mm(dmix, w_o[l], "nt", F32, "mm_out_dx")
        g_big["out"][l] = _mm(sv["m"], dmix, "tn", BF16, "mm_out_dw")
        dbg1, dbg2, dyg, dyp = _merge_bwd(dm, sv["proj"], sv["y_gla"], sv["y_pool"], rb, "merge_bwd")
        dr = _mm(dyp, w_po[l], "nt", F32, "mm_po_dx")
        g_big["po"][l] = _mm(sv["r"], dyp, "tn", BF16, "mm_po_dw")
        dp_, d_ps, g_pgl = _pool_bwd(dr, sv["y0"], sv["pd"], pc, w_pgf[l], vec(pool_scale, l), ctx_len, D, "pool_bwd")
        g_small["pool_scale"][l] = d_ps
        g_big["pg"][l] = g_pgl.astype(BF16)
        du = _mm(dyg, w_go[l], "nt", F32, "mm_go_dx")
        g_big["go"][l] = _mm(sv["u"], dyg, "tn", BF16, "mm_go_dw")
        do, dg, d_gg = _glaout_bwd(du, sv["o_f"], sv["o_b"], sv["proj"], vec(gla_norm_gain, l), rb, "glaout_bwd")
        g_small["gla_gain"][l] = sum2(d_gg)
        dq_f, dk_f, dv_f, dla_f = _gla_bwd(sv["proj"], sv["la"], do, sv["s_f"], False, rb, D, None, "gla_bwd_f")
        dq, dk, dv, dla_b = _gla_bwd(sv["proj"], sv["la"], do, sv["s_b"], True, rb, D, (dq_f, dk_f, dv_f), "gla_bwd_b")
        dalr, g_wdu, g_bdu = _decay_bwd(dla_f, dla_b, sv["alr"], wdu_bd[l], bdu_full[l], rb, "decay_bwd")
        g_small["wdu"][l] = jnp.stack([g_wdu[:GATE_RANK, :DK], g_wdu[GATE_RANK:2 * GATE_RANK, DK:]])
        g_small["bdu"][l] = g_bdu.reshape(2, DK)
        dproj = jnp.concatenate([dv, dg, dbg1, dbg2, dq, dk, dp_], axis=1)
        dh = _mm(dproj, w_main[l], "nt", F32, "mm_in_dx") + _mm(dalr, w_alr[l], "nt", F32, "mm_alr_dx")
        g_main = _mm(sv["h"], dproj, "tn", BF16, "mm_in_dw")
        g_alr = _mm(sv["h"], dalr, "tn", BF16, "mm_alr_dw")
        g_big["w_in"][l] = jnp.concatenate(
            [g_main[:, 4 * D:4 * D + DK], g_main[:, 4 * D + DK:5 * D], g_main[:, :D], g_main[:, D:2 * D],
             g_alr[:, :2 * GATE_RANK], g_main[:, 5 * D:], g_main[:, 2 * D:4 * D]], axis=1)
        dxo = dxa
    grad_xa, d_scm0, d_shm0 = _mod_bwd(dxo, dh, saved[0]["xa"], mods[0], SC_M, SH_M, rb, "mod_bwd")
    dmods[0]["sc_m"], dmods[0]["sh_m"] = d_scm0, d_shm0
    grad_x = grad_xa[ctx_len:].reshape(1, seq, D)

    order = ("sh_m", "sc_m", "gt_m", "sh_f", "sc_f", "gt_f")
    dmod = jnp.stack([jnp.concatenate([dmods[l][k] for k in order], axis=2) for l in range(L)])
    dmod = dmod.reshape(-1)
    sm = lambda k: jnp.stack([a.reshape(-1) for a in g_small[k]]).reshape(-1)
    small_keys = ("gla_gain", "pool_scale", "mix_g", "mix_b", "ffn_g", "ffn_b", "wdu", "bdu")
    small_part = jnp.concatenate([sm(k) for k in small_keys])
    small_g, dmod_g = _gather_flat([small_part, dmod], "ag_small_grads")
    small_sum = _sum_lead(small_g.reshape(N_DEV, -1, 128), "sum_small").reshape(-1)
    off = 0
    rep = {}
    for k, n in zip(small_keys, (L * D, L * DP, L * D, L * D, L * D, L * D, L * 2 * GATE_RANK * DK, L * 2 * DK)):
        rep[k] = small_sum[off:off + n]
        off += n
    g_wdu_mine = lax.dynamic_slice_in_dim(rep["wdu"].reshape(L, 2, GATE_RANK, DK), me * dku, dku, axis=3)
    g_bdu_mine = lax.dynamic_slice_in_dim(rep["bdu"].reshape(L, 2, DK), me * dku, dku, axis=2)

    dmod_all = dmod_g.reshape(N_DEV, L, 2, N_MOD * D)
    dm_ctx = _sum_lead(dmod_all[:, :, 0].reshape(N_DEV, L, N_MOD * D), "sum_dmod_ctx")
    dm_rows = jnp.concatenate([jnp.swapaxes(dmod_all[:, :, 1], 0, 1), dm_ctx[:, None],
                               jnp.zeros((L, ncond - N_DEV - 1, N_MOD * D), F32)], axis=1)
    g_b_ada = _sum_lead(jnp.swapaxes(dm_rows, 0, 1), "sum_b_ada")
    dm_mine = lax.dynamic_slice_in_dim(dm_rows, me * wsh, wsh, axis=2).astype(BF16)
    g_w_ada = jnp.stack([_mm(s_cond, dm_mine[l], "tn", F32, "ada_dw") for l in range(L)])
    ds_part = _sum_lead(jnp.stack([_mm(dm_mine[l], w_ada[l], "nt", F32, "ada_dx") for l in range(L)]), "sum_ds")
    (ds_all,) = _gather_flat([ds_part[N_DEV]], "ag_ds")
    ds_ctx = _sum_lead(ds_all.reshape(N_DEV, 1, D), "sum_ds_ctx")
    g_c_ctx = _silu_grad(c_ctx.reshape(1, D), ds_ctx, "silu_grad").reshape(D)

    def col_chunks(gl):
        g = jnp.stack(gl)
        return jnp.transpose(g.reshape(L, g.shape[1], N_DEV, -1), (2, 0, 1, 3))

    def row_chunks(gl):
        g = jnp.stack(gl)
        return jnp.swapaxes(g.reshape(L, N_DEV, -1, g.shape[2]), 0, 1)

    gpg = jnp.stack(g_big["pg"])
    send = [col_chunks(g_big["w_in"]), row_chunks(g_big["go"]), col_chunks(g_big["po"]), row_chunks(g_big["out"]),
            col_chunks(g_big["fi"]), row_chunks(g_big["fo"]),
            jnp.transpose(gpg.reshape(L, ng, N_DEV, pg // N_DEV, pg), (2, 0, 1, 3, 4))]
    r_in, r_go, r_po, r_out, r_fi, r_fo, r_pg = _all_to_all(send, "a2a_grads")

    one = lambda g: g[None]
    table = [
        ("c_ctx", one(g_c_ctx), c_ctx, m_c_ctx, v_c_ctx),
        ("w_ada", one(g_w_ada), w_ada, m_w_ada, v_w_ada),
        ("b_ada", one(g_b_ada), b_ada, m_b_ada, v_b_ada),
        ("w_in", r_in, w_in, m_w_in, v_w_in),
        ("w_decay_up", one(g_wdu_mine), w_decay_up, m_w_decay_up, v_w_decay_up),
        ("b_decay_up", one(g_bdu_mine), b_decay_up, m_b_decay_up, v_b_decay_up),
        ("gla_norm_gain", one(rep["gla_gain"].reshape(L, D)), gla_norm_gain, m_gla_norm_gain, v_gla_norm_gain),
        ("w_pool_group", r_pg, w_pool_group, m_w_pool_group, v_w_pool_group),
        ("pool_scale", one(rep["pool_scale"].reshape(L, DP)), pool_scale, m_pool_scale, v_pool_scale),
        ("w_gla_out", r_go, w_gla_out, m_w_gla_out, v_w_gla_out),
        ("w_pool_out", r_po, w_pool_out, m_w_pool_out, v_w_pool_out),
        ("w_out", r_out, w_out, m_w_out, v_w_out),
        ("ln_mix_gain", one(rep["mix_g"].reshape(L, D)), ln_mix_gain, m_ln_mix_gain, v_ln_mix_gain),
        ("ln_mix_bias", one(rep["mix_b"].reshape(L, D)), ln_mix_bias, m_ln_mix_bias, v_ln_mix_bias),
        ("w_ffn_in", r_fi, w_ffn_in, m_w_ffn_in, v_w_ffn_in),
        ("w_ffn_out", r_fo, w_ffn_out, m_w_ffn_out, v_w_ffn_out),
        ("ln_ffn_gain", one(rep["ffn_g"].reshape(L, D)), ln_ffn_gain, m_ln_ffn_gain, v_ln_ffn_gain),
        ("ln_ffn_bias", one(rep["ffn_b"].reshape(L, D)), ln_ffn_bias, m_ln_ffn_bias, v_ln_ffn_bias),
    ]
    grads, deltas, new_m, new_v = [], [], [], []
    for nm, gs, w, m, v in table:
        g, d, mn, vn = _adamw_nd(gs, w, m, v, "adamw_" + nm)
        grads.append(g)
        deltas.append(d)
        new_m.append(mn)
        new_v.append(vn)
    return (loss, grad_x, *grads, *deltas, *new_m, *new_v)
```

```python
import functools
import math

import numpy as np
import jax
import jax.numpy as jnp
from jax import lax
from jax.experimental import pallas as pl
from jax.experimental.pallas import tpu as pltpu

F32 = jnp.float32
BF16 = jnp.bfloat16

N_DEV = 8
N_HEADS = 4
GATE_RANK = 16
GATE_NORM = 16.0
CHUNK = 64
GRID_W = 64
POOL_WINDOWS = (2, 4, 8, 16)
N_MOD = 6
LN_EPS = 1e-5
RMS_EPS = 1e-6
ALR_PAD = 128
POOL_TB = 256
POOL_PAD_ROWS = 8
ADAM_LR = 0.001
ADAM_B1 = 0.9
ADAM_B2 = 0.999
ADAM_EPS = 1e-08
ADAM_WD = 0.01
ADAM_STEP = 10
VMEM_LIMIT = 56 * 1024 * 1024
MESH = pl.DeviceIdType.MESH


def _cparams(sem=None):
    return pltpu.CompilerParams(dimension_semantics=sem, vmem_limit_bytes=VMEM_LIMIT)


def _pick(dim, cap, mult):
    best = None
    for d in range(mult, min(dim, cap) + 1, mult):
        if dim % d == 0:
            best = d
    return best if best is not None else dim


def _sig(x):
    return 1.0 / (1.0 + jnp.exp(-x))


def _silu(x):
    return x * _sig(x)


def _dot(a, b):
    return lax.dot_general(a, b, (((1,), (0,)), ((), ())), preferred_element_type=F32)


def _dot_nt(a, b):
    return lax.dot_general(a, b, (((1,), (1,)), ((), ())), preferred_element_type=F32)


def _dot_tn(a, b):
    return lax.dot_general(a, b, (((0,), (0,)), ((), ())), preferred_element_type=F32)


def _split2(x):
    hi = x.astype(BF16)
    lo = (x - hi.astype(F32)).astype(BF16)
    return hi, lo


def _dot2(m_b, x):
    hi, lo = _split2(x)
    return _dot(m_b, hi) + _dot(m_b, lo)


def _dot3(m_b, x):
    h1 = x.astype(BF16)
    r1 = x - h1.astype(F32)
    h2 = r1.astype(BF16)
    h3 = (r1 - h2.astype(F32)).astype(BF16)
    return _dot(m_b, h1) + _dot(m_b, h2) + _dot(m_b, h3)


def _my_pos():
    return lax.axis_index("x"), lax.axis_index("y"), lax.axis_index("c")


def _all_gather(arrs, name):
    n = len(arrs)
    srcs = [a.reshape((a.shape[0], 1) + a.shape[1:]) for a in arrs]
    outs = [jax.ShapeDtypeStruct((a.shape[0], N_DEV) + a.shape[1:], a.dtype) for a in arrs]

    def body(*refs):
        in_refs, out_refs = refs[:n], refs[n:2 * n]
        send_sems, recv_sems, local_sems = refs[2 * n:]
        x, y, c = _my_pos()
        me, sibling = (x, y, c), (x, y, 1 - c)
        chips = [(1 - x, y), (x, 1 - y), (1 - x, 1 - y)]

        def slot(t, pos):
            return out_refs[t].at[:, pl.ds(4 * pos[0] + 2 * pos[1] + pos[2], 1)]

        def copy(t, k, block, to, src=None):
            return pltpu.make_async_remote_copy(
                src_ref=slot(t, block) if src is None else src, dst_ref=slot(t, block),
                send_sem=send_sems.at[t * 7 + k], recv_sem=recv_sems.at[t * 7 + k],
                device_id=to, device_id_type=MESH)

        mine = [pltpu.make_async_copy(in_refs[t], slot(t, me), local_sems.at[t]) for t in range(n)]
        for cp in mine:
            cp.start()
        first = []
        for t in range(n):
            first.append(copy(t, 0, me, sibling, src=in_refs[t]))
            first += [copy(t, 1 + j, me, (*chip, c), src=in_refs[t]) for j, chip in enumerate(chips)]
        for cp in first:
            cp.start()
        passed = []
        for j, chip in enumerate(chips):
            for t in range(n):
                copy(t, 1 + j, (*chip, c), me).wait_recv()
                fwd = copy(t, 4 + j, (*chip, c), sibling)
                fwd.start()
                passed.append(fwd)
        for t in range(n):
            copy(t, 0, sibling, me).wait_recv()
            for j, chip in enumerate(chips):
                copy(t, 4 + j, (*chip, 1 - c), me).wait_recv()
        for cp in first + passed:
            cp.wait_send()
        for cp in mine:
            cp.wait()

    any_spec = pl.BlockSpec(memory_space=pl.ANY)
    res = pl.pallas_call(
        body, name=name, out_shape=outs,
        in_specs=[any_spec] * n, out_specs=[any_spec] * n,
        scratch_shapes=[pltpu.SemaphoreType.DMA((7 * n,)), pltpu.SemaphoreType.DMA((7 * n,)),
                        pltpu.SemaphoreType.DMA((n,))],
        compiler_params=pltpu.CompilerParams(has_side_effects=True),
    )(*srcs)
    return list(res)


def _gather_flat(vecs, name):
    padded = []
    for v in vecs:
        n = v.shape[0]
        padded.append(jnp.pad(v, (0, -n % 128)).reshape(1, -1, 128))
    res = _all_gather(padded, name)
    return [r.reshape(N_DEV, -1)[:, :v.shape[0]] for r, v in zip(res, vecs)]


N_CHIP = 4


def _comm_call(body, name, arrs, outs, n_sems):
    any_spec = pl.BlockSpec(memory_space=pl.ANY)
    n = len(arrs)
    res = pl.pallas_call(
        body, name=name, out_shape=outs,
        in_specs=[any_spec] * n, out_specs=[any_spec] * len(outs),
        scratch_shapes=[pltpu.SemaphoreType.DMA((s,)) for s in n_sems],
        compiler_params=pltpu.CompilerParams(has_side_effects=True),
    )(*arrs)
    return list(res)


def _sibling_exchange(arrs, name):
    n = len(arrs)
    outs = [jax.ShapeDtypeStruct((N_CHIP,) + a.shape[1:], a.dtype) for a in arrs]

    def body(*refs):
        in_refs, out_refs = refs[:n], refs[n:2 * n]
        send_sems, recv_sems = refs[2 * n:]
        x, y, c = _my_pos()
        copies = []
        for t in range(n):
            for k in range(N_CHIP):
                cp = pltpu.make_async_remote_copy(
                    src_ref=in_refs[t].at[pl.ds(2 * k + (1 - c), 1)], dst_ref=out_refs[t].at[pl.ds(k, 1)],
                    send_sem=send_sems.at[t * N_CHIP + k], recv_sem=recv_sems.at[t * N_CHIP + k],
                    device_id=(x, y, 1 - c), device_id_type=MESH)
                cp.start()
                copies.append(cp)
        for cp in copies:
            cp.wait_recv()
        for cp in copies:
            cp.wait_send()

    return _comm_call(body, name, arrs, outs, (N_CHIP * n, N_CHIP * n))


def _chip_exchange(arrs, name):
    n = len(arrs)
    outs = [jax.ShapeDtypeStruct(a.shape, a.dtype) for a in arrs]

    def body(*refs):
        in_refs, out_refs = refs[:n], refs[n:2 * n]
        send_sems, recv_sems, local_sems = refs[2 * n:]
        x, y, c = _my_pos()
        chip = 2 * x + y
        mine, copies = [], []
        for t in range(n):
            cp = pltpu.make_async_copy(in_refs[t].at[pl.ds(chip, 1)], out_refs[t].at[pl.ds(chip, 1)], local_sems.at[t])
            cp.start()
            mine.append(cp)
            for m in range(1, N_CHIP):
                px, py = x ^ (m >> 1), y ^ (m & 1)
                peer = 2 * px + py
                cp = pltpu.make_async_remote_copy(
                    src_ref=in_refs[t].at[pl.ds(peer, 1)], dst_ref=out_refs[t].at[pl.ds(chip, 1)],
                    send_sem=send_sems.at[t * 3 + m - 1], recv_sem=recv_sems.at[t * 3 + m - 1],
                    device_id=(px, py, c), device_id_type=MESH)
                cp.start()
                copies.append((cp, t, m, peer))
        for cp, t, m, peer in copies:
            pltpu.make_async_remote_copy(
                src_ref=in_refs[t].at[pl.ds(peer, 1)], dst_ref=out_refs[t].at[pl.ds(peer, 1)],
                send_sem=send_sems.at[t * 3 + m - 1], recv_sem=recv_sems.at[t * 3 + m - 1],
                device_id=(x, y, c), device_id_type=MESH).wait_recv()
        for cp, t, m, peer in copies:
            cp.wait_send()
        for cp in mine:
            cp.wait()

    return _comm_call(body, name, arrs, outs, (3 * n, 3 * n, n))


def _pair_add(g, r, name):
    _, R, C = g.shape
    cpad = -(-C // 128) * 128
    rt = _pick(R, max(16, (1 << 20) // (2 * cpad)), 16)
    cidx = lax.axis_index("c").astype(jnp.int32).reshape(1)

    def body(c_ref, g_ref, r_ref, o_ref):
        o_ref[...] = (g_ref[...].astype(F32) + r_ref[...].astype(F32)).astype(o_ref.dtype)

    return pl.pallas_call(
        body, name=name, out_shape=jax.ShapeDtypeStruct((N_CHIP, R, C), g.dtype),
        grid_spec=pltpu.PrefetchScalarGridSpec(
            num_scalar_prefetch=1, grid=(N_CHIP, R // rt),
            in_specs=[pl.BlockSpec((None, rt, C), lambda k, i, c_ref: (2 * k + c_ref[0], i, 0)),
                      pl.BlockSpec((None, rt, C), lambda k, i, c_ref: (k, i, 0))],
            out_specs=pl.BlockSpec((None, rt, C), lambda k, i, c_ref: (k, i, 0))),
        compiler_params=_cparams(("parallel", "parallel")),
    )(cidx, g, r)


def _reduce_scatter(chunks, tag):
    flat = [g.reshape(N_DEV, -1, g.shape[-1]) for g in chunks]
    sib = _sibling_exchange(flat, "rs_sibling" + tag)
    pairs = [_pair_add(g, r, "rs_pair_add" + tag) for g, r in zip(flat, sib)]
    return _chip_exchange(pairs, "rs_chips" + tag)


def _mm(a, b, mode, out_dtype=F32, name="mm", bias=None, add=None, b_pre=(), b_shard=False,
        a_half=False, b_half=False, out_shard=False):
    npre = len(b_pre)
    bshape = b.shape[npre:]
    if mode == "nn":
        M, K = a.shape
        if b_shard:
            K2, N = bshape[1], N_DEV * bshape[2]
        else:
            K2, N = bshape
    elif mode == "nt":
        M, K = (a.shape[1], 2 * a.shape[2]) if a_half else a.shape
        if b_shard:
            N, K2 = bshape[1], N_DEV * bshape[2]
        else:
            N, K2 = bshape
    else:
        K, M = a.shape
        K2, N = (b.shape[1], 2 * b.shape[2]) if b_half else bshape
    assert K == K2, (a.shape, b.shape, mode)
    tm = _pick(M, 1100, 16) if mode != "tn" else _pick(M, 1024, 128)
    tn = _pick(N, 1024, 128)
    tk = _pick(K, 2176, 128)
    if b_shard and mode == "nn":
        tn = bshape[2]
    if b_shard and mode == "nt":
        tk = bshape[2]
    if out_shard:
        tn = N // N_DEV
    nk = K // tk
    none_pre = (None,) * npre
    if mode == "nn":
        a_spec = pl.BlockSpec((tm, tk), lambda i, j, k: (i, k))
        if b_shard:
            b_spec = pl.BlockSpec(none_pre + (None, tk, tn), lambda i, j, k: b_pre + (j, k, 0))
        else:
            b_spec = pl.BlockSpec(none_pre + (tk, tn), lambda i, j, k: b_pre + (k, j))
        dot = _dot
    elif mode == "nt":
        if a_half:
            nkh = a.shape[2] // tk
            a_spec = pl.BlockSpec((None, tm, tk), lambda i, j, k: (k // nkh, i, k % nkh))
        else:
            a_spec = pl.BlockSpec((tm, tk), lambda i, j, k: (i, k))
        if b_shard:
            b_spec = pl.BlockSpec(none_pre + (None, tn, tk), lambda i, j, k: b_pre + (k, j, 0))
        else:
            b_spec = pl.BlockSpec(none_pre + (tn, tk), lambda i, j, k: b_pre + (j, k))
        dot = _dot_nt
    else:
        a_spec = pl.BlockSpec((tk, tm), lambda i, j, k: (k, i))
        if b_half:
            nnh = b.shape[2] // tn
            b_spec = pl.BlockSpec((None, tk, tn), lambda i, j, k: (j // nnh, k, j % nnh))
        else:
            b_spec = pl.BlockSpec(none_pre + (tk, tn), lambda i, j, k: b_pre + (k, j))
        dot = _dot_tn
    in_specs = [a_spec, b_spec]
    args = [a, b]
    if bias is not None:
        in_specs.append(pl.BlockSpec((1, tn), lambda i, j, k: (0, j)))
        args.append(bias)
    if add is not None:
        in_specs.append(pl.BlockSpec((tm, tn), lambda i, j, k: (i, j)))
        args.append(add)
    n_in = len(args)
    if out_shard:
        o_spec = pl.BlockSpec((None, tm, tn), lambda i, j, k: (j, i, 0))
        o_shape = jax.ShapeDtypeStruct((N_DEV, M, tn), out_dtype)
    else:
        o_spec = pl.BlockSpec((tm, tn), lambda i, j, k: (i, j))
        o_shape = jax.ShapeDtypeStruct((M, N), out_dtype)

    def body(*refs):
        a_ref, b_ref = refs[0], refs[1]
        bias_ref = refs[2] if bias is not None else None
        add_ref = refs[n_in - 1] if add is not None else None
        o_ref = refs[n_in]
        p = dot(a_ref[...].astype(BF16), b_ref[...].astype(BF16))

        def finish(acc):
            if bias_ref is not None:
                acc = acc + bias_ref[...]
            if add_ref is not None:
                acc = acc + add_ref[...]
            o_ref[...] = acc.astype(o_ref.dtype)

        if nk == 1:
            finish(p)
        else:
            acc_ref = refs[-1]
            k = pl.program_id(2)

            @pl.when(k == 0)
            def _():
                acc_ref[...] = p

            @pl.when(k > 0)
            def _():
                acc_ref[...] += p

            @pl.when(k == nk - 1)
            def _():
                finish(acc_ref[...])

    return pl.pallas_call(
        body, name=name, grid=(M // tm, N // tn, nk),
        in_specs=in_specs, out_specs=o_spec, out_shape=o_shape,
        scratch_shapes=[pltpu.VMEM((tm, tn), F32)] if nk > 1 else [],
        compiler_params=_cparams(("parallel", "parallel", "arbitrary")),
    )(*args)


def _ffn_in_fwd(h2, w_fi, l, name):
    T, D = h2.shape
    n = w_fi.shape[3]
    nh = N_DEV // 2
    dff = nh * n
    tm = _pick(T, 600, 16)

    def body(a_ref, bg_ref, bu_ref, ff_ref, s_ref):
        a = a_ref[...]
        g = _dot(a, bg_ref[...])
        u = _dot(a, bu_ref[...])
        ff_ref[0] = g
        ff_ref[1] = u
        s_ref[...] = _swiglu_f(g, u).astype(BF16)

    return pl.pallas_call(
        body, name=name, grid=(T // tm, nh),
        in_specs=[pl.BlockSpec((tm, D), lambda i, j: (i, 0)),
                  pl.BlockSpec((None, None, D, n), lambda i, j: (l, j, 0, 0)),
                  pl.BlockSpec((None, None, D, n), lambda i, j: (l, nh + j, 0, 0))],
        out_specs=[pl.BlockSpec((2, tm, n), lambda i, j: (0, i, j)), pl.BlockSpec((tm, n), lambda i, j: (i, j))],
        out_shape=[jax.ShapeDtypeStruct((2, T, dff), F32), jax.ShapeDtypeStruct((T, dff), BF16)],
        compiler_params=_cparams(("parallel", "parallel")),
    )(h2, w_fi, w_fi)


def _ffn_out_dx(dffn, w_fo, l, ff, name):
    T, D = dffn.shape
    dff = ff.shape[2]
    tm = _pick(T, 600, 16)
    tw = _pick(dff, 1408, 128)

    def body(a_ref, b_ref, ff_ref, o_ref):
        ds = _dot_nt(a_ref[...], b_ref[...])
        _, vjp = jax.vjp(_swiglu_f, ff_ref[0], ff_ref[1])
        dg, du = vjp(ds)
        o_ref[0] = dg.astype(BF16)
        o_ref[1] = du.astype(BF16)

    return pl.pallas_call(
        body, name=name, grid=(T // tm, dff // tw),
        in_specs=[pl.BlockSpec((tm, D), lambda i, j: (i, 0)),
                  pl.BlockSpec((None, tw, D), lambda i, j: (l, j, 0)),
                  pl.BlockSpec((2, tm, tw), lambda i, j: (0, i, j))],
        out_specs=pl.BlockSpec((2, tm, tw), lambda i, j: (0, i, j)),
        out_shape=jax.ShapeDtypeStruct((2, T, dff), BF16),
        compiler_params=_cparams(("parallel", "parallel")),
    )(dffn, w_fo, ff)


def _row(rb, w, col=0):
    return pl.BlockSpec((rb, w), lambda i: (i, col))


def _modspec(d, sec):
    return pl.BlockSpec((None, 1, d), lambda i: (jnp.minimum(i, 1), 0, sec))


def _vec(w):
    return pl.BlockSpec((1, w), lambda i: (0, 0))


def _acc2(w):
    return pl.BlockSpec((None, 1, w), lambda i: (jnp.minimum(i, 1), 0, 0))


def _accum(ref, val):
    i = pl.program_id(0)

    @pl.when(i <= 1)
    def _():
        ref[...] = val

    @pl.when(i > 1)
    def _():
        ref[...] += val


def _acc_shape(w):
    return jax.ShapeDtypeStruct((2, 1, w), F32)


def _mod_f(x, sc, sh):
    return x * (1.0 + sc) + sh


def _mod_fwd(xa, mod, sec_sc, sec_sh, rb, name):
    T, D = xa.shape

    def body(x_ref, sc_ref, sh_ref, h_ref):
        h_ref[...] = _mod_f(x_ref[...], sc_ref[...], sh_ref[...]).astype(BF16)

    return pl.pallas_call(
        body, name=name, grid=(T // rb,),
        in_specs=[_row(rb, D), _modspec(D, sec_sc), _modspec(D, sec_sh)],
        out_specs=_row(rb, D), out_shape=jax.ShapeDtypeStruct((T, D), BF16),
        compiler_params=_cparams(("parallel",)),
    )(xa, mod, mod)


def _mod_bwd(dxa, dh, xa, mod, sec_sc, sec_sh, rb, name):
    T, D = xa.shape

    def body(dxa_ref, dh_ref, x_ref, sc_ref, sh_ref, dx_ref, dsc_ref, dsh_ref):
        _, vjp = jax.vjp(_mod_f, x_ref[...], sc_ref[...], sh_ref[...])
        dx, dsc, dsh = vjp(dh_ref[...])
        dx_ref[...] = dxa_ref[...] + dx
        _accum(dsc_ref, dsc)
        _accum(dsh_ref, dsh)

    return pl.pallas_call(
        body, name=name, grid=(T // rb,),
        in_specs=[_row(rb, D), _row(rb, D), _row(rb, D), _modspec(D, sec_sc), _modspec(D, sec_sh)],
        out_specs=[_row(rb, D), _acc2(D), _acc2(D)],
        out_shape=[jax.ShapeDtypeStruct((T, D), F32), _acc_shape(D), _acc_shape(D)],
        compiler_params=_cparams(("arbitrary",)),
    )(dxa, dh, xa, mod, mod)


def _ln_f(alpha, x, mix, gt, gain, bias):
    z = alpha * x + gt * mix
    mu = jnp.mean(z, axis=-1, keepdims=True)
    zc = z - mu
    var = jnp.mean(zc * zc, axis=-1, keepdims=True)
    return zc * lax.rsqrt(var + LN_EPS) * gain + bias


def _unit_fwd(alpha, x, mix, mod, sec_gt, gain, bias, next_mod, rb, name):
    T, D = x.shape
    has_mod = next_mod is not None

    def body(*refs):
        if has_mod:
            x_ref, mix_ref, gt_ref, g_ref, b_ref, sc_ref, sh_ref, xo_ref, h_ref = refs
        else:
            x_ref, mix_ref, gt_ref, g_ref, b_ref, xo_ref = refs
        xo = _ln_f(alpha, x_ref[...], mix_ref[...], gt_ref[...], g_ref[...], b_ref[...])
        xo_ref[...] = xo
        if has_mod:
            h_ref[...] = _mod_f(xo, sc_ref[...], sh_ref[...]).astype(BF16)

    in_specs = [_row(rb, D), _row(rb, D), _modspec(D, sec_gt), _vec(D), _vec(D)]
    args = [x, mix, mod, gain, bias]
    out_specs = [_row(rb, D)]
    out_shape = [jax.ShapeDtypeStruct((T, D), F32)]
    if has_mod:
        nm, s_sc, s_sh = next_mod
        in_specs += [_modspec(D, s_sc), _modspec(D, s_sh)]
        args += [nm, nm]
        out_specs.append(_row(rb, D))
        out_shape.append(jax.ShapeDtypeStruct((T, D), BF16))
    res = pl.pallas_call(
        body, name=name, grid=(T // rb,), in_specs=in_specs, out_specs=out_specs, out_shape=out_shape,
        compiler_params=_cparams(("parallel",)),
    )(*args)
    return (res[0], res[1]) if has_mod else (res[0], None)


def _unit_bwd(alpha, dxo, dh, x, mix, mod, sec_gt, gain, bias, next_mod, rb, name):
    T, D = x.shape
    has_mod = next_mod is not None

    def body(*refs):
        if has_mod:
            (dxo_ref, dh_ref, x_ref, mix_ref, gt_ref, g_ref, b_ref, sc_ref, sh_ref,
             dx_ref, dmix_ref, dgt_ref, dg_ref, db_ref, dsc_ref, dsh_ref) = refs
        else:
            (dxo_ref, x_ref, mix_ref, gt_ref, g_ref, b_ref,
             dx_ref, dmix_ref, dgt_ref, dg_ref, db_ref) = refs
        xo, vjp = jax.vjp(functools.partial(_ln_f, alpha), x_ref[...], mix_ref[...], gt_ref[...],
                          g_ref[...], b_ref[...])
        dxo_t = dxo_ref[...]
        if has_mod:
            _, vjp_m = jax.vjp(_mod_f, xo, sc_ref[...], sh_ref[...])
            dxo_m, dsc, dsh = vjp_m(dh_ref[...])
            dxo_t = dxo_t + dxo_m
            _accum(dsc_ref, dsc)
            _accum(dsh_ref, dsh)
        dx, dmix, dgt, dg, db = vjp(dxo_t)
        dx_ref[...] = dx
        dmix_ref[...] = dmix.astype(BF16)
        _accum(dgt_ref, dgt)
        _accum(dg_ref, dg)
        _accum(db_ref, db)

    in_specs = [_row(rb, D)]
    args = [dxo]
    if has_mod:
        in_specs.append(_row(rb, D))
        args.append(dh)
    in_specs += [_row(rb, D), _row(rb, D), _modspec(D, sec_gt), _vec(D), _vec(D)]
    args += [x, mix, mod, gain, bias]
    out_specs = [_row(rb, D), _row(rb, D), _acc2(D), _acc2(D), _acc2(D)]
    out_shape = [jax.ShapeDtypeStruct((T, D), F32), jax.ShapeDtypeStruct((T, D), BF16),
                 _acc_shape(D), _acc_shape(D), _acc_shape(D)]
    if has_mod:
        nm, s_sc, s_sh = next_mod
        in_specs += [_modspec(D, s_sc), _modspec(D, s_sh)]
        args += [nm, nm]
        out_specs += [_acc2(D), _acc2(D)]
        out_shape += [_acc_shape(D), _acc_shape(D)]
    res = pl.pallas_call(
        body, name=name, grid=(T // rb,), in_specs=in_specs, out_specs=out_specs, out_shape=out_shape,
        compiler_params=_cparams(("arbitrary",)),
    )(*args)
    if has_mod:
        return res
    return list(res) + [None, None]


def _log_sigmoid(z):
    return jnp.minimum(z, 0.0) - jnp.log(1.0 + jnp.exp(-jnp.abs(z)))


def _decay_fwd(alr, wdu, bdu, rb, name):
    T = alr.shape[0]
    W = wdu.shape[1]

    def body(a_ref, w_ref, b_ref, la_ref):
        z = _dot(a_ref[...].astype(BF16), w_ref[...]) + b_ref[...]
        la_ref[...] = _log_sigmoid(z) * (1.0 / GATE_NORM)

    return pl.pallas_call(
        body, name=name, grid=(T // rb,),
        in_specs=[_row(rb, ALR_PAD), pl.BlockSpec((ALR_PAD, W), lambda i: (0, 0)), _vec(W)],
        out_specs=_row(rb, W), out_shape=jax.ShapeDtypeStruct((T, W), F32),
        compiler_params=_cparams(("parallel",)),
    )(alr, wdu, bdu)


def _decay_bwd(dla_f, dla_b, alr, wdu, bdu, rb, name):
    T = alr.shape[0]
    W = wdu.shape[1]
    DK = W // 2

    def body(df_ref, db_ref, a_ref, w_ref, b_ref, dalr_ref, gw_ref, gb_ref):
        i = pl.program_id(0)
        ab = a_ref[...].astype(BF16)
        z = _dot(ab, w_ref[...]) + b_ref[...]
        dla = jnp.concatenate([df_ref[...], db_ref[...]], axis=1)
        dz = dla * _sig(-z) * (1.0 / GATE_NORM)
        dzb = dz.astype(BF16)
        dalr_ref[...] = _dot_nt(dzb, w_ref[...]).astype(BF16)
        gw = _dot_tn(ab, dzb)
        gb = jnp.sum(dz, axis=0, keepdims=True)

        @pl.when(i == 0)
        def _():
            gw_ref[...] = gw
            gb_ref[...] = gb

        @pl.when(i > 0)
        def _():
            gw_ref[...] += gw
            gb_ref[...] += gb

    return pl.pallas_call(
        body, name=name, grid=(T // rb,),
        in_specs=[_row(rb, DK), _row(rb, DK), _row(rb, ALR_PAD), pl.BlockSpec((ALR_PAD, W), lambda i: (0, 0)), _vec(W)],
        out_specs=[_row(rb, ALR_PAD), pl.BlockSpec((ALR_PAD, W), lambda i: (0, 0)), _vec(W)],
        out_shape=[jax.ShapeDtypeStruct((T, ALR_PAD), BF16), jax.ShapeDtypeStruct((ALR_PAD, W), F32),
                   jax.ShapeDtypeStruct((1, W), F32)],
        compiler_params=_cparams(("arbitrary",)),
    )(dla_f, dla_b, alr, wdu, bdu)


def _tri(rev):
    m = np.tril(np.ones((CHUNK, CHUNK), np.float32))
    return jnp.asarray(m.T if rev else m, BF16)


def _gla_chunk_common(q_ref, k_ref, v_ref, la_ref, tri_ref, rows, rev, scale_q):
    mid = CHUNK // 2 if rev else CHUNK // 2 - 1
    last_i = 0 if rev else CHUNK - 1
    q = q_ref[rows, :] * scale_q
    k = k_ref[rows, :]
    v = v_ref[rows, :]
    cum = _dot3(tri_ref[...], la_ref[rows, :])
    ref = cum[mid:mid + 1, :]
    last = cum[last_i:last_i + 1, :]
    e_q = jnp.exp(cum - ref)
    e_k = jnp.exp(ref - cum)
    e_c = jnp.exp(cum)
    e_s = jnp.exp(last - cum)
    e_l = jnp.exp(last)
    ri = lax.broadcasted_iota(jnp.int32, (CHUNK, CHUNK), 0)
    ci = lax.broadcasted_iota(jnp.int32, (CHUNK, CHUNK), 1)
    mask = (ci >= ri) if rev else (ci <= ri)
    return q, k, v, e_q, e_k, e_c, e_s, e_l, mask, last_i


def _gla_specs(rb, hk, hv, D, rbmap, la_col0):
    q_col0 = 4 * D // hk
    k_col0 = q_col0 + N_HEADS
    return [
        pl.BlockSpec((rb, hk), lambda h, i: (rbmap(i), q_col0 + h)),
        pl.BlockSpec((rb, hk), lambda h, i: (rbmap(i), k_col0 + h)),
        pl.BlockSpec((rb, hv), lambda h, i: (rbmap(i), h)),
        pl.BlockSpec((rb, hk), lambda h, i: (rbmap(i), la_col0 + h)),
        pl.BlockSpec((CHUNK, CHUNK), lambda h, i: (0, 0)),
    ]


def _gla_fwd(proj, la, rev, rb, D, name):
    T = proj.shape[0]
    nb = T // rb
    ncb = rb // CHUNK
    hk, hv = D // 2 // N_HEADS, D // N_HEADS
    scale_q = float(hk) ** -0.5
    rbmap = (lambda i: jnp.where(i == 0, 0, nb - i)) if rev else (lambda i: i)

    def body(q_ref, k_ref, v_ref, la_ref, tri_ref, o_ref, s_ref, st_ref):
        @pl.when(pl.program_id(1) == 0)
        def _():
            st_ref[...] = jnp.zeros_like(st_ref)

        order = range(ncb - 1, -1, -1) if rev else range(ncb)
        for cc in order:
            rows = slice(cc * CHUNK, (cc + 1) * CHUNK)
            q, k, v, e_q, e_k, e_c, e_s, e_l, mask, _ = _gla_chunk_common(
                q_ref, k_ref, v_ref, la_ref, tri_ref, rows, rev, scale_q)
            vb = v.astype(BF16)
            a = jnp.where(mask, _dot_nt((q * e_q).astype(BF16), (k * e_k).astype(BF16)), 0.0)
            st = st_ref[...]
            s_ref[cc] = st
            o = _dot(a.astype(BF16), vb) + _dot_nt((q * e_c).astype(BF16), st.astype(BF16))
            o_ref[rows, :] = o
            st_ref[...] = st * e_l + _dot_tn(vb, (k * e_s).astype(BF16))

    la_col0 = N_HEADS if rev else 0
    return pl.pallas_call(
        body, name=name, grid=(N_HEADS, nb),
        in_specs=_gla_specs(rb, hk, hv, D, rbmap, la_col0),
        out_specs=[pl.BlockSpec((rb, hv), lambda h, i: (rbmap(i), h)),
                   pl.BlockSpec((None, ncb, hv, hk), lambda h, i: (h, rbmap(i), 0, 0))],
        out_shape=[jax.ShapeDtypeStruct((T, D), F32),
                   jax.ShapeDtypeStruct((N_HEADS, T // CHUNK, hv, hk), F32)],
        scratch_shapes=[pltpu.VMEM((hv, hk), F32)],
        compiler_params=_cparams(("parallel", "arbitrary")),
    )(proj, proj, proj, la, _tri(rev))


def _gla_bwd(proj, la, do, states, rev, rb, D, prev, name):
    T = proj.shape[0]
    nb = T // rb
    ncb = rb // CHUNK
    hk, hv = D // 2 // N_HEADS, D // N_HEADS
    DK = D // 2
    scale_q = float(hk) ** -0.5
    if rev:
        rbmap = lambda i: jnp.where(i == nb - 1, 0, i + 1)
    else:
        rbmap = lambda i: nb - 1 - i
    has_prev = prev is not None
    out_dt = BF16 if has_prev else F32

    def body(*refs):
        q_ref, k_ref, v_ref, la_ref, tri_ref, trit_ref, do_ref, s_ref = refs[:8]
        n_in = 11 if has_prev else 8
        pq_ref, pk_ref, pv_ref = refs[8:11] if has_prev else (None, None, None)
        dq_ref, dk_ref, dv_ref, dla_ref, ds_ref = refs[n_in:]

        @pl.when(pl.program_id(1) == 0)
        def _():
            ds_ref[...] = jnp.zeros_like(ds_ref)

        order = range(ncb) if rev else range(ncb - 1, -1, -1)
        for cc in order:
            rows = slice(cc * CHUNK, (cc + 1) * CHUNK)
            q, k, v, e_q, e_k, e_c, e_s, e_l, mask, last_i = _gla_chunk_common(
                q_ref, k_ref, v_ref, la_ref, tri_ref, rows, rev, scale_q)
            vb = v.astype(BF16)
            qi = (q * e_q).astype(BF16)
            ki = (k * e_k).astype(BF16)
            qc = (q * e_c).astype(BF16)
            ks = (k * e_s).astype(BF16)
            a = jnp.where(mask, _dot_nt(qi, ki), 0.0).astype(BF16)
            st0 = s_ref[cc]
            st0b = st0.astype(BF16)
            dst1 = ds_ref[...]
            dst1b = dst1.astype(BF16)
            dob = do_ref[rows, :].astype(BF16)
            da = jnp.where(mask, _dot_nt(dob, vb), 0.0).astype(BF16)
            dv = _dot_tn(a, dob) + _dot_nt(ks, dst1b)
            dq_inter = _dot(dob, st0b) * e_c
            dk_inter = _dot(vb, dst1b) * e_s
            dq_s = _dot(da, ki) * e_q + dq_inter
            dk = _dot_tn(da, qi) * e_k + dk_inter
            extra = (jnp.sum(k * dk_inter, axis=0, keepdims=True)
                     + e_l * jnp.sum(dst1 * st0, axis=0, keepdims=True))
            rowi = lax.broadcasted_iota(jnp.int32, (CHUNK, hk), 0)
            dcum = q * dq_s - k * dk + jnp.where(rowi == last_i, extra, 0.0)
            dla_ref[rows, :] = _dot3(trit_ref[...], dcum)
            dq = dq_s * scale_q
            if has_prev:
                dq = dq + pq_ref[rows, :]
                dk = dk + pk_ref[rows, :]
                dv = dv + pv_ref[rows, :]
            dq_ref[rows, :] = dq.astype(out_dt)
            dk_ref[rows, :] = dk.astype(out_dt)
            dv_ref[rows, :] = dv.astype(out_dt)
            ds_ref[...] = dst1 * e_l + _dot_tn(dob, qc)

    la_col0 = N_HEADS if rev else 0
    in_specs = _gla_specs(rb, hk, hv, D, rbmap, la_col0)
    in_specs += [pl.BlockSpec((CHUNK, CHUNK), lambda h, i: (0, 0)),
                 pl.BlockSpec((rb, hv), lambda h, i: (rbmap(i), h)),
                 pl.BlockSpec((None, ncb, hv, hk), lambda h, i: (h, rbmap(i), 0, 0))]
    args = [proj, proj, proj, la, _tri(rev), _tri(not rev), do, states]
    hk_spec = pl.BlockSpec((rb, hk), lambda h, i: (rbmap(i), h))
    hv_spec = pl.BlockSpec((rb, hv), lambda h, i: (rbmap(i), h))
    if has_prev:
        in_specs += [hk_spec, hk_spec, hv_spec]
        args += list(prev)
    return pl.pallas_call(
        body, name=name, grid=(N_HEADS, nb), in_specs=in_specs,
        out_specs=[hk_spec, hk_spec, hv_spec, hk_spec],
        out_shape=[jax.ShapeDtypeStruct((T, DK), out_dt), jax.ShapeDtypeStruct((T, DK), out_dt),
                   jax.ShapeDtypeStruct((T, D), out_dt), jax.ShapeDtypeStruct((T, DK), F32)],
        scratch_shapes=[pltpu.VMEM((hv, hk), F32)],
        compiler_params=_cparams(("parallel", "arbitrary")),
    )(*args)


def _glaout_f(of, ob, g, gain):
    o = of + ob
    n = o * lax.rsqrt(jnp.mean(o * o, axis=-1, keepdims=True) + RMS_EPS)
    return n * gain * _silu(g)


def _glaout_fwd(o_f, o_b, proj, gain, rb, name):
    T, D = o_f.shape
    hv = D // N_HEADS

    def body(of_ref, ob_ref, g_ref, gn_ref, u_ref):
        for h in range(N_HEADS):
            cs = slice(h * hv, (h + 1) * hv)
            u_ref[:, cs] = _glaout_f(of_ref[:, cs], ob_ref[:, cs], g_ref[:, cs], gn_ref[:, cs]).astype(BF16)

    return pl.pallas_call(
        body, name=name, grid=(T // rb,),
        in_specs=[_row(rb, D), _row(rb, D), _row(rb, D, 1), _vec(D)],
        out_specs=_row(rb, D), out_shape=jax.ShapeDtypeStruct((T, D), BF16),
        compiler_params=_cparams(("parallel",)),
    )(o_f, o_b, proj, gain)


def _glaout_bwd(du, o_f, o_b, proj, gain, rb, name):
    T, D = o_f.shape
    hv = D // N_HEADS

    def body(du_ref, of_ref, ob_ref, g_ref, gn_ref, do_ref, dg_ref, dgn_ref, tmp_ref):
        for h in range(N_HEADS):
            cs = slice(h * hv, (h + 1) * hv)
            _, vjp = jax.vjp(_glaout_f, of_ref[:, cs], ob_ref[:, cs], g_ref[:, cs], gn_ref[:, cs])
            d_of, _, dg, dgn = vjp(du_ref[:, cs])
            do_ref[:, cs] = d_of
            dg_ref[:, cs] = dg.astype(BF16)
            tmp_ref[:, cs] = dgn
        _accum(dgn_ref, tmp_ref[...])

    return pl.pallas_call(
        body, name=name, grid=(T // rb,),
        in_specs=[_row(rb, D), _row(rb, D), _row(rb, D), _row(rb, D, 1), _vec(D)],
        out_specs=[_row(rb, D), _row(rb, D), _acc2(D)],
        out_shape=[jax.ShapeDtypeStruct((T, D), F32), jax.ShapeDtypeStruct((T, D), BF16), _acc_shape(D)],
        scratch_shapes=[pltpu.VMEM((1, D), F32)],
        compiler_params=_cparams(("arbitrary",)),
    )(du, o_f, o_b, proj, gain)


def _merge_f(bg1, bg2, yg, yp):
    return _sig(bg1) * yg + _sig(bg2) * yp


def _merge_fwd(proj, y_gla, y_pool, rb, name):
    T, D = y_gla.shape

    def body(b1_ref, b2_ref, yg_ref, yp_ref, m_ref):
        m_ref[...] = _merge_f(b1_ref[...], b2_ref[...], yg_ref[...], yp_ref[...]).astype(BF16)

    return pl.pallas_call(
        body, name=name, grid=(T // rb,),
        in_specs=[_row(rb, D, 2), _row(rb, D, 3), _row(rb, D), _row(rb, D)],
        out_specs=_row(rb, D), out_shape=jax.ShapeDtypeStruct((T, D), BF16),
        compiler_params=_cparams(("parallel",)),
    )(proj, proj, y_gla, y_pool)


def _merge_bwd(dm, proj, y_gla, y_pool, rb, name):
    T, D = y_gla.shape

    def body(dm_ref, b1_ref, b2_ref, yg_ref, yp_ref, d1_ref, d2_ref, dyg_ref, dyp_ref):
        _, vjp = jax.vjp(_merge_f, b1_ref[...], b2_ref[...], yg_ref[...], yp_ref[...])
        d1, d2, dyg, dyp = vjp(dm_ref[...])
        d1_ref[...] = d1.astype(BF16)
        d2_ref[...] = d2.astype(BF16)
        dyg_ref[...] = dyg.astype(BF16)
        dyp_ref[...] = dyp.astype(BF16)

    return pl.pallas_call(
        body, name=name, grid=(T // rb,),
        in_specs=[_row(rb, D), _row(rb, D, 2), _row(rb, D, 3), _row(rb, D), _row(rb, D)],
        out_specs=[_row(rb, D)] * 4, out_shape=[jax.ShapeDtypeStruct((T, D), BF16)] * 4,
        compiler_params=_cparams(("parallel",)),
    )(dm, proj, proj, y_gla, y_pool)


def _swiglu_f(gate, up):
    return _silu(gate) * up


def _pool_consts(ctx_len, seq):
    rows = seq // GRID_W
    reps = POOL_TB // GRID_W
    mw, bc, cw, ch, cc = [], [], [], [], []
    for w in POOL_WINDOWS:
        lo, hi = w // 2, w - w // 2 - 1

        def band(n):
            i = np.arange(n)[:, None]
            j = np.arange(n)[None, :]
            return ((j - i >= -lo) & (j - i <= hi)).astype(np.float32)

        def count(n):
            i = np.arange(n)
            return (np.minimum(i + hi + 1, n) - np.maximum(i - lo, 0)).astype(np.float32)

        mw.append(np.kron(np.eye(reps, dtype=np.float32), band(GRID_W)))
        bc.append(band(ctx_len))
        cw.append(np.tile(count(GRID_W), reps)[:, None])
        ch.append(np.repeat(count(rows), GRID_W)[:, None])
        cc.append(count(ctx_len)[:, None])
    mw, bc = np.stack(mw), np.stack(bc)
    return dict(
        mw=jnp.asarray(mw, BF16), mwt=jnp.asarray(mw.transpose(0, 2, 1), BF16),
        bc=jnp.asarray(bc, BF16), bct=jnp.asarray(bc.transpose(0, 2, 1), BF16),
        cw=jnp.asarray(np.stack(cw)), ch=jnp.asarray(np.stack(ch)), cc=jnp.asarray(np.stack(cc)))


def _gspec(*shape):
    nd = len(shape)
    return pl.BlockSpec((None,) + tuple(shape), lambda g: (g,) + (0,) * nd)


def _pool_fwd(proj, pc, wg, scale, ctx_len, D, name):
    T = proj.shape[0]
    seq = T - ctx_len
    dp = D // 2
    pg = dp // len(POOL_WINDOWS)
    nblk = seq // POOL_TB
    padt = POOL_PAD_ROWS * GRID_W
    p_col0 = 5 * D // pg

    def body(p_ref, mw_ref, bc_ref, cw_ref, ch_ref, cc_ref, wg_ref, sc_ref, pd_ref, y0_ref, r_ref, pad_ref):
        g = pl.program_id(0)

        def tail(rows, mean, x):
            pdb = (mean - x).astype(BF16)
            y0 = _dot(pdb, wg_ref[...])
            pd_ref[rows, :] = pdb
            y0_ref[rows, :] = y0
            r_ref[rows, :] = (y0 * sc_ref[...]).astype(BF16)

        xc = p_ref[0:ctx_len, :]
        tail(slice(0, ctx_len), _dot2(bc_ref[...], xc) / cc_ref[...], xc)

        pad_ref[0:padt, :] = jnp.zeros((padt, pg), F32)
        pad_ref[padt + seq:, :] = jnp.zeros((padt, pg), F32)

        def wpass(b, carry):
            rows = pl.ds(pl.multiple_of(ctx_len + b * POOL_TB, CHUNK), POOL_TB)
            dst = pl.ds(pl.multiple_of(padt + b * POOL_TB, CHUNK), POOL_TB)
            pad_ref[dst, :] = _dot2(mw_ref[...], p_ref[rows, :]) / cw_ref[...]
            return carry

        lax.fori_loop(0, nblk, wpass, 0)

        for gi, w in enumerate(POOL_WINDOWS):
            lo, hi = w // 2, w - w // 2 - 1

            @pl.when(g == gi)
            def _():
                def hpass(b, carry):
                    acc = jnp.zeros((POOL_TB, pg), F32)
                    for d in range(-lo, hi + 1):
                        src = pl.ds(pl.multiple_of(padt + b * POOL_TB + d * GRID_W, CHUNK), POOL_TB)
                        acc = acc + pad_ref[src, :]
                    mean = acc / ch_ref[pl.ds(pl.multiple_of(b * POOL_TB, CHUNK), POOL_TB), :]
                    rows = pl.ds(pl.multiple_of(ctx_len + b * POOL_TB, CHUNK), POOL_TB)
                    tail(rows, mean, p_ref[rows, :])
                    return carry

                lax.fori_loop(0, nblk, hpass, 0)

    col = lambda g: (0, g)
    return pl.pallas_call(
        body, name=name, grid=(len(POOL_WINDOWS),),
        in_specs=[pl.BlockSpec((T, pg), lambda g: (0, p_col0 + g)),
                  _gspec(POOL_TB, POOL_TB), _gspec(ctx_len, ctx_len), _gspec(POOL_TB, 1), _gspec(seq, 1),
                  _gspec(ctx_len, 1), _gspec(pg, pg), pl.BlockSpec((1, pg), col)],
        out_specs=[pl.BlockSpec((T, pg), col)] * 3,
        out_shape=[jax.ShapeDtypeStruct((T, dp), BF16), jax.ShapeDtypeStruct((T, dp), F32),
                   jax.ShapeDtypeStruct((T, dp), BF16)],
        scratch_shapes=[pltpu.VMEM((seq + 2 * padt, pg), F32)],
        compiler_params=_cparams(("arbitrary",)),
    )(proj, pc["mw"], pc["bc"], pc["cw"], pc["ch"], pc["cc"], wg, scale)


def _pool_bwd(dr, y0, pd, pc, wg, scale, ctx_len, D, name):
    T = dr.shape[0]
    seq = T - ctx_len
    dp = D // 2
    ng = len(POOL_WINDOWS)
    pg = dp // ng
    nblk = seq // POOL_TB
    padt = POOL_PAD_ROWS * GRID_W

    def body(dr_ref, y0_ref, pd_ref, mwt_ref, bct_ref, cw_ref, ch_ref, cc_ref, wg_ref, sc_ref,
             dp_ref, dsc_ref, gwg_ref, pad_ref, dpd_ref):
        g = pl.program_id(0)
        dsc_ref[...] = jnp.zeros_like(dsc_ref)
        gwg_ref[...] = jnp.zeros_like(gwg_ref)

        def head(rows):
            drv = dr_ref[rows, :]
            dsc_ref[...] += jnp.sum(drv * y0_ref[rows, :], axis=0, keepdims=True)
            dy0 = (drv * sc_ref[...]).astype(BF16)
            gwg_ref[...] += _dot_tn(pd_ref[rows, :], dy0)
            return _dot_nt(dy0, wg_ref[...])

        crow = slice(0, ctx_len)
        dpd_c = head(crow)
        dp_ref[crow, :] = (_dot2(bct_ref[...], dpd_c / cc_ref[...]) - dpd_c).astype(BF16)

        pad_ref[0:padt, :] = jnp.zeros((padt, pg), F32)
        pad_ref[padt + seq:, :] = jnp.zeros((padt, pg), F32)

        def first(b, carry):
            rows = pl.ds(pl.multiple_of(ctx_len + b * POOL_TB, CHUNK), POOL_TB)
            lrows = pl.ds(pl.multiple_of(b * POOL_TB, CHUNK), POOL_TB)
            dst = pl.ds(pl.multiple_of(padt + b * POOL_TB, CHUNK), POOL_TB)
            dpd = head(rows)
            dpd_ref[lrows, :] = dpd
            pad_ref[dst, :] = dpd / ch_ref[lrows, :]
            return carry

        lax.fori_loop(0, nblk, first, 0)

        for gi, w in enumerate(POOL_WINDOWS):
            lo, hi = w // 2, w - w // 2 - 1

            @pl.when(g == gi)
            def _():
                def second(b, carry):
                    acc = jnp.zeros((POOL_TB, pg), F32)
                    for d in range(-hi, lo + 1):
                        src = pl.ds(pl.multiple_of(padt + b * POOL_TB + d * GRID_W, CHUNK), POOL_TB)
                        acc = acc + pad_ref[src, :]
                    rows = pl.ds(pl.multiple_of(ctx_len + b * POOL_TB, CHUNK), POOL_TB)
                    lrows = pl.ds(pl.multiple_of(b * POOL_TB, CHUNK), POOL_TB)
                    dx = _dot2(mwt_ref[...], acc / cw_ref[...]) - dpd_ref[lrows, :]
                    dp_ref[rows, :] = dx.astype(BF16)
                    return carry

                lax.fori_loop(0, nblk, second, 0)

    col = lambda g: (0, g)
    return pl.pallas_call(
        body, name=name, grid=(ng,),
        in_specs=[pl.BlockSpec((T, pg), col), pl.BlockSpec((T, pg), col), pl.BlockSpec((T, pg), col),
                  _gspec(POOL_TB, POOL_TB), _gspec(ctx_len, ctx_len), _gspec(POOL_TB, 1), _gspec(seq, 1),
                  _gspec(ctx_len, 1), _gspec(pg, pg), pl.BlockSpec((1, pg), col)],
        out_specs=[pl.BlockSpec((T, pg), col), pl.BlockSpec((1, pg), col), _gspec(pg, pg)],
        out_shape=[jax.ShapeDtypeStruct((T, dp), BF16), jax.ShapeDtypeStruct((1, dp), F32),
                   jax.ShapeDtypeStruct((ng, pg, pg), F32)],
        scratch_shapes=[pltpu.VMEM((seq + 2 * padt, pg), F32), pltpu.VMEM((seq, pg), F32)],
        compiler_params=_cparams(("arbitrary",)),
    )(dr, y0, pd, pc["mwt"], pc["bct"], pc["cw"], pc["ch"], pc["cc"], wg, scale)


def _loss_head(x2, target, rb, name):
    T, D = x2.shape

    def body(y_ref, t_ref, dy_ref, l_ref):
        i = pl.program_id(0)

        @pl.when(i == 0)
        def _():
            dy_ref[...] = jnp.zeros_like(dy_ref)
            l_ref[...] = jnp.zeros_like(l_ref)

        @pl.when(i > 0)
        def _():
            e = y_ref[...] - t_ref[...]
            dy_ref[...] = e * (1.0 / D)
            l_ref[...] += 0.5 * jnp.sum(jnp.mean(e * e, axis=-1, keepdims=True), axis=0, keepdims=True)

    return pl.pallas_call(
        body, name=name, grid=(T // rb,),
        in_specs=[_row(rb, D), pl.BlockSpec((rb, D), lambda i: (jnp.maximum(i - 1, 0), 0))],
        out_specs=[_row(rb, D), pl.BlockSpec((8, 128), lambda i: (0, 0))],
        out_shape=[jax.ShapeDtypeStruct((T, D), F32), jax.ShapeDtypeStruct((8, 128), F32)],
        compiler_params=_cparams(("arbitrary",)),
    )(x2, target)


def _sum_lead(x, name):
    S, R, C = x.shape

    def body(x_ref, o_ref):
        acc = x_ref[0]
        for s in range(1, S):
            acc = acc + x_ref[s]
        o_ref[...] = acc

    return pl.pallas_call(
        body, name=name, out_shape=jax.ShapeDtypeStruct((R, C), F32),
        compiler_params=_cparams(),
    )(x)


def _silu_rows(cond, name):
    def body(c_ref, o_ref):
        o_ref[...] = _silu(c_ref[...]).astype(BF16)

    return pl.pallas_call(body, name=name, out_shape=jax.ShapeDtypeStruct(cond.shape, BF16),
                          compiler_params=_cparams())(cond)


def _silu_grad(cond, ds, name):
    def body(c_ref, d_ref, o_ref):
        _, vjp = jax.vjp(_silu, c_ref[...])
        o_ref[...] = vjp(d_ref[...])[0]

    return pl.pallas_call(body, name=name, out_shape=jax.ShapeDtypeStruct(cond.shape, F32),
                          compiler_params=_cparams())(cond, ds)


def _adamw_math(g, w_ref, m_ref, v_ref, go_ref, d_ref, mo_ref, vo_ref):
    c1 = 1.0 / (1.0 - ADAM_B1 ** ADAM_STEP)
    c2 = 1.0 / (1.0 - ADAM_B2 ** ADAM_STEP)
    mn = ADAM_B1 * m_ref[...] + (1.0 - ADAM_B1) * g
    vn = ADAM_B2 * v_ref[...] + (1.0 - ADAM_B2) * (g * g)
    go_ref[...] = g
    mo_ref[...] = mn
    vo_ref[...] = vn
    d_ref[...] = -ADAM_LR * ((mn * c1) / (jnp.sqrt(vn * c2) + ADAM_EPS) + ADAM_WD * w_ref[...])


def _adamw(gs, w, m, v, name):
    S, R, C = gs.shape
    cpad = -(-C // 128) * 128
    rt = _pick(R, max(16, (1 << 20) // (4 * cpad)), 16)

    def body(g_ref, w_ref, m_ref, v_ref, go_ref, d_ref, mo_ref, vo_ref):
        g = g_ref[0].astype(F32)
        for s in range(1, S):
            g = g + g_ref[s].astype(F32)
        _adamw_math(g, w_ref, m_ref, v_ref, go_ref, d_ref, mo_ref, vo_ref)

    blk = pl.BlockSpec((rt, C), lambda i: (i, 0))
    return pl.pallas_call(
        body, name=name, grid=(R // rt,),
        in_specs=[pl.BlockSpec((S, rt, C), lambda i: (0, i, 0)), blk, blk, blk],
        out_specs=[blk] * 4, out_shape=[jax.ShapeDtypeStruct((R, C), F32)] * 4,
        compiler_params=_cparams(("parallel",)),
    )(gs, w, m, v)


def _adamw_layer(gs, w, m, v, l, prev, name):
    S, R, C = gs.shape
    L = w.shape[0]
    cpad = -(-C // 128) * 128
    rt = _pick(R, max(16, (1 << 20) // (4 * cpad)), 16)

    def body(*refs):
        g_ref, w_ref, m_ref, v_ref = refs[:4]
        go_ref, d_ref, mo_ref, vo_ref = refs[-4:]
        g = g_ref[0].astype(F32)
        for s in range(1, S):
            g = g + g_ref[s].astype(F32)
        _adamw_math(g, w_ref, m_ref, v_ref, go_ref, d_ref, mo_ref, vo_ref)

    blk = pl.BlockSpec((None, rt, C), lambda i: (l, i, 0))
    in_specs = [pl.BlockSpec((S, rt, C), lambda i: (0, i, 0)), blk, blk, blk]
    args = [gs, w, m, v]
    aliases = {}
    if prev is not None:
        in_specs += [pl.BlockSpec(memory_space=pl.ANY)] * 4
        args += list(prev)
        aliases = {4 + q: q for q in range(4)}
    return pl.pallas_call(
        body, name=name, grid=(R // rt,), in_specs=in_specs,
        out_specs=[blk] * 4, out_shape=[jax.ShapeDtypeStruct((L, R, C), F32)] * 4,
        input_output_aliases=aliases,
        compiler_params=_cparams(("parallel",)),
    )(*args)


def _adamw_nd(gs, w, m, v, name):
    shp = w.shape
    if len(shp) == 1:
        r, c = 1, shp[0]
    else:
        r, c = int(np.prod(shp[:-1])), shp[-1]
    outs = _adamw(gs.reshape(gs.shape[0], r, c), w.reshape(r, c), m.reshape(r, c), v.reshape(r, c), name)
    return [o.reshape(shp) for o in outs]


def kernel(x, c, ctx, c_ctx, w_ada, b_ada, w_in, w_decay_up, b_decay_up, gla_norm_gain, w_pool_group, pool_scale, w_gla_out, w_pool_out, w_out, ln_mix_gain, ln_mix_bias, w_ffn_in, w_ffn_out, ln_ffn_gain, ln_ffn_bias, loss_target, m_c_ctx, m_w_ada, m_b_ada, m_w_in, m_w_decay_up, m_b_decay_up, m_gla_norm_gain, m_w_pool_group, m_pool_scale, m_w_gla_out, m_w_pool_out, m_w_out, m_ln_mix_gain, m_ln_mix_bias, m_w_ffn_in, m_w_ffn_out, m_ln_ffn_gain, m_ln_ffn_bias, v_c_ctx, v_w_ada, v_b_ada, v_w_in, v_w_decay_up, v_b_decay_up, v_gla_norm_gain, v_w_pool_group, v_pool_scale, v_w_gla_out, v_w_pool_out, v_w_out, v_ln_mix_gain, v_ln_mix_bias, v_w_ffn_in, v_w_ffn_out, v_ln_ffn_gain, v_ln_ffn_bias):
    L, D = w_ada.shape[0], w_ada.shape[1]
    seq, ctx_len = x.shape[1], ctx.shape[1]
    T = seq + ctx_len
    rb = ctx_len
    DK = D // 2
    DP = D // 2
    ng = len(POOL_WINDOWS)
    pg = DP // ng
    dff = w_ffn_out.shape[1] * N_DEV
    alpha = (2.0 * L) ** 0.25
    assert seq % rb == 0 and rb % CHUNK == 0 and seq % POOL_TB == 0 and ctx_len % 8 == 0
    xi, yi, ci = _my_pos()
    me = 4 * xi + 2 * yi + ci
    pc = _pool_consts(ctx_len, seq)

    big = [w_in, w_gla_out, w_pool_out, w_out, w_ffn_in, w_ffn_out, w_pool_group]
    g_in, g_go, g_po, g_out, g_fi, g_fo, g_pg = _all_gather([w.astype(BF16) for w in big], "ag_weights")
    cat_cols = lambda g: jnp.swapaxes(g, 1, 2).reshape(g.shape[0], g.shape[2], -1)
    w_in_f = cat_cols(g_in)
    o_q, o_k, o_v, o_g, o_a = 0, DK, 2 * DK, 2 * DK + D, 2 * DK + 2 * D
    o_p, o_bg = o_a + 2 * GATE_RANK, o_a + 2 * GATE_RANK + DP
    w_main = jnp.concatenate([w_in_f[:, :, o_v:o_g], w_in_f[:, :, o_g:o_a], w_in_f[:, :, o_bg:],
                              w_in_f[:, :, o_q:o_k], w_in_f[:, :, o_k:o_v], w_in_f[:, :, o_p:o_bg]], axis=2)
    w_alr = jnp.pad(w_in_f[:, :, o_a:o_p], ((0, 0), (0, 0), (0, ALR_PAD - 2 * GATE_RANK)))
    w_go = g_go.reshape(L, D, D)
    w_po = g_po
    w_o = g_out.reshape(L, D, D)
    w_fi = g_fi
    w_fo = g_fo.reshape(L, dff, D)
    w_pgf = jnp.swapaxes(g_pg, 1, 2).reshape(L, ng, pg, pg)

    dku = w_decay_up.shape[-1]
    small_in = jnp.concatenate([c.reshape(-1), w_decay_up.reshape(-1), b_decay_up.reshape(-1)])
    (small_all,) = _gather_flat([small_in], "ag_small")
    c_all = small_all[:, :D]
    n_wdu = L * 2 * GATE_RANK * dku
    wdu_all = small_all[:, D:D + n_wdu].reshape(N_DEV, L, 2, GATE_RANK, dku)
    wdu_full = jnp.transpose(wdu_all, (1, 2, 3, 0, 4)).reshape(L, 2, GATE_RANK, DK)
    bdu_all = small_all[:, D + n_wdu:].reshape(N_DEV, L, 2, dku)
    bdu_full = jnp.transpose(bdu_all, (1, 2, 0, 3)).reshape(L, 1, 2 * DK)
    wdu_bd = jnp.zeros((L, ALR_PAD, 2 * DK), F32)
    wdu_bd = wdu_bd.at[:, :GATE_RANK, :DK].set(wdu_full[:, 0])
    wdu_bd = wdu_bd.at[:, GATE_RANK:2 * GATE_RANK, DK:].set(wdu_full[:, 1]).astype(BF16)

    ncond = 16
    cond = jnp.concatenate([c_all, c_ctx.reshape(1, D), jnp.zeros((ncond - N_DEV - 1, D), F32)], axis=0)
    s_cond = _silu_rows(cond, "silu_cond")
    wsh = w_ada.shape[-1]
    b_ada_mine = lax.dynamic_slice_in_dim(b_ada, me * wsh, wsh, axis=1)
    mod_part = jnp.stack([_mm(s_cond, w_ada, "nn", F32, "mod_mm", bias=b_ada_mine[l:l + 1], b_pre=(l,))
                          for l in range(L)])
    (mod_all,) = _all_gather([mod_part], "ag_mod")
    mod_all = jnp.swapaxes(mod_all, 1, 2).reshape(L, ncond, N_MOD * D)
    mod_lat = lax.dynamic_slice_in_dim(mod_all, me, 1, axis=1)
    mods = jnp.concatenate([mod_all[:, N_DEV:N_DEV + 1], mod_lat], axis=1).reshape(L, 2, 1, N_MOD * D)
    SH_M, SC_M, GT_M, SH_F, SC_F, GT_F = range(N_MOD)

    xa = jnp.concatenate([ctx[0], x[0]], axis=0)
    vec = lambda a, l: a[l].reshape(1, -1)
    saved = []
    h = _mod_fwd(xa, mods[0], SC_M, SH_M, rb, "mod_fwd")
    for l in range(L):
        proj = _mm(h, w_main, "nn", F32, "mm_in", b_pre=(l,))
        alr = _mm(h, w_alr, "nn", F32, "mm_alr", b_pre=(l,))
        la = _decay_fwd(alr, wdu_bd[l], bdu_full[l], rb, "decay_fwd")
        o_f, s_f = _gla_fwd(proj, la, False, rb, D, "gla_fwd_f")
        o_b, s_b = _gla_fwd(proj, la, True, rb, D, "gla_fwd_b")
        u = _glaout_fwd(o_f, o_b, proj, vec(gla_norm_gain, l), rb, "glaout_fwd")
        y_gla = _mm(u, w_go, "nn", F32, "mm_go", b_pre=(l,))
        pd, y0, r = _pool_fwd(proj, pc, w_pgf[l], vec(pool_scale, l), ctx_len, D, "pool_fwd")
        y_pool = _mm(r, w_po, "nn", F32, "mm_po", b_pre=(l,), b_shard=True)
        m_ = _merge_fwd(proj, y_gla, y_pool, rb, "merge_fwd")
        mix = _mm(m_, w_o, "nn", F32, "mm_out", b_pre=(l,))
        x1, h2 = _unit_fwd(alpha, xa, mix, mods[l], GT_M, vec(ln_mix_gain, l), vec(ln_mix_bias, l),
                           (mods[l], SC_F, SH_F), rb, "unit_mix_fwd")
        ff, s_ = _ffn_in_fwd(h2, w_fi, l, "mm_fi_swiglu")
        ffn = _mm(s_, w_fo, "nn", F32, "mm_fo", b_pre=(l,))
        nxt = (mods[l + 1], SC_M, SH_M) if l + 1 < L else None
        x2, h_next = _unit_fwd(alpha, x1, ffn, mods[l], GT_F, vec(ln_ffn_gain, l), vec(ln_ffn_bias, l),
                               nxt, rb, "unit_ffn_fwd")
        saved.append(dict(xa=xa, h=h, proj=proj, alr=alr, la=la, o_f=o_f, o_b=o_b, s_f=s_f, s_b=s_b, u=u,
                          y_gla=y_gla, pd=pd, y0=y0, r=r, y_pool=y_pool, m=m_, mix=mix, x1=x1, h2=h2, ff=ff,
                          s=s_, ffn=ffn))
        xa, h = x2, h_next

    dxo, loss_part = _loss_head(xa, loss_target[0], rb, "loss_head")
    loss = lax.psum(loss_part[0, 0], ("x", "y", "c"))

    big_params = [("w_in", w_in, m_w_in, v_w_in), ("w_gla_out", w_gla_out, m_w_gla_out, v_w_gla_out),
                  ("w_pool_out", w_pool_out, m_w_pool_out, v_w_pool_out), ("w_out", w_out, m_w_out, v_w_out),
                  ("w_ffn_in", w_ffn_in, m_w_ffn_in, v_w_ffn_in), ("w_ffn_out", w_ffn_out, m_w_ffn_out, v_w_ffn_out),
                  ("w_pool_group", w_pool_group, m_w_pool_group, v_w_pool_group)]
    big_out = {nm: None for nm, _, _, _ in big_params}
    g_small = {k: [None] * L for k in ("gla_gain", "pool_scale", "mix_g", "mix_b", "ffn_g", "ffn_b", "wdu", "bdu")}
    dmods = [None] * L
    dh = None
    sum2 = lambda a: a[0] + a[1]
    rows8 = lambda g: g.reshape(N_DEV, g.shape[0] // N_DEV, g.shape[1])
    for l in range(L - 1, -1, -1):
        sv = saved[l]
        nxt = (mods[l + 1], SC_M, SH_M) if l + 1 < L else None
        dx1, dffn, d_gtf, d_gf, d_bf, d_scm_n, d_shm_n = _unit_bwd(
            alpha, dxo, dh, sv["x1"], sv["ffn"], mods[l], GT_F, vec(ln_ffn_gain, l), vec(ln_ffn_bias, l),
            nxt, rb, "unit_ffn_bwd")
        if nxt is not None:
            dmods[l + 1]["sc_m"], dmods[l + 1]["sh_m"] = d_scm_n, d_shm_n
        dmods[l] = dict(gt_f=d_gtf)
        g_small["ffn_g"][l], g_small["ffn_b"][l] = sum2(d_gf), sum2(d_bf)
        dff_ = _ffn_out_dx(dffn, w_fo, l, sv["ff"], "mm_fo_dx_swiglu")
        c_fo = rows8(_mm(sv["s"], dffn, "tn", BF16, "mm_fo_dw"))
        dh2 = _mm(dff_, w_fi, "nt", F32, "mm_fi_dx", b_pre=(l,), b_shard=True, a_half=True)
        c_fi = _mm(sv["h2"], dff_, "tn", BF16, "mm_fi_dw", b_half=True, out_shard=True)
        dxa, dmix, d_gtm, d_gm, d_bm, d_scf, d_shf = _unit_bwd(
            alpha, dx1, dh2, sv["xa"], sv["mix"], mods[l], GT_M, vec(ln_mix_gain, l), vec(ln_mix_bias, l),
            (mods[l], SC_F, SH_F), rb, "unit_mix_bwd")
        dmods[l].update(gt_m=d_gtm, sc_f=d_scf, sh_f=d_shf)
        g_small["mix_g"][l], g_small["mix_b"][l] = sum2(d_gm), sum2(d_bm)
        dm = _mm(dmix, w_o, "nt", F32, "mm_out_dx", b_pre=(l,))
        c_out = rows8(_mm(sv["m"], dmix, "tn", BF16, "mm_out_dw"))
        dbg1, dbg2, dyg, dyp = _merge_bwd(dm, sv["proj"], sv["y_gla"], sv["y_pool"], rb, "merge_bwd")
        dr = _mm(dyp, w_po, "nt", F32, "mm_po_dx", b_pre=(l,), b_shard=True)
        c_po = _mm(sv["r"], dyp, "tn", BF16, "mm_po_dw", out_shard=True)
        dp_, d_ps, g_pgl = _pool_bwd(dr, sv["y0"], sv["pd"], pc, w_pgf[l], vec(pool_scale, l), ctx_len, D, "pool_bwd")
        g_small["pool_scale"][l] = d_ps
        c_pg = jnp.swapaxes(g_pgl.astype(BF16).reshape(ng, N_DEV, pg // N_DEV, pg), 0, 1).reshape(N_DEV, -1, pg)
        du = _mm(dyg, w_go, "nt", F32, "mm_go_dx", b_pre=(l,))
        c_go = rows8(_mm(sv["u"], dyg, "tn", BF16, "mm_go_dw"))
        do, dg, d_gg = _glaout_bwd(du, sv["o_f"], sv["o_b"], sv["proj"], vec(gla_norm_gain, l), rb, "glaout_bwd")
        g_small["gla_gain"][l] = sum2(d_gg)
        dq_f, dk_f, dv_f, dla_f = _gla_bwd(sv["proj"], sv["la"], do, sv["s_f"], False, rb, D, None, "gla_bwd_f")
        dq, dk, dv, dla_b = _gla_bwd(sv["proj"], sv["la"], do, sv["s_b"], True, rb, D, (dq_f, dk_f, dv_f), "gla_bwd_b")
        dalr, g_wdu, g_bdu = _decay_bwd(dla_f, dla_b, sv["alr"], wdu_bd[l], bdu_full[l], rb, "decay_bwd")
        g_small["wdu"][l] = jnp.stack([g_wdu[:GATE_RANK, :DK], g_wdu[GATE_RANK:2 * GATE_RANK, DK:]])
        g_small["bdu"][l] = g_bdu.reshape(2, DK)
        dproj = jnp.concatenate([dv, dg, dbg1, dbg2, dq, dk, dp_], axis=1)
        dh_alr = _mm(dalr, w_alr, "nt", F32, "mm_alr_dx", b_pre=(l,))
        dh = _mm(dproj, w_main, "nt", F32, "mm_in_dx", b_pre=(l,), add=dh_alr)
        g_main = _mm(sv["h"], dproj, "tn", BF16, "mm_in_dw")
        g_alr = _mm(sv["h"], dalr, "tn", BF16, "mm_alr_dw")
        g_in = jnp.concatenate(
            [g_main[:, 4 * D:4 * D + DK], g_main[:, 4 * D + DK:5 * D], g_main[:, :D], g_main[:, D:2 * D],
             g_alr[:, :2 * GATE_RANK], g_main[:, 5 * D:], g_main[:, 2 * D:4 * D]], axis=1)
        c_in = jnp.swapaxes(g_in.reshape(D, N_DEV, -1), 0, 1)
        parts = _reduce_scatter([c_in, c_go, c_po, c_out, c_fi, c_fo, c_pg], "_l%d" % l)
        for (nm, w, m, v), gs in zip(big_params, parts):
            R, C = gs.shape[1], gs.shape[2]
            big_out[nm] = _adamw_layer(gs, w.reshape(L, R, C), m.reshape(L, R, C), v.reshape(L, R, C), l,
                                       big_out[nm], "adamw_" + nm)
        dxo = dxa
    grad_xa, d_scm0, d_shm0 = _mod_bwd(dxo, dh, saved[0]["xa"], mods[0], SC_M, SH_M, rb, "mod_bwd")
    dmods[0]["sc_m"], dmods[0]["sh_m"] = d_scm0, d_shm0
    grad_x = grad_xa[ctx_len:].reshape(1, seq, D)

    order = ("sh_m", "sc_m", "gt_m", "sh_f", "sc_f", "gt_f")
    dmod = jnp.stack([jnp.concatenate([dmods[l][k] for k in order], axis=2) for l in range(L)])
    dmod = dmod.reshape(-1)
    sm = lambda k: jnp.stack([a.reshape(-1) for a in g_small[k]]).reshape(-1)
    small_keys = ("gla_gain", "pool_scale", "mix_g", "mix_b", "ffn_g", "ffn_b", "wdu", "bdu")
    small_part = jnp.concatenate([sm(k) for k in small_keys])
    small_g, dmod_g = _gather_flat([small_part, dmod], "ag_small_grads")
    small_sum = _sum_lead(small_g.reshape(N_DEV, -1, 128), "sum_small").reshape(-1)
    off = 0
    rep = {}
    for k, n in zip(small_keys, (L * D, L * DP, L * D, L * D, L * D, L * D, L * 2 * GATE_RANK * DK, L * 2 * DK)):
        rep[k] = small_sum[off:off + n]
        off += n
    g_wdu_mine = lax.dynamic_slice_in_dim(rep["wdu"].reshape(L, 2, GATE_RANK, DK), me * dku, dku, axis=3)
    g_bdu_mine = lax.dynamic_slice_in_dim(rep["bdu"].reshape(L, 2, DK), me * dku, dku, axis=2)

    dmod_all = dmod_g.reshape(N_DEV, L, 2, N_MOD * D)
    dm_ctx = _sum_lead(dmod_all[:, :, 0].reshape(N_DEV, L, N_MOD * D), "sum_dmod_ctx")
    dm_rows = jnp.concatenate([jnp.swapaxes(dmod_all[:, :, 1], 0, 1), dm_ctx[:, None],
                               jnp.zeros((L, ncond - N_DEV - 1, N_MOD * D), F32)], axis=1)
    g_b_ada = _sum_lead(jnp.swapaxes(dm_rows, 0, 1), "sum_b_ada")
    dm_mine = lax.dynamic_slice_in_dim(dm_rows, me * wsh, wsh, axis=2).astype(BF16)
    g_w_ada = jnp.stack([_mm(s_cond, dm_mine[l], "tn", F32, "ada_dw") for l in range(L)])
    ds_part = _sum_lead(jnp.stack([_mm(dm_mine[l], w_ada, "nt", F32, "ada_dx", b_pre=(l,)) for l in range(L)]),
                        "sum_ds")
    (ds_all,) = _gather_flat([ds_part[N_DEV]], "ag_ds")
    ds_ctx = _sum_lead(ds_all.reshape(N_DEV, 1, D), "sum_ds_ctx")
    g_c_ctx = _silu_grad(c_ctx.reshape(1, D), ds_ctx, "silu_grad").reshape(D)

    one = lambda g: g[None]
    small_table = {
        "c_ctx": (one(g_c_ctx), c_ctx, m_c_ctx, v_c_ctx),
        "w_ada": (one(g_w_ada), w_ada, m_w_ada, v_w_ada),
        "b_ada": (one(g_b_ada), b_ada, m_b_ada, v_b_ada),
        "w_decay_up": (one(g_wdu_mine), w_decay_up, m_w_decay_up, v_w_decay_up),
        "b_decay_up": (one(g_bdu_mine), b_decay_up, m_b_decay_up, v_b_decay_up),
        "gla_norm_gain": (one(rep["gla_gain"].reshape(L, D)), gla_norm_gain, m_gla_norm_gain, v_gla_norm_gain),
        "pool_scale": (one(rep["pool_scale"].reshape(L, DP)), pool_scale, m_pool_scale, v_pool_scale),
        "ln_mix_gain": (one(rep["mix_g"].reshape(L, D)), ln_mix_gain, m_ln_mix_gain, v_ln_mix_gain),
        "ln_mix_bias": (one(rep["mix_b"].reshape(L, D)), ln_mix_bias, m_ln_mix_bias, v_ln_mix_bias),
        "ln_ffn_gain": (one(rep["ffn_g"].reshape(L, D)), ln_ffn_gain, m_ln_ffn_gain, v_ln_ffn_gain),
        "ln_ffn_bias": (one(rep["ffn_b"].reshape(L, D)), ln_ffn_bias, m_ln_ffn_bias, v_ln_ffn_bias),
    }
    big_shapes = {nm: w.shape for nm, w, _, _ in big_params}
    names = ("c_ctx", "w_ada", "b_ada", "w_in", "w_decay_up", "b_decay_up", "gla_norm_gain", "w_pool_group",
             "pool_scale", "w_gla_out", "w_pool_out", "w_out", "ln_mix_gain", "ln_mix_bias", "w_ffn_in", "w_ffn_out",
             "ln_ffn_gain", "ln_ffn_bias")
    grads, deltas, new_m, new_v = [], [], [], []
    for nm in names:
        if nm in small_table:
            res = _adamw_nd(*small_table[nm], "adamw_" + nm)
        else:
            res = [o.reshape(big_shapes[nm]) for o in big_out[nm]]
        for lst, o in zip((grads, deltas, new_m, new_v), res):
            lst.append(o)
    return (loss, grad_x, *grads, *deltas, *new_m, *new_v)
```

```python
import functools
import math

import numpy as np
import jax
import jax.numpy as jnp
from jax import lax
from jax.experimental import pallas as pl
from jax.experimental.pallas import tpu as pltpu

F32 = jnp.float32
BF16 = jnp.bfloat16

N_DEV = 8
N_HEADS = 4
GATE_RANK = 16
GATE_NORM = 16.0
CHUNK = 64
GRID_W = 64
POOL_WINDOWS = (2, 4, 8, 16)
N_MOD = 6
LN_EPS = 1e-5
RMS_EPS = 1e-6
ALR_PAD = 128
POOL_TB = 256
POOL_PAD_ROWS = 8
ADAM_LR = 0.001
ADAM_B1 = 0.9
ADAM_B2 = 0.999
ADAM_EPS = 1e-08
ADAM_WD = 0.01
ADAM_STEP = 10
VMEM_LIMIT = 56 * 1024 * 1024
MESH = pl.DeviceIdType.MESH


def _cparams(sem=None):
    return pltpu.CompilerParams(dimension_semantics=sem, vmem_limit_bytes=VMEM_LIMIT)


def _pick(dim, cap, mult):
    best = None
    for d in range(mult, min(dim, cap) + 1, mult):
        if dim % d == 0:
            best = d
    return best if best is not None else dim


def _sig(x):
    return 1.0 / (1.0 + jnp.exp(-x))


def _silu(x):
    return x * _sig(x)


def _dot(a, b):
    return lax.dot_general(a, b, (((1,), (0,)), ((), ())), preferred_element_type=F32)


def _dot_nt(a, b):
    return lax.dot_general(a, b, (((1,), (1,)), ((), ())), preferred_element_type=F32)


def _dot_tn(a, b):
    return lax.dot_general(a, b, (((0,), (0,)), ((), ())), preferred_element_type=F32)


def _split2(x):
    hi = x.astype(BF16)
    lo = (x - hi.astype(F32)).astype(BF16)
    return hi, lo


def _dot2(m_b, x):
    hi, lo = _split2(x)
    return _dot(m_b, hi) + _dot(m_b, lo)


def _dot3(m_b, x):
    h1 = x.astype(BF16)
    r1 = x - h1.astype(F32)
    h2 = r1.astype(BF16)
    h3 = (r1 - h2.astype(F32)).astype(BF16)
    return _dot(m_b, h1) + _dot(m_b, h2) + _dot(m_b, h3)


def _my_pos():
    return lax.axis_index("x"), lax.axis_index("y"), lax.axis_index("c")


def _all_gather(arrs, name):
    n = len(arrs)
    srcs = [a.reshape((a.shape[0], 1) + a.shape[1:]) for a in arrs]
    outs = [jax.ShapeDtypeStruct((a.shape[0], N_DEV) + a.shape[1:], a.dtype) for a in arrs]

    def body(*refs):
        in_refs, out_refs = refs[:n], refs[n:2 * n]
        send_sems, recv_sems, local_sems = refs[2 * n:]
        x, y, c = _my_pos()
        me, sibling = (x, y, c), (x, y, 1 - c)
        chips = [(1 - x, y), (x, 1 - y), (1 - x, 1 - y)]

        def slot(t, pos):
            return out_refs[t].at[:, pl.ds(4 * pos[0] + 2 * pos[1] + pos[2], 1)]

        def copy(t, k, block, to, src=None):
            return pltpu.make_async_remote_copy(
                src_ref=slot(t, block) if src is None else src, dst_ref=slot(t, block),
                send_sem=send_sems.at[t * 7 + k], recv_sem=recv_sems.at[t * 7 + k],
                device_id=to, device_id_type=MESH)

        mine = [pltpu.make_async_copy(in_refs[t], slot(t, me), local_sems.at[t]) for t in range(n)]
        for cp in mine:
            cp.start()
        first = []
        for t in range(n):
            first.append(copy(t, 0, me, sibling, src=in_refs[t]))
            first += [copy(t, 1 + j, me, (*chip, c), src=in_refs[t]) for j, chip in enumerate(chips)]
        for cp in first:
            cp.start()
        passed = []
        for j, chip in enumerate(chips):
            for t in range(n):
                copy(t, 1 + j, (*chip, c), me).wait_recv()
                fwd = copy(t, 4 + j, (*chip, c), sibling)
                fwd.start()
                passed.append(fwd)
        for t in range(n):
            copy(t, 0, sibling, me).wait_recv()
            for j, chip in enumerate(chips):
                copy(t, 4 + j, (*chip, 1 - c), me).wait_recv()
        for cp in first + passed:
            cp.wait_send()
        for cp in mine:
            cp.wait()

    any_spec = pl.BlockSpec(memory_space=pl.ANY)
    res = pl.pallas_call(
        body, name=name, out_shape=outs,
        in_specs=[any_spec] * n, out_specs=[any_spec] * n,
        scratch_shapes=[pltpu.SemaphoreType.DMA((7 * n,)), pltpu.SemaphoreType.DMA((7 * n,)),
                        pltpu.SemaphoreType.DMA((n,))],
        compiler_params=pltpu.CompilerParams(has_side_effects=True),
    )(*srcs)
    return list(res)


def _gather_flat(vecs, name):
    padded = []
    for v in vecs:
        n = v.shape[0]
        padded.append(jnp.pad(v, (0, -n % 128)).reshape(1, -1, 128))
    res = _all_gather(padded, name)
    return [r.reshape(N_DEV, -1)[:, :v.shape[0]] for r, v in zip(res, vecs)]


N_CHIP = 4


def _comm_call(body, name, arrs, outs, n_sems):
    any_spec = pl.BlockSpec(memory_space=pl.ANY)
    n = len(arrs)
    res = pl.pallas_call(
        body, name=name, out_shape=outs,
        in_specs=[any_spec] * n, out_specs=[any_spec] * len(outs),
        scratch_shapes=[pltpu.SemaphoreType.DMA((s,)) for s in n_sems],
        compiler_params=pltpu.CompilerParams(has_side_effects=True),
    )(*arrs)
    return list(res)


def _sibling_exchange(arrs, name):
    n = len(arrs)
    outs = [jax.ShapeDtypeStruct((N_CHIP,) + a.shape[1:], a.dtype) for a in arrs]

    def body(*refs):
        in_refs, out_refs = refs[:n], refs[n:2 * n]
        send_sems, recv_sems = refs[2 * n:]
        x, y, c = _my_pos()
        copies = []
        for t in range(n):
            for k in range(N_CHIP):
                cp = pltpu.make_async_remote_copy(
                    src_ref=in_refs[t].at[pl.ds(2 * k + (1 - c), 1)], dst_ref=out_refs[t].at[pl.ds(k, 1)],
                    send_sem=send_sems.at[t * N_CHIP + k], recv_sem=recv_sems.at[t * N_CHIP + k],
                    device_id=(x, y, 1 - c), device_id_type=MESH)
                cp.start()
                copies.append(cp)
        for cp in copies:
            cp.wait_recv()
        for cp in copies:
            cp.wait_send()

    return _comm_call(body, name, arrs, outs, (N_CHIP * n, N_CHIP * n))


class _Job:
    def __init__(self, arrs, outs, n_sems, start, finish):
        self.arrs, self.outs, self.n_sems, self.start, self.finish = arrs, outs, n_sems, start, finish


def _gather_job(stacked, l):
    n = len(stacked)
    outs = [jax.ShapeDtypeStruct((N_DEV,) + a.shape[1:], a.dtype) for a in stacked]

    def parts(in_refs, out_refs, sems):
        send_sems, recv_sems, local_sems = sems
        x, y, c = _my_pos()
        me, sibling = (x, y, c), (x, y, 1 - c)
        chips = [(1 - x, y), (x, 1 - y), (1 - x, 1 - y)]
        src = lambda t: in_refs[t].at[pl.ds(l, 1)]

        def slot(t, pos):
            return out_refs[t].at[pl.ds(4 * pos[0] + 2 * pos[1] + pos[2], 1)]

        def copy(t, k, block, to, from_input=False):
            return pltpu.make_async_remote_copy(
                src_ref=src(t) if from_input else slot(t, block), dst_ref=slot(t, block),
                send_sem=send_sems.at[t * 7 + k], recv_sem=recv_sems.at[t * 7 + k],
                device_id=to, device_id_type=MESH)

        mine = [pltpu.make_async_copy(src(t), slot(t, me), local_sems.at[t]) for t in range(n)]
        first = []
        for t in range(n):
            first.append(copy(t, 0, me, sibling, True))
            first += [copy(t, 1 + j, me, (*chip, c), True) for j, chip in enumerate(chips)]
        return me, sibling, chips, copy, mine, first

    def start(in_refs, out_refs, sems):
        _, _, _, _, mine, first = parts(in_refs, out_refs, sems)
        for cp in mine + first:
            cp.start()

    def finish(in_refs, out_refs, sems):
        me, sibling, chips, copy, mine, first = parts(in_refs, out_refs, sems)
        passed = []
        for j, chip in enumerate(chips):
            for t in range(n):
                copy(t, 1 + j, (*chip, me[2]), me).wait_recv()
                fwd = copy(t, 4 + j, (*chip, me[2]), sibling)
                fwd.start()
                passed.append(fwd)
        for t in range(n):
            copy(t, 0, sibling, me).wait_recv()
            for j, chip in enumerate(chips):
                copy(t, 4 + j, (*chip, 1 - me[2]), me).wait_recv()
        for cp in first + passed:
            cp.wait_send()
        for cp in mine:
            cp.wait()

    return _Job(stacked, outs, (7 * n, 7 * n, n), start, finish)


def _chip_job(arrs):
    n = len(arrs)
    outs = [jax.ShapeDtypeStruct(a.shape, a.dtype) for a in arrs]

    def parts(in_refs, out_refs, sems):
        send_sems, recv_sems, local_sems = sems
        x, y, c = _my_pos()
        chip = 2 * x + y
        mine, sends, recvs = [], [], []
        for t in range(n):
            mine.append(pltpu.make_async_copy(in_refs[t].at[pl.ds(chip, 1)], out_refs[t].at[pl.ds(chip, 1)],
                                              local_sems.at[t]))
            for m in range(1, N_CHIP):
                px, py = x ^ (m >> 1), y ^ (m & 1)
                peer = 2 * px + py
                sends.append(pltpu.make_async_remote_copy(
                    src_ref=in_refs[t].at[pl.ds(peer, 1)], dst_ref=out_refs[t].at[pl.ds(chip, 1)],
                    send_sem=send_sems.at[t * 3 + m - 1], recv_sem=recv_sems.at[t * 3 + m - 1],
                    device_id=(px, py, c), device_id_type=MESH))
                recvs.append(pltpu.make_async_remote_copy(
                    src_ref=in_refs[t].at[pl.ds(peer, 1)], dst_ref=out_refs[t].at[pl.ds(peer, 1)],
                    send_sem=send_sems.at[t * 3 + m - 1], recv_sem=recv_sems.at[t * 3 + m - 1],
                    device_id=(x, y, c), device_id_type=MESH))
        return mine, sends, recvs

    def start(in_refs, out_refs, sems):
        mine, sends, _ = parts(in_refs, out_refs, sems)
        for cp in mine + sends:
            cp.start()

    def finish(in_refs, out_refs, sems):
        mine, sends, recvs = parts(in_refs, out_refs, sems)
        for cp in recvs:
            cp.wait_recv()
        for cp in sends:
            cp.wait_send()
        for cp in mine:
            cp.wait()

    return _Job(arrs, outs, (3 * n, 3 * n, n), start, finish)


def _run_job(job, name):
    n = len(job.arrs)

    def body(*refs):
        ins, outs, sems = refs[:n], refs[n:n + len(job.outs)], refs[n + len(job.outs):]
        job.start(ins, outs, sems)
        job.finish(ins, outs, sems)

    return _comm_call(body, name, job.arrs, job.outs, job.n_sems)


def _carry(job, body, grid, in_specs, out_specs, out_shape, scratch_shapes, args):
    out_specs = list(out_specs) if isinstance(out_specs, (list, tuple)) else [out_specs]
    out_shape = list(out_shape) if isinstance(out_shape, (list, tuple)) else [out_shape]
    n_ci, n_co, n_cs = len(in_specs), len(out_specs), len(scratch_shapes)
    n_ji, n_jo = len(job.arrs), len(job.outs)
    any_spec = pl.BlockSpec(memory_space=pl.ANY)
    total = int(np.prod(grid))

    def wrapped(*refs):
        cin, jin = refs[:n_ci], refs[n_ci:n_ci + n_ji]
        o0 = n_ci + n_ji
        cout, jout = refs[o0:o0 + n_co], refs[o0 + n_co:o0 + n_co + n_jo]
        s0 = o0 + n_co + n_jo
        cscr, jsems = refs[s0:s0 + n_cs], refs[s0 + n_cs:]
        step = pl.program_id(0)
        for d in range(1, len(grid)):
            step = step * grid[d] + pl.program_id(d)

        @pl.when(step == 0)
        def _():
            job.start(jin, jout, jsems)

        body(*cin, *cout, *cscr)

        @pl.when(step == total - 1)
        def _():
            job.finish(jin, jout, jsems)

    return (wrapped, list(in_specs) + [any_spec] * n_ji, out_specs + [any_spec] * n_jo,
            out_shape + list(job.outs),
            list(scratch_shapes) + [pltpu.SemaphoreType.DMA((s,)) for s in job.n_sems],
            list(args) + list(job.arrs), n_co)


def _pair_add(g, r, name):
    _, R, C = g.shape
    cpad = -(-C // 128) * 128
    rt = _pick(R, max(16, (1 << 20) // (2 * cpad)), 16)
    cidx = lax.axis_index("c").astype(jnp.int32).reshape(1)

    def body(c_ref, g_ref, r_ref, o_ref):
        o_ref[...] = (g_ref[...].astype(F32) + r_ref[...].astype(F32)).astype(o_ref.dtype)

    return pl.pallas_call(
        body, name=name, out_shape=jax.ShapeDtypeStruct((N_CHIP, R, C), g.dtype),
        grid_spec=pltpu.PrefetchScalarGridSpec(
            num_scalar_prefetch=1, grid=(N_CHIP, R // rt),
            in_specs=[pl.BlockSpec((None, rt, C), lambda k, i, c_ref: (2 * k + c_ref[0], i, 0)),
                      pl.BlockSpec((None, rt, C), lambda k, i, c_ref: (k, i, 0))],
            out_specs=pl.BlockSpec((None, rt, C), lambda k, i, c_ref: (k, i, 0))),
        compiler_params=_cparams(("parallel", "parallel")),
    )(cidx, g, r)


def _pair_sums(chunks, tag):
    flat = [g.reshape(N_DEV, -1, g.shape[-1]) for g in chunks]
    sib = _sibling_exchange(flat, "rs_sibling" + tag)
    return [_pair_add(g, r, "rs_pair_add" + tag) for g, r in zip(flat, sib)]


def _mm(a, b, mode, out_dtype=F32, name="mm", bias=None, add=None, b_pre=(), b_shard=False,
        a_half=False, b_half=False, out_shard=False, job=None):
    npre = len(b_pre)
    bshape = b.shape[npre:]
    if mode == "nn":
        M, K = a.shape
        if b_shard:
            K2, N = bshape[1], N_DEV * bshape[2]
        else:
            K2, N = bshape
    elif mode == "nt":
        M, K = (a.shape[1], 2 * a.shape[2]) if a_half else a.shape
        if b_shard:
            N, K2 = bshape[1], N_DEV * bshape[2]
        else:
            N, K2 = bshape
    else:
        K, M = a.shape
        K2, N = (b.shape[1], 2 * b.shape[2]) if b_half else bshape
    assert K == K2, (a.shape, b.shape, mode)
    tm = _pick(M, 1100, 16) if mode != "tn" else _pick(M, 1024, 128)
    tn = _pick(N, 1024, 128)
    tk = _pick(K, 2176, 128)
    if b_shard and mode == "nn":
        tn = bshape[2]
    if b_shard and mode == "nt":
        tk = bshape[2]
    if out_shard:
        tn = N // N_DEV
    nk = K // tk
    none_pre = (None,) * npre
    if mode == "nn":
        a_spec = pl.BlockSpec((tm, tk), lambda i, j, k: (i, k))
        if b_shard:
            b_spec = pl.BlockSpec(none_pre + (None, tk, tn), lambda i, j, k: b_pre + (j, k, 0))
        else:
            b_spec = pl.BlockSpec(none_pre + (tk, tn), lambda i, j, k: b_pre + (k, j))
        dot = _dot
    elif mode == "nt":
        if a_half:
            nkh = a.shape[2] // tk
            a_spec = pl.BlockSpec((None, tm, tk), lambda i, j, k: (k // nkh, i, k % nkh))
        else:
            a_spec = pl.BlockSpec((tm, tk), lambda i, j, k: (i, k))
        if b_shard:
            b_spec = pl.BlockSpec(none_pre + (None, tn, tk), lambda i, j, k: b_pre + (k, j, 0))
        else:
            b_spec = pl.BlockSpec(none_pre + (tn, tk), lambda i, j, k: b_pre + (j, k))
        dot = _dot_nt
    else:
        a_spec = pl.BlockSpec((tk, tm), lambda i, j, k: (k, i))
        if b_half:
            nnh = b.shape[2] // tn
            b_spec = pl.BlockSpec((None, tk, tn), lambda i, j, k: (j // nnh, k, j % nnh))
        else:
            b_spec = pl.BlockSpec(none_pre + (tk, tn), lambda i, j, k: b_pre + (k, j))
        dot = _dot_tn
    in_specs = [a_spec, b_spec]
    args = [a, b]
    if bias is not None:
        in_specs.append(pl.BlockSpec((1, tn), lambda i, j, k: (0, j)))
        args.append(bias)
    if add is not None:
        in_specs.append(pl.BlockSpec((tm, tn), lambda i, j, k: (i, j)))
        args.append(add)
    n_in = len(args)
    if out_shard:
        o_spec = pl.BlockSpec((None, tm, tn), lambda i, j, k: (j, i, 0))
        o_shape = jax.ShapeDtypeStruct((N_DEV, M, tn), out_dtype)
    else:
        o_spec = pl.BlockSpec((tm, tn), lambda i, j, k: (i, j))
        o_shape = jax.ShapeDtypeStruct((M, N), out_dtype)

    def body(*refs):
        a_ref, b_ref = refs[0], refs[1]
        bias_ref = refs[2] if bias is not None else None
        add_ref = refs[n_in - 1] if add is not None else None
        o_ref = refs[n_in]
        p = dot(a_ref[...].astype(BF16), b_ref[...].astype(BF16))

        def finish(acc):
            if bias_ref is not None:
                acc = acc + bias_ref[...]
            if add_ref is not None:
                acc = acc + add_ref[...]
            o_ref[...] = acc.astype(o_ref.dtype)

        if nk == 1:
            finish(p)
        else:
            acc_ref = refs[-1]
            k = pl.program_id(2)

            @pl.when(k == 0)
            def _():
                acc_ref[...] = p

            @pl.when(k > 0)
            def _():
                acc_ref[...] += p

            @pl.when(k == nk - 1)
            def _():
                finish(acc_ref[...])

    grid = (M // tm, N // tn, nk)
    scratch = [pltpu.VMEM((tm, tn), F32)] if nk > 1 else []
    if job is None:
        return pl.pallas_call(
            body, name=name, grid=grid, in_specs=in_specs, out_specs=o_spec, out_shape=o_shape,
            scratch_shapes=scratch, compiler_params=_cparams(("parallel", "parallel", "arbitrary")),
        )(*args)
    return _call_carrying(job, body, name, grid, in_specs, o_spec, o_shape, scratch, args)


def _call_carrying(job, body, name, grid, in_specs, out_specs, out_shape, scratch, args):
    body, in_specs, out_specs, out_shape, scratch, args, n_co = _carry(
        job, body, grid, in_specs, out_specs, out_shape, scratch, args)
    res = pl.pallas_call(
        body, name=name, grid=grid, in_specs=in_specs, out_specs=out_specs, out_shape=out_shape,
        scratch_shapes=scratch, compiler_params=_cparams(("arbitrary",) * len(grid)),
    )(*args)
    own = res[0] if n_co == 1 else list(res[:n_co])
    return own, list(res[n_co:])


def _ffn_in_fwd(h2, w_fi, l, name, job=None):
    T, D = h2.shape
    n = w_fi.shape[3]
    nh = N_DEV // 2
    dff = nh * n
    tm = _pick(T, 600, 16)

    def body(a_ref, bg_ref, bu_ref, ff_ref, s_ref):
        a = a_ref[...]
        g = _dot(a, bg_ref[...])
        u = _dot(a, bu_ref[...])
        ff_ref[0] = g
        ff_ref[1] = u
        s_ref[...] = _swiglu_f(g, u).astype(BF16)

    grid = (T // tm, nh)
    in_specs = [pl.BlockSpec((tm, D), lambda i, j: (i, 0)),
                pl.BlockSpec((None, None, D, n), lambda i, j: (l, j, 0, 0)),
                pl.BlockSpec((None, None, D, n), lambda i, j: (l, nh + j, 0, 0))]
    out_specs = [pl.BlockSpec((2, tm, n), lambda i, j: (0, i, j)), pl.BlockSpec((tm, n), lambda i, j: (i, j))]
    out_shape = [jax.ShapeDtypeStruct((2, T, dff), F32), jax.ShapeDtypeStruct((T, dff), BF16)]
    args = (h2, w_fi, w_fi)
    if job is None:
        return pl.pallas_call(
            body, name=name, grid=grid, in_specs=in_specs, out_specs=out_specs, out_shape=out_shape,
            compiler_params=_cparams(("parallel", "parallel")),
        )(*args)
    return _call_carrying(job, body, name, grid, in_specs, out_specs, out_shape, [], args)


def _ffn_out_dx(dffn, w_fo, l, ff, name, job=None):
    T, D = dffn.shape
    dff = ff.shape[2]
    tm = _pick(T, 600, 16)
    tw = _pick(dff, 1408, 128)

    def body(a_ref, b_ref, ff_ref, o_ref):
        ds = _dot_nt(a_ref[...], b_ref[...])
        _, vjp = jax.vjp(_swiglu_f, ff_ref[0], ff_ref[1])
        dg, du = vjp(ds)
        o_ref[0] = dg.astype(BF16)
        o_ref[1] = du.astype(BF16)

    grid = (T // tm, dff // tw)
    in_specs = [pl.BlockSpec((tm, D), lambda i, j: (i, 0)),
                pl.BlockSpec((None, tw, D), lambda i, j: (l, j, 0)),
                pl.BlockSpec((2, tm, tw), lambda i, j: (0, i, j))]
    out_specs = pl.BlockSpec((2, tm, tw), lambda i, j: (0, i, j))
    out_shape = jax.ShapeDtypeStruct((2, T, dff), BF16)
    args = (dffn, w_fo, ff)
    if job is None:
        return pl.pallas_call(
            body, name=name, grid=grid, in_specs=in_specs, out_specs=out_specs, out_shape=out_shape,
            compiler_params=_cparams(("parallel", "parallel")),
        )(*args)
    return _call_carrying(job, body, name, grid, in_specs, out_specs, out_shape, [], args)


def _row(rb, w, col=0):
    return pl.BlockSpec((rb, w), lambda i: (i, col))


def _modspec(d, sec):
    return pl.BlockSpec((None, 1, d), lambda i: (jnp.minimum(i, 1), 0, sec))


def _vec(w):
    return pl.BlockSpec((1, w), lambda i: (0, 0))


def _acc2(w):
    return pl.BlockSpec((None, 1, w), lambda i: (jnp.minimum(i, 1), 0, 0))


def _accum(ref, val):
    i = pl.program_id(0)

    @pl.when(i <= 1)
    def _():
        ref[...] = val

    @pl.when(i > 1)
    def _():
        ref[...] += val


def _acc_shape(w):
    return jax.ShapeDtypeStruct((2, 1, w), F32)


def _mod_f(x, sc, sh):
    return x * (1.0 + sc) + sh


def _mod_fwd(xa, mod, sec_sc, sec_sh, rb, name):
    T, D = xa.shape

    def body(x_ref, sc_ref, sh_ref, h_ref):
        h_ref[...] = _mod_f(x_ref[...], sc_ref[...], sh_ref[...]).astype(BF16)

    return pl.pallas_call(
        body, name=name, grid=(T // rb,),
        in_specs=[_row(rb, D), _modspec(D, sec_sc), _modspec(D, sec_sh)],
        out_specs=_row(rb, D), out_shape=jax.ShapeDtypeStruct((T, D), BF16),
        compiler_params=_cparams(("parallel",)),
    )(xa, mod, mod)


def _mod_bwd(dxa, dh, xa, mod, sec_sc, sec_sh, rb, name):
    T, D = xa.shape

    def body(dxa_ref, dh_ref, x_ref, sc_ref, sh_ref, dx_ref, dsc_ref, dsh_ref):
        _, vjp = jax.vjp(_mod_f, x_ref[...], sc_ref[...], sh_ref[...])
        dx, dsc, dsh = vjp(dh_ref[...])
        dx_ref[...] = dxa_ref[...] + dx
        _accum(dsc_ref, dsc)
        _accum(dsh_ref, dsh)

    return pl.pallas_call(
        body, name=name, grid=(T // rb,),
        in_specs=[_row(rb, D), _row(rb, D), _row(rb, D), _modspec(D, sec_sc), _modspec(D, sec_sh)],
        out_specs=[_row(rb, D), _acc2(D), _acc2(D)],
        out_shape=[jax.ShapeDtypeStruct((T, D), F32), _acc_shape(D), _acc_shape(D)],
        compiler_params=_cparams(("arbitrary",)),
    )(dxa, dh, xa, mod, mod)


def _ln_f(alpha, x, mix, gt, gain, bias):
    z = alpha * x + gt * mix
    mu = jnp.mean(z, axis=-1, keepdims=True)
    zc = z - mu
    var = jnp.mean(zc * zc, axis=-1, keepdims=True)
    return zc * lax.rsqrt(var + LN_EPS) * gain + bias


def _unit_fwd(alpha, x, mix, mod, sec_gt, gain, bias, next_mod, rb, name):
    T, D = x.shape
    has_mod = next_mod is not None

    def body(*refs):
        if has_mod:
            x_ref, mix_ref, gt_ref, g_ref, b_ref, sc_ref, sh_ref, xo_ref, h_ref = refs
        else:
            x_ref, mix_ref, gt_ref, g_ref, b_ref, xo_ref = refs
        xo = _ln_f(alpha, x_ref[...], mix_ref[...], gt_ref[...], g_ref[...], b_ref[...])
        xo_ref[...] = xo
        if has_mod:
            h_ref[...] = _mod_f(xo, sc_ref[...], sh_ref[...]).astype(BF16)

    in_specs = [_row(rb, D), _row(rb, D), _modspec(D, sec_gt), _vec(D), _vec(D)]
    args = [x, mix, mod, gain, bias]
    out_specs = [_row(rb, D)]
    out_shape = [jax.ShapeDtypeStruct((T, D), F32)]
    if has_mod:
        nm, s_sc, s_sh = next_mod
        in_specs += [_modspec(D, s_sc), _modspec(D, s_sh)]
        args += [nm, nm]
        out_specs.append(_row(rb, D))
        out_shape.append(jax.ShapeDtypeStruct((T, D), BF16))
    res = pl.pallas_call(
        body, name=name, grid=(T // rb,), in_specs=in_specs, out_specs=out_specs, out_shape=out_shape,
        compiler_params=_cparams(("parallel",)),
    )(*args)
    return (res[0], res[1]) if has_mod else (res[0], None)


def _unit_bwd(alpha, dxo, dh, x, mix, mod, sec_gt, gain, bias, next_mod, rb, name):
    T, D = x.shape
    has_mod = next_mod is not None

    def body(*refs):
        if has_mod:
            (dxo_ref, dh_ref, x_ref, mix_ref, gt_ref, g_ref, b_ref, sc_ref, sh_ref,
             dx_ref, dmix_ref, dgt_ref, dg_ref, db_ref, dsc_ref, dsh_ref) = refs
        else:
            (dxo_ref, x_ref, mix_ref, gt_ref, g_ref, b_ref,
             dx_ref, dmix_ref, dgt_ref, dg_ref, db_ref) = refs
        xo, vjp = jax.vjp(functools.partial(_ln_f, alpha), x_ref[...], mix_ref[...], gt_ref[...],
                          g_ref[...], b_ref[...])
        dxo_t = dxo_ref[...]
        if has_mod:
            _, vjp_m = jax.vjp(_mod_f, xo, sc_ref[...], sh_ref[...])
            dxo_m, dsc, dsh = vjp_m(dh_ref[...])
            dxo_t = dxo_t + dxo_m
            _accum(dsc_ref, dsc)
            _accum(dsh_ref, dsh)
        dx, dmix, dgt, dg, db = vjp(dxo_t)
        dx_ref[...] = dx
        dmix_ref[...] = dmix.astype(BF16)
        _accum(dgt_ref, dgt)
        _accum(dg_ref, dg)
        _accum(db_ref, db)

    in_specs = [_row(rb, D)]
    args = [dxo]
    if has_mod:
        in_specs.append(_row(rb, D))
        args.append(dh)
    in_specs += [_row(rb, D), _row(rb, D), _modspec(D, sec_gt), _vec(D), _vec(D)]
    args += [x, mix, mod, gain, bias]
    out_specs = [_row(rb, D), _row(rb, D), _acc2(D), _acc2(D), _acc2(D)]
    out_shape = [jax.ShapeDtypeStruct((T, D), F32), jax.ShapeDtypeStruct((T, D), BF16),
                 _acc_shape(D), _acc_shape(D), _acc_shape(D)]
    if has_mod:
        nm, s_sc, s_sh = next_mod
        in_specs += [_modspec(D, s_sc), _modspec(D, s_sh)]
        args += [nm, nm]
        out_specs += [_acc2(D), _acc2(D)]
        out_shape += [_acc_shape(D), _acc_shape(D)]
    res = pl.pallas_call(
        body, name=name, grid=(T // rb,), in_specs=in_specs, out_specs=out_specs, out_shape=out_shape,
        compiler_params=_cparams(("arbitrary",)),
    )(*args)
    if has_mod:
        return res
    return list(res) + [None, None]


def _log_sigmoid(z):
    return jnp.minimum(z, 0.0) - jnp.log(1.0 + jnp.exp(-jnp.abs(z)))


def _decay_fwd(alr, wdu, bdu, rb, name):
    T = alr.shape[0]
    W = wdu.shape[1]

    def body(a_ref, w_ref, b_ref, la_ref):
        z = _dot(a_ref[...].astype(BF16), w_ref[...]) + b_ref[...]
        la_ref[...] = _log_sigmoid(z) * (1.0 / GATE_NORM)

    return pl.pallas_call(
        body, name=name, grid=(T // rb,),
        in_specs=[_row(rb, ALR_PAD), pl.BlockSpec((ALR_PAD, W), lambda i: (0, 0)), _vec(W)],
        out_specs=_row(rb, W), out_shape=jax.ShapeDtypeStruct((T, W), F32),
        compiler_params=_cparams(("parallel",)),
    )(alr, wdu, bdu)


def _decay_bwd(dla_f, dla_b, alr, wdu, bdu, rb, name):
    T = alr.shape[0]
    W = wdu.shape[1]
    DK = W // 2

    def body(df_ref, db_ref, a_ref, w_ref, b_ref, dalr_ref, gw_ref, gb_ref):
        i = pl.program_id(0)
        ab = a_ref[...].astype(BF16)
        z = _dot(ab, w_ref[...]) + b_ref[...]
        dla = jnp.concatenate([df_ref[...], db_ref[...]], axis=1)
        dz = dla * _sig(-z) * (1.0 / GATE_NORM)
        dzb = dz.astype(BF16)
        dalr_ref[...] = _dot_nt(dzb, w_ref[...]).astype(BF16)
        gw = _dot_tn(ab, dzb)
        gb = jnp.sum(dz, axis=0, keepdims=True)

        @pl.when(i == 0)
        def _():
            gw_ref[...] = gw
            gb_ref[...] = gb

        @pl.when(i > 0)
        def _():
            gw_ref[...] += gw
            gb_ref[...] += gb

    return pl.pallas_call(
        body, name=name, grid=(T // rb,),
        in_specs=[_row(rb, DK), _row(rb, DK), _row(rb, ALR_PAD), pl.BlockSpec((ALR_PAD, W), lambda i: (0, 0)), _vec(W)],
        out_specs=[_row(rb, ALR_PAD), pl.BlockSpec((ALR_PAD, W), lambda i: (0, 0)), _vec(W)],
        out_shape=[jax.ShapeDtypeStruct((T, ALR_PAD), BF16), jax.ShapeDtypeStruct((ALR_PAD, W), F32),
                   jax.ShapeDtypeStruct((1, W), F32)],
        compiler_params=_cparams(("arbitrary",)),
    )(dla_f, dla_b, alr, wdu, bdu)


def _tri(rev):
    m = np.tril(np.ones((CHUNK, CHUNK), np.float32))
    return jnp.asarray(m.T if rev else m, BF16)


def _gla_chunk_common(q_ref, k_ref, v_ref, la_ref, tri_ref, rows, rev, scale_q):
    mid = CHUNK // 2 if rev else CHUNK // 2 - 1
    last_i = 0 if rev else CHUNK - 1
    q = q_ref[rows, :] * scale_q
    k = k_ref[rows, :]
    v = v_ref[rows, :]
    cum = _dot3(tri_ref[...], la_ref[rows, :])
    ref = cum[mid:mid + 1, :]
    last = cum[last_i:last_i + 1, :]
    e_q = jnp.exp(cum - ref)
    e_k = jnp.exp(ref - cum)
    e_c = jnp.exp(cum)
    e_s = jnp.exp(last - cum)
    e_l = jnp.exp(last)
    ri = lax.broadcasted_iota(jnp.int32, (CHUNK, CHUNK), 0)
    ci = lax.broadcasted_iota(jnp.int32, (CHUNK, CHUNK), 1)
    mask = (ci >= ri) if rev else (ci <= ri)
    return q, k, v, e_q, e_k, e_c, e_s, e_l, mask, last_i


def _gla_specs(rb, hk, hv, D, rbmap, la_col0):
    q_col0 = 4 * D // hk
    k_col0 = q_col0 + N_HEADS
    return [
        pl.BlockSpec((rb, hk), lambda h, i: (rbmap(i), q_col0 + h)),
        pl.BlockSpec((rb, hk), lambda h, i: (rbmap(i), k_col0 + h)),
        pl.BlockSpec((rb, hv), lambda h, i: (rbmap(i), h)),
        pl.BlockSpec((rb, hk), lambda h, i: (rbmap(i), la_col0 + h)),
        pl.BlockSpec((CHUNK, CHUNK), lambda h, i: (0, 0)),
    ]


def _gla_fwd(proj, la, rev, rb, D, name):
    T = proj.shape[0]
    nb = T // rb
    ncb = rb // CHUNK
    hk, hv = D // 2 // N_HEADS, D // N_HEADS
    scale_q = float(hk) ** -0.5
    rbmap = (lambda i: jnp.where(i == 0, 0, nb - i)) if rev else (lambda i: i)

    def body(q_ref, k_ref, v_ref, la_ref, tri_ref, o_ref, s_ref, st_ref):
        @pl.when(pl.program_id(1) == 0)
        def _():
            st_ref[...] = jnp.zeros_like(st_ref)

        order = range(ncb - 1, -1, -1) if rev else range(ncb)
        for cc in order:
            rows = slice(cc * CHUNK, (cc + 1) * CHUNK)
            q, k, v, e_q, e_k, e_c, e_s, e_l, mask, _ = _gla_chunk_common(
                q_ref, k_ref, v_ref, la_ref, tri_ref, rows, rev, scale_q)
            vb = v.astype(BF16)
            a = jnp.where(mask, _dot_nt((q * e_q).astype(BF16), (k * e_k).astype(BF16)), 0.0)
            st = st_ref[...]
            s_ref[cc] = st
            o = _dot(a.astype(BF16), vb) + _dot_nt((q * e_c).astype(BF16), st.astype(BF16))
            o_ref[rows, :] = o
            st_ref[...] = st * e_l + _dot_tn(vb, (k * e_s).astype(BF16))

    la_col0 = N_HEADS if rev else 0
    return pl.pallas_call(
        body, name=name, grid=(N_HEADS, nb),
        in_specs=_gla_specs(rb, hk, hv, D, rbmap, la_col0),
        out_specs=[pl.BlockSpec((rb, hv), lambda h, i: (rbmap(i), h)),
                   pl.BlockSpec((None, ncb, hv, hk), lambda h, i: (h, rbmap(i), 0, 0))],
        out_shape=[jax.ShapeDtypeStruct((T, D), F32),
                   jax.ShapeDtypeStruct((N_HEADS, T // CHUNK, hv, hk), F32)],
        scratch_shapes=[pltpu.VMEM((hv, hk), F32)],
        compiler_params=_cparams(("parallel", "arbitrary")),
    )(proj, proj, proj, la, _tri(rev))


def _gla_bwd(proj, la, do, states, rev, rb, D, prev, name):
    T = proj.shape[0]
    nb = T // rb
    ncb = rb // CHUNK
    hk, hv = D // 2 // N_HEADS, D // N_HEADS
    DK = D // 2
    scale_q = float(hk) ** -0.5
    if rev:
        rbmap = lambda i: jnp.where(i == nb - 1, 0, i + 1)
    else:
        rbmap = lambda i: nb - 1 - i
    has_prev = prev is not None
    out_dt = BF16 if has_prev else F32

    def body(*refs):
        q_ref, k_ref, v_ref, la_ref, tri_ref, trit_ref, do_ref, s_ref = refs[:8]
        n_in = 11 if has_prev else 8
        pq_ref, pk_ref, pv_ref = refs[8:11] if has_prev else (None, None, None)
        dq_ref, dk_ref, dv_ref, dla_ref, ds_ref = refs[n_in:]

        @pl.when(pl.program_id(1) == 0)
        def _():
            ds_ref[...] = jnp.zeros_like(ds_ref)

        order = range(ncb) if rev else range(ncb - 1, -1, -1)
        for cc in order:
            rows = slice(cc * CHUNK, (cc + 1) * CHUNK)
            q, k, v, e_q, e_k, e_c, e_s, e_l, mask, last_i = _gla_chunk_common(
                q_ref, k_ref, v_ref, la_ref, tri_ref, rows, rev, scale_q)
            vb = v.astype(BF16)
            qi = (q * e_q).astype(BF16)
            ki = (k * e_k).astype(BF16)
            qc = (q * e_c).astype(BF16)
            ks = (k * e_s).astype(BF16)
            a = jnp.where(mask, _dot_nt(qi, ki), 0.0).astype(BF16)
            st0 = s_ref[cc]
            st0b = st0.astype(BF16)
            dst1 = ds_ref[...]
            dst1b = dst1.astype(BF16)
            dob = do_ref[rows, :].astype(BF16)
            da = jnp.where(mask, _dot_nt(dob, vb), 0.0).astype(BF16)
            dv = _dot_tn(a, dob) + _dot_nt(ks, dst1b)
            dq_inter = _dot(dob, st0b) * e_c
            dk_inter = _dot(vb, dst1b) * e_s
            dq_s = _dot(da, ki) * e_q + dq_inter
            dk = _dot_tn(da, qi) * e_k + dk_inter
            extra = (jnp.sum(k * dk_inter, axis=0, keepdims=True)
                     + e_l * jnp.sum(dst1 * st0, axis=0, keepdims=True))
            rowi = lax.broadcasted_iota(jnp.int32, (CHUNK, hk), 0)
            dcum = q * dq_s - k * dk + jnp.where(rowi == last_i, extra, 0.0)
            dla_ref[rows, :] = _dot3(trit_ref[...], dcum)
            dq = dq_s * scale_q
            if has_prev:
                dq = dq + pq_ref[rows, :]
                dk = dk + pk_ref[rows, :]
                dv = dv + pv_ref[rows, :]
            dq_ref[rows, :] = dq.astype(out_dt)
            dk_ref[rows, :] = dk.astype(out_dt)
            dv_ref[rows, :] = dv.astype(out_dt)
            ds_ref[...] = dst1 * e_l + _dot_tn(dob, qc)

    la_col0 = N_HEADS if rev else 0
    in_specs = _gla_specs(rb, hk, hv, D, rbmap, la_col0)
    in_specs += [pl.BlockSpec((CHUNK, CHUNK), lambda h, i: (0, 0)),
                 pl.BlockSpec((rb, hv), lambda h, i: (rbmap(i), h)),
                 pl.BlockSpec((None, ncb, hv, hk), lambda h, i: (h, rbmap(i), 0, 0))]
    args = [proj, proj, proj, la, _tri(rev), _tri(not rev), do, states]
    hk_spec = pl.BlockSpec((rb, hk), lambda h, i: (rbmap(i), h))
    hv_spec = pl.BlockSpec((rb, hv), lambda h, i: (rbmap(i), h))
    if has_prev:
        in_specs += [hk_spec, hk_spec, hv_spec]
        args += list(prev)
    return pl.pallas_call(
        body, name=name, grid=(N_HEADS, nb), in_specs=in_specs,
        out_specs=[hk_spec, hk_spec, hv_spec, hk_spec],
        out_shape=[jax.ShapeDtypeStruct((T, DK), out_dt), jax.ShapeDtypeStruct((T, DK), out_dt),
                   jax.ShapeDtypeStruct((T, D), out_dt), jax.ShapeDtypeStruct((T, DK), F32)],
        scratch_shapes=[pltpu.VMEM((hv, hk), F32)],
        compiler_params=_cparams(("parallel", "arbitrary")),
    )(*args)


def _glaout_f(of, ob, g, gain):
    o = of + ob
    n = o * lax.rsqrt(jnp.mean(o * o, axis=-1, keepdims=True) + RMS_EPS)
    return n * gain * _silu(g)


def _glaout_fwd(o_f, o_b, proj, gain, rb, name):
    T, D = o_f.shape
    hv = D // N_HEADS

    def body(of_ref, ob_ref, g_ref, gn_ref, u_ref):
        for h in range(N_HEADS):
            cs = slice(h * hv, (h + 1) * hv)
            u_ref[:, cs] = _glaout_f(of_ref[:, cs], ob_ref[:, cs], g_ref[:, cs], gn_ref[:, cs]).astype(BF16)

    return pl.pallas_call(
        body, name=name, grid=(T // rb,),
        in_specs=[_row(rb, D), _row(rb, D), _row(rb, D, 1), _vec(D)],
        out_specs=_row(rb, D), out_shape=jax.ShapeDtypeStruct((T, D), BF16),
        compiler_params=_cparams(("parallel",)),
    )(o_f, o_b, proj, gain)


def _glaout_bwd(du, o_f, o_b, proj, gain, rb, name):
    T, D = o_f.shape
    hv = D // N_HEADS

    def body(du_ref, of_ref, ob_ref, g_ref, gn_ref, do_ref, dg_ref, dgn_ref, tmp_ref):
        for h in range(N_HEADS):
            cs = slice(h * hv, (h + 1) * hv)
            _, vjp = jax.vjp(_glaout_f, of_ref[:, cs], ob_ref[:, cs], g_ref[:, cs], gn_ref[:, cs])
            d_of, _, dg, dgn = vjp(du_ref[:, cs])
            do_ref[:, cs] = d_of
            dg_ref[:, cs] = dg.astype(BF16)
            tmp_ref[:, cs] = dgn
        _accum(dgn_ref, tmp_ref[...])

    return pl.pallas_call(
        body, name=name, grid=(T // rb,),
        in_specs=[_row(rb, D), _row(rb, D), _row(rb, D), _row(rb, D, 1), _vec(D)],
        out_specs=[_row(rb, D), _row(rb, D), _acc2(D)],
        out_shape=[jax.ShapeDtypeStruct((T, D), F32), jax.ShapeDtypeStruct((T, D), BF16), _acc_shape(D)],
        scratch_shapes=[pltpu.VMEM((1, D), F32)],
        compiler_params=_cparams(("arbitrary",)),
    )(du, o_f, o_b, proj, gain)


def _merge_f(bg1, bg2, yg, yp):
    return _sig(bg1) * yg + _sig(bg2) * yp


def _merge_fwd(proj, y_gla, y_pool, rb, name):
    T, D = y_gla.shape

    def body(b1_ref, b2_ref, yg_ref, yp_ref, m_ref):
        m_ref[...] = _merge_f(b1_ref[...], b2_ref[...], yg_ref[...], yp_ref[...]).astype(BF16)

    return pl.pallas_call(
        body, name=name, grid=(T // rb,),
        in_specs=[_row(rb, D, 2), _row(rb, D, 3), _row(rb, D), _row(rb, D)],
        out_specs=_row(rb, D), out_shape=jax.ShapeDtypeStruct((T, D), BF16),
        compiler_params=_cparams(("parallel",)),
    )(proj, proj, y_gla, y_pool)


def _merge_bwd(dm, proj, y_gla, y_pool, rb, name):
    T, D = y_gla.shape

    def body(dm_ref, b1_ref, b2_ref, yg_ref, yp_ref, d1_ref, d2_ref, dyg_ref, dyp_ref):
        _, vjp = jax.vjp(_merge_f, b1_ref[...], b2_ref[...], yg_ref[...], yp_ref[...])
        d1, d2, dyg, dyp = vjp(dm_ref[...])
        d1_ref[...] = d1.astype(BF16)
        d2_ref[...] = d2.astype(BF16)
        dyg_ref[...] = dyg.astype(BF16)
        dyp_ref[...] = dyp.astype(BF16)

    return pl.pallas_call(
        body, name=name, grid=(T // rb,),
        in_specs=[_row(rb, D), _row(rb, D, 2), _row(rb, D, 3), _row(rb, D), _row(rb, D)],
        out_specs=[_row(rb, D)] * 4, out_shape=[jax.ShapeDtypeStruct((T, D), BF16)] * 4,
        compiler_params=_cparams(("parallel",)),
    )(dm, proj, proj, y_gla, y_pool)


def _swiglu_f(gate, up):
    return _silu(gate) * up


def _pool_consts(ctx_len, seq):
    rows = seq // GRID_W
    reps = POOL_TB // GRID_W
    mw, bc, cw, ch, cc = [], [], [], [], []
    for w in POOL_WINDOWS:
        lo, hi = w // 2, w - w // 2 - 1

        def band(n):
            i = np.arange(n)[:, None]
            j = np.arange(n)[None, :]
            return ((j - i >= -lo) & (j - i <= hi)).astype(np.float32)

        def count(n):
            i = np.arange(n)
            return (np.minimum(i + hi + 1, n) - np.maximum(i - lo, 0)).astype(np.float32)

        mw.append(np.kron(np.eye(reps, dtype=np.float32), band(GRID_W)))
        bc.append(band(ctx_len))
        cw.append(np.tile(count(GRID_W), reps)[:, None])
        ch.append(np.repeat(count(rows), GRID_W)[:, None])
        cc.append(count(ctx_len)[:, None])
    mw, bc = np.stack(mw), np.stack(bc)
    return dict(
        mw=jnp.asarray(mw, BF16), mwt=jnp.asarray(mw.transpose(0, 2, 1), BF16),
        bc=jnp.asarray(bc, BF16), bct=jnp.asarray(bc.transpose(0, 2, 1), BF16),
        cw=jnp.asarray(np.stack(cw)), ch=jnp.asarray(np.stack(ch)), cc=jnp.asarray(np.stack(cc)))


def _gspec(*shape):
    nd = len(shape)
    return pl.BlockSpec((None,) + tuple(shape), lambda g: (g,) + (0,) * nd)


def _pool_fwd(proj, pc, wg, scale, ctx_len, D, name):
    T = proj.shape[0]
    seq = T - ctx_len
    dp = D // 2
    pg = dp // len(POOL_WINDOWS)
    nblk = seq // POOL_TB
    padt = POOL_PAD_ROWS * GRID_W
    p_col0 = 5 * D // pg

    def body(p_ref, mw_ref, bc_ref, cw_ref, ch_ref, cc_ref, wg_ref, sc_ref, pd_ref, y0_ref, r_ref, pad_ref):
        g = pl.program_id(0)

        def tail(rows, mean, x):
            pdb = (mean - x).astype(BF16)
            y0 = _dot(pdb, wg_ref[...])
            pd_ref[rows, :] = pdb
            y0_ref[rows, :] = y0
            r_ref[rows, :] = (y0 * sc_ref[...]).astype(BF16)

        xc = p_ref[0:ctx_len, :]
        tail(slice(0, ctx_len), _dot2(bc_ref[...], xc) / cc_ref[...], xc)

        pad_ref[0:padt, :] = jnp.zeros((padt, pg), F32)
        pad_ref[padt + seq:, :] = jnp.zeros((padt, pg), F32)

        def wpass(b, carry):
            rows = pl.ds(pl.multiple_of(ctx_len + b * POOL_TB, CHUNK), POOL_TB)
            dst = pl.ds(pl.multiple_of(padt + b * POOL_TB, CHUNK), POOL_TB)
            pad_ref[dst, :] = _dot2(mw_ref[...], p_ref[rows, :]) / cw_ref[...]
            return carry

        lax.fori_loop(0, nblk, wpass, 0)

        for gi, w in enumerate(POOL_WINDOWS):
            lo, hi = w // 2, w - w // 2 - 1

            @pl.when(g == gi)
            def _():
                def hpass(b, carry):
                    acc = jnp.zeros((POOL_TB, pg), F32)
                    for d in range(-lo, hi + 1):
                        src = pl.ds(pl.multiple_of(padt + b * POOL_TB + d * GRID_W, CHUNK), POOL_TB)
                        acc = acc + pad_ref[src, :]
                    mean = acc / ch_ref[pl.ds(pl.multiple_of(b * POOL_TB, CHUNK), POOL_TB), :]
                    rows = pl.ds(pl.multiple_of(ctx_len + b * POOL_TB, CHUNK), POOL_TB)
                    tail(rows, mean, p_ref[rows, :])
                    return carry

                lax.fori_loop(0, nblk, hpass, 0)

    col = lambda g: (0, g)
    return pl.pallas_call(
        body, name=name, grid=(len(POOL_WINDOWS),),
        in_specs=[pl.BlockSpec((T, pg), lambda g: (0, p_col0 + g)),
                  _gspec(POOL_TB, POOL_TB), _gspec(ctx_len, ctx_len), _gspec(POOL_TB, 1), _gspec(seq, 1),
                  _gspec(ctx_len, 1), _gspec(pg, pg), pl.BlockSpec((1, pg), col)],
        out_specs=[pl.BlockSpec((T, pg), col)] * 3,
        out_shape=[jax.ShapeDtypeStruct((T, dp), BF16), jax.ShapeDtypeStruct((T, dp), F32),
                   jax.ShapeDtypeStruct((T, dp), BF16)],
        scratch_shapes=[pltpu.VMEM((seq + 2 * padt, pg), F32)],
        compiler_params=_cparams(("arbitrary",)),
    )(proj, pc["mw"], pc["bc"], pc["cw"], pc["ch"], pc["cc"], wg, scale)


def _pool_bwd(dr, y0, pd, pc, wg, scale, ctx_len, D, name):
    T = dr.shape[0]
    seq = T - ctx_len
    dp = D // 2
    ng = len(POOL_WINDOWS)
    pg = dp // ng
    nblk = seq // POOL_TB
    padt = POOL_PAD_ROWS * GRID_W

    def body(dr_ref, y0_ref, pd_ref, mwt_ref, bct_ref, cw_ref, ch_ref, cc_ref, wg_ref, sc_ref,
             dp_ref, dsc_ref, gwg_ref, pad_ref, dpd_ref):
        g = pl.program_id(0)
        dsc_ref[...] = jnp.zeros_like(dsc_ref)
        gwg_ref[...] = jnp.zeros_like(gwg_ref)

        def head(rows):
            drv = dr_ref[rows, :]
            dsc_ref[...] += jnp.sum(drv * y0_ref[rows, :], axis=0, keepdims=True)
            dy0 = (drv * sc_ref[...]).astype(BF16)
            gwg_ref[...] += _dot_tn(pd_ref[rows, :], dy0)
            return _dot_nt(dy0, wg_ref[...])

        crow = slice(0, ctx_len)
        dpd_c = head(crow)
        dp_ref[crow, :] = (_dot2(bct_ref[...], dpd_c / cc_ref[...]) - dpd_c).astype(BF16)

        pad_ref[0:padt, :] = jnp.zeros((padt, pg), F32)
        pad_ref[padt + seq:, :] = jnp.zeros((padt, pg), F32)

        def first(b, carry):
            rows = pl.ds(pl.multiple_of(ctx_len + b * POOL_TB, CHUNK), POOL_TB)
            lrows = pl.ds(pl.multiple_of(b * POOL_TB, CHUNK), POOL_TB)
            dst = pl.ds(pl.multiple_of(padt + b * POOL_TB, CHUNK), POOL_TB)
            dpd = head(rows)
            dpd_ref[lrows, :] = dpd
            pad_ref[dst, :] = dpd / ch_ref[lrows, :]
            return carry

        lax.fori_loop(0, nblk, first, 0)

        for gi, w in enumerate(POOL_WINDOWS):
            lo, hi = w // 2, w - w // 2 - 1

            @pl.when(g == gi)
            def _():
                def second(b, carry):
                    acc = jnp.zeros((POOL_TB, pg), F32)
                    for d in range(-hi, lo + 1):
                        src = pl.ds(pl.multiple_of(padt + b * POOL_TB + d * GRID_W, CHUNK), POOL_TB)
                        acc = acc + pad_ref[src, :]
                    rows = pl.ds(pl.multiple_of(ctx_len + b * POOL_TB, CHUNK), POOL_TB)
                    lrows = pl.ds(pl.multiple_of(b * POOL_TB, CHUNK), POOL_TB)
                    dx = _dot2(mwt_ref[...], acc / cw_ref[...]) - dpd_ref[lrows, :]
                    dp_ref[rows, :] = dx.astype(BF16)
                    return carry

                lax.fori_loop(0, nblk, second, 0)

    col = lambda g: (0, g)
    return pl.pallas_call(
        body, name=name, grid=(ng,),
        in_specs=[pl.BlockSpec((T, pg), col), pl.BlockSpec((T, pg), col), pl.BlockSpec((T, pg), col),
                  _gspec(POOL_TB, POOL_TB), _gspec(ctx_len, ctx_len), _gspec(POOL_TB, 1), _gspec(seq, 1),
                  _gspec(ctx_len, 1), _gspec(pg, pg), pl.BlockSpec((1, pg), col)],
        out_specs=[pl.BlockSpec((T, pg), col), pl.BlockSpec((1, pg), col), _gspec(pg, pg)],
        out_shape=[jax.ShapeDtypeStruct((T, dp), BF16), jax.ShapeDtypeStruct((1, dp), F32),
                   jax.ShapeDtypeStruct((ng, pg, pg), F32)],
        scratch_shapes=[pltpu.VMEM((seq + 2 * padt, pg), F32), pltpu.VMEM((seq, pg), F32)],
        compiler_params=_cparams(("arbitrary",)),
    )(dr, y0, pd, pc["mwt"], pc["bct"], pc["cw"], pc["ch"], pc["cc"], wg, scale)


def _loss_head(x2, target, rb, name):
    T, D = x2.shape

    def body(y_ref, t_ref, dy_ref, l_ref):
        i = pl.program_id(0)

        @pl.when(i == 0)
        def _():
            dy_ref[...] = jnp.zeros_like(dy_ref)
            l_ref[...] = jnp.zeros_like(l_ref)

        @pl.when(i > 0)
        def _():
            e = y_ref[...] - t_ref[...]
            dy_ref[...] = e * (1.0 / D)
            l_ref[...] += 0.5 * jnp.sum(jnp.mean(e * e, axis=-1, keepdims=True), axis=0, keepdims=True)

    return pl.pallas_call(
        body, name=name, grid=(T // rb,),
        in_specs=[_row(rb, D), pl.BlockSpec((rb, D), lambda i: (jnp.maximum(i - 1, 0), 0))],
        out_specs=[_row(rb, D), pl.BlockSpec((8, 128), lambda i: (0, 0))],
        out_shape=[jax.ShapeDtypeStruct((T, D), F32), jax.ShapeDtypeStruct((8, 128), F32)],
        compiler_params=_cparams(("arbitrary",)),
    )(x2, target)


def _sum_lead(x, name):
    S, R, C = x.shape

    def body(x_ref, o_ref):
        acc = x_ref[0]
        for s in range(1, S):
            acc = acc + x_ref[s]
        o_ref[...] = acc

    return pl.pallas_call(
        body, name=name, out_shape=jax.ShapeDtypeStruct((R, C), F32),
        compiler_params=_cparams(),
    )(x)


def _silu_rows(cond, name):
    def body(c_ref, o_ref):
        o_ref[...] = _silu(c_ref[...]).astype(BF16)

    return pl.pallas_call(body, name=name, out_shape=jax.ShapeDtypeStruct(cond.shape, BF16),
                          compiler_params=_cparams())(cond)


def _silu_grad(cond, ds, name):
    def body(c_ref, d_ref, o_ref):
        _, vjp = jax.vjp(_silu, c_ref[...])
        o_ref[...] = vjp(d_ref[...])[0]

    return pl.pallas_call(body, name=name, out_shape=jax.ShapeDtypeStruct(cond.shape, F32),
                          compiler_params=_cparams())(cond, ds)


def _adamw_math(g, w_ref, m_ref, v_ref, go_ref, d_ref, mo_ref, vo_ref):
    c1 = 1.0 / (1.0 - ADAM_B1 ** ADAM_STEP)
    c2 = 1.0 / (1.0 - ADAM_B2 ** ADAM_STEP)
    mn = ADAM_B1 * m_ref[...] + (1.0 - ADAM_B1) * g
    vn = ADAM_B2 * v_ref[...] + (1.0 - ADAM_B2) * (g * g)
    go_ref[...] = g
    mo_ref[...] = mn
    vo_ref[...] = vn
    d_ref[...] = -ADAM_LR * ((mn * c1) / (jnp.sqrt(vn * c2) + ADAM_EPS) + ADAM_WD * w_ref[...])


def _adamw(gs, w, m, v, name):
    S, R, C = gs.shape
    cpad = -(-C // 128) * 128
    rt = _pick(R, max(16, (1 << 20) // (4 * cpad)), 16)

    def body(g_ref, w_ref, m_ref, v_ref, go_ref, d_ref, mo_ref, vo_ref):
        g = g_ref[0].astype(F32)
        for s in range(1, S):
            g = g + g_ref[s].astype(F32)
        _adamw_math(g, w_ref, m_ref, v_ref, go_ref, d_ref, mo_ref, vo_ref)

    blk = pl.BlockSpec((rt, C), lambda i: (i, 0))
    return pl.pallas_call(
        body, name=name, grid=(R // rt,),
        in_specs=[pl.BlockSpec((S, rt, C), lambda i: (0, i, 0)), blk, blk, blk],
        out_specs=[blk] * 4, out_shape=[jax.ShapeDtypeStruct((R, C), F32)] * 4,
        compiler_params=_cparams(("parallel",)),
    )(gs, w, m, v)


def _adamw_layer(gs, w, m, v, l, prev, name):
    S, R, C = gs.shape
    L = w.shape[0]
    cpad = -(-C // 128) * 128
    rt = _pick(R, max(16, (1 << 20) // (4 * cpad)), 16)

    def body(*refs):
        g_ref, w_ref, m_ref, v_ref = refs[:4]
        go_ref, d_ref, mo_ref, vo_ref = refs[-4:]
        g = g_ref[0].astype(F32)
        for s in range(1, S):
            g = g + g_ref[s].astype(F32)
        _adamw_math(g, w_ref, m_ref, v_ref, go_ref, d_ref, mo_ref, vo_ref)

    blk = pl.BlockSpec((None, rt, C), lambda i: (l, i, 0))
    in_specs = [pl.BlockSpec((S, rt, C), lambda i: (0, i, 0)), blk, blk, blk]
    args = [gs, w, m, v]
    aliases = {}
    if prev is not None:
        in_specs += [pl.BlockSpec(memory_space=pl.ANY)] * 4
        args += list(prev)
        aliases = {4 + q: q for q in range(4)}
    return pl.pallas_call(
        body, name=name, grid=(R // rt,), in_specs=in_specs,
        out_specs=[blk] * 4, out_shape=[jax.ShapeDtypeStruct((L, R, C), F32)] * 4,
        input_output_aliases=aliases,
        compiler_params=_cparams(("parallel",)),
    )(*args)


def _adamw_nd(gs, w, m, v, name):
    shp = w.shape
    if len(shp) == 1:
        r, c = 1, shp[0]
    else:
        r, c = int(np.prod(shp[:-1])), shp[-1]
    outs = _adamw(gs.reshape(gs.shape[0], r, c), w.reshape(r, c), m.reshape(r, c), v.reshape(r, c), name)
    return [o.reshape(shp) for o in outs]


def kernel(x, c, ctx, c_ctx, w_ada, b_ada, w_in, w_decay_up, b_decay_up, gla_norm_gain, w_pool_group, pool_scale, w_gla_out, w_pool_out, w_out, ln_mix_gain, ln_mix_bias, w_ffn_in, w_ffn_out, ln_ffn_gain, ln_ffn_bias, loss_target, m_c_ctx, m_w_ada, m_b_ada, m_w_in, m_w_decay_up, m_b_decay_up, m_gla_norm_gain, m_w_pool_group, m_pool_scale, m_w_gla_out, m_w_pool_out, m_w_out, m_ln_mix_gain, m_ln_mix_bias, m_w_ffn_in, m_w_ffn_out, m_ln_ffn_gain, m_ln_ffn_bias, v_c_ctx, v_w_ada, v_b_ada, v_w_in, v_w_decay_up, v_b_decay_up, v_gla_norm_gain, v_w_pool_group, v_pool_scale, v_w_gla_out, v_w_pool_out, v_w_out, v_ln_mix_gain, v_ln_mix_bias, v_w_ffn_in, v_w_ffn_out, v_ln_ffn_gain, v_ln_ffn_bias):
    L, D = w_ada.shape[0], w_ada.shape[1]
    seq, ctx_len = x.shape[1], ctx.shape[1]
    T = seq + ctx_len
    rb = ctx_len
    DK = D // 2
    DP = D // 2
    ng = len(POOL_WINDOWS)
    pg = DP // ng
    dff = w_ffn_out.shape[1] * N_DEV
    alpha = (2.0 * L) ** 0.25
    assert seq % rb == 0 and rb % CHUNK == 0 and seq % POOL_TB == 0 and ctx_len % 8 == 0
    xi, yi, ci = _my_pos()
    me = 4 * xi + 2 * yi + ci
    pc = _pool_consts(ctx_len, seq)

    shards = dict(w_in=w_in.astype(BF16), go=w_gla_out.astype(BF16), po=w_pool_out.astype(BF16),
                  out=w_out.astype(BF16), fi=w_ffn_in.astype(BF16), fo=w_ffn_out.astype(BF16),
                  pg=w_pool_group.astype(BF16).reshape(L, ng * pg // N_DEV, pg))
    wkeys = ("w_in", "go", "po", "out", "fi", "fo", "pg")
    o_q, o_k, o_v, o_g, o_a = 0, DK, 2 * DK, 2 * DK + D, 2 * DK + 2 * D
    o_p, o_bg = o_a + 2 * GATE_RANK, o_a + 2 * GATE_RANK + DP

    def prepared(gw):
        w_in_f = jnp.swapaxes(gw["w_in"], 0, 1).reshape(D, -1)
        main = jnp.concatenate([w_in_f[:, o_v:o_g], w_in_f[:, o_g:o_a], w_in_f[:, o_bg:],
                                w_in_f[:, o_q:o_k], w_in_f[:, o_k:o_v], w_in_f[:, o_p:o_bg]], axis=1)
        alr_w = jnp.pad(w_in_f[:, o_a:o_p], ((0, 0), (0, ALR_PAD - 2 * GATE_RANK)))
        pgf = jnp.swapaxes(gw["pg"].reshape(N_DEV, ng, pg // N_DEV, pg), 0, 1).reshape(ng, pg, pg)
        return dict(main=main, alr=alr_w, go=gw["go"].reshape(D, D), po=gw["po"], out=gw["out"].reshape(D, D),
                    fi=gw["fi"][None], fo=gw["fo"].reshape(1, dff, D), pg=pgf)

    def gather_next(fn, names, l, nxt):
        if l + 1 >= L:
            return fn(None)
        res, outs = fn(_gather_job([shards[k] for k in names], l + 1))
        nxt.update(zip(names, outs))
        return res

    gathered = dict(zip(wkeys, _run_job(_gather_job([shards[k] for k in wkeys], 0), "ag_layer0")))

    dku = w_decay_up.shape[-1]
    small_in = jnp.concatenate([c.reshape(-1), w_decay_up.reshape(-1), b_decay_up.reshape(-1)])
    (small_all,) = _gather_flat([small_in], "ag_small")
    c_all = small_all[:, :D]
    n_wdu = L * 2 * GATE_RANK * dku
    wdu_all = small_all[:, D:D + n_wdu].reshape(N_DEV, L, 2, GATE_RANK, dku)
    wdu_full = jnp.transpose(wdu_all, (1, 2, 3, 0, 4)).reshape(L, 2, GATE_RANK, DK)
    bdu_all = small_all[:, D + n_wdu:].reshape(N_DEV, L, 2, dku)
    bdu_full = jnp.transpose(bdu_all, (1, 2, 0, 3)).reshape(L, 1, 2 * DK)
    wdu_bd = jnp.zeros((L, ALR_PAD, 2 * DK), F32)
    wdu_bd = wdu_bd.at[:, :GATE_RANK, :DK].set(wdu_full[:, 0])
    wdu_bd = wdu_bd.at[:, GATE_RANK:2 * GATE_RANK, DK:].set(wdu_full[:, 1]).astype(BF16)

    ncond = 16
    cond = jnp.concatenate([c_all, c_ctx.reshape(1, D), jnp.zeros((ncond - N_DEV - 1, D), F32)], axis=0)
    s_cond = _silu_rows(cond, "silu_cond")
    wsh = w_ada.shape[-1]
    b_ada_mine = lax.dynamic_slice_in_dim(b_ada, me * wsh, wsh, axis=1)
    mod_part = jnp.stack([_mm(s_cond, w_ada, "nn", F32, "mod_mm", bias=b_ada_mine[l:l + 1], b_pre=(l,))
                          for l in range(L)])
    (mod_all,) = _all_gather([mod_part], "ag_mod")
    mod_all = jnp.swapaxes(mod_all, 1, 2).reshape(L, ncond, N_MOD * D)
    mod_lat = lax.dynamic_slice_in_dim(mod_all, me, 1, axis=1)
    mods = jnp.concatenate([mod_all[:, N_DEV:N_DEV + 1], mod_lat], axis=1).reshape(L, 2, 1, N_MOD * D)
    SH_M, SC_M, GT_M, SH_F, SC_F, GT_F = range(N_MOD)

    xa = jnp.concatenate([ctx[0], x[0]], axis=0)
    vec = lambda a, l: a[l].reshape(1, -1)
    saved = []
    h = _mod_fwd(xa, mods[0], SC_M, SH_M, rb, "mod_fwd")
    weights = []
    for l in range(L):
        W = prepared(gathered)
        weights.append(W)
        gathered = {}
        proj = gather_next(lambda j: _mm(h, W["main"], "nn", F32, "mm_in", job=j), ["w_in"], l, gathered)
        alr = _mm(h, W["alr"], "nn", F32, "mm_alr")
        la = _decay_fwd(alr, wdu_bd[l], bdu_full[l], rb, "decay_fwd")
        o_f, s_f = _gla_fwd(proj, la, False, rb, D, "gla_fwd_f")
        o_b, s_b = _gla_fwd(proj, la, True, rb, D, "gla_fwd_b")
        u = _glaout_fwd(o_f, o_b, proj, vec(gla_norm_gain, l), rb, "glaout_fwd")
        y_gla = gather_next(lambda j: _mm(u, W["go"], "nn", F32, "mm_go", job=j), ["go"], l, gathered)
        pd, y0, r = _pool_fwd(proj, pc, W["pg"], vec(pool_scale, l), ctx_len, D, "pool_fwd")
        y_pool = gather_next(lambda j: _mm(r, W["po"], "nn", F32, "mm_po", b_shard=True, job=j), ["po", "pg"], l,
                             gathered)
        m_ = _merge_fwd(proj, y_gla, y_pool, rb, "merge_fwd")
        mix = gather_next(lambda j: _mm(m_, W["out"], "nn", F32, "mm_out", job=j), ["out"], l, gathered)
        x1, h2 = _unit_fwd(alpha, xa, mix, mods[l], GT_M, vec(ln_mix_gain, l), vec(ln_mix_bias, l),
                           (mods[l], SC_F, SH_F), rb, "unit_mix_fwd")
        ff, s_ = gather_next(lambda j: _ffn_in_fwd(h2, W["fi"], 0, "mm_fi_swiglu", job=j), ["fi"], l, gathered)
        ffn = gather_next(lambda j: _mm(s_, W["fo"], "nn", F32, "mm_fo", b_pre=(0,), job=j), ["fo"], l, gathered)
        nxt = (mods[l + 1], SC_M, SH_M) if l + 1 < L else None
        x2, h_next = _unit_fwd(alpha, x1, ffn, mods[l], GT_F, vec(ln_ffn_gain, l), vec(ln_ffn_bias, l),
                               nxt, rb, "unit_ffn_fwd")
        saved.append(dict(xa=xa, h=h, proj=proj, alr=alr, la=la, o_f=o_f, o_b=o_b, s_f=s_f, s_b=s_b, u=u,
                          y_gla=y_gla, pd=pd, y0=y0, r=r, y_pool=y_pool, m=m_, mix=mix, x1=x1, h2=h2, ff=ff,
                          s=s_, ffn=ffn))
        xa, h = x2, h_next

    dxo, loss_part = _loss_head(xa, loss_target[0], rb, "loss_head")
    loss = lax.psum(loss_part[0, 0], ("x", "y", "c"))

    big_params = [("w_in", w_in, m_w_in, v_w_in), ("w_gla_out", w_gla_out, m_w_gla_out, v_w_gla_out),
                  ("w_pool_out", w_pool_out, m_w_pool_out, v_w_pool_out), ("w_out", w_out, m_w_out, v_w_out),
                  ("w_ffn_in", w_ffn_in, m_w_ffn_in, v_w_ffn_in), ("w_ffn_out", w_ffn_out, m_w_ffn_out, v_w_ffn_out),
                  ("w_pool_group", w_pool_group, m_w_pool_group, v_w_pool_group)]
    big_out = {nm: None for nm, _, _, _ in big_params}
    g_small = {k: [None] * L for k in ("gla_gain", "pool_scale", "mix_g", "mix_b", "ffn_g", "ffn_b", "wdu", "bdu")}
    dmods = [None] * L
    dh = None
    sum2 = lambda a: a[0] + a[1]
    rows8 = lambda g: g.reshape(N_DEV, g.shape[0] // N_DEV, g.shape[1])

    def apply_adamw(parts, layer):
        for (nm, w, m, v), gs in zip(big_params, parts):
            R, C = gs.shape[1], gs.shape[2]
            big_out[nm] = _adamw_layer(gs, w.reshape(L, R, C), m.reshape(L, R, C), v.reshape(L, R, C), layer,
                                       big_out[nm], "adamw_" + nm)

    pending = None
    arrived = {}

    def exchange_behind(fn, idxs):
        if pending is None:
            return fn(None)
        res, outs = fn(_chip_job([pending[i] for i in idxs]))
        arrived.update(zip(idxs, outs))
        return res

    for l in range(L - 1, -1, -1):
        sv = saved[l]
        W = weights[l]
        nxt = (mods[l + 1], SC_M, SH_M) if l + 1 < L else None
        dx1, dffn, d_gtf, d_gf, d_bf, d_scm_n, d_shm_n = _unit_bwd(
            alpha, dxo, dh, sv["x1"], sv["ffn"], mods[l], GT_F, vec(ln_ffn_gain, l), vec(ln_ffn_bias, l),
            nxt, rb, "unit_ffn_bwd")
        if nxt is not None:
            dmods[l + 1]["sc_m"], dmods[l + 1]["sh_m"] = d_scm_n, d_shm_n
        dmods[l] = dict(gt_f=d_gtf)
        g_small["ffn_g"][l], g_small["ffn_b"][l] = sum2(d_gf), sum2(d_bf)
        dff_ = exchange_behind(lambda j: _ffn_out_dx(dffn, W["fo"], 0, sv["ff"], "mm_fo_dx_swiglu", job=j),
                               [5, 1, 2, 3, 6])
        c_fo = rows8(_mm(sv["s"], dffn, "tn", BF16, "mm_fo_dw"))
        dh2 = exchange_behind(lambda j: _mm(dff_, W["fi"], "nt", F32, "mm_fi_dx", b_pre=(0,), b_shard=True,
                                            a_half=True, job=j), [4])
        c_fi = exchange_behind(lambda j: _mm(sv["h2"], dff_, "tn", BF16, "mm_fi_dw", b_half=True, out_shard=True,
                                             job=j), [0])
        if pending is not None:
            apply_adamw([arrived[i] for i in range(len(big_params))], l + 1)
        dxa, dmix, d_gtm, d_gm, d_bm, d_scf, d_shf = _unit_bwd(
            alpha, dx1, dh2, sv["xa"], sv["mix"], mods[l], GT_M, vec(ln_mix_gain, l), vec(ln_mix_bias, l),
            (mods[l], SC_F, SH_F), rb, "unit_mix_bwd")
        dmods[l].update(gt_m=d_gtm, sc_f=d_scf, sh_f=d_shf)
        g_small["mix_g"][l], g_small["mix_b"][l] = sum2(d_gm), sum2(d_bm)
        dm = _mm(dmix, W["out"], "nt", F32, "mm_out_dx")
        c_out = rows8(_mm(sv["m"], dmix, "tn", BF16, "mm_out_dw"))
        dbg1, dbg2, dyg, dyp = _merge_bwd(dm, sv["proj"], sv["y_gla"], sv["y_pool"], rb, "merge_bwd")
        dr = _mm(dyp, W["po"], "nt", F32, "mm_po_dx", b_shard=True)
        c_po = _mm(sv["r"], dyp, "tn", BF16, "mm_po_dw", out_shard=True)
        dp_, d_ps, g_pgl = _pool_bwd(dr, sv["y0"], sv["pd"], pc, W["pg"], vec(pool_scale, l), ctx_len, D, "pool_bwd")
        g_small["pool_scale"][l] = d_ps
        c_pg = jnp.swapaxes(g_pgl.astype(BF16).reshape(ng, N_DEV, pg // N_DEV, pg), 0, 1).reshape(N_DEV, -1, pg)
        du = _mm(dyg, W["go"], "nt", F32, "mm_go_dx")
        c_go = rows8(_mm(sv["u"], dyg, "tn", BF16, "mm_go_dw"))
        do, dg, d_gg = _glaout_bwd(du, sv["o_f"], sv["o_b"], sv["proj"], vec(gla_norm_gain, l), rb, "glaout_bwd")
        g_small["gla_gain"][l] = sum2(d_gg)
        dq_f, dk_f, dv_f, dla_f = _gla_bwd(sv["proj"], sv["la"], do, sv["s_f"], False, rb, D, None, "gla_bwd_f")
        dq, dk, dv, dla_b = _gla_bwd(sv["proj"], sv["la"], do, sv["s_b"], True, rb, D, (dq_f, dk_f, dv_f), "gla_bwd_b")
        dalr, g_wdu, g_bdu = _decay_bwd(dla_f, dla_b, sv["alr"], wdu_bd[l], bdu_full[l], rb, "decay_bwd")
        g_small["wdu"][l] = jnp.stack([g_wdu[:GATE_RANK, :DK], g_wdu[GATE_RANK:2 * GATE_RANK, DK:]])
        g_small["bdu"][l] = g_bdu.reshape(2, DK)
        dproj = jnp.concatenate([dv, dg, dbg1, dbg2, dq, dk, dp_], axis=1)
        dh_alr = _mm(dalr, W["alr"], "nt", F32, "mm_alr_dx")
        dh = _mm(dproj, W["main"], "nt", F32, "mm_in_dx", add=dh_alr)
        g_main = _mm(sv["h"], dproj, "tn", BF16, "mm_in_dw")
        g_alr = _mm(sv["h"], dalr, "tn", BF16, "mm_alr_dw")
        g_in = jnp.concatenate(
            [g_main[:, 4 * D:4 * D + DK], g_main[:, 4 * D + DK:5 * D], g_main[:, :D], g_main[:, D:2 * D],
             g_alr[:, :2 * GATE_RANK], g_main[:, 5 * D:], g_main[:, 2 * D:4 * D]], axis=1)
        c_in = jnp.swapaxes(g_in.reshape(D, N_DEV, -1), 0, 1)
        pending = _pair_sums([c_in, c_go, c_po, c_out, c_fi, c_fo, c_pg], "_l%d" % l)
        arrived = {}
        dxo = dxa
    apply_adamw(_run_job(_chip_job(pending), "rs_chips_l0"), 0)
    grad_xa, d_scm0, d_shm0 = _mod_bwd(dxo, dh, saved[0]["xa"], mods[0], SC_M, SH_M, rb, "mod_bwd")
    dmods[0]["sc_m"], dmods[0]["sh_m"] = d_scm0, d_shm0
    grad_x = grad_xa[ctx_len:].reshape(1, seq, D)

    order = ("sh_m", "sc_m", "gt_m", "sh_f", "sc_f", "gt_f")
    dmod = jnp.stack([jnp.concatenate([dmods[l][k] for k in order], axis=2) for l in range(L)])
    dmod = dmod.reshape(-1)
    sm = lambda k: jnp.stack([a.reshape(-1) for a in g_small[k]]).reshape(-1)
    small_keys = ("gla_gain", "pool_scale", "mix_g", "mix_b", "ffn_g", "ffn_b", "wdu", "bdu")
    small_part = jnp.concatenate([sm(k) for k in small_keys])
    small_g, dmod_g = _gather_flat([small_part, dmod], "ag_small_grads")
    small_sum = _sum_lead(small_g.reshape(N_DEV, -1, 128), "sum_small").reshape(-1)
    off = 0
    rep = {}
    for k, n in zip(small_keys, (L * D, L * DP, L * D, L * D, L * D, L * D, L * 2 * GATE_RANK * DK, L * 2 * DK)):
        rep[k] = small_sum[off:off + n]
        off += n
    g_wdu_mine = lax.dynamic_slice_in_dim(rep["wdu"].reshape(L, 2, GATE_RANK, DK), me * dku, dku, axis=3)
    g_bdu_mine = lax.dynamic_slice_in_dim(rep["bdu"].reshape(L, 2, DK), me * dku, dku, axis=2)

    dmod_all = dmod_g.reshape(N_DEV, L, 2, N_MOD * D)
    dm_ctx = _sum_lead(dmod_all[:, :, 0].reshape(N_DEV, L, N_MOD * D), "sum_dmod_ctx")
    dm_rows = jnp.concatenate([jnp.swapaxes(dmod_all[:, :, 1], 0, 1), dm_ctx[:, None],
                               jnp.zeros((L, ncond - N_DEV - 1, N_MOD * D), F32)], axis=1)
    g_b_ada = _sum_lead(jnp.swapaxes(dm_rows, 0, 1), "sum_b_ada")
    dm_mine = lax.dynamic_slice_in_dim(dm_rows, me * wsh, wsh, axis=2).astype(BF16)
    g_w_ada = jnp.stack([_mm(s_cond, dm_mine[l], "tn", F32, "ada_dw") for l in range(L)])
    ds_part = _sum_lead(jnp.stack([_mm(dm_mine[l], w_ada, "nt", F32, "ada_dx", b_pre=(l,)) for l in range(L)]),
                        "sum_ds")
    (ds_all,) = _gather_flat([ds_part[N_DEV]], "ag_ds")
    ds_ctx = _sum_lead(ds_all.reshape(N_DEV, 1, D), "sum_ds_ctx")
    g_c_ctx = _silu_grad(c_ctx.reshape(1, D), ds_ctx, "silu_grad").reshape(D)

    one = lambda g: g[None]
    small_table = {
        "c_ctx": (one(g_c_ctx), c_ctx, m_c_ctx, v_c_ctx),
        "w_ada": (one(g_w_ada), w_ada, m_w_ada, v_w_ada),
        "b_ada": (one(g_b_ada), b_ada, m_b_ada, v_b_ada),
        "w_decay_up": (one(g_wdu_mine), w_decay_up, m_w_decay_up, v_w_decay_up),
        "b_decay_up": (one(g_bdu_mine), b_decay_up, m_b_decay_up, v_b_decay_up),
        "gla_norm_gain": (one(rep["gla_gain"].reshape(L, D)), gla_norm_gain, m_gla_norm_gain, v_gla_norm_gain),
        "pool_scale": (one(rep["pool_scale"].reshape(L, DP)), pool_scale, m_pool_scale, v_pool_scale),
        "ln_mix_gain": (one(rep["mix_g"].reshape(L, D)), ln_mix_gain, m_ln_mix_gain, v_ln_mix_gain),
        "ln_mix_bias": (one(rep["mix_b"].reshape(L, D)), ln_mix_bias, m_ln_mix_bias, v_ln_mix_bias),
        "ln_ffn_gain": (one(rep["ffn_g"].reshape(L, D)), ln_ffn_gain, m_ln_ffn_gain, v_ln_ffn_gain),
        "ln_ffn_bias": (one(rep["ffn_b"].reshape(L, D)), ln_ffn_bias, m_ln_ffn_bias, v_ln_ffn_bias),
    }
    big_shapes = {nm: w.shape for nm, w, _, _ in big_params}
    names = ("c_ctx", "w_ada", "b_ada", "w_in", "w_decay_up", "b_decay_up", "gla_norm_gain", "w_pool_group",
             "pool_scale", "w_gla_out", "w_pool_out", "w_out", "ln_mix_gain", "ln_mix_bias", "w_ffn_in", "w_ffn_out",
             "ln_ffn_gain", "ln_ffn_bias")
    grads, deltas, new_m, new_v = [], [], [], []
    for nm in names:
        if nm in small_table:
            res = _adamw_nd(*small_table[nm], "adamw_" + nm)
        else:
            res = [o.reshape(big_shapes[nm]) for o in big_out[nm]]
        for lst, o in zip((grads, deltas, new_m, new_v), res):
            lst.append(o)
    return (loss, grad_x, *grads, *deltas, *new_m, *new_v)
```

```python
import functools
import math

import numpy as np
import jax
import jax.numpy as jnp
from jax import lax
from jax.experimental import pallas as pl
from jax.experimental.pallas import tpu as pltpu

F32 = jnp.float32
BF16 = jnp.bfloat16

N_DEV = 8
N_HEADS = 4
GATE_RANK = 16
GATE_NORM = 16.0
CHUNK = 64
GRID_W = 64
POOL_WINDOWS = (2, 4, 8, 16)
N_MOD = 6
LN_EPS = 1e-5
RMS_EPS = 1e-6
ALR_PAD = 128
POOL_TB = 256
POOL_PAD_ROWS = 8
ADAM_LR = 0.001
ADAM_B1 = 0.9
ADAM_B2 = 0.999
ADAM_EPS = 1e-08
ADAM_WD = 0.01
ADAM_STEP = 10
VMEM_LIMIT = 56 * 1024 * 1024
MESH = pl.DeviceIdType.MESH


def _cparams(sem=None):
    return pltpu.CompilerParams(dimension_semantics=sem, vmem_limit_bytes=VMEM_LIMIT)


def _pick(dim, cap, mult):
    best = None
    for d in range(mult, min(dim, cap) + 1, mult):
        if dim % d == 0:
            best = d
    return best if best is not None else dim


def _sig(x):
    return 1.0 / (1.0 + jnp.exp(-x))


def _silu(x):
    return x * _sig(x)


def _dot(a, b):
    return lax.dot_general(a, b, (((1,), (0,)), ((), ())), preferred_element_type=F32)


def _dot_nt(a, b):
    return lax.dot_general(a, b, (((1,), (1,)), ((), ())), preferred_element_type=F32)


def _dot_tn(a, b):
    return lax.dot_general(a, b, (((0,), (0,)), ((), ())), preferred_element_type=F32)


def _split2(x):
    hi = x.astype(BF16)
    lo = (x - hi.astype(F32)).astype(BF16)
    return hi, lo


def _dot2(m_b, x):
    hi, lo = _split2(x)
    return _dot(m_b, hi) + _dot(m_b, lo)


def _dot3(m_b, x):
    h1 = x.astype(BF16)
    r1 = x - h1.astype(F32)
    h2 = r1.astype(BF16)
    h3 = (r1 - h2.astype(F32)).astype(BF16)
    return _dot(m_b, h1) + _dot(m_b, h2) + _dot(m_b, h3)


def _my_pos():
    return lax.axis_index("x"), lax.axis_index("y"), lax.axis_index("c")


def _all_gather(arrs, name):
    n = len(arrs)
    srcs = [a.reshape((a.shape[0], 1) + a.shape[1:]) for a in arrs]
    outs = [jax.ShapeDtypeStruct((a.shape[0], N_DEV) + a.shape[1:], a.dtype) for a in arrs]

    def body(*refs):
        in_refs, out_refs = refs[:n], refs[n:2 * n]
        send_sems, recv_sems, local_sems = refs[2 * n:]
        x, y, c = _my_pos()
        me, sibling = (x, y, c), (x, y, 1 - c)
        chips = [(1 - x, y), (x, 1 - y), (1 - x, 1 - y)]

        def slot(t, pos):
            return out_refs[t].at[:, pl.ds(4 * pos[0] + 2 * pos[1] + pos[2], 1)]

        def copy(t, k, block, to, src=None):
            return pltpu.make_async_remote_copy(
                src_ref=slot(t, block) if src is None else src, dst_ref=slot(t, block),
                send_sem=send_sems.at[t * 7 + k], recv_sem=recv_sems.at[t * 7 + k],
                device_id=to, device_id_type=MESH)

        mine = [pltpu.make_async_copy(in_refs[t], slot(t, me), local_sems.at[t]) for t in range(n)]
        for cp in mine:
            cp.start()
        first = []
        for t in range(n):
            first.append(copy(t, 0, me, sibling, src=in_refs[t]))
            first += [copy(t, 1 + j, me, (*chip, c), src=in_refs[t]) for j, chip in enumerate(chips)]
        for cp in first:
            cp.start()
        passed = []
        for j, chip in enumerate(chips):
            for t in range(n):
                copy(t, 1 + j, (*chip, c), me).wait_recv()
                fwd = copy(t, 4 + j, (*chip, c), sibling)
                fwd.start()
                passed.append(fwd)
        for t in range(n):
            copy(t, 0, sibling, me).wait_recv()
            for j, chip in enumerate(chips):
                copy(t, 4 + j, (*chip, 1 - c), me).wait_recv()
        for cp in first + passed:
            cp.wait_send()
        for cp in mine:
            cp.wait()

    any_spec = pl.BlockSpec(memory_space=pl.ANY)
    res = pl.pallas_call(
        body, name=name, out_shape=outs,
        in_specs=[any_spec] * n, out_specs=[any_spec] * n,
        scratch_shapes=[pltpu.SemaphoreType.DMA((7 * n,)), pltpu.SemaphoreType.DMA((7 * n,)),
                        pltpu.SemaphoreType.DMA((n,))],
        compiler_params=pltpu.CompilerParams(has_side_effects=True),
    )(*srcs)
    return list(res)


def _gather_flat(vecs, name):
    padded = []
    for v in vecs:
        n = v.shape[0]
        padded.append(jnp.pad(v, (0, -n % 128)).reshape(1, -1, 128))
    res = _all_gather(padded, name)
    return [r.reshape(N_DEV, -1)[:, :v.shape[0]] for r, v in zip(res, vecs)]


N_CHIP = 4


def _comm_call(body, name, arrs, outs, n_sems):
    any_spec = pl.BlockSpec(memory_space=pl.ANY)
    n = len(arrs)
    res = pl.pallas_call(
        body, name=name, out_shape=outs,
        in_specs=[any_spec] * n, out_specs=[any_spec] * len(outs),
        scratch_shapes=[pltpu.SemaphoreType.DMA((s,)) for s in n_sems],
        compiler_params=pltpu.CompilerParams(has_side_effects=True),
    )(*arrs)
    return list(res)


def _sibling_job(arrs):
    n = len(arrs)
    outs = [jax.ShapeDtypeStruct((N_CHIP,) + a.shape[1:], a.dtype) for a in arrs]

    def copies(in_refs, out_refs, sems):
        send_sems, recv_sems = sems
        x, y, c = _my_pos()
        return [pltpu.make_async_remote_copy(
            src_ref=in_refs[t].at[pl.ds(2 * k + (1 - c), 1)], dst_ref=out_refs[t].at[pl.ds(k, 1)],
            send_sem=send_sems.at[t * N_CHIP + k], recv_sem=recv_sems.at[t * N_CHIP + k],
            device_id=(x, y, 1 - c), device_id_type=MESH) for t in range(n) for k in range(N_CHIP)]

    def start(in_refs, out_refs, sems):
        for cp in copies(in_refs, out_refs, sems):
            cp.start()

    def finish(in_refs, out_refs, sems):
        cps = copies(in_refs, out_refs, sems)
        for cp in cps:
            cp.wait_recv()
        for cp in cps:
            cp.wait_send()

    return _Job(arrs, outs, (N_CHIP * n, N_CHIP * n), start, finish)


class _Job:
    def __init__(self, arrs, outs, n_sems, start, finish):
        self.arrs, self.outs, self.n_sems, self.start, self.finish = arrs, outs, n_sems, start, finish


def _gather_job(stacked, l):
    n = len(stacked)
    outs = [jax.ShapeDtypeStruct((N_DEV,) + a.shape[1:], a.dtype) for a in stacked]

    def parts(in_refs, out_refs, sems):
        send_sems, recv_sems, local_sems = sems
        x, y, c = _my_pos()
        me, sibling = (x, y, c), (x, y, 1 - c)
        chips = [(1 - x, y), (x, 1 - y), (1 - x, 1 - y)]
        src = lambda t: in_refs[t].at[pl.ds(l, 1)]

        def slot(t, pos):
            return out_refs[t].at[pl.ds(4 * pos[0] + 2 * pos[1] + pos[2], 1)]

        def copy(t, k, block, to, from_input=False):
            return pltpu.make_async_remote_copy(
                src_ref=src(t) if from_input else slot(t, block), dst_ref=slot(t, block),
                send_sem=send_sems.at[t * 7 + k], recv_sem=recv_sems.at[t * 7 + k],
                device_id=to, device_id_type=MESH)

        mine = [pltpu.make_async_copy(src(t), slot(t, me), local_sems.at[t]) for t in range(n)]
        first = []
        for t in range(n):
            first.append(copy(t, 0, me, sibling, True))
            first += [copy(t, 1 + j, me, (*chip, c), True) for j, chip in enumerate(chips)]
        return me, sibling, chips, copy, mine, first

    def start(in_refs, out_refs, sems):
        _, _, _, _, mine, first = parts(in_refs, out_refs, sems)
        for cp in mine + first:
            cp.start()

    def finish(in_refs, out_refs, sems):
        me, sibling, chips, copy, mine, first = parts(in_refs, out_refs, sems)
        passed = []
        for j, chip in enumerate(chips):
            for t in range(n):
                copy(t, 1 + j, (*chip, me[2]), me).wait_recv()
                fwd = copy(t, 4 + j, (*chip, me[2]), sibling)
                fwd.start()
                passed.append(fwd)
        for t in range(n):
            copy(t, 0, sibling, me).wait_recv()
            for j, chip in enumerate(chips):
                copy(t, 4 + j, (*chip, 1 - me[2]), me).wait_recv()
        for cp in first + passed:
            cp.wait_send()
        for cp in mine:
            cp.wait()

    return _Job(stacked, outs, (7 * n, 7 * n, n), start, finish)


def _chip_job(arrs):
    n = len(arrs)
    outs = [jax.ShapeDtypeStruct(a.shape, a.dtype) for a in arrs]

    def parts(in_refs, out_refs, sems):
        send_sems, recv_sems, local_sems = sems
        x, y, c = _my_pos()
        chip = 2 * x + y
        mine, sends, recvs = [], [], []
        for t in range(n):
            mine.append(pltpu.make_async_copy(in_refs[t].at[pl.ds(chip, 1)], out_refs[t].at[pl.ds(chip, 1)],
                                              local_sems.at[t]))
            for m in range(1, N_CHIP):
                px, py = x ^ (m >> 1), y ^ (m & 1)
                peer = 2 * px + py
                sends.append(pltpu.make_async_remote_copy(
                    src_ref=in_refs[t].at[pl.ds(peer, 1)], dst_ref=out_refs[t].at[pl.ds(chip, 1)],
                    send_sem=send_sems.at[t * 3 + m - 1], recv_sem=recv_sems.at[t * 3 + m - 1],
                    device_id=(px, py, c), device_id_type=MESH))
                recvs.append(pltpu.make_async_remote_copy(
                    src_ref=in_refs[t].at[pl.ds(peer, 1)], dst_ref=out_refs[t].at[pl.ds(peer, 1)],
                    send_sem=send_sems.at[t * 3 + m - 1], recv_sem=recv_sems.at[t * 3 + m - 1],
                    device_id=(x, y, c), device_id_type=MESH))
        return mine, sends, recvs

    def start(in_refs, out_refs, sems):
        mine, sends, _ = parts(in_refs, out_refs, sems)
        for cp in mine + sends:
            cp.start()

    def finish(in_refs, out_refs, sems):
        mine, sends, recvs = parts(in_refs, out_refs, sems)
        for cp in recvs:
            cp.wait_recv()
        for cp in sends:
            cp.wait_send()
        for cp in mine:
            cp.wait()

    return _Job(arrs, outs, (3 * n, 3 * n, n), start, finish)


def _run_job(job, name):
    n = len(job.arrs)

    def body(*refs):
        ins, outs, sems = refs[:n], refs[n:n + len(job.outs)], refs[n + len(job.outs):]
        job.start(ins, outs, sems)
        job.finish(ins, outs, sems)

    return _comm_call(body, name, job.arrs, job.outs, job.n_sems)


def _carry(job, body, grid, in_specs, out_specs, out_shape, scratch_shapes, args):
    out_specs = list(out_specs) if isinstance(out_specs, (list, tuple)) else [out_specs]
    out_shape = list(out_shape) if isinstance(out_shape, (list, tuple)) else [out_shape]
    n_ci, n_co, n_cs = len(in_specs), len(out_specs), len(scratch_shapes)
    n_ji, n_jo = len(job.arrs), len(job.outs)
    any_spec = pl.BlockSpec(memory_space=pl.ANY)
    total = int(np.prod(grid))

    def wrapped(*refs):
        cin, jin = refs[:n_ci], refs[n_ci:n_ci + n_ji]
        o0 = n_ci + n_ji
        cout, jout = refs[o0:o0 + n_co], refs[o0 + n_co:o0 + n_co + n_jo]
        s0 = o0 + n_co + n_jo
        cscr, jsems = refs[s0:s0 + n_cs], refs[s0 + n_cs:]
        step = pl.program_id(0)
        for d in range(1, len(grid)):
            step = step * grid[d] + pl.program_id(d)

        @pl.when(step == 0)
        def _():
            job.start(jin, jout, jsems)

        body(*cin, *cout, *cscr)

        @pl.when(step == total - 1)
        def _():
            job.finish(jin, jout, jsems)

    return (wrapped, list(in_specs) + [any_spec] * n_ji, out_specs + [any_spec] * n_jo,
            out_shape + list(job.outs),
            list(scratch_shapes) + [pltpu.SemaphoreType.DMA((s,)) for s in job.n_sems],
            list(args) + list(job.arrs), n_co)


def _pair_add(g, r, name):
    _, R, C = g.shape
    cpad = -(-C // 128) * 128
    rt = _pick(R, max(16, (1 << 20) // (2 * cpad)), 16)
    cidx = lax.axis_index("c").astype(jnp.int32).reshape(1)

    def body(c_ref, g_ref, r_ref, o_ref):
        o_ref[...] = (g_ref[...].astype(F32) + r_ref[...].astype(F32)).astype(o_ref.dtype)

    return pl.pallas_call(
        body, name=name, out_shape=jax.ShapeDtypeStruct((N_CHIP, R, C), g.dtype),
        grid_spec=pltpu.PrefetchScalarGridSpec(
            num_scalar_prefetch=1, grid=(N_CHIP, R // rt),
            in_specs=[pl.BlockSpec((None, rt, C), lambda k, i, c_ref: (2 * k + c_ref[0], i, 0)),
                      pl.BlockSpec((None, rt, C), lambda k, i, c_ref: (k, i, 0))],
            out_specs=pl.BlockSpec((None, rt, C), lambda k, i, c_ref: (k, i, 0))),
        compiler_params=_cparams(("parallel", "parallel")),
    )(cidx, g, r)


def _pair_adds(chunks, sib, tag):
    return [_pair_add(g, r, "rs_pair_add" + tag) for g, r in zip(chunks, sib)]


def _mm(a, b, mode, out_dtype=F32, name="mm", bias=None, add=None, b_pre=(), b_shard=False,
        a_half=False, b_half=False, out_shard=False, job=None):
    npre = len(b_pre)
    bshape = b.shape[npre:]
    if mode == "nn":
        M, K = a.shape
        if b_shard:
            K2, N = bshape[1], N_DEV * bshape[2]
        else:
            K2, N = bshape
    elif mode == "nt":
        M, K = (a.shape[1], 2 * a.shape[2]) if a_half else a.shape
        if b_shard:
            N, K2 = bshape[1], N_DEV * bshape[2]
        else:
            N, K2 = bshape
    else:
        K, M = a.shape
        K2, N = (b.shape[1], 2 * b.shape[2]) if b_half else bshape
    assert K == K2, (a.shape, b.shape, mode)
    tm = _pick(M, 1100, 16) if mode != "tn" else _pick(M, 1024, 128)
    tn = _pick(N, 1024, 128)
    tk = _pick(K, 2176, 128)
    if b_shard and mode == "nn":
        tn = bshape[2]
    if b_shard and mode == "nt":
        tk = bshape[2]
    if out_shard:
        tn = N // N_DEV
    nk = K // tk
    none_pre = (None,) * npre
    if mode == "nn":
        a_spec = pl.BlockSpec((tm, tk), lambda i, j, k: (i, k))
        if b_shard:
            b_spec = pl.BlockSpec(none_pre + (None, tk, tn), lambda i, j, k: b_pre + (j, k, 0))
        else:
            b_spec = pl.BlockSpec(none_pre + (tk, tn), lambda i, j, k: b_pre + (k, j))
        dot = _dot
    elif mode == "nt":
        if a_half:
            nkh = a.shape[2] // tk
            a_spec = pl.BlockSpec((None, tm, tk), lambda i, j, k: (k // nkh, i, k % nkh))
        else:
            a_spec = pl.BlockSpec((tm, tk), lambda i, j, k: (i, k))
        if b_shard:
            b_spec = pl.BlockSpec(none_pre + (None, tn, tk), lambda i, j, k: b_pre + (k, j, 0))
        else:
            b_spec = pl.BlockSpec(none_pre + (tn, tk), lambda i, j, k: b_pre + (j, k))
        dot = _dot_nt
    else:
        a_spec = pl.BlockSpec((tk, tm), lambda i, j, k: (k, i))
        if b_half:
            nnh = b.shape[2] // tn
            b_spec = pl.BlockSpec((None, tk, tn), lambda i, j, k: (j // nnh, k, j % nnh))
        else:
            b_spec = pl.BlockSpec(none_pre + (tk, tn), lambda i, j, k: b_pre + (k, j))
        dot = _dot_tn
    in_specs = [a_spec, b_spec]
    args = [a, b]
    if bias is not None:
        in_specs.append(pl.BlockSpec((1, tn), lambda i, j, k: (0, j)))
        args.append(bias)
    if add is not None:
        in_specs.append(pl.BlockSpec((tm, tn), lambda i, j, k: (i, j)))
        args.append(add)
    n_in = len(args)
    if out_shard:
        o_spec = pl.BlockSpec((None, tm, tn), lambda i, j, k: (j, i, 0))
        o_shape = jax.ShapeDtypeStruct((N_DEV, M, tn), out_dtype)
    else:
        o_spec = pl.BlockSpec((tm, tn), lambda i, j, k: (i, j))
        o_shape = jax.ShapeDtypeStruct((M, N), out_dtype)

    def body(*refs):
        a_ref, b_ref = refs[0], refs[1]
        bias_ref = refs[2] if bias is not None else None
        add_ref = refs[n_in - 1] if add is not None else None
        o_ref = refs[n_in]
        p = dot(a_ref[...].astype(BF16), b_ref[...].astype(BF16))

        def finish(acc):
            if bias_ref is not None:
                acc = acc + bias_ref[...]
            if add_ref is not None:
                acc = acc + add_ref[...]
            o_ref[...] = acc.astype(o_ref.dtype)

        if nk == 1:
            finish(p)
        else:
            acc_ref = refs[-1]
            k = pl.program_id(2)

            @pl.when(k == 0)
            def _():
                acc_ref[...] = p

            @pl.when(k > 0)
            def _():
                acc_ref[...] += p

            @pl.when(k == nk - 1)
            def _():
                finish(acc_ref[...])

    grid = (M // tm, N // tn, nk)
    scratch = [pltpu.VMEM((tm, tn), F32)] if nk > 1 else []
    if job is None:
        return pl.pallas_call(
            body, name=name, grid=grid, in_specs=in_specs, out_specs=o_spec, out_shape=o_shape,
            scratch_shapes=scratch, compiler_params=_cparams(("parallel", "parallel", "arbitrary")),
        )(*args)
    return _call_carrying(job, body, name, grid, in_specs, o_spec, o_shape, scratch, args)


def _call_carrying(job, body, name, grid, in_specs, out_specs, out_shape, scratch, args):
    body, in_specs, out_specs, out_shape, scratch, args, n_co = _carry(
        job, body, grid, in_specs, out_specs, out_shape, scratch, args)
    res = pl.pallas_call(
        body, name=name, grid=grid, in_specs=in_specs, out_specs=out_specs, out_shape=out_shape,
        scratch_shapes=scratch, compiler_params=_cparams(("arbitrary",) * len(grid)),
    )(*args)
    own = res[0] if n_co == 1 else list(res[:n_co])
    return own, list(res[n_co:])


def _ffn_in_fwd(h2, w_fi, l, name, job=None):
    T, D = h2.shape
    n = w_fi.shape[3]
    nh = N_DEV // 2
    dff = nh * n
    tm = _pick(T, 600, 16)

    def body(a_ref, bg_ref, bu_ref, ff_ref, s_ref):
        a = a_ref[...]
        g = _dot(a, bg_ref[...])
        u = _dot(a, bu_ref[...])
        ff_ref[0] = g
        ff_ref[1] = u
        s_ref[...] = _swiglu_f(g, u).astype(BF16)

    grid = (T // tm, nh)
    in_specs = [pl.BlockSpec((tm, D), lambda i, j: (i, 0)),
                pl.BlockSpec((None, None, D, n), lambda i, j: (l, j, 0, 0)),
                pl.BlockSpec((None, None, D, n), lambda i, j: (l, nh + j, 0, 0))]
    out_specs = [pl.BlockSpec((2, tm, n), lambda i, j: (0, i, j)), pl.BlockSpec((tm, n), lambda i, j: (i, j))]
    out_shape = [jax.ShapeDtypeStruct((2, T, dff), F32), jax.ShapeDtypeStruct((T, dff), BF16)]
    args = (h2, w_fi, w_fi)
    if job is None:
        return pl.pallas_call(
            body, name=name, grid=grid, in_specs=in_specs, out_specs=out_specs, out_shape=out_shape,
            compiler_params=_cparams(("parallel", "parallel")),
        )(*args)
    return _call_carrying(job, body, name, grid, in_specs, out_specs, out_shape, [], args)


def _ffn_out_dx(dffn, w_fo, l, ff, name, job=None):
    T, D = dffn.shape
    dff = ff.shape[2]
    tm = _pick(T, 600, 16)
    tw = _pick(dff, 1408, 128)

    def body(a_ref, b_ref, ff_ref, o_ref):
        ds = _dot_nt(a_ref[...], b_ref[...])
        _, vjp = jax.vjp(_swiglu_f, ff_ref[0], ff_ref[1])
        dg, du = vjp(ds)
        o_ref[0] = dg.astype(BF16)
        o_ref[1] = du.astype(BF16)

    grid = (T // tm, dff // tw)
    in_specs = [pl.BlockSpec((tm, D), lambda i, j: (i, 0)),
                pl.BlockSpec((None, tw, D), lambda i, j: (l, j, 0)),
                pl.BlockSpec((2, tm, tw), lambda i, j: (0, i, j))]
    out_specs = pl.BlockSpec((2, tm, tw), lambda i, j: (0, i, j))
    out_shape = jax.ShapeDtypeStruct((2, T, dff), BF16)
    args = (dffn, w_fo, ff)
    if job is None:
        return pl.pallas_call(
            body, name=name, grid=grid, in_specs=in_specs, out_specs=out_specs, out_shape=out_shape,
            compiler_params=_cparams(("parallel", "parallel")),
        )(*args)
    return _call_carrying(job, body, name, grid, in_specs, out_specs, out_shape, [], args)


def _row(rb, w, col=0):
    return pl.BlockSpec((rb, w), lambda i: (i, col))


def _modspec(d, sec):
    return pl.BlockSpec((None, 1, d), lambda i: (jnp.minimum(i, 1), 0, sec))


def _vec(w):
    return pl.BlockSpec((1, w), lambda i: (0, 0))


def _acc2(w):
    return pl.BlockSpec((None, 1, w), lambda i: (jnp.minimum(i, 1), 0, 0))


def _accum(ref, val):
    i = pl.program_id(0)

    @pl.when(i <= 1)
    def _():
        ref[...] = val

    @pl.when(i > 1)
    def _():
        ref[...] += val


def _acc_shape(w):
    return jax.ShapeDtypeStruct((2, 1, w), F32)


def _mod_f(x, sc, sh):
    return x * (1.0 + sc) + sh


def _mod_fwd(xa, mod, sec_sc, sec_sh, rb, name):
    T, D = xa.shape

    def body(x_ref, sc_ref, sh_ref, h_ref):
        h_ref[...] = _mod_f(x_ref[...], sc_ref[...], sh_ref[...]).astype(BF16)

    return pl.pallas_call(
        body, name=name, grid=(T // rb,),
        in_specs=[_row(rb, D), _modspec(D, sec_sc), _modspec(D, sec_sh)],
        out_specs=_row(rb, D), out_shape=jax.ShapeDtypeStruct((T, D), BF16),
        compiler_params=_cparams(("parallel",)),
    )(xa, mod, mod)


def _mod_bwd(dxa, dh, xa, mod, sec_sc, sec_sh, rb, name):
    T, D = xa.shape

    def body(dxa_ref, dh_ref, x_ref, sc_ref, sh_ref, dx_ref, dsc_ref, dsh_ref):
        _, vjp = jax.vjp(_mod_f, x_ref[...], sc_ref[...], sh_ref[...])
        dx, dsc, dsh = vjp(dh_ref[...])
        dx_ref[...] = dxa_ref[...] + dx
        _accum(dsc_ref, dsc)
        _accum(dsh_ref, dsh)

    return pl.pallas_call(
        body, name=name, grid=(T // rb,),
        in_specs=[_row(rb, D), _row(rb, D), _row(rb, D), _modspec(D, sec_sc), _modspec(D, sec_sh)],
        out_specs=[_row(rb, D), _acc2(D), _acc2(D)],
        out_shape=[jax.ShapeDtypeStruct((T, D), F32), _acc_shape(D), _acc_shape(D)],
        compiler_params=_cparams(("arbitrary",)),
    )(dxa, dh, xa, mod, mod)


def _ln_f(alpha, x, mix, gt, gain, bias):
    z = alpha * x + gt * mix
    mu = jnp.mean(z, axis=-1, keepdims=True)
    zc = z - mu
    var = jnp.mean(zc * zc, axis=-1, keepdims=True)
    return zc * lax.rsqrt(var + LN_EPS) * gain + bias


def _unit_fwd(alpha, x, mix, mod, sec_gt, gain, bias, next_mod, rb, name):
    T, D = x.shape
    has_mod = next_mod is not None

    def body(*refs):
        if has_mod:
            x_ref, mix_ref, gt_ref, g_ref, b_ref, sc_ref, sh_ref, xo_ref, h_ref = refs
        else:
            x_ref, mix_ref, gt_ref, g_ref, b_ref, xo_ref = refs
        xo = _ln_f(alpha, x_ref[...], mix_ref[...], gt_ref[...], g_ref[...], b_ref[...])
        xo_ref[...] = xo
        if has_mod:
            h_ref[...] = _mod_f(xo, sc_ref[...], sh_ref[...]).astype(BF16)

    in_specs = [_row(rb, D), _row(rb, D), _modspec(D, sec_gt), _vec(D), _vec(D)]
    args = [x, mix, mod, gain, bias]
    out_specs = [_row(rb, D)]
    out_shape = [jax.ShapeDtypeStruct((T, D), F32)]
    if has_mod:
        nm, s_sc, s_sh = next_mod
        in_specs += [_modspec(D, s_sc), _modspec(D, s_sh)]
        args += [nm, nm]
        out_specs.append(_row(rb, D))
        out_shape.append(jax.ShapeDtypeStruct((T, D), BF16))
    res = pl.pallas_call(
        body, name=name, grid=(T // rb,), in_specs=in_specs, out_specs=out_specs, out_shape=out_shape,
        compiler_params=_cparams(("parallel",)),
    )(*args)
    return (res[0], res[1]) if has_mod else (res[0], None)


def _unit_bwd(alpha, dxo, dh, x, mix, mod, sec_gt, gain, bias, next_mod, rb, name, job=None):
    T, D = x.shape
    has_mod = next_mod is not None

    def body(*refs):
        if has_mod:
            (dxo_ref, dh_ref, x_ref, mix_ref, gt_ref, g_ref, b_ref, sc_ref, sh_ref,
             dx_ref, dmix_ref, dgt_ref, dg_ref, db_ref, dsc_ref, dsh_ref) = refs
        else:
            (dxo_ref, x_ref, mix_ref, gt_ref, g_ref, b_ref,
             dx_ref, dmix_ref, dgt_ref, dg_ref, db_ref) = refs
        xo, vjp = jax.vjp(functools.partial(_ln_f, alpha), x_ref[...], mix_ref[...], gt_ref[...],
                          g_ref[...], b_ref[...])
        dxo_t = dxo_ref[...]
        if has_mod:
            _, vjp_m = jax.vjp(_mod_f, xo, sc_ref[...], sh_ref[...])
            dxo_m, dsc, dsh = vjp_m(dh_ref[...])
            dxo_t = dxo_t + dxo_m
            _accum(dsc_ref, dsc)
            _accum(dsh_ref, dsh)
        dx, dmix, dgt, dg, db = vjp(dxo_t)
        dx_ref[...] = dx
        dmix_ref[...] = dmix.astype(BF16)
        _accum(dgt_ref, dgt)
        _accum(dg_ref, dg)
        _accum(db_ref, db)

    in_specs = [_row(rb, D)]
    args = [dxo]
    if has_mod:
        in_specs.append(_row(rb, D))
        args.append(dh)
    in_specs += [_row(rb, D), _row(rb, D), _modspec(D, sec_gt), _vec(D), _vec(D)]
    args += [x, mix, mod, gain, bias]
    out_specs = [_row(rb, D), _row(rb, D), _acc2(D), _acc2(D), _acc2(D)]
    out_shape = [jax.ShapeDtypeStruct((T, D), F32), jax.ShapeDtypeStruct((T, D), BF16),
                 _acc_shape(D), _acc_shape(D), _acc_shape(D)]
    if has_mod:
        nm, s_sc, s_sh = next_mod
        in_specs += [_modspec(D, s_sc), _modspec(D, s_sh)]
        args += [nm, nm]
        out_specs += [_acc2(D), _acc2(D)]
        out_shape += [_acc_shape(D), _acc_shape(D)]
    if job is None:
        res = pl.pallas_call(
            body, name=name, grid=(T // rb,), in_specs=in_specs, out_specs=out_specs, out_shape=out_shape,
            compiler_params=_cparams(("arbitrary",)),
        )(*args)
        job_res = None
    else:
        res, job_res = _call_carrying(job, body, name, (T // rb,), in_specs, out_specs, out_shape, [], args)
    res = list(res) if has_mod else list(res) + [None, None]
    return res if job is None else (res, job_res)


def _log_sigmoid(z):
    return jnp.minimum(z, 0.0) - jnp.log(1.0 + jnp.exp(-jnp.abs(z)))


def _decay_fwd(alr, wdu, bdu, rb, name):
    T = alr.shape[0]
    W = wdu.shape[1]

    def body(a_ref, w_ref, b_ref, la_ref):
        z = _dot(a_ref[...].astype(BF16), w_ref[...]) + b_ref[...]
        la_ref[...] = _log_sigmoid(z) * (1.0 / GATE_NORM)

    return pl.pallas_call(
        body, name=name, grid=(T // rb,),
        in_specs=[_row(rb, ALR_PAD), pl.BlockSpec((ALR_PAD, W), lambda i: (0, 0)), _vec(W)],
        out_specs=_row(rb, W), out_shape=jax.ShapeDtypeStruct((T, W), F32),
        compiler_params=_cparams(("parallel",)),
    )(alr, wdu, bdu)


def _decay_bwd(dla_f, dla_b, alr, wdu, bdu, rb, name):
    T = alr.shape[0]
    W = wdu.shape[1]
    DK = W // 2

    def body(df_ref, db_ref, a_ref, w_ref, b_ref, dalr_ref, gw_ref, gb_ref):
        i = pl.program_id(0)
        ab = a_ref[...].astype(BF16)
        z = _dot(ab, w_ref[...]) + b_ref[...]
        dla = jnp.concatenate([df_ref[...], db_ref[...]], axis=1)
        dz = dla * _sig(-z) * (1.0 / GATE_NORM)
        dzb = dz.astype(BF16)
        dalr_ref[...] = _dot_nt(dzb, w_ref[...]).astype(BF16)
        gw = _dot_tn(ab, dzb)
        gb = jnp.sum(dz, axis=0, keepdims=True)

        @pl.when(i == 0)
        def _():
            gw_ref[...] = gw
            gb_ref[...] = gb

        @pl.when(i > 0)
        def _():
            gw_ref[...] += gw
            gb_ref[...] += gb

    return pl.pallas_call(
        body, name=name, grid=(T // rb,),
        in_specs=[_row(rb, DK), _row(rb, DK), _row(rb, ALR_PAD), pl.BlockSpec((ALR_PAD, W), lambda i: (0, 0)), _vec(W)],
        out_specs=[_row(rb, ALR_PAD), pl.BlockSpec((ALR_PAD, W), lambda i: (0, 0)), _vec(W)],
        out_shape=[jax.ShapeDtypeStruct((T, ALR_PAD), BF16), jax.ShapeDtypeStruct((ALR_PAD, W), F32),
                   jax.ShapeDtypeStruct((1, W), F32)],
        compiler_params=_cparams(("arbitrary",)),
    )(dla_f, dla_b, alr, wdu, bdu)


def _tri(rev):
    m = np.tril(np.ones((CHUNK, CHUNK), np.float32))
    return jnp.asarray(m.T if rev else m, BF16)


def _gla_chunk_common(q_ref, k_ref, v_ref, la_ref, tri_ref, rows, ck, cv, rev, scale_q):
    mid = CHUNK // 2 if rev else CHUNK // 2 - 1
    last_i = 0 if rev else CHUNK - 1
    q = q_ref[rows, ck] * scale_q
    k = k_ref[rows, ck]
    v = v_ref[rows, cv]
    cum = _dot3(tri_ref[...], la_ref[rows, ck])
    ref = cum[mid:mid + 1, :]
    last = cum[last_i:last_i + 1, :]
    e_q = jnp.exp(cum - ref)
    e_k = jnp.exp(ref - cum)
    e_c = jnp.exp(cum)
    e_s = jnp.exp(last - cum)
    e_l = jnp.exp(last)
    ri = lax.broadcasted_iota(jnp.int32, (CHUNK, CHUNK), 0)
    ci = lax.broadcasted_iota(jnp.int32, (CHUNK, CHUNK), 1)
    mask = (ci >= ri) if rev else (ci <= ri)
    return q, k, v, e_q, e_k, e_c, e_s, e_l, mask, last_i


GLA_HEADS_PER_STEP = 2


def _gla_specs(rb, hk, hv, D, rbmap, rev):
    hp = GLA_HEADS_PER_STEP
    q_col0 = 4 * D // (hp * hk)
    k_col0 = q_col0 + N_HEADS // hp
    la_col0 = N_HEADS // hp if rev else 0
    return [
        pl.BlockSpec((rb, hp * hk), lambda h, i: (rbmap(i), q_col0 + h)),
        pl.BlockSpec((rb, hp * hk), lambda h, i: (rbmap(i), k_col0 + h)),
        pl.BlockSpec((rb, hp * hv), lambda h, i: (rbmap(i), h)),
        pl.BlockSpec((rb, hp * hk), lambda h, i: (rbmap(i), la_col0 + h)),
        pl.BlockSpec((CHUNK, CHUNK), lambda h, i: (0, 0)),
    ]


def _gla_call(job, body, name, grid, in_specs, out_specs, out_shape, scratch, args):
    if job is None:
        return pl.pallas_call(
            body, name=name, grid=grid, in_specs=in_specs, out_specs=out_specs, out_shape=out_shape,
            scratch_shapes=scratch, compiler_params=_cparams(("parallel", "arbitrary")),
        )(*args)
    return _call_carrying(job, body, name, grid, in_specs, out_specs, out_shape, scratch, args)


def _gla_fwd(proj, la, rev, rb, D, name, job=None):
    T = proj.shape[0]
    nb = T // rb
    ncb = rb // CHUNK
    hp = GLA_HEADS_PER_STEP
    hk, hv = D // 2 // N_HEADS, D // N_HEADS
    scale_q = float(hk) ** -0.5
    rbmap = (lambda i: jnp.where(i == 0, 0, nb - i)) if rev else (lambda i: i)

    def body(q_ref, k_ref, v_ref, la_ref, tri_ref, o_ref, s_ref, st_ref):
        @pl.when(pl.program_id(1) == 0)
        def _():
            st_ref[...] = jnp.zeros_like(st_ref)

        order = range(ncb - 1, -1, -1) if rev else range(ncb)
        for cc in order:
            rows = slice(cc * CHUNK, (cc + 1) * CHUNK)
            for hh in range(hp):
                ck, cv = slice(hh * hk, (hh + 1) * hk), slice(hh * hv, (hh + 1) * hv)
                q, k, v, e_q, e_k, e_c, e_s, e_l, mask, _ = _gla_chunk_common(
                    q_ref, k_ref, v_ref, la_ref, tri_ref, rows, ck, cv, rev, scale_q)
                vb = v.astype(BF16)
                a = jnp.where(mask, _dot_nt((q * e_q).astype(BF16), (k * e_k).astype(BF16)), 0.0)
                st = st_ref[hh]
                s_ref[hh, cc] = st
                o = _dot(a.astype(BF16), vb) + _dot_nt((q * e_c).astype(BF16), st.astype(BF16))
                o_ref[rows, cv] = o
                st_ref[hh] = st * e_l + _dot_tn(vb, (k * e_s).astype(BF16))

    return _gla_call(
        job, body, name, (N_HEADS // hp, nb), _gla_specs(rb, hk, hv, D, rbmap, rev),
        [pl.BlockSpec((rb, hp * hv), lambda h, i: (rbmap(i), h)),
         pl.BlockSpec((hp, ncb, hv, hk), lambda h, i: (h, rbmap(i), 0, 0))],
        [jax.ShapeDtypeStruct((T, D), F32), jax.ShapeDtypeStruct((N_HEADS, T // CHUNK, hv, hk), F32)],
        [pltpu.VMEM((hp, hv, hk), F32)], (proj, proj, proj, la, _tri(rev)))


def _gla_bwd(proj, la, do, states, rev, rb, D, prev, name, job=None):
    T = proj.shape[0]
    nb = T // rb
    ncb = rb // CHUNK
    hp = GLA_HEADS_PER_STEP
    hk, hv = D // 2 // N_HEADS, D // N_HEADS
    DK = D // 2
    scale_q = float(hk) ** -0.5
    if rev:
        rbmap = lambda i: jnp.where(i == nb - 1, 0, i + 1)
    else:
        rbmap = lambda i: nb - 1 - i
    has_prev = prev is not None
    out_dt = BF16 if has_prev else F32

    def body(*refs):
        q_ref, k_ref, v_ref, la_ref, tri_ref, trit_ref, do_ref, s_ref = refs[:8]
        n_in = 11 if has_prev else 8
        pq_ref, pk_ref, pv_ref = refs[8:11] if has_prev else (None, None, None)
        dq_ref, dk_ref, dv_ref, dla_ref, ds_ref = refs[n_in:]

        @pl.when(pl.program_id(1) == 0)
        def _():
            ds_ref[...] = jnp.zeros_like(ds_ref)

        order = range(ncb) if rev else range(ncb - 1, -1, -1)
        for cc in order:
            rows = slice(cc * CHUNK, (cc + 1) * CHUNK)
            for hh in range(hp):
                ck, cv = slice(hh * hk, (hh + 1) * hk), slice(hh * hv, (hh + 1) * hv)
                q, k, v, e_q, e_k, e_c, e_s, e_l, mask, last_i = _gla_chunk_common(
                    q_ref, k_ref, v_ref, la_ref, tri_ref, rows, ck, cv, rev, scale_q)
                vb = v.astype(BF16)
                qi = (q * e_q).astype(BF16)
                ki = (k * e_k).astype(BF16)
                qc = (q * e_c).astype(BF16)
                ks = (k * e_s).astype(BF16)
                a = jnp.where(mask, _dot_nt(qi, ki), 0.0).astype(BF16)
                st0 = s_ref[hh, cc]
                st0b = st0.astype(BF16)
                dst1 = ds_ref[hh]
                dst1b = dst1.astype(BF16)
                dob = do_ref[rows, cv].astype(BF16)
                da = jnp.where(mask, _dot_nt(dob, vb), 0.0).astype(BF16)
                dv = _dot_tn(a, dob) + _dot_nt(ks, dst1b)
                dq_inter = _dot(dob, st0b) * e_c
                dk_inter = _dot(vb, dst1b) * e_s
                dq_s = _dot(da, ki) * e_q + dq_inter
                dk = _dot_tn(da, qi) * e_k + dk_inter
                extra = (jnp.sum(k * dk_inter, axis=0, keepdims=True)
                         + e_l * jnp.sum(dst1 * st0, axis=0, keepdims=True))
                rowi = lax.broadcasted_iota(jnp.int32, (CHUNK, hk), 0)
                dcum = q * dq_s - k * dk + jnp.where(rowi == last_i, extra, 0.0)
                dla_ref[rows, ck] = _dot3(trit_ref[...], dcum)
                dq = dq_s * scale_q
                if has_prev:
                    dq = dq + pq_ref[rows, ck]
                    dk = dk + pk_ref[rows, ck]
                    dv = dv + pv_ref[rows, cv]
                dq_ref[rows, ck] = dq.astype(out_dt)
                dk_ref[rows, ck] = dk.astype(out_dt)
                dv_ref[rows, cv] = dv.astype(out_dt)
                ds_ref[hh] = dst1 * e_l + _dot_tn(dob, qc)

    in_specs = _gla_specs(rb, hk, hv, D, rbmap, rev)
    in_specs += [pl.BlockSpec((CHUNK, CHUNK), lambda h, i: (0, 0)),
                 pl.BlockSpec((rb, hp * hv), lambda h, i: (rbmap(i), h)),
                 pl.BlockSpec((hp, ncb, hv, hk), lambda h, i: (h, rbmap(i), 0, 0))]
    args = [proj, proj, proj, la, _tri(rev), _tri(not rev), do, states]
    hk_spec = pl.BlockSpec((rb, hp * hk), lambda h, i: (rbmap(i), h))
    hv_spec = pl.BlockSpec((rb, hp * hv), lambda h, i: (rbmap(i), h))
    if has_prev:
        in_specs += [hk_spec, hk_spec, hv_spec]
        args += list(prev)
    return _gla_call(
        job, body, name, (N_HEADS // hp, nb), in_specs, [hk_spec, hk_spec, hv_spec, hk_spec],
        [jax.ShapeDtypeStruct((T, DK), out_dt), jax.ShapeDtypeStruct((T, DK), out_dt),
         jax.ShapeDtypeStruct((T, D), out_dt), jax.ShapeDtypeStruct((T, DK), F32)],
        [pltpu.VMEM((hp, hv, hk), F32)], args)


def _glaout_f(of, ob, g, gain):
    o = of + ob
    n = o * lax.rsqrt(jnp.mean(o * o, axis=-1, keepdims=True) + RMS_EPS)
    return n * gain * _silu(g)


def _glaout_fwd(o_f, o_b, proj, gain, rb, name):
    T, D = o_f.shape
    hv = D // N_HEADS

    def body(of_ref, ob_ref, g_ref, gn_ref, u_ref):
        for h in range(N_HEADS):
            cs = slice(h * hv, (h + 1) * hv)
            u_ref[:, cs] = _glaout_f(of_ref[:, cs], ob_ref[:, cs], g_ref[:, cs], gn_ref[:, cs]).astype(BF16)

    return pl.pallas_call(
        body, name=name, grid=(T // rb,),
        in_specs=[_row(rb, D), _row(rb, D), _row(rb, D, 1), _vec(D)],
        out_specs=_row(rb, D), out_shape=jax.ShapeDtypeStruct((T, D), BF16),
        compiler_params=_cparams(("parallel",)),
    )(o_f, o_b, proj, gain)


def _glaout_bwd(du, o_f, o_b, proj, gain, rb, name):
    T, D = o_f.shape
    hv = D // N_HEADS

    def body(du_ref, of_ref, ob_ref, g_ref, gn_ref, do_ref, dg_ref, dgn_ref, tmp_ref):
        for h in range(N_HEADS):
            cs = slice(h * hv, (h + 1) * hv)
            _, vjp = jax.vjp(_glaout_f, of_ref[:, cs], ob_ref[:, cs], g_ref[:, cs], gn_ref[:, cs])
            d_of, _, dg, dgn = vjp(du_ref[:, cs])
            do_ref[:, cs] = d_of
            dg_ref[:, cs] = dg.astype(BF16)
            tmp_ref[:, cs] = dgn
        _accum(dgn_ref, tmp_ref[...])

    return pl.pallas_call(
        body, name=name, grid=(T // rb,),
        in_specs=[_row(rb, D), _row(rb, D), _row(rb, D), _row(rb, D, 1), _vec(D)],
        out_specs=[_row(rb, D), _row(rb, D), _acc2(D)],
        out_shape=[jax.ShapeDtypeStruct((T, D), F32), jax.ShapeDtypeStruct((T, D), BF16), _acc_shape(D)],
        scratch_shapes=[pltpu.VMEM((1, D), F32)],
        compiler_params=_cparams(("arbitrary",)),
    )(du, o_f, o_b, proj, gain)


def _merge_f(bg1, bg2, yg, yp):
    return _sig(bg1) * yg + _sig(bg2) * yp


def _merge_fwd(proj, y_gla, y_pool, rb, name):
    T, D = y_gla.shape

    def body(b1_ref, b2_ref, yg_ref, yp_ref, m_ref):
        m_ref[...] = _merge_f(b1_ref[...], b2_ref[...], yg_ref[...], yp_ref[...]).astype(BF16)

    return pl.pallas_call(
        body, name=name, grid=(T // rb,),
        in_specs=[_row(rb, D, 2), _row(rb, D, 3), _row(rb, D), _row(rb, D)],
        out_specs=_row(rb, D), out_shape=jax.ShapeDtypeStruct((T, D), BF16),
        compiler_params=_cparams(("parallel",)),
    )(proj, proj, y_gla, y_pool)


def _merge_bwd(dm, proj, y_gla, y_pool, rb, name):
    T, D = y_gla.shape

    def body(dm_ref, b1_ref, b2_ref, yg_ref, yp_ref, d1_ref, d2_ref, dyg_ref, dyp_ref):
        _, vjp = jax.vjp(_merge_f, b1_ref[...], b2_ref[...], yg_ref[...], yp_ref[...])
        d1, d2, dyg, dyp = vjp(dm_ref[...])
        d1_ref[...] = d1.astype(BF16)
        d2_ref[...] = d2.astype(BF16)
        dyg_ref[...] = dyg.astype(BF16)
        dyp_ref[...] = dyp.astype(BF16)

    return pl.pallas_call(
        body, name=name, grid=(T // rb,),
        in_specs=[_row(rb, D), _row(rb, D, 2), _row(rb, D, 3), _row(rb, D), _row(rb, D)],
        out_specs=[_row(rb, D)] * 4, out_shape=[jax.ShapeDtypeStruct((T, D), BF16)] * 4,
        compiler_params=_cparams(("parallel",)),
    )(dm, proj, proj, y_gla, y_pool)


def _swiglu_f(gate, up):
    return _silu(gate) * up


def _pool_consts(ctx_len, seq):
    rows = seq // GRID_W
    reps = POOL_TB // GRID_W
    mw, bc, cw, ch, cc = [], [], [], [], []
    for w in POOL_WINDOWS:
        lo, hi = w // 2, w - w // 2 - 1

        def band(n):
            i = np.arange(n)[:, None]
            j = np.arange(n)[None, :]
            return ((j - i >= -lo) & (j - i <= hi)).astype(np.float32)

        def count(n):
            i = np.arange(n)
            return (np.minimum(i + hi + 1, n) - np.maximum(i - lo, 0)).astype(np.float32)

        mw.append(np.kron(np.eye(reps, dtype=np.float32), band(GRID_W)))
        bc.append(band(ctx_len))
        cw.append(np.tile(count(GRID_W), reps)[:, None])
        ch.append(np.repeat(count(rows), GRID_W)[:, None])
        cc.append(count(ctx_len)[:, None])
    mw, bc = np.stack(mw), np.stack(bc)
    return dict(
        mw=jnp.asarray(mw, BF16), mwt=jnp.asarray(mw.transpose(0, 2, 1), BF16),
        bc=jnp.asarray(bc, BF16), bct=jnp.asarray(bc.transpose(0, 2, 1), BF16),
        cw=jnp.asarray(np.stack(cw)), ch=jnp.asarray(np.stack(ch)), cc=jnp.asarray(np.stack(cc)))


def _gspec(*shape):
    nd = len(shape)
    return pl.BlockSpec((None,) + tuple(shape), lambda g: (g,) + (0,) * nd)


def _pool_fwd(proj, pc, wg, scale, ctx_len, D, name):
    T = proj.shape[0]
    seq = T - ctx_len
    dp = D // 2
    pg = dp // len(POOL_WINDOWS)
    nblk = seq // POOL_TB
    padt = POOL_PAD_ROWS * GRID_W
    p_col0 = 5 * D // pg

    def body(p_ref, mw_ref, bc_ref, cw_ref, ch_ref, cc_ref, wg_ref, sc_ref, pd_ref, y0_ref, r_ref, pad_ref):
        g = pl.program_id(0)

        def tail(rows, mean, x):
            pdb = (mean - x).astype(BF16)
            y0 = _dot(pdb, wg_ref[...])
            pd_ref[rows, :] = pdb
            y0_ref[rows, :] = y0
            r_ref[rows, :] = (y0 * sc_ref[...]).astype(BF16)

        xc = p_ref[0:ctx_len, :]
        tail(slice(0, ctx_len), _dot2(bc_ref[...], xc) / cc_ref[...], xc)

        pad_ref[0:padt, :] = jnp.zeros((padt, pg), F32)
        pad_ref[padt + seq:, :] = jnp.zeros((padt, pg), F32)

        def wpass(b, carry):
            rows = pl.ds(pl.multiple_of(ctx_len + b * POOL_TB, CHUNK), POOL_TB)
            dst = pl.ds(pl.multiple_of(padt + b * POOL_TB, CHUNK), POOL_TB)
            pad_ref[dst, :] = _dot2(mw_ref[...], p_ref[rows, :]) / cw_ref[...]
            return carry

        lax.fori_loop(0, nblk, wpass, 0)

        for gi, w in enumerate(POOL_WINDOWS):
            lo, hi = w // 2, w - w // 2 - 1

            @pl.when(g == gi)
            def _():
                def hpass(b, carry):
                    acc = jnp.zeros((POOL_TB, pg), F32)
                    for d in range(-lo, hi + 1):
                        src = pl.ds(pl.multiple_of(padt + b * POOL_TB + d * GRID_W, CHUNK), POOL_TB)
                        acc = acc + pad_ref[src, :]
                    mean = acc / ch_ref[pl.ds(pl.multiple_of(b * POOL_TB, CHUNK), POOL_TB), :]
                    rows = pl.ds(pl.multiple_of(ctx_len + b * POOL_TB, CHUNK), POOL_TB)
                    tail(rows, mean, p_ref[rows, :])
                    return carry

                lax.fori_loop(0, nblk, hpass, 0)

    col = lambda g: (0, g)
    return pl.pallas_call(
        body, name=name, grid=(len(POOL_WINDOWS),),
        in_specs=[pl.BlockSpec((T, pg), lambda g: (0, p_col0 + g)),
                  _gspec(POOL_TB, POOL_TB), _gspec(ctx_len, ctx_len), _gspec(POOL_TB, 1), _gspec(seq, 1),
                  _gspec(ctx_len, 1), _gspec(pg, pg), pl.BlockSpec((1, pg), col)],
        out_specs=[pl.BlockSpec((T, pg), col)] * 3,
        out_shape=[jax.ShapeDtypeStruct((T, dp), BF16), jax.ShapeDtypeStruct((T, dp), F32),
                   jax.ShapeDtypeStruct((T, dp), BF16)],
        scratch_shapes=[pltpu.VMEM((seq + 2 * padt, pg), F32)],
        compiler_params=_cparams(("arbitrary",)),
    )(proj, pc["mw"], pc["bc"], pc["cw"], pc["ch"], pc["cc"], wg, scale)


def _pool_bwd(dr, y0, pd, pc, wg, scale, ctx_len, D, name):
    T = dr.shape[0]
    seq = T - ctx_len
    dp = D // 2
    ng = len(POOL_WINDOWS)
    pg = dp // ng
    nblk = seq // POOL_TB
    padt = POOL_PAD_ROWS * GRID_W

    def body(dr_ref, y0_ref, pd_ref, mwt_ref, bct_ref, cw_ref, ch_ref, cc_ref, wg_ref, sc_ref,
             dp_ref, dsc_ref, gwg_ref, pad_ref, dpd_ref):
        g = pl.program_id(0)
        dsc_ref[...] = jnp.zeros_like(dsc_ref)
        gwg_ref[...] = jnp.zeros_like(gwg_ref)

        def head(rows):
            drv = dr_ref[rows, :]
            dsc_ref[...] += jnp.sum(drv * y0_ref[rows, :], axis=0, keepdims=True)
            dy0 = (drv * sc_ref[...]).astype(BF16)
            gwg_ref[...] += _dot_tn(pd_ref[rows, :], dy0)
            return _dot_nt(dy0, wg_ref[...])

        crow = slice(0, ctx_len)
        dpd_c = head(crow)
        dp_ref[crow, :] = (_dot2(bct_ref[...], dpd_c / cc_ref[...]) - dpd_c).astype(BF16)

        pad_ref[0:padt, :] = jnp.zeros((padt, pg), F32)
        pad_ref[padt + seq:, :] = jnp.zeros((padt, pg), F32)

        def first(b, carry):
            rows = pl.ds(pl.multiple_of(ctx_len + b * POOL_TB, CHUNK), POOL_TB)
            lrows = pl.ds(pl.multiple_of(b * POOL_TB, CHUNK), POOL_TB)
            dst = pl.ds(pl.multiple_of(padt + b * POOL_TB, CHUNK), POOL_TB)
            dpd = head(rows)
            dpd_ref[lrows, :] = dpd
            pad_ref[dst, :] = dpd / ch_ref[lrows, :]
            return carry

        lax.fori_loop(0, nblk, first, 0)

        for gi, w in enumerate(POOL_WINDOWS):
            lo, hi = w // 2, w - w // 2 - 1

            @pl.when(g == gi)
            def _():
                def second(b, carry):
                    acc = jnp.zeros((POOL_TB, pg), F32)
                    for d in range(-hi, lo + 1):
                        src = pl.ds(pl.multiple_of(padt + b * POOL_TB + d * GRID_W, CHUNK), POOL_TB)
                        acc = acc + pad_ref[src, :]
                    rows = pl.ds(pl.multiple_of(ctx_len + b * POOL_TB, CHUNK), POOL_TB)
                    lrows = pl.ds(pl.multiple_of(b * POOL_TB, CHUNK), POOL_TB)
                    dx = _dot2(mwt_ref[...], acc / cw_ref[...]) - dpd_ref[lrows, :]
                    dp_ref[rows, :] = dx.astype(BF16)
                    return carry

                lax.fori_loop(0, nblk, second, 0)

    col = lambda g: (0, g)
    return pl.pallas_call(
        body, name=name, grid=(ng,),
        in_specs=[pl.BlockSpec((T, pg), col), pl.BlockSpec((T, pg), col), pl.BlockSpec((T, pg), col),
                  _gspec(POOL_TB, POOL_TB), _gspec(ctx_len, ctx_len), _gspec(POOL_TB, 1), _gspec(seq, 1),
                  _gspec(ctx_len, 1), _gspec(pg, pg), pl.BlockSpec((1, pg), col)],
        out_specs=[pl.BlockSpec((T, pg), col), pl.BlockSpec((1, pg), col), _gspec(pg, pg)],
        out_shape=[jax.ShapeDtypeStruct((T, dp), BF16), jax.ShapeDtypeStruct((1, dp), F32),
                   jax.ShapeDtypeStruct((ng, pg, pg), F32)],
        scratch_shapes=[pltpu.VMEM((seq + 2 * padt, pg), F32), pltpu.VMEM((seq, pg), F32)],
        compiler_params=_cparams(("arbitrary",)),
    )(dr, y0, pd, pc["mwt"], pc["bct"], pc["cw"], pc["ch"], pc["cc"], wg, scale)


def _loss_head(x2, target, rb, name):
    T, D = x2.shape

    def body(y_ref, t_ref, dy_ref, l_ref):
        i = pl.program_id(0)

        @pl.when(i == 0)
        def _():
            dy_ref[...] = jnp.zeros_like(dy_ref)
            l_ref[...] = jnp.zeros_like(l_ref)

        @pl.when(i > 0)
        def _():
            e = y_ref[...] - t_ref[...]
            dy_ref[...] = e * (1.0 / D)
            l_ref[...] += 0.5 * jnp.sum(jnp.mean(e * e, axis=-1, keepdims=True), axis=0, keepdims=True)

    return pl.pallas_call(
        body, name=name, grid=(T // rb,),
        in_specs=[_row(rb, D), pl.BlockSpec((rb, D), lambda i: (jnp.maximum(i - 1, 0), 0))],
        out_specs=[_row(rb, D), pl.BlockSpec((8, 128), lambda i: (0, 0))],
        out_shape=[jax.ShapeDtypeStruct((T, D), F32), jax.ShapeDtypeStruct((8, 128), F32)],
        compiler_params=_cparams(("arbitrary",)),
    )(x2, target)


def _sum_lead(x, name):
    S, R, C = x.shape

    def body(x_ref, o_ref):
        acc = x_ref[0]
        for s in range(1, S):
            acc = acc + x_ref[s]
        o_ref[...] = acc

    return pl.pallas_call(
        body, name=name, out_shape=jax.ShapeDtypeStruct((R, C), F32),
        compiler_params=_cparams(),
    )(x)


def _silu_rows(cond, name):
    def body(c_ref, o_ref):
        o_ref[...] = _silu(c_ref[...]).astype(BF16)

    return pl.pallas_call(body, name=name, out_shape=jax.ShapeDtypeStruct(cond.shape, BF16),
                          compiler_params=_cparams())(cond)


def _silu_grad(cond, ds, name):
    def body(c_ref, d_ref, o_ref):
        _, vjp = jax.vjp(_silu, c_ref[...])
        o_ref[...] = vjp(d_ref[...])[0]

    return pl.pallas_call(body, name=name, out_shape=jax.ShapeDtypeStruct(cond.shape, F32),
                          compiler_params=_cparams())(cond, ds)


def _adamw_math(g, w_ref, m_ref, v_ref, go_ref, d_ref, mo_ref, vo_ref):
    c1 = 1.0 / (1.0 - ADAM_B1 ** ADAM_STEP)
    c2 = 1.0 / (1.0 - ADAM_B2 ** ADAM_STEP)
    mn = ADAM_B1 * m_ref[...] + (1.0 - ADAM_B1) * g
    vn = ADAM_B2 * v_ref[...] + (1.0 - ADAM_B2) * (g * g)
    go_ref[...] = g
    mo_ref[...] = mn
    vo_ref[...] = vn
    d_ref[...] = -ADAM_LR * ((mn * c1) / (jnp.sqrt(vn * c2) + ADAM_EPS) + ADAM_WD * w_ref[...])


def _adamw(gs, w, m, v, name):
    S, R, C = gs.shape
    cpad = -(-C // 128) * 128
    rt = _pick(R, max(16, (1 << 20) // (4 * cpad)), 16)

    def body(g_ref, w_ref, m_ref, v_ref, go_ref, d_ref, mo_ref, vo_ref):
        g = g_ref[0].astype(F32)
        for s in range(1, S):
            g = g + g_ref[s].astype(F32)
        _adamw_math(g, w_ref, m_ref, v_ref, go_ref, d_ref, mo_ref, vo_ref)

    blk = pl.BlockSpec((rt, C), lambda i: (i, 0))
    return pl.pallas_call(
        body, name=name, grid=(R // rt,),
        in_specs=[pl.BlockSpec((S, rt, C), lambda i: (0, i, 0)), blk, blk, blk],
        out_specs=[blk] * 4, out_shape=[jax.ShapeDtypeStruct((R, C), F32)] * 4,
        compiler_params=_cparams(("parallel",)),
    )(gs, w, m, v)


def _adamw_layer(gs, w, m, v, l, prev, name):
    S, R, C = gs.shape
    L = w.shape[0]
    cpad = -(-C // 128) * 128
    rt = _pick(R, max(16, (1 << 20) // (4 * cpad)), 16)

    def body(*refs):
        g_ref, w_ref, m_ref, v_ref = refs[:4]
        go_ref, d_ref, mo_ref, vo_ref = refs[-4:]
        g = g_ref[0].astype(F32)
        for s in range(1, S):
            g = g + g_ref[s].astype(F32)
        _adamw_math(g, w_ref, m_ref, v_ref, go_ref, d_ref, mo_ref, vo_ref)

    blk = pl.BlockSpec((None, rt, C), lambda i: (l, i, 0))
    in_specs = [pl.BlockSpec((S, rt, C), lambda i: (0, i, 0)), blk, blk, blk]
    args = [gs, w, m, v]
    aliases = {}
    if prev is not None:
        in_specs += [pl.BlockSpec(memory_space=pl.ANY)] * 4
        args += list(prev)
        aliases = {4 + q: q for q in range(4)}
    return pl.pallas_call(
        body, name=name, grid=(R // rt,), in_specs=in_specs,
        out_specs=[blk] * 4, out_shape=[jax.ShapeDtypeStruct((L, R, C), F32)] * 4,
        input_output_aliases=aliases,
        compiler_params=_cparams(("parallel",)),
    )(*args)


def _adamw_nd(gs, w, m, v, name):
    shp = w.shape
    if len(shp) == 1:
        r, c = 1, shp[0]
    else:
        r, c = int(np.prod(shp[:-1])), shp[-1]
    outs = _adamw(gs.reshape(gs.shape[0], r, c), w.reshape(r, c), m.reshape(r, c), v.reshape(r, c), name)
    return [o.reshape(shp) for o in outs]


def kernel(x, c, ctx, c_ctx, w_ada, b_ada, w_in, w_decay_up, b_decay_up, gla_norm_gain, w_pool_group, pool_scale, w_gla_out, w_pool_out, w_out, ln_mix_gain, ln_mix_bias, w_ffn_in, w_ffn_out, ln_ffn_gain, ln_ffn_bias, loss_target, m_c_ctx, m_w_ada, m_b_ada, m_w_in, m_w_decay_up, m_b_decay_up, m_gla_norm_gain, m_w_pool_group, m_pool_scale, m_w_gla_out, m_w_pool_out, m_w_out, m_ln_mix_gain, m_ln_mix_bias, m_w_ffn_in, m_w_ffn_out, m_ln_ffn_gain, m_ln_ffn_bias, v_c_ctx, v_w_ada, v_b_ada, v_w_in, v_w_decay_up, v_b_decay_up, v_gla_norm_gain, v_w_pool_group, v_pool_scale, v_w_gla_out, v_w_pool_out, v_w_out, v_ln_mix_gain, v_ln_mix_bias, v_w_ffn_in, v_w_ffn_out, v_ln_ffn_gain, v_ln_ffn_bias):
    L, D = w_ada.shape[0], w_ada.shape[1]
    seq, ctx_len = x.shape[1], ctx.shape[1]
    T = seq + ctx_len
    rb = ctx_len
    DK = D // 2
    DP = D // 2
    ng = len(POOL_WINDOWS)
    pg = DP // ng
    dff = w_ffn_out.shape[1] * N_DEV
    alpha = (2.0 * L) ** 0.25
    assert seq % rb == 0 and rb % CHUNK == 0 and seq % POOL_TB == 0 and ctx_len % 8 == 0
    xi, yi, ci = _my_pos()
    me = 4 * xi + 2 * yi + ci
    pc = _pool_consts(ctx_len, seq)

    shards = dict(w_in=w_in.astype(BF16), go=w_gla_out.astype(BF16), po=w_pool_out.astype(BF16),
                  out=w_out.astype(BF16), fi=w_ffn_in.astype(BF16), fo=w_ffn_out.astype(BF16),
                  pg=w_pool_group.astype(BF16).reshape(L, ng * pg // N_DEV, pg))
    wkeys = ("w_in", "go", "po", "out", "fi", "fo", "pg")
    o_q, o_k, o_v, o_g, o_a = 0, DK, 2 * DK, 2 * DK + D, 2 * DK + 2 * D
    o_p, o_bg = o_a + 2 * GATE_RANK, o_a + 2 * GATE_RANK + DP

    def prepared(gw):
        w_in_f = jnp.swapaxes(gw["w_in"], 0, 1).reshape(D, -1)
        main = jnp.concatenate([w_in_f[:, o_v:o_g], w_in_f[:, o_g:o_a], w_in_f[:, o_bg:],
                                w_in_f[:, o_q:o_k], w_in_f[:, o_k:o_v], w_in_f[:, o_p:o_bg]], axis=1)
        alr_w = jnp.pad(w_in_f[:, o_a:o_p], ((0, 0), (0, ALR_PAD - 2 * GATE_RANK)))
        pgf = jnp.swapaxes(gw["pg"].reshape(N_DEV, ng, pg // N_DEV, pg), 0, 1).reshape(ng, pg, pg)
        return dict(main=main, alr=alr_w, go=gw["go"].reshape(D, D), po=gw["po"], out=gw["out"].reshape(D, D),
                    fi=gw["fi"][None], fo=gw["fo"].reshape(1, dff, D), pg=pgf)

    def gather_next(fn, names, l, nxt):
        if l + 1 >= L:
            return fn(None)
        res, outs = fn(_gather_job([shards[k] for k in names], l + 1))
        nxt.update(zip(names, outs))
        return res

    gathered = dict(zip(wkeys, _run_job(_gather_job([shards[k] for k in wkeys], 0), "ag_layer0")))

    dku = w_decay_up.shape[-1]
    small_in = jnp.concatenate([c.reshape(-1), w_decay_up.reshape(-1), b_decay_up.reshape(-1)])
    (small_all,) = _gather_flat([small_in], "ag_small")
    c_all = small_all[:, :D]
    n_wdu = L * 2 * GATE_RANK * dku
    wdu_all = small_all[:, D:D + n_wdu].reshape(N_DEV, L, 2, GATE_RANK, dku)
    wdu_full = jnp.transpose(wdu_all, (1, 2, 3, 0, 4)).reshape(L, 2, GATE_RANK, DK)
    bdu_all = small_all[:, D + n_wdu:].reshape(N_DEV, L, 2, dku)
    bdu_full = jnp.transpose(bdu_all, (1, 2, 0, 3)).reshape(L, 1, 2 * DK)
    wdu_bd = jnp.zeros((L, ALR_PAD, 2 * DK), F32)
    wdu_bd = wdu_bd.at[:, :GATE_RANK, :DK].set(wdu_full[:, 0])
    wdu_bd = wdu_bd.at[:, GATE_RANK:2 * GATE_RANK, DK:].set(wdu_full[:, 1]).astype(BF16)

    ncond = 16
    cond = jnp.concatenate([c_all, c_ctx.reshape(1, D), jnp.zeros((ncond - N_DEV - 1, D), F32)], axis=0)
    s_cond = _silu_rows(cond, "silu_cond")
    wsh = w_ada.shape[-1]
    b_ada_mine = lax.dynamic_slice_in_dim(b_ada, me * wsh, wsh, axis=1)
    mod_part = jnp.stack([_mm(s_cond, w_ada, "nn", F32, "mod_mm", bias=b_ada_mine[l:l + 1], b_pre=(l,))
                          for l in range(L)])
    (mod_all,) = _all_gather([mod_part], "ag_mod")
    mod_all = jnp.swapaxes(mod_all, 1, 2).reshape(L, ncond, N_MOD * D)
    mod_lat = lax.dynamic_slice_in_dim(mod_all, me, 1, axis=1)
    mods = jnp.concatenate([mod_all[:, N_DEV:N_DEV + 1], mod_lat], axis=1).reshape(L, 2, 1, N_MOD * D)
    SH_M, SC_M, GT_M, SH_F, SC_F, GT_F = range(N_MOD)

    xa = jnp.concatenate([ctx[0], x[0]], axis=0)
    vec = lambda a, l: a[l].reshape(1, -1)
    saved = []
    h = _mod_fwd(xa, mods[0], SC_M, SH_M, rb, "mod_fwd")
    weights = []
    for l in range(L):
        W = prepared(gathered)
        weights.append(W)
        gathered = {}
        proj = gather_next(lambda j: _mm(h, W["main"], "nn", F32, "mm_in", job=j), ["w_in"], l, gathered)
        alr = _mm(h, W["alr"], "nn", F32, "mm_alr")
        la = _decay_fwd(alr, wdu_bd[l], bdu_full[l], rb, "decay_fwd")
        o_f, s_f = gather_next(lambda j: _gla_fwd(proj, la, False, rb, D, "gla_fwd_f", job=j), ["go", "out"], l,
                               gathered)
        o_b, s_b = gather_next(lambda j: _gla_fwd(proj, la, True, rb, D, "gla_fwd_b", job=j), ["po", "pg"], l,
                               gathered)
        u = _glaout_fwd(o_f, o_b, proj, vec(gla_norm_gain, l), rb, "glaout_fwd")
        y_gla = _mm(u, W["go"], "nn", F32, "mm_go")
        pd, y0, r = _pool_fwd(proj, pc, W["pg"], vec(pool_scale, l), ctx_len, D, "pool_fwd")
        y_pool = _mm(r, W["po"], "nn", F32, "mm_po", b_shard=True)
        m_ = _merge_fwd(proj, y_gla, y_pool, rb, "merge_fwd")
        mix = _mm(m_, W["out"], "nn", F32, "mm_out")
        x1, h2 = _unit_fwd(alpha, xa, mix, mods[l], GT_M, vec(ln_mix_gain, l), vec(ln_mix_bias, l),
                           (mods[l], SC_F, SH_F), rb, "unit_mix_fwd")
        ff, s_ = gather_next(lambda j: _ffn_in_fwd(h2, W["fi"], 0, "mm_fi_swiglu", job=j), ["fi"], l, gathered)
        ffn = gather_next(lambda j: _mm(s_, W["fo"], "nn", F32, "mm_fo", b_pre=(0,), job=j), ["fo"], l, gathered)
        nxt = (mods[l + 1], SC_M, SH_M) if l + 1 < L else None
        x2, h_next = _unit_fwd(alpha, x1, ffn, mods[l], GT_F, vec(ln_ffn_gain, l), vec(ln_ffn_bias, l),
                               nxt, rb, "unit_ffn_fwd")
        saved.append(dict(xa=xa, h=h, proj=proj, alr=alr, la=la, o_f=o_f, o_b=o_b, s_f=s_f, s_b=s_b, u=u,
                          y_gla=y_gla, pd=pd, y0=y0, r=r, y_pool=y_pool, m=m_, mix=mix, x1=x1, h2=h2, ff=ff,
                          s=s_, ffn=ffn))
        xa, h = x2, h_next

    dxo, loss_part = _loss_head(xa, loss_target[0], rb, "loss_head")
    loss = lax.psum(loss_part[0, 0], ("x", "y", "c"))

    big_params = [("w_in", w_in, m_w_in, v_w_in), ("w_gla_out", w_gla_out, m_w_gla_out, v_w_gla_out),
                  ("w_pool_out", w_pool_out, m_w_pool_out, v_w_pool_out), ("w_out", w_out, m_w_out, v_w_out),
                  ("w_ffn_in", w_ffn_in, m_w_ffn_in, v_w_ffn_in), ("w_ffn_out", w_ffn_out, m_w_ffn_out, v_w_ffn_out),
                  ("w_pool_group", w_pool_group, m_w_pool_group, v_w_pool_group)]
    big_out = {nm: None for nm, _, _, _ in big_params}
    g_small = {k: [None] * L for k in ("gla_gain", "pool_scale", "mix_g", "mix_b", "ffn_g", "ffn_b", "wdu", "bdu")}
    dmods = [None] * L
    dh = None
    sum2 = lambda a: a[0] + a[1]
    rows8 = lambda g: g.reshape(N_DEV, g.shape[0] // N_DEV, g.shape[1])

    def apply_adamw(parts, layer):
        for (nm, w, m, v), gs in zip(big_params, parts):
            R, C = gs.shape[1], gs.shape[2]
            big_out[nm] = _adamw_layer(gs, w.reshape(L, R, C), m.reshape(L, R, C), v.reshape(L, R, C), layer,
                                       big_out[nm], "adamw_" + nm)

    LATE = (0, 1, 2, 3, 6)
    late_chunks = None
    arrived = {}

    def behind(fn, job, positions):
        if job is None:
            return fn(None)
        res, outs = fn(job)
        if positions is None:
            return res, outs
        arrived.update(zip(positions, outs))
        return res

    for l in range(L - 1, -1, -1):
        sv = saved[l]
        W = weights[l]
        nxt = (mods[l + 1], SC_M, SH_M) if l + 1 < L else None
        unit = lambda j: _unit_bwd(alpha, dxo, dh, sv["x1"], sv["ffn"], mods[l], GT_F, vec(ln_ffn_gain, l),
                                   vec(ln_ffn_bias, l), nxt, rb, "unit_ffn_bwd", job=j)
        late_pairs = None
        if late_chunks is None:
            res = unit(None)
        else:
            res, sib = behind(unit, _sibling_job(late_chunks), None)
            late_pairs = _pair_adds(late_chunks, sib, "_late")
        dx1, dffn, d_gtf, d_gf, d_bf, d_scm_n, d_shm_n = res
        if nxt is not None:
            dmods[l + 1]["sc_m"], dmods[l + 1]["sh_m"] = d_scm_n, d_shm_n
        dmods[l] = dict(gt_f=d_gtf)
        g_small["ffn_g"][l], g_small["ffn_b"][l] = sum2(d_gf), sum2(d_bf)
        dff_ = behind(lambda j: _ffn_out_dx(dffn, W["fo"], 0, sv["ff"], "mm_fo_dx_swiglu", job=j),
                      _chip_job(late_pairs[1:]) if late_pairs else None, LATE[1:])
        c_fo = rows8(_mm(sv["s"], dffn, "tn", BF16, "mm_fo_dw"))
        dh2 = behind(lambda j: _mm(dff_, W["fi"], "nt", F32, "mm_fi_dx", b_pre=(0,), b_shard=True, a_half=True,
                                   job=j), _chip_job(late_pairs[:1]) if late_pairs else None, LATE[:1])
        c_fi = _mm(sv["h2"], dff_, "tn", BF16, "mm_fi_dw", b_half=True, out_shard=True)
        if late_pairs:
            apply_adamw([arrived[i] for i in range(len(big_params))], l + 1)
            arrived = {}
        ffn_chunks = [c_fi, c_fo]
        res, sib = behind(lambda j: _unit_bwd(
            alpha, dx1, dh2, sv["xa"], sv["mix"], mods[l], GT_M, vec(ln_mix_gain, l), vec(ln_mix_bias, l),
            (mods[l], SC_F, SH_F), rb, "unit_mix_bwd", job=j), _sibling_job(ffn_chunks), None)
        dxa, dmix, d_gtm, d_gm, d_bm, d_scf, d_shf = res
        ffn_pairs = _pair_adds(ffn_chunks, sib, "_ffn")
        dmods[l].update(gt_m=d_gtm, sc_f=d_scf, sh_f=d_shf)
        g_small["mix_g"][l], g_small["mix_b"][l] = sum2(d_gm), sum2(d_bm)
        dm = _mm(dmix, W["out"], "nt", F32, "mm_out_dx")
        c_out = rows8(_mm(sv["m"], dmix, "tn", BF16, "mm_out_dw"))
        dbg1, dbg2, dyg, dyp = _merge_bwd(dm, sv["proj"], sv["y_gla"], sv["y_pool"], rb, "merge_bwd")
        dr = _mm(dyp, W["po"], "nt", F32, "mm_po_dx", b_shard=True)
        c_po = _mm(sv["r"], dyp, "tn", BF16, "mm_po_dw", out_shard=True)
        dp_, d_ps, g_pgl = _pool_bwd(dr, sv["y0"], sv["pd"], pc, W["pg"], vec(pool_scale, l), ctx_len, D, "pool_bwd")
        g_small["pool_scale"][l] = d_ps
        c_pg = jnp.swapaxes(g_pgl.astype(BF16).reshape(ng, N_DEV, pg // N_DEV, pg), 0, 1).reshape(N_DEV, -1, pg)
        du = _mm(dyg, W["go"], "nt", F32, "mm_go_dx")
        c_go = rows8(_mm(sv["u"], dyg, "tn", BF16, "mm_go_dw"))
        do, dg, d_gg = _glaout_bwd(du, sv["o_f"], sv["o_b"], sv["proj"], vec(gla_norm_gain, l), rb, "glaout_bwd")
        g_small["gla_gain"][l] = sum2(d_gg)
        dq_f, dk_f, dv_f, dla_f = behind(lambda j: _gla_bwd(
            sv["proj"], sv["la"], do, sv["s_f"], False, rb, D, None, "gla_bwd_f", job=j),
            _chip_job(ffn_pairs[:1]), (4,))
        dq, dk, dv, dla_b = behind(lambda j: _gla_bwd(
            sv["proj"], sv["la"], do, sv["s_b"], True, rb, D, (dq_f, dk_f, dv_f), "gla_bwd_b", job=j),
            _chip_job(ffn_pairs[1:]), (5,))
        dalr, g_wdu, g_bdu = _decay_bwd(dla_f, dla_b, sv["alr"], wdu_bd[l], bdu_full[l], rb, "decay_bwd")
        g_small["wdu"][l] = jnp.stack([g_wdu[:GATE_RANK, :DK], g_wdu[GATE_RANK:2 * GATE_RANK, DK:]])
        g_small["bdu"][l] = g_bdu.reshape(2, DK)
        dproj = jnp.concatenate([dv, dg, dbg1, dbg2, dq, dk, dp_], axis=1)
        dh_alr = _mm(dalr, W["alr"], "nt", F32, "mm_alr_dx")
        dh = _mm(dproj, W["main"], "nt", F32, "mm_in_dx", add=dh_alr)
        g_main = _mm(sv["h"], dproj, "tn", BF16, "mm_in_dw")
        g_alr = _mm(sv["h"], dalr, "tn", BF16, "mm_alr_dw")
        g_in = jnp.concatenate(
            [g_main[:, 4 * D:4 * D + DK], g_main[:, 4 * D + DK:5 * D], g_main[:, :D], g_main[:, D:2 * D],
             g_alr[:, :2 * GATE_RANK], g_main[:, 5 * D:], g_main[:, 2 * D:4 * D]], axis=1)
        c_in = jnp.swapaxes(g_in.reshape(D, N_DEV, -1), 0, 1)
        late_chunks = [c_in, c_go, c_po, c_out, c_pg]
        dxo = dxa
    sib = _run_job(_sibling_job(late_chunks), "rs_sibling_last")
    arrived.update(zip(LATE, _run_job(_chip_job(_pair_adds(late_chunks, sib, "_last")), "rs_chips_last")))
    apply_adamw([arrived[i] for i in range(len(big_params))], 0)
    grad_xa, d_scm0, d_shm0 = _mod_bwd(dxo, dh, saved[0]["xa"], mods[0], SC_M, SH_M, rb, "mod_bwd")
    dmods[0]["sc_m"], dmods[0]["sh_m"] = d_scm0, d_shm0
    grad_x = grad_xa[ctx_len:].reshape(1, seq, D)

    order = ("sh_m", "sc_m", "gt_m", "sh_f", "sc_f", "gt_f")
    dmod = jnp.stack([jnp.concatenate([dmods[l][k] for k in order], axis=2) for l in range(L)])
    dmod = dmod.reshape(-1)
    sm = lambda k: jnp.stack([a.reshape(-1) for a in g_small[k]]).reshape(-1)
    small_keys = ("gla_gain", "pool_scale", "mix_g", "mix_b", "ffn_g", "ffn_b", "wdu", "bdu")
    small_part = jnp.concatenate([sm(k) for k in small_keys])
    small_g, dmod_g = _gather_flat([small_part, dmod], "ag_small_grads")
    small_sum = _sum_lead(small_g.reshape(N_DEV, -1, 128), "sum_small").reshape(-1)
    off = 0
    rep = {}
    for k, n in zip(small_keys, (L * D, L * DP, L * D, L * D, L * D, L * D, L * 2 * GATE_RANK * DK, L * 2 * DK)):
        rep[k] = small_sum[off:off + n]
        off += n
    g_wdu_mine = lax.dynamic_slice_in_dim(rep["wdu"].reshape(L, 2, GATE_RANK, DK), me * dku, dku, axis=3)
    g_bdu_mine = lax.dynamic_slice_in_dim(rep["bdu"].reshape(L, 2, DK), me * dku, dku, axis=2)

    dmod_all = dmod_g.reshape(N_DEV, L, 2, N_MOD * D)
    dm_ctx = _sum_lead(dmod_all[:, :, 0].reshape(N_DEV, L, N_MOD * D), "sum_dmod_ctx")
    dm_rows = jnp.concatenate([jnp.swapaxes(dmod_all[:, :, 1], 0, 1), dm_ctx[:, None],
                               jnp.zeros((L, ncond - N_DEV - 1, N_MOD * D), F32)], axis=1)
    g_b_ada = _sum_lead(jnp.swapaxes(dm_rows, 0, 1), "sum_b_ada")
    dm_mine = lax.dynamic_slice_in_dim(dm_rows, me * wsh, wsh, axis=2).astype(BF16)
    g_w_ada = jnp.stack([_mm(s_cond, dm_mine[l], "tn", F32, "ada_dw") for l in range(L)])
    ds_part = _sum_lead(jnp.stack([_mm(dm_mine[l], w_ada, "nt", F32, "ada_dx", b_pre=(l,)) for l in range(L)]),
                        "sum_ds")
    (ds_all,) = _gather_flat([ds_part[N_DEV]], "ag_ds")
    ds_ctx = _sum_lead(ds_all.reshape(N_DEV, 1, D), "sum_ds_ctx")
    g_c_ctx = _silu_grad(c_ctx.reshape(1, D), ds_ctx, "silu_grad").reshape(D)

    one = lambda g: g[None]
    small_table = {
        "c_ctx": (one(g_c_ctx), c_ctx, m_c_ctx, v_c_ctx),
        "w_ada": (one(g_w_ada), w_ada, m_w_ada, v_w_ada),
        "b_ada": (one(g_b_ada), b_ada, m_b_ada, v_b_ada),
        "w_decay_up": (one(g_wdu_mine), w_decay_up, m_w_decay_up, v_w_decay_up),
        "b_decay_up": (one(g_bdu_mine), b_decay_up, m_b_decay_up, v_b_decay_up),
        "gla_norm_gain": (one(rep["gla_gain"].reshape(L, D)), gla_norm_gain, m_gla_norm_gain, v_gla_norm_gain),
        "pool_scale": (one(rep["pool_scale"].reshape(L, DP)), pool_scale, m_pool_scale, v_pool_scale),
        "ln_mix_gain": (one(rep["mix_g"].reshape(L, D)), ln_mix_gain, m_ln_mix_gain, v_ln_mix_gain),
        "ln_mix_bias": (one(rep["mix_b"].reshape(L, D)), ln_mix_bias, m_ln_mix_bias, v_ln_mix_bias),
        "ln_ffn_gain": (one(rep["ffn_g"].reshape(L, D)), ln_ffn_gain, m_ln_ffn_gain, v_ln_ffn_gain),
        "ln_ffn_bias": (one(rep["ffn_b"].reshape(L, D)), ln_ffn_bias, m_ln_ffn_bias, v_ln_ffn_bias),
    }
    big_shapes = {nm: w.shape for nm, w, _, _ in big_params}
    names = ("c_ctx", "w_ada", "b_ada", "w_in", "w_decay_up", "b_decay_up", "gla_norm_gain", "w_pool_group",
             "pool_scale", "w_gla_out", "w_pool_out", "w_out", "ln_mix_gain", "ln_mix_bias", "w_ffn_in", "w_ffn_out",
             "ln_ffn_gain", "ln_ffn_bias")
    grads, deltas, new_m, new_v = [], [], [], []
    for nm in names:
        if nm in small_table:
            res = _adamw_nd(*small_table[nm], "adamw_" + nm)
        else:
            res = [o.reshape(big_shapes[nm]) for o in big_out[nm]]
        for lst, o in zip((grads, deltas, new_m, new_v), res):
            lst.append(o)
    return (loss, grad_x, *grads, *deltas, *new_m, *new_v)
```

```python
import functools
import math

import numpy as np
import jax
import jax.numpy as jnp
from jax import lax
from jax.experimental import pallas as pl
from jax.experimental.pallas import tpu as pltpu

F32 = jnp.float32
BF16 = jnp.bfloat16

N_DEV = 8
N_HEADS = 4
GATE_RANK = 16
GATE_NORM = 16.0
CHUNK = 64
GRID_W = 64
POOL_WINDOWS = (2, 4, 8, 16)
N_MOD = 6
LN_EPS = 1e-5
RMS_EPS = 1e-6
ALR_PAD = 128
POOL_TB = 256
POOL_PAD_ROWS = 8
ADAM_LR = 0.001
ADAM_B1 = 0.9
ADAM_B2 = 0.999
ADAM_EPS = 1e-08
ADAM_WD = 0.01
ADAM_STEP = 10
VMEM_LIMIT = 56 * 1024 * 1024
MESH = pl.DeviceIdType.MESH


def _cparams(sem=None):
    return pltpu.CompilerParams(dimension_semantics=sem, vmem_limit_bytes=VMEM_LIMIT)


def _pick(dim, cap, mult):
    best = None
    for d in range(mult, min(dim, cap) + 1, mult):
        if dim % d == 0:
            best = d
    return best if best is not None else dim


def _sig(x):
    return 1.0 / (1.0 + jnp.exp(-x))


def _silu(x):
    return x * _sig(x)


def _dot(a, b):
    return lax.dot_general(a, b, (((1,), (0,)), ((), ())), preferred_element_type=F32)


def _dot_nt(a, b):
    return lax.dot_general(a, b, (((1,), (1,)), ((), ())), preferred_element_type=F32)


def _dot_tn(a, b):
    return lax.dot_general(a, b, (((0,), (0,)), ((), ())), preferred_element_type=F32)


def _split2(x):
    hi = x.astype(BF16)
    lo = (x - hi.astype(F32)).astype(BF16)
    return hi, lo


def _dot2(m_b, x):
    hi, lo = _split2(x)
    return _dot(m_b, hi) + _dot(m_b, lo)


def _dot3(m_b, x):
    h1 = x.astype(BF16)
    r1 = x - h1.astype(F32)
    h2 = r1.astype(BF16)
    h3 = (r1 - h2.astype(F32)).astype(BF16)
    return _dot(m_b, h1) + _dot(m_b, h2) + _dot(m_b, h3)


def _my_pos():
    return lax.axis_index("x"), lax.axis_index("y"), lax.axis_index("c")


def _all_gather(arrs, name):
    n = len(arrs)
    srcs = [a.reshape((a.shape[0], 1) + a.shape[1:]) for a in arrs]
    outs = [jax.ShapeDtypeStruct((a.shape[0], N_DEV) + a.shape[1:], a.dtype) for a in arrs]

    def body(*refs):
        in_refs, out_refs = refs[:n], refs[n:2 * n]
        send_sems, recv_sems, local_sems = refs[2 * n:]
        x, y, c = _my_pos()
        me, sibling = (x, y, c), (x, y, 1 - c)
        chips = [(1 - x, y), (x, 1 - y), (1 - x, 1 - y)]

        def slot(t, pos):
            return out_refs[t].at[:, pl.ds(4 * pos[0] + 2 * pos[1] + pos[2], 1)]

        def copy(t, k, block, to, src=None):
            return pltpu.make_async_remote_copy(
                src_ref=slot(t, block) if src is None else src, dst_ref=slot(t, block),
                send_sem=send_sems.at[t * 7 + k], recv_sem=recv_sems.at[t * 7 + k],
                device_id=to, device_id_type=MESH)

        mine = [pltpu.make_async_copy(in_refs[t], slot(t, me), local_sems.at[t]) for t in range(n)]
        for cp in mine:
            cp.start()
        first = []
        for t in range(n):
            first.append(copy(t, 0, me, sibling, src=in_refs[t]))
            first += [copy(t, 1 + j, me, (*chip, c), src=in_refs[t]) for j, chip in enumerate(chips)]
        for cp in first:
            cp.start()
        passed = []
        for j, chip in enumerate(chips):
            for t in range(n):
                copy(t, 1 + j, (*chip, c), me).wait_recv()
                fwd = copy(t, 4 + j, (*chip, c), sibling)
                fwd.start()
                passed.append(fwd)
        for t in range(n):
            copy(t, 0, sibling, me).wait_recv()
            for j, chip in enumerate(chips):
                copy(t, 4 + j, (*chip, 1 - c), me).wait_recv()
        for cp in first + passed:
            cp.wait_send()
        for cp in mine:
            cp.wait()

    any_spec = pl.BlockSpec(memory_space=pl.ANY)
    res = pl.pallas_call(
        body, name=name, out_shape=outs,
        in_specs=[any_spec] * n, out_specs=[any_spec] * n,
        scratch_shapes=[pltpu.SemaphoreType.DMA((7 * n,)), pltpu.SemaphoreType.DMA((7 * n,)),
                        pltpu.SemaphoreType.DMA((n,))],
        compiler_params=pltpu.CompilerParams(has_side_effects=True),
    )(*srcs)
    return list(res)


def _gather_flat(vecs, name):
    padded = []
    for v in vecs:
        n = v.shape[0]
        padded.append(jnp.pad(v, (0, -n % 128)).reshape(1, -1, 128))
    res = _all_gather(padded, name)
    return [r.reshape(N_DEV, -1)[:, :v.shape[0]] for r, v in zip(res, vecs)]


N_CHIP = 4


def _comm_call(body, name, arrs, outs, n_sems):
    any_spec = pl.BlockSpec(memory_space=pl.ANY)
    n = len(arrs)
    res = pl.pallas_call(
        body, name=name, out_shape=outs,
        in_specs=[any_spec] * n, out_specs=[any_spec] * len(outs),
        scratch_shapes=[pltpu.SemaphoreType.DMA((s,)) for s in n_sems],
        compiler_params=pltpu.CompilerParams(has_side_effects=True),
    )(*arrs)
    return list(res)


def _sibling_job(arrs):
    n = len(arrs)
    outs = [jax.ShapeDtypeStruct((N_CHIP,) + a.shape[1:], a.dtype) for a in arrs]

    def copies(in_refs, out_refs, sems):
        send_sems, recv_sems = sems
        x, y, c = _my_pos()
        return [pltpu.make_async_remote_copy(
            src_ref=in_refs[t].at[pl.ds(2 * k + (1 - c), 1)], dst_ref=out_refs[t].at[pl.ds(k, 1)],
            send_sem=send_sems.at[t * N_CHIP + k], recv_sem=recv_sems.at[t * N_CHIP + k],
            device_id=(x, y, 1 - c), device_id_type=MESH) for t in range(n) for k in range(N_CHIP)]

    def start(in_refs, out_refs, sems):
        for cp in copies(in_refs, out_refs, sems):
            cp.start()

    def finish(in_refs, out_refs, sems):
        cps = copies(in_refs, out_refs, sems)
        for cp in cps:
            cp.wait_recv()
        for cp in cps:
            cp.wait_send()

    return _Job(arrs, outs, (N_CHIP * n, N_CHIP * n), start, finish)


class _Job:
    def __init__(self, arrs, outs, n_sems, start, finish):
        self.arrs, self.outs, self.n_sems, self.start, self.finish = arrs, outs, n_sems, start, finish


def _gather_job(stacked, l):
    n = len(stacked)
    outs = [jax.ShapeDtypeStruct((N_DEV,) + a.shape[1:], a.dtype) for a in stacked]

    def parts(in_refs, out_refs, sems):
        send_sems, recv_sems, local_sems = sems
        x, y, c = _my_pos()
        me, sibling = (x, y, c), (x, y, 1 - c)
        chips = [(1 - x, y), (x, 1 - y), (1 - x, 1 - y)]
        src = lambda t: in_refs[t].at[pl.ds(l, 1)]

        def slot(t, pos):
            return out_refs[t].at[pl.ds(4 * pos[0] + 2 * pos[1] + pos[2], 1)]

        def copy(t, k, block, to, from_input=False):
            return pltpu.make_async_remote_copy(
                src_ref=src(t) if from_input else slot(t, block), dst_ref=slot(t, block),
                send_sem=send_sems.at[t * 7 + k], recv_sem=recv_sems.at[t * 7 + k],
                device_id=to, device_id_type=MESH)

        mine = [pltpu.make_async_copy(src(t), slot(t, me), local_sems.at[t]) for t in range(n)]
        first = []
        for t in range(n):
            first.append(copy(t, 0, me, sibling, True))
            first += [copy(t, 1 + j, me, (*chip, c), True) for j, chip in enumerate(chips)]
        return me, sibling, chips, copy, mine, first

    def start(in_refs, out_refs, sems):
        _, _, _, _, mine, first = parts(in_refs, out_refs, sems)
        for cp in mine + first:
            cp.start()

    def finish(in_refs, out_refs, sems):
        me, sibling, chips, copy, mine, first = parts(in_refs, out_refs, sems)
        passed = []
        for j, chip in enumerate(chips):
            for t in range(n):
                copy(t, 1 + j, (*chip, me[2]), me).wait_recv()
                fwd = copy(t, 4 + j, (*chip, me[2]), sibling)
                fwd.start()
                passed.append(fwd)
        for t in range(n):
            copy(t, 0, sibling, me).wait_recv()
            for j, chip in enumerate(chips):
                copy(t, 4 + j, (*chip, 1 - me[2]), me).wait_recv()
        for cp in first + passed:
            cp.wait_send()
        for cp in mine:
            cp.wait()

    return _Job(stacked, outs, (7 * n, 7 * n, n), start, finish)


def _chip_job(arrs):
    n = len(arrs)
    outs = [jax.ShapeDtypeStruct(a.shape, a.dtype) for a in arrs]

    def parts(in_refs, out_refs, sems):
        send_sems, recv_sems, local_sems = sems
        x, y, c = _my_pos()
        chip = 2 * x + y
        mine, sends, recvs = [], [], []
        for t in range(n):
            mine.append(pltpu.make_async_copy(in_refs[t].at[pl.ds(chip, 1)], out_refs[t].at[pl.ds(chip, 1)],
                                              local_sems.at[t]))
            for m in range(1, N_CHIP):
                px, py = x ^ (m >> 1), y ^ (m & 1)
                peer = 2 * px + py
                sends.append(pltpu.make_async_remote_copy(
                    src_ref=in_refs[t].at[pl.ds(peer, 1)], dst_ref=out_refs[t].at[pl.ds(chip, 1)],
                    send_sem=send_sems.at[t * 3 + m - 1], recv_sem=recv_sems.at[t * 3 + m - 1],
                    device_id=(px, py, c), device_id_type=MESH))
                recvs.append(pltpu.make_async_remote_copy(
                    src_ref=in_refs[t].at[pl.ds(peer, 1)], dst_ref=out_refs[t].at[pl.ds(peer, 1)],
                    send_sem=send_sems.at[t * 3 + m - 1], recv_sem=recv_sems.at[t * 3 + m - 1],
                    device_id=(x, y, c), device_id_type=MESH))
        return mine, sends, recvs

    def start(in_refs, out_refs, sems):
        mine, sends, _ = parts(in_refs, out_refs, sems)
        for cp in mine + sends:
            cp.start()

    def finish(in_refs, out_refs, sems):
        mine, sends, recvs = parts(in_refs, out_refs, sems)
        for cp in recvs:
            cp.wait_recv()
        for cp in sends:
            cp.wait_send()
        for cp in mine:
            cp.wait()

    return _Job(arrs, outs, (3 * n, 3 * n, n), start, finish)


def _run_job(job, name):
    n = len(job.arrs)

    def body(*refs):
        ins, outs, sems = refs[:n], refs[n:n + len(job.outs)], refs[n + len(job.outs):]
        job.start(ins, outs, sems)
        job.finish(ins, outs, sems)

    return _comm_call(body, name, job.arrs, job.outs, job.n_sems)


def _carry(job, body, grid, in_specs, out_specs, out_shape, scratch_shapes, args):
    out_specs = list(out_specs) if isinstance(out_specs, (list, tuple)) else [out_specs]
    out_shape = list(out_shape) if isinstance(out_shape, (list, tuple)) else [out_shape]
    n_ci, n_co, n_cs = len(in_specs), len(out_specs), len(scratch_shapes)
    n_ji, n_jo = len(job.arrs), len(job.outs)
    any_spec = pl.BlockSpec(memory_space=pl.ANY)
    total = int(np.prod(grid))

    def wrapped(*refs):
        cin, jin = refs[:n_ci], refs[n_ci:n_ci + n_ji]
        o0 = n_ci + n_ji
        cout, jout = refs[o0:o0 + n_co], refs[o0 + n_co:o0 + n_co + n_jo]
        s0 = o0 + n_co + n_jo
        cscr, jsems = refs[s0:s0 + n_cs], refs[s0 + n_cs:]
        step = pl.program_id(0)
        for d in range(1, len(grid)):
            step = step * grid[d] + pl.program_id(d)

        @pl.when(step == 0)
        def _():
            job.start(jin, jout, jsems)

        body(*cin, *cout, *cscr)

        @pl.when(step == total - 1)
        def _():
            job.finish(jin, jout, jsems)

    return (wrapped, list(in_specs) + [any_spec] * n_ji, out_specs + [any_spec] * n_jo,
            out_shape + list(job.outs),
            list(scratch_shapes) + [pltpu.SemaphoreType.DMA((s,)) for s in job.n_sems],
            list(args) + list(job.arrs), n_co)


def _pair_add(g, r, name):
    _, R, C = g.shape
    cpad = -(-C // 128) * 128
    rt = _pick(R, max(16, (1 << 20) // (2 * cpad)), 16)
    cidx = lax.axis_index("c").astype(jnp.int32).reshape(1)

    def body(c_ref, g_ref, r_ref, o_ref):
        o_ref[...] = (g_ref[...].astype(F32) + r_ref[...].astype(F32)).astype(o_ref.dtype)

    return pl.pallas_call(
        body, name=name, out_shape=jax.ShapeDtypeStruct((N_CHIP, R, C), g.dtype),
        grid_spec=pltpu.PrefetchScalarGridSpec(
            num_scalar_prefetch=1, grid=(N_CHIP, R // rt),
            in_specs=[pl.BlockSpec((None, rt, C), lambda k, i, c_ref: (2 * k + c_ref[0], i, 0)),
                      pl.BlockSpec((None, rt, C), lambda k, i, c_ref: (k, i, 0))],
            out_specs=pl.BlockSpec((None, rt, C), lambda k, i, c_ref: (k, i, 0))),
        compiler_params=_cparams(("parallel", "parallel")),
    )(cidx, g, r)


def _pair_adds(chunks, sib, tag):
    return [_pair_add(g, r, "rs_pair_add" + tag) for g, r in zip(chunks, sib)]


def _mm(a, b, mode, out_dtype=F32, name="mm", bias=None, add=None, b_pre=(), b_shard=False,
        a_half=False, b_half=False, out_shard=False, job=None):
    npre = len(b_pre)
    bshape = b.shape[npre:]
    if mode == "nn":
        M, K = a.shape
        if b_shard:
            K2, N = bshape[1], N_DEV * bshape[2]
        else:
            K2, N = bshape
    elif mode == "nt":
        M, K = (a.shape[1], 2 * a.shape[2]) if a_half else a.shape
        if b_shard:
            N, K2 = bshape[1], N_DEV * bshape[2]
        else:
            N, K2 = bshape
    else:
        K, M = a.shape
        K2, N = (b.shape[1], 2 * b.shape[2]) if b_half else bshape
    assert K == K2, (a.shape, b.shape, mode)
    tm = _pick(M, 1100, 16) if mode != "tn" else _pick(M, 1024, 128)
    tn = _pick(N, 1024, 128)
    tk = _pick(K, 2816 if mode == "nt" else 2176, 128)
    if b_shard and mode == "nn":
        tn = bshape[2]
    if b_shard and mode == "nt":
        tk = bshape[2]
    if out_shard:
        tn = N // N_DEV
    nk = K // tk
    none_pre = (None,) * npre
    if mode == "nn":
        a_spec = pl.BlockSpec((tm, tk), lambda i, j, k: (i, k))
        if b_shard:
            b_spec = pl.BlockSpec(none_pre + (None, tk, tn), lambda i, j, k: b_pre + (j, k, 0))
        else:
            b_spec = pl.BlockSpec(none_pre + (tk, tn), lambda i, j, k: b_pre + (k, j))
        dot = _dot
    elif mode == "nt":
        if a_half:
            nkh = a.shape[2] // tk
            a_spec = pl.BlockSpec((None, tm, tk), lambda i, j, k: (k // nkh, i, k % nkh))
        else:
            a_spec = pl.BlockSpec((tm, tk), lambda i, j, k: (i, k))
        if b_shard:
            b_spec = pl.BlockSpec(none_pre + (None, tn, tk), lambda i, j, k: b_pre + (k, j, 0))
        else:
            b_spec = pl.BlockSpec(none_pre + (tn, tk), lambda i, j, k: b_pre + (j, k))
        dot = _dot_nt
    else:
        a_spec = pl.BlockSpec((tk, tm), lambda i, j, k: (k, i))
        if b_half:
            nnh = b.shape[2] // tn
            b_spec = pl.BlockSpec((None, tk, tn), lambda i, j, k: (j // nnh, k, j % nnh))
        else:
            b_spec = pl.BlockSpec(none_pre + (tk, tn), lambda i, j, k: b_pre + (k, j))
        dot = _dot_tn
    in_specs = [a_spec, b_spec]
    args = [a, b]
    if bias is not None:
        in_specs.append(pl.BlockSpec((1, tn), lambda i, j, k: (0, j)))
        args.append(bias)
    if add is not None:
        in_specs.append(pl.BlockSpec((tm, tn), lambda i, j, k: (i, j)))
        args.append(add)
    n_in = len(args)
    if out_shard:
        o_spec = pl.BlockSpec((None, tm, tn), lambda i, j, k: (j, i, 0))
        o_shape = jax.ShapeDtypeStruct((N_DEV, M, tn), out_dtype)
    else:
        o_spec = pl.BlockSpec((tm, tn), lambda i, j, k: (i, j))
        o_shape = jax.ShapeDtypeStruct((M, N), out_dtype)

    def body(*refs):
        a_ref, b_ref = refs[0], refs[1]
        bias_ref = refs[2] if bias is not None else None
        add_ref = refs[n_in - 1] if add is not None else None
        o_ref = refs[n_in]
        p = dot(a_ref[...].astype(BF16), b_ref[...].astype(BF16))

        def finish(acc):
            if bias_ref is not None:
                acc = acc + bias_ref[...]
            if add_ref is not None:
                acc = acc + add_ref[...]
            o_ref[...] = acc.astype(o_ref.dtype)

        if nk == 1:
            finish(p)
        else:
            acc_ref = refs[-1]
            k = pl.program_id(2)

            @pl.when(k == 0)
            def _():
                acc_ref[...] = p

            @pl.when(k > 0)
            def _():
                acc_ref[...] += p

            @pl.when(k == nk - 1)
            def _():
                finish(acc_ref[...])

    grid = (M // tm, N // tn, nk)
    scratch = [pltpu.VMEM((tm, tn), F32)] if nk > 1 else []
    if job is None:
        return pl.pallas_call(
            body, name=name, grid=grid, in_specs=in_specs, out_specs=o_spec, out_shape=o_shape,
            scratch_shapes=scratch, compiler_params=_cparams(("parallel", "parallel", "arbitrary")),
        )(*args)
    return _call_carrying(job, body, name, grid, in_specs, o_spec, o_shape, scratch, args)


def _call_carrying(job, body, name, grid, in_specs, out_specs, out_shape, scratch, args):
    body, in_specs, out_specs, out_shape, scratch, args, n_co = _carry(
        job, body, grid, in_specs, out_specs, out_shape, scratch, args)
    res = pl.pallas_call(
        body, name=name, grid=grid, in_specs=in_specs, out_specs=out_specs, out_shape=out_shape,
        scratch_shapes=scratch, compiler_params=_cparams(("arbitrary",) * len(grid)),
    )(*args)
    own = res[0] if n_co == 1 else list(res[:n_co])
    return own, list(res[n_co:])


def _ffn_in_fwd(h2, w_fi, l, name, job=None):
    T, D = h2.shape
    n = w_fi.shape[3]
    nh = N_DEV // 2
    dff = nh * n
    tm = _pick(T, 600, 16)

    def body(a_ref, bg_ref, bu_ref, ff_ref, s_ref):
        a = a_ref[...]
        g = _dot(a, bg_ref[...])
        u = _dot(a, bu_ref[...])
        ff_ref[0] = g
        ff_ref[1] = u
        s_ref[...] = _swiglu_f(g, u).astype(BF16)

    grid = (T // tm, nh)
    in_specs = [pl.BlockSpec((tm, D), lambda i, j: (i, 0)),
                pl.BlockSpec((None, None, D, n), lambda i, j: (l, j, 0, 0)),
                pl.BlockSpec((None, None, D, n), lambda i, j: (l, nh + j, 0, 0))]
    out_specs = [pl.BlockSpec((2, tm, n), lambda i, j: (0, i, j)), pl.BlockSpec((tm, n), lambda i, j: (i, j))]
    out_shape = [jax.ShapeDtypeStruct((2, T, dff), F32), jax.ShapeDtypeStruct((T, dff), BF16)]
    args = (h2, w_fi, w_fi)
    if job is None:
        return pl.pallas_call(
            body, name=name, grid=grid, in_specs=in_specs, out_specs=out_specs, out_shape=out_shape,
            compiler_params=_cparams(("parallel", "parallel")),
        )(*args)
    return _call_carrying(job, body, name, grid, in_specs, out_specs, out_shape, [], args)


def _ffn_out_dx(dffn, w_fo, l, ff, name, job=None):
    T, D = dffn.shape
    dff = ff.shape[2]
    tm = _pick(T, 600, 16)
    tw = _pick(dff, 1408, 128)

    def body(a_ref, b_ref, ff_ref, o_ref):
        ds = _dot_nt(a_ref[...], b_ref[...])
        _, vjp = jax.vjp(_swiglu_f, ff_ref[0], ff_ref[1])
        dg, du = vjp(ds)
        o_ref[0] = dg.astype(BF16)
        o_ref[1] = du.astype(BF16)

    grid = (T // tm, dff // tw)
    in_specs = [pl.BlockSpec((tm, D), lambda i, j: (i, 0)),
                pl.BlockSpec((None, tw, D), lambda i, j: (l, j, 0)),
                pl.BlockSpec((2, tm, tw), lambda i, j: (0, i, j))]
    out_specs = pl.BlockSpec((2, tm, tw), lambda i, j: (0, i, j))
    out_shape = jax.ShapeDtypeStruct((2, T, dff), BF16)
    args = (dffn, w_fo, ff)
    if job is None:
        return pl.pallas_call(
            body, name=name, grid=grid, in_specs=in_specs, out_specs=out_specs, out_shape=out_shape,
            compiler_params=_cparams(("parallel", "parallel")),
        )(*args)
    return _call_carrying(job, body, name, grid, in_specs, out_specs, out_shape, [], args)


def _row(rb, w, col=0):
    return pl.BlockSpec((rb, w), lambda i: (i, col))


def _modspec(d, sec):
    return pl.BlockSpec((None, 1, d), lambda i: (jnp.minimum(i, 1), 0, sec))


def _vec(w):
    return pl.BlockSpec((1, w), lambda i: (0, 0))


def _acc2(w):
    return pl.BlockSpec((None, 1, w), lambda i: (jnp.minimum(i, 1), 0, 0))


def _accum(ref, val):
    i = pl.program_id(0)

    @pl.when(i <= 1)
    def _():
        ref[...] = val

    @pl.when(i > 1)
    def _():
        ref[...] += val


def _acc_shape(w):
    return jax.ShapeDtypeStruct((2, 1, w), F32)


def _mod_f(x, sc, sh):
    return x * (1.0 + sc) + sh


def _mod_fwd(xa, mod, sec_sc, sec_sh, rb, name):
    T, D = xa.shape

    def body(x_ref, sc_ref, sh_ref, h_ref):
        h_ref[...] = _mod_f(x_ref[...], sc_ref[...], sh_ref[...]).astype(BF16)

    return pl.pallas_call(
        body, name=name, grid=(T // rb,),
        in_specs=[_row(rb, D), _modspec(D, sec_sc), _modspec(D, sec_sh)],
        out_specs=_row(rb, D), out_shape=jax.ShapeDtypeStruct((T, D), BF16),
        compiler_params=_cparams(("parallel",)),
    )(xa, mod, mod)


def _mod_bwd(dxa, dh, xa, mod, sec_sc, sec_sh, rb, name):
    T, D = xa.shape

    def body(dxa_ref, dh_ref, x_ref, sc_ref, sh_ref, dx_ref, dsc_ref, dsh_ref):
        _, vjp = jax.vjp(_mod_f, x_ref[...], sc_ref[...], sh_ref[...])
        dx, dsc, dsh = vjp(dh_ref[...])
        dx_ref[...] = dxa_ref[...] + dx
        _accum(dsc_ref, dsc)
        _accum(dsh_ref, dsh)

    return pl.pallas_call(
        body, name=name, grid=(T // rb,),
        in_specs=[_row(rb, D), _row(rb, D), _row(rb, D), _modspec(D, sec_sc), _modspec(D, sec_sh)],
        out_specs=[_row(rb, D), _acc2(D), _acc2(D)],
        out_shape=[jax.ShapeDtypeStruct((T, D), F32), _acc_shape(D), _acc_shape(D)],
        compiler_params=_cparams(("arbitrary",)),
    )(dxa, dh, xa, mod, mod)


def _ln_f(alpha, x, mix, gt, gain, bias):
    z = alpha * x + gt * mix
    mu = jnp.mean(z, axis=-1, keepdims=True)
    zc = z - mu
    var = jnp.mean(zc * zc, axis=-1, keepdims=True)
    return zc * lax.rsqrt(var + LN_EPS) * gain + bias


def _unit_fwd(alpha, x, mix, mod, sec_gt, gain, bias, next_mod, rb, name):
    T, D = x.shape
    has_mod = next_mod is not None

    def body(*refs):
        if has_mod:
            x_ref, mix_ref, gt_ref, g_ref, b_ref, sc_ref, sh_ref, xo_ref, h_ref = refs
        else:
            x_ref, mix_ref, gt_ref, g_ref, b_ref, xo_ref = refs
        xo = _ln_f(alpha, x_ref[...], mix_ref[...], gt_ref[...], g_ref[...], b_ref[...])
        xo_ref[...] = xo
        if has_mod:
            h_ref[...] = _mod_f(xo, sc_ref[...], sh_ref[...]).astype(BF16)

    in_specs = [_row(rb, D), _row(rb, D), _modspec(D, sec_gt), _vec(D), _vec(D)]
    args = [x, mix, mod, gain, bias]
    out_specs = [_row(rb, D)]
    out_shape = [jax.ShapeDtypeStruct((T, D), F32)]
    if has_mod:
        nm, s_sc, s_sh = next_mod
        in_specs += [_modspec(D, s_sc), _modspec(D, s_sh)]
        args += [nm, nm]
        out_specs.append(_row(rb, D))
        out_shape.append(jax.ShapeDtypeStruct((T, D), BF16))
    res = pl.pallas_call(
        body, name=name, grid=(T // rb,), in_specs=in_specs, out_specs=out_specs, out_shape=out_shape,
        compiler_params=_cparams(("parallel",)),
    )(*args)
    return (res[0], res[1]) if has_mod else (res[0], None)


def _unit_bwd(alpha, dxo, dh, x, mix, mod, sec_gt, gain, bias, next_mod, rb, name, job=None):
    T, D = x.shape
    has_mod = next_mod is not None

    def body(*refs):
        if has_mod:
            (dxo_ref, dh_ref, x_ref, mix_ref, gt_ref, g_ref, b_ref, sc_ref, sh_ref,
             dx_ref, dmix_ref, dgt_ref, dg_ref, db_ref, dsc_ref, dsh_ref) = refs
        else:
            (dxo_ref, x_ref, mix_ref, gt_ref, g_ref, b_ref,
             dx_ref, dmix_ref, dgt_ref, dg_ref, db_ref) = refs
        xo, vjp = jax.vjp(functools.partial(_ln_f, alpha), x_ref[...], mix_ref[...], gt_ref[...],
                          g_ref[...], b_ref[...])
        dxo_t = dxo_ref[...]
        if has_mod:
            _, vjp_m = jax.vjp(_mod_f, xo, sc_ref[...], sh_ref[...])
            dxo_m, dsc, dsh = vjp_m(dh_ref[...])
            dxo_t = dxo_t + dxo_m
            _accum(dsc_ref, dsc)
            _accum(dsh_ref, dsh)
        dx, dmix, dgt, dg, db = vjp(dxo_t)
        dx_ref[...] = dx
        dmix_ref[...] = dmix.astype(BF16)
        _accum(dgt_ref, dgt)
        _accum(dg_ref, dg)
        _accum(db_ref, db)

    in_specs = [_row(rb, D)]
    args = [dxo]
    if has_mod:
        in_specs.append(_row(rb, D))
        args.append(dh)
    in_specs += [_row(rb, D), _row(rb, D), _modspec(D, sec_gt), _vec(D), _vec(D)]
    args += [x, mix, mod, gain, bias]
    out_specs = [_row(rb, D), _row(rb, D), _acc2(D), _acc2(D), _acc2(D)]
    out_shape = [jax.ShapeDtypeStruct((T, D), F32), jax.ShapeDtypeStruct((T, D), BF16),
                 _acc_shape(D), _acc_shape(D), _acc_shape(D)]
    if has_mod:
        nm, s_sc, s_sh = next_mod
        in_specs += [_modspec(D, s_sc), _modspec(D, s_sh)]
        args += [nm, nm]
        out_specs += [_acc2(D), _acc2(D)]
        out_shape += [_acc_shape(D), _acc_shape(D)]
    if job is None:
        res = pl.pallas_call(
            body, name=name, grid=(T // rb,), in_specs=in_specs, out_specs=out_specs, out_shape=out_shape,
            compiler_params=_cparams(("arbitrary",)),
        )(*args)
        job_res = None
    else:
        res, job_res = _call_carrying(job, body, name, (T // rb,), in_specs, out_specs, out_shape, [], args)
    res = list(res) if has_mod else list(res) + [None, None]
    return res if job is None else (res, job_res)


def _log_sigmoid(z):
    return jnp.minimum(z, 0.0) - jnp.log(1.0 + jnp.exp(-jnp.abs(z)))


def _decay_fwd(alr, wdu, bdu, rb, name):
    T = alr.shape[0]
    W = wdu.shape[1]

    def body(a_ref, w_ref, b_ref, la_ref):
        z = _dot(a_ref[...].astype(BF16), w_ref[...]) + b_ref[...]
        la_ref[...] = _log_sigmoid(z) * (1.0 / GATE_NORM)

    return pl.pallas_call(
        body, name=name, grid=(T // rb,),
        in_specs=[_row(rb, ALR_PAD), pl.BlockSpec((ALR_PAD, W), lambda i: (0, 0)), _vec(W)],
        out_specs=_row(rb, W), out_shape=jax.ShapeDtypeStruct((T, W), F32),
        compiler_params=_cparams(("parallel",)),
    )(alr, wdu, bdu)


def _decay_bwd(dla_f, dla_b, alr, wdu, bdu, rb, name):
    T = alr.shape[0]
    W = wdu.shape[1]
    DK = W // 2

    def body(df_ref, db_ref, a_ref, w_ref, b_ref, dalr_ref, gw_ref, gb_ref):
        i = pl.program_id(0)
        ab = a_ref[...].astype(BF16)
        z = _dot(ab, w_ref[...]) + b_ref[...]
        dla = jnp.concatenate([df_ref[...], db_ref[...]], axis=1)
        dz = dla * _sig(-z) * (1.0 / GATE_NORM)
        dzb = dz.astype(BF16)
        dalr_ref[...] = _dot_nt(dzb, w_ref[...]).astype(BF16)
        gw = _dot_tn(ab, dzb)
        gb = jnp.sum(dz, axis=0, keepdims=True)

        @pl.when(i == 0)
        def _():
            gw_ref[...] = gw
            gb_ref[...] = gb

        @pl.when(i > 0)
        def _():
            gw_ref[...] += gw
            gb_ref[...] += gb

    return pl.pallas_call(
        body, name=name, grid=(T // rb,),
        in_specs=[_row(rb, DK), _row(rb, DK), _row(rb, ALR_PAD), pl.BlockSpec((ALR_PAD, W), lambda i: (0, 0)), _vec(W)],
        out_specs=[_row(rb, ALR_PAD), pl.BlockSpec((ALR_PAD, W), lambda i: (0, 0)), _vec(W)],
        out_shape=[jax.ShapeDtypeStruct((T, ALR_PAD), BF16), jax.ShapeDtypeStruct((ALR_PAD, W), F32),
                   jax.ShapeDtypeStruct((1, W), F32)],
        compiler_params=_cparams(("arbitrary",)),
    )(dla_f, dla_b, alr, wdu, bdu)


def _tri(rev, ncb):
    m = np.tril(np.ones((CHUNK, CHUNK), np.float32))
    return jnp.asarray(np.kron(np.eye(ncb, dtype=np.float32), m.T if rev else m), BF16)


def _gla_block_common(q_ref, k_ref, v_ref, la_ref, tri_ref, ck, cv, rev, scale_q, ncb):
    mid = CHUNK // 2 if rev else CHUNK // 2 - 1
    last_i = 0 if rev else CHUNK - 1
    rb = ncb * CHUNK
    hk = ck.stop - ck.start
    q = q_ref[:, ck] * scale_q
    k = k_ref[:, ck]
    v = v_ref[:, cv]
    cum = _dot3(tri_ref[...], la_ref[:, ck])
    per_chunk = lambda i: jnp.concatenate(
        [jnp.broadcast_to(cum[c * CHUNK + i:c * CHUNK + i + 1, :], (CHUNK, hk)) for c in range(ncb)], axis=0)
    ref, last = per_chunk(mid), per_chunk(last_i)
    e_q = jnp.exp(cum - ref)
    e_k = jnp.exp(ref - cum)
    e_c = jnp.exp(cum)
    e_s = jnp.exp(last - cum)
    e_l = [jnp.exp(cum[c * CHUNK + last_i:c * CHUNK + last_i + 1, :]) for c in range(ncb)]
    ri = lax.broadcasted_iota(jnp.int32, (rb, rb), 0)
    ci = lax.broadcasted_iota(jnp.int32, (rb, rb), 1)
    mask = (ri // CHUNK == ci // CHUNK) & ((ci >= ri) if rev else (ci <= ri))
    return q, k, v, e_q, e_k, e_c, e_s, e_l, mask, last_i


GLA_HEADS_PER_STEP = 1


def _gla_specs(rb, hk, hv, D, rbmap, rev):
    hp = GLA_HEADS_PER_STEP
    q_col0 = 4 * D // (hp * hk)
    k_col0 = q_col0 + N_HEADS // hp
    la_col0 = N_HEADS // hp if rev else 0
    return [
        pl.BlockSpec((rb, hp * hk), lambda h, i: (rbmap(i), q_col0 + h)),
        pl.BlockSpec((rb, hp * hk), lambda h, i: (rbmap(i), k_col0 + h)),
        pl.BlockSpec((rb, hp * hv), lambda h, i: (rbmap(i), h)),
        pl.BlockSpec((rb, hp * hk), lambda h, i: (rbmap(i), la_col0 + h)),
        pl.BlockSpec((rb, rb), lambda h, i: (0, 0)),
    ]


def _gla_call(job, body, name, grid, in_specs, out_specs, out_shape, scratch, args):
    if job is None:
        return pl.pallas_call(
            body, name=name, grid=grid, in_specs=in_specs, out_specs=out_specs, out_shape=out_shape,
            scratch_shapes=scratch, compiler_params=_cparams(("parallel", "arbitrary")),
        )(*args)
    return _call_carrying(job, body, name, grid, in_specs, out_specs, out_shape, scratch, args)


def _gla_fwd(proj, la, rev, rb, D, name, job=None):
    T = proj.shape[0]
    nb = T // rb
    ncb = rb // CHUNK
    hp = GLA_HEADS_PER_STEP
    hk, hv = D // 2 // N_HEADS, D // N_HEADS
    scale_q = float(hk) ** -0.5
    rbmap = (lambda i: jnp.where(i == 0, 0, nb - i)) if rev else (lambda i: i)

    def body(q_ref, k_ref, v_ref, la_ref, tri_ref, o_ref, s_ref, st_ref):
        @pl.when(pl.program_id(1) == 0)
        def _():
            st_ref[...] = jnp.zeros_like(st_ref)

        order = range(ncb - 1, -1, -1) if rev else range(ncb)
        for hh in range(hp):
            ck, cv = slice(hh * hk, (hh + 1) * hk), slice(hh * hv, (hh + 1) * hv)
            q, k, v, e_q, e_k, e_c, e_s, e_l, mask, _ = _gla_block_common(
                q_ref, k_ref, v_ref, la_ref, tri_ref, ck, cv, rev, scale_q, ncb)
            vb = v.astype(BF16)
            a = jnp.where(mask, _dot_nt((q * e_q).astype(BF16), (k * e_k).astype(BF16)), 0.0)
            o_intra = _dot(a.astype(BF16), vb)
            qc = (q * e_c).astype(BF16)
            ks = (k * e_s).astype(BF16)
            st = st_ref[hh]
            for cc in order:
                rows = slice(cc * CHUNK, (cc + 1) * CHUNK)
                s_ref[hh, cc] = st
                o_ref[rows, cv] = o_intra[rows] + _dot_nt(qc[rows], st.astype(BF16))
                st = st * e_l[cc] + _dot_tn(vb[rows], ks[rows])
            st_ref[hh] = st

    return _gla_call(
        job, body, name, (N_HEADS // hp, nb), _gla_specs(rb, hk, hv, D, rbmap, rev),
        [pl.BlockSpec((rb, hp * hv), lambda h, i: (rbmap(i), h)),
         pl.BlockSpec((hp, ncb, hv, hk), lambda h, i: (h, rbmap(i), 0, 0))],
        [jax.ShapeDtypeStruct((T, D), F32), jax.ShapeDtypeStruct((N_HEADS, T // CHUNK, hv, hk), F32)],
        [pltpu.VMEM((hp, hv, hk), F32)], (proj, proj, proj, la, _tri(rev, ncb)))


def _gla_bwd(proj, la, do, states, rev, rb, D, prev, name, job=None):
    T = proj.shape[0]
    nb = T // rb
    ncb = rb // CHUNK
    hp = GLA_HEADS_PER_STEP
    hk, hv = D // 2 // N_HEADS, D // N_HEADS
    DK = D // 2
    scale_q = float(hk) ** -0.5
    if rev:
        rbmap = lambda i: jnp.where(i == nb - 1, 0, i + 1)
    else:
        rbmap = lambda i: nb - 1 - i
    has_prev = prev is not None
    out_dt = BF16 if has_prev else F32

    def body(*refs):
        q_ref, k_ref, v_ref, la_ref, tri_ref, trit_ref, do_ref, s_ref = refs[:8]
        n_in = 11 if has_prev else 8
        pq_ref, pk_ref, pv_ref = refs[8:11] if has_prev else (None, None, None)
        dq_ref, dk_ref, dv_ref, dla_ref, ds_ref = refs[n_in:]

        @pl.when(pl.program_id(1) == 0)
        def _():
            ds_ref[...] = jnp.zeros_like(ds_ref)

        order = range(ncb) if rev else range(ncb - 1, -1, -1)
        for hh in range(hp):
            ck, cv = slice(hh * hk, (hh + 1) * hk), slice(hh * hv, (hh + 1) * hv)
            q, k, v, e_q, e_k, e_c, e_s, e_l, mask, last_i = _gla_block_common(
                q_ref, k_ref, v_ref, la_ref, tri_ref, ck, cv, rev, scale_q, ncb)
            vb = v.astype(BF16)
            qi = (q * e_q).astype(BF16)
            ki = (k * e_k).astype(BF16)
            qc = (q * e_c).astype(BF16)
            ks = (k * e_s).astype(BF16)
            a = jnp.where(mask, _dot_nt(qi, ki), 0.0).astype(BF16)
            dob = do_ref[:, cv].astype(BF16)
            da = jnp.where(mask, _dot_nt(dob, vb), 0.0).astype(BF16)
            dv_intra = _dot_tn(a, dob)
            dq_intra = _dot(da, ki) * e_q
            dk_intra = _dot_tn(da, qi) * e_k
            rowi = lax.broadcasted_iota(jnp.int32, (CHUNK, hk), 0)
            dst = ds_ref[hh]
            for cc in order:
                rows = slice(cc * CHUNK, (cc + 1) * CHUNK)
                st0 = s_ref[hh, cc]
                dstb = dst.astype(BF16)
                dv = dv_intra[rows] + _dot_nt(ks[rows], dstb)
                dk_inter = _dot(vb[rows], dstb) * e_s[rows]
                dq_s = dq_intra[rows] + _dot(dob[rows], st0.astype(BF16)) * e_c[rows]
                dk = dk_intra[rows] + dk_inter
                extra = (jnp.sum(k[rows] * dk_inter, axis=0, keepdims=True)
                         + e_l[cc] * jnp.sum(dst * st0, axis=0, keepdims=True))
                dla_ref[rows, ck] = q[rows] * dq_s - k[rows] * dk + jnp.where(rowi == last_i, extra, 0.0)
                dq = dq_s * scale_q
                if has_prev:
                    dq = dq + pq_ref[rows, ck]
                    dk = dk + pk_ref[rows, ck]
                    dv = dv + pv_ref[rows, cv]
                dq_ref[rows, ck] = dq.astype(out_dt)
                dk_ref[rows, ck] = dk.astype(out_dt)
                dv_ref[rows, cv] = dv.astype(out_dt)
                dst = dst * e_l[cc] + _dot_tn(dob[rows], qc[rows])
            ds_ref[hh] = dst
            dla_ref[:, ck] = _dot3(trit_ref[...], dla_ref[:, ck])

    in_specs = _gla_specs(rb, hk, hv, D, rbmap, rev)
    in_specs += [pl.BlockSpec((rb, rb), lambda h, i: (0, 0)),
                 pl.BlockSpec((rb, hp * hv), lambda h, i: (rbmap(i), h)),
                 pl.BlockSpec((hp, ncb, hv, hk), lambda h, i: (h, rbmap(i), 0, 0))]
    args = [proj, proj, proj, la, _tri(rev, ncb), _tri(not rev, ncb), do, states]
    hk_spec = pl.BlockSpec((rb, hp * hk), lambda h, i: (rbmap(i), h))
    hv_spec = pl.BlockSpec((rb, hp * hv), lambda h, i: (rbmap(i), h))
    if has_prev:
        in_specs += [hk_spec, hk_spec, hv_spec]
        args += list(prev)
    return _gla_call(
        job, body, name, (N_HEADS // hp, nb), in_specs, [hk_spec, hk_spec, hv_spec, hk_spec],
        [jax.ShapeDtypeStruct((T, DK), out_dt), jax.ShapeDtypeStruct((T, DK), out_dt),
         jax.ShapeDtypeStruct((T, D), out_dt), jax.ShapeDtypeStruct((T, DK), F32)],
        [pltpu.VMEM((hp, hv, hk), F32)], args)


def _glaout_f(of, ob, g, gain):
    o = of + ob
    n = o * lax.rsqrt(jnp.mean(o * o, axis=-1, keepdims=True) + RMS_EPS)
    return n * gain * _silu(g)


def _glaout_fwd(o_f, o_b, proj, gain, rb, name):
    T, D = o_f.shape
    hv = D // N_HEADS

    def body(of_ref, ob_ref, g_ref, gn_ref, u_ref):
        for h in range(N_HEADS):
            cs = slice(h * hv, (h + 1) * hv)
            u_ref[:, cs] = _glaout_f(of_ref[:, cs], ob_ref[:, cs], g_ref[:, cs], gn_ref[:, cs]).astype(BF16)

    return pl.pallas_call(
        body, name=name, grid=(T // rb,),
        in_specs=[_row(rb, D), _row(rb, D), _row(rb, D, 1), _vec(D)],
        out_specs=_row(rb, D), out_shape=jax.ShapeDtypeStruct((T, D), BF16),
        compiler_params=_cparams(("parallel",)),
    )(o_f, o_b, proj, gain)


def _glaout_bwd(du, o_f, o_b, proj, gain, rb, name):
    T, D = o_f.shape
    hv = D // N_HEADS

    def body(du_ref, of_ref, ob_ref, g_ref, gn_ref, do_ref, dg_ref, dgn_ref, tmp_ref):
        for h in range(N_HEADS):
            cs = slice(h * hv, (h + 1) * hv)
            _, vjp = jax.vjp(_glaout_f, of_ref[:, cs], ob_ref[:, cs], g_ref[:, cs], gn_ref[:, cs])
            d_of, _, dg, dgn = vjp(du_ref[:, cs])
            do_ref[:, cs] = d_of
            dg_ref[:, cs] = dg.astype(BF16)
            tmp_ref[:, cs] = dgn
        _accum(dgn_ref, tmp_ref[...])

    return pl.pallas_call(
        body, name=name, grid=(T // rb,),
        in_specs=[_row(rb, D), _row(rb, D), _row(rb, D), _row(rb, D, 1), _vec(D)],
        out_specs=[_row(rb, D), _row(rb, D), _acc2(D)],
        out_shape=[jax.ShapeDtypeStruct((T, D), F32), jax.ShapeDtypeStruct((T, D), BF16), _acc_shape(D)],
        scratch_shapes=[pltpu.VMEM((1, D), F32)],
        compiler_params=_cparams(("arbitrary",)),
    )(du, o_f, o_b, proj, gain)


def _merge_f(bg1, bg2, yg, yp):
    return _sig(bg1) * yg + _sig(bg2) * yp


def _merge_fwd(proj, y_gla, y_pool, rb, name):
    T, D = y_gla.shape

    def body(b1_ref, b2_ref, yg_ref, yp_ref, m_ref):
        m_ref[...] = _merge_f(b1_ref[...], b2_ref[...], yg_ref[...], yp_ref[...]).astype(BF16)

    return pl.pallas_call(
        body, name=name, grid=(T // rb,),
        in_specs=[_row(rb, D, 2), _row(rb, D, 3), _row(rb, D), _row(rb, D)],
        out_specs=_row(rb, D), out_shape=jax.ShapeDtypeStruct((T, D), BF16),
        compiler_params=_cparams(("parallel",)),
    )(proj, proj, y_gla, y_pool)


def _merge_bwd(dm, proj, y_gla, y_pool, rb, name):
    T, D = y_gla.shape

    def body(dm_ref, b1_ref, b2_ref, yg_ref, yp_ref, d1_ref, d2_ref, dyg_ref, dyp_ref):
        _, vjp = jax.vjp(_merge_f, b1_ref[...], b2_ref[...], yg_ref[...], yp_ref[...])
        d1, d2, dyg, dyp = vjp(dm_ref[...])
        d1_ref[...] = d1.astype(BF16)
        d2_ref[...] = d2.astype(BF16)
        dyg_ref[...] = dyg.astype(BF16)
        dyp_ref[...] = dyp.astype(BF16)

    return pl.pallas_call(
        body, name=name, grid=(T // rb,),
        in_specs=[_row(rb, D), _row(rb, D, 2), _row(rb, D, 3), _row(rb, D), _row(rb, D)],
        out_specs=[_row(rb, D)] * 4, out_shape=[jax.ShapeDtypeStruct((T, D), BF16)] * 4,
        compiler_params=_cparams(("parallel",)),
    )(dm, proj, proj, y_gla, y_pool)


def _swiglu_f(gate, up):
    return _silu(gate) * up


def _pool_consts(ctx_len, seq):
    rows = seq // GRID_W
    reps = POOL_TB // GRID_W
    mw, bc, cw, ch, cc = [], [], [], [], []
    for w in POOL_WINDOWS:
        lo, hi = w // 2, w - w // 2 - 1

        def band(n):
            i = np.arange(n)[:, None]
            j = np.arange(n)[None, :]
            return ((j - i >= -lo) & (j - i <= hi)).astype(np.float32)

        def count(n):
            i = np.arange(n)
            return (np.minimum(i + hi + 1, n) - np.maximum(i - lo, 0)).astype(np.float32)

        mw.append(np.kron(np.eye(reps, dtype=np.float32), band(GRID_W)))
        bc.append(band(ctx_len))
        cw.append(np.tile(count(GRID_W), reps)[:, None])
        ch.append(np.repeat(count(rows), GRID_W)[:, None])
        cc.append(count(ctx_len)[:, None])
    mw, bc = np.stack(mw), np.stack(bc)
    return dict(
        mw=jnp.asarray(mw, BF16), mwt=jnp.asarray(mw.transpose(0, 2, 1), BF16),
        bc=jnp.asarray(bc, BF16), bct=jnp.asarray(bc.transpose(0, 2, 1), BF16),
        cw=jnp.asarray(np.stack(cw)), ch=jnp.asarray(np.stack(ch)), cc=jnp.asarray(np.stack(cc)))


def _gspec(*shape):
    nd = len(shape)
    return pl.BlockSpec((None,) + tuple(shape), lambda g: (g,) + (0,) * nd)


def _pool_fwd(proj, pc, wg, scale, ctx_len, D, name):
    T = proj.shape[0]
    seq = T - ctx_len
    dp = D // 2
    pg = dp // len(POOL_WINDOWS)
    nblk = seq // POOL_TB
    padt = POOL_PAD_ROWS * GRID_W
    p_col0 = 5 * D // pg

    def body(p_ref, mw_ref, bc_ref, cw_ref, ch_ref, cc_ref, wg_ref, sc_ref, pd_ref, y0_ref, r_ref, pad_ref):
        g = pl.program_id(0)

        def tail(rows, mean, x):
            pdb = (mean - x).astype(BF16)
            y0 = _dot(pdb, wg_ref[...])
            pd_ref[rows, :] = pdb
            y0_ref[rows, :] = y0
            r_ref[rows, :] = (y0 * sc_ref[...]).astype(BF16)

        xc = p_ref[0:ctx_len, :]
        tail(slice(0, ctx_len), _dot2(bc_ref[...], xc) / cc_ref[...], xc)

        pad_ref[0:padt, :] = jnp.zeros((padt, pg), F32)
        pad_ref[padt + seq:, :] = jnp.zeros((padt, pg), F32)

        def wpass(b, carry):
            rows = pl.ds(pl.multiple_of(ctx_len + b * POOL_TB, CHUNK), POOL_TB)
            dst = pl.ds(pl.multiple_of(padt + b * POOL_TB, CHUNK), POOL_TB)
            pad_ref[dst, :] = _dot2(mw_ref[...], p_ref[rows, :]) / cw_ref[...]
            return carry

        lax.fori_loop(0, nblk, wpass, 0)

        for gi, w in enumerate(POOL_WINDOWS):
            lo, hi = w // 2, w - w // 2 - 1

            @pl.when(g == gi)
            def _():
                def hpass(b, carry):
                    acc = jnp.zeros((POOL_TB, pg), F32)
                    for d in range(-lo, hi + 1):
                        src = pl.ds(pl.multiple_of(padt + b * POOL_TB + d * GRID_W, CHUNK), POOL_TB)
                        acc = acc + pad_ref[src, :]
                    mean = acc / ch_ref[pl.ds(pl.multiple_of(b * POOL_TB, CHUNK), POOL_TB), :]
                    rows = pl.ds(pl.multiple_of(ctx_len + b * POOL_TB, CHUNK), POOL_TB)
                    tail(rows, mean, p_ref[rows, :])
                    return carry

                lax.fori_loop(0, nblk, hpass, 0)

    col = lambda g: (0, g)
    return pl.pallas_call(
        body, name=name, grid=(len(POOL_WINDOWS),),
        in_specs=[pl.BlockSpec((T, pg), lambda g: (0, p_col0 + g)),
                  _gspec(POOL_TB, POOL_TB), _gspec(ctx_len, ctx_len), _gspec(POOL_TB, 1), _gspec(seq, 1),
                  _gspec(ctx_len, 1), _gspec(pg, pg), pl.BlockSpec((1, pg), col)],
        out_specs=[pl.BlockSpec((T, pg), col)] * 3,
        out_shape=[jax.ShapeDtypeStruct((T, dp), BF16), jax.ShapeDtypeStruct((T, dp), F32),
                   jax.ShapeDtypeStruct((T, dp), BF16)],
        scratch_shapes=[pltpu.VMEM((seq + 2 * padt, pg), F32)],
        compiler_params=_cparams(("arbitrary",)),
    )(proj, pc["mw"], pc["bc"], pc["cw"], pc["ch"], pc["cc"], wg, scale)


def _pool_bwd(dr, y0, pd, pc, wg, scale, ctx_len, D, name):
    T = dr.shape[0]
    seq = T - ctx_len
    dp = D // 2
    ng = len(POOL_WINDOWS)
    pg = dp // ng
    nblk = seq // POOL_TB
    padt = POOL_PAD_ROWS * GRID_W

    def body(dr_ref, y0_ref, pd_ref, mwt_ref, bct_ref, cw_ref, ch_ref, cc_ref, wg_ref, sc_ref,
             dp_ref, dsc_ref, gwg_ref, pad_ref, dpd_ref):
        g = pl.program_id(0)
        dsc_ref[...] = jnp.zeros_like(dsc_ref)
        gwg_ref[...] = jnp.zeros_like(gwg_ref)

        def head(rows):
            drv = dr_ref[rows, :]
            dsc_ref[...] += jnp.sum(drv * y0_ref[rows, :], axis=0, keepdims=True)
            dy0 = (drv * sc_ref[...]).astype(BF16)
            gwg_ref[...] += _dot_tn(pd_ref[rows, :], dy0)
            return _dot_nt(dy0, wg_ref[...])

        crow = slice(0, ctx_len)
        dpd_c = head(crow)
        dp_ref[crow, :] = (_dot2(bct_ref[...], dpd_c / cc_ref[...]) - dpd_c).astype(BF16)

        pad_ref[0:padt, :] = jnp.zeros((padt, pg), F32)
        pad_ref[padt + seq:, :] = jnp.zeros((padt, pg), F32)

        def first(b, carry):
            rows = pl.ds(pl.multiple_of(ctx_len + b * POOL_TB, CHUNK), POOL_TB)
            lrows = pl.ds(pl.multiple_of(b * POOL_TB, CHUNK), POOL_TB)
            dst = pl.ds(pl.multiple_of(padt + b * POOL_TB, CHUNK), POOL_TB)
            dpd = head(rows)
            dpd_ref[lrows, :] = dpd
            pad_ref[dst, :] = dpd / ch_ref[lrows, :]
            return carry

        lax.fori_loop(0, nblk, first, 0)

        for gi, w in enumerate(POOL_WINDOWS):
            lo, hi = w // 2, w - w // 2 - 1

            @pl.when(g == gi)
            def _():
                def second(b, carry):
                    acc = jnp.zeros((POOL_TB, pg), F32)
                    for d in range(-hi, lo + 1):
                        src = pl.ds(pl.multiple_of(padt + b * POOL_TB + d * GRID_W, CHUNK), POOL_TB)
                        acc = acc + pad_ref[src, :]
                    rows = pl.ds(pl.multiple_of(ctx_len + b * POOL_TB, CHUNK), POOL_TB)
                    lrows = pl.ds(pl.multiple_of(b * POOL_TB, CHUNK), POOL_TB)
                    dx = _dot2(mwt_ref[...], acc / cw_ref[...]) - dpd_ref[lrows, :]
                    dp_ref[rows, :] = dx.astype(BF16)
                    return carry

                lax.fori_loop(0, nblk, second, 0)

    col = lambda g: (0, g)
    return pl.pallas_call(
        body, name=name, grid=(ng,),
        in_specs=[pl.BlockSpec((T, pg), col), pl.BlockSpec((T, pg), col), pl.BlockSpec((T, pg), col),
                  _gspec(POOL_TB, POOL_TB), _gspec(ctx_len, ctx_len), _gspec(POOL_TB, 1), _gspec(seq, 1),
                  _gspec(ctx_len, 1), _gspec(pg, pg), pl.BlockSpec((1, pg), col)],
        out_specs=[pl.BlockSpec((T, pg), col), pl.BlockSpec((1, pg), col), _gspec(pg, pg)],
        out_shape=[jax.ShapeDtypeStruct((T, dp), BF16), jax.ShapeDtypeStruct((1, dp), F32),
                   jax.ShapeDtypeStruct((ng, pg, pg), F32)],
        scratch_shapes=[pltpu.VMEM((seq + 2 * padt, pg), F32), pltpu.VMEM((seq, pg), F32)],
        compiler_params=_cparams(("arbitrary",)),
    )(dr, y0, pd, pc["mwt"], pc["bct"], pc["cw"], pc["ch"], pc["cc"], wg, scale)


def _loss_head(x2, target, rb, name):
    T, D = x2.shape

    def body(y_ref, t_ref, dy_ref, l_ref):
        i = pl.program_id(0)

        @pl.when(i == 0)
        def _():
            dy_ref[...] = jnp.zeros_like(dy_ref)
            l_ref[...] = jnp.zeros_like(l_ref)

        @pl.when(i > 0)
        def _():
            e = y_ref[...] - t_ref[...]
            dy_ref[...] = e * (1.0 / D)
            l_ref[...] += 0.5 * jnp.sum(jnp.mean(e * e, axis=-1, keepdims=True), axis=0, keepdims=True)

    return pl.pallas_call(
        body, name=name, grid=(T // rb,),
        in_specs=[_row(rb, D), pl.BlockSpec((rb, D), lambda i: (jnp.maximum(i - 1, 0), 0))],
        out_specs=[_row(rb, D), pl.BlockSpec((8, 128), lambda i: (0, 0))],
        out_shape=[jax.ShapeDtypeStruct((T, D), F32), jax.ShapeDtypeStruct((8, 128), F32)],
        compiler_params=_cparams(("arbitrary",)),
    )(x2, target)


def _sum_lead(x, name):
    S, R, C = x.shape

    def body(x_ref, o_ref):
        acc = x_ref[0]
        for s in range(1, S):
            acc = acc + x_ref[s]
        o_ref[...] = acc

    return pl.pallas_call(
        body, name=name, out_shape=jax.ShapeDtypeStruct((R, C), F32),
        compiler_params=_cparams(),
    )(x)


def _silu_rows(cond, name):
    def body(c_ref, o_ref):
        o_ref[...] = _silu(c_ref[...]).astype(BF16)

    return pl.pallas_call(body, name=name, out_shape=jax.ShapeDtypeStruct(cond.shape, BF16),
                          compiler_params=_cparams())(cond)


def _silu_grad(cond, ds, name):
    def body(c_ref, d_ref, o_ref):
        _, vjp = jax.vjp(_silu, c_ref[...])
        o_ref[...] = vjp(d_ref[...])[0]

    return pl.pallas_call(body, name=name, out_shape=jax.ShapeDtypeStruct(cond.shape, F32),
                          compiler_params=_cparams())(cond, ds)


def _adamw_math(g, w_ref, m_ref, v_ref, go_ref, d_ref, mo_ref, vo_ref):
    c1 = 1.0 / (1.0 - ADAM_B1 ** ADAM_STEP)
    c2 = 1.0 / (1.0 - ADAM_B2 ** ADAM_STEP)
    mn = ADAM_B1 * m_ref[...] + (1.0 - ADAM_B1) * g
    vn = ADAM_B2 * v_ref[...] + (1.0 - ADAM_B2) * (g * g)
    go_ref[...] = g
    mo_ref[...] = mn
    vo_ref[...] = vn
    d_ref[...] = -ADAM_LR * ((mn * c1) / (jnp.sqrt(vn * c2) + ADAM_EPS) + ADAM_WD * w_ref[...])


def _adamw(gs, w, m, v, name):
    S, R, C = gs.shape
    cpad = -(-C // 128) * 128
    rt = _pick(R, max(16, (1 << 20) // (4 * cpad)), 16)

    def body(g_ref, w_ref, m_ref, v_ref, go_ref, d_ref, mo_ref, vo_ref):
        g = g_ref[0].astype(F32)
        for s in range(1, S):
            g = g + g_ref[s].astype(F32)
        _adamw_math(g, w_ref, m_ref, v_ref, go_ref, d_ref, mo_ref, vo_ref)

    blk = pl.BlockSpec((rt, C), lambda i: (i, 0))
    return pl.pallas_call(
        body, name=name, grid=(R // rt,),
        in_specs=[pl.BlockSpec((S, rt, C), lambda i: (0, i, 0)), blk, blk, blk],
        out_specs=[blk] * 4, out_shape=[jax.ShapeDtypeStruct((R, C), F32)] * 4,
        compiler_params=_cparams(("parallel",)),
    )(gs, w, m, v)


def _adamw_layer(gs, w, m, v, l, prev, name):
    S, R, C = gs.shape
    L = w.shape[0]
    cpad = -(-C // 128) * 128
    rt = _pick(R, max(16, (1 << 20) // (4 * cpad)), 16)

    def body(*refs):
        g_ref, w_ref, m_ref, v_ref = refs[:4]
        go_ref, d_ref, mo_ref, vo_ref = refs[-4:]
        g = g_ref[0].astype(F32)
        for s in range(1, S):
            g = g + g_ref[s].astype(F32)
        _adamw_math(g, w_ref, m_ref, v_ref, go_ref, d_ref, mo_ref, vo_ref)

    blk = pl.BlockSpec((None, rt, C), lambda i: (l, i, 0))
    in_specs = [pl.BlockSpec((S, rt, C), lambda i: (0, i, 0)), blk, blk, blk]
    args = [gs, w, m, v]
    aliases = {}
    if prev is not None:
        in_specs += [pl.BlockSpec(memory_space=pl.ANY)] * 4
        args += list(prev)
        aliases = {4 + q: q for q in range(4)}
    return pl.pallas_call(
        body, name=name, grid=(R // rt,), in_specs=in_specs,
        out_specs=[blk] * 4, out_shape=[jax.ShapeDtypeStruct((L, R, C), F32)] * 4,
        input_output_aliases=aliases,
        compiler_params=_cparams(("parallel",)),
    )(*args)


def _adamw_nd(gs, w, m, v, name):
    shp = w.shape
    if len(shp) == 1:
        r, c = 1, shp[0]
    else:
        r, c = int(np.prod(shp[:-1])), shp[-1]
    outs = _adamw(gs.reshape(gs.shape[0], r, c), w.reshape(r, c), m.reshape(r, c), v.reshape(r, c), name)
    return [o.reshape(shp) for o in outs]


def kernel(x, c, ctx, c_ctx, w_ada, b_ada, w_in, w_decay_up, b_decay_up, gla_norm_gain, w_pool_group, pool_scale, w_gla_out, w_pool_out, w_out, ln_mix_gain, ln_mix_bias, w_ffn_in, w_ffn_out, ln_ffn_gain, ln_ffn_bias, loss_target, m_c_ctx, m_w_ada, m_b_ada, m_w_in, m_w_decay_up, m_b_decay_up, m_gla_norm_gain, m_w_pool_group, m_pool_scale, m_w_gla_out, m_w_pool_out, m_w_out, m_ln_mix_gain, m_ln_mix_bias, m_w_ffn_in, m_w_ffn_out, m_ln_ffn_gain, m_ln_ffn_bias, v_c_ctx, v_w_ada, v_b_ada, v_w_in, v_w_decay_up, v_b_decay_up, v_gla_norm_gain, v_w_pool_group, v_pool_scale, v_w_gla_out, v_w_pool_out, v_w_out, v_ln_mix_gain, v_ln_mix_bias, v_w_ffn_in, v_w_ffn_out, v_ln_ffn_gain, v_ln_ffn_bias):
    L, D = w_ada.shape[0], w_ada.shape[1]
    seq, ctx_len = x.shape[1], ctx.shape[1]
    T = seq + ctx_len
    rb = ctx_len
    DK = D // 2
    DP = D // 2
    ng = len(POOL_WINDOWS)
    pg = DP // ng
    dff = w_ffn_out.shape[1] * N_DEV
    alpha = (2.0 * L) ** 0.25
    assert seq % rb == 0 and rb % CHUNK == 0 and seq % POOL_TB == 0 and ctx_len % 8 == 0
    xi, yi, ci = _my_pos()
    me = 4 * xi + 2 * yi + ci
    pc = _pool_consts(ctx_len, seq)

    shards = dict(w_in=w_in.astype(BF16), go=w_gla_out.astype(BF16), po=w_pool_out.astype(BF16),
                  out=w_out.astype(BF16), fi=w_ffn_in.astype(BF16), fo=w_ffn_out.astype(BF16),
                  pg=w_pool_group.astype(BF16).reshape(L, ng * pg // N_DEV, pg))
    wkeys = ("w_in", "go", "po", "out", "fi", "fo", "pg")
    o_q, o_k, o_v, o_g, o_a = 0, DK, 2 * DK, 2 * DK + D, 2 * DK + 2 * D
    o_p, o_bg = o_a + 2 * GATE_RANK, o_a + 2 * GATE_RANK + DP

    def prepared(gw):
        w_in_f = jnp.swapaxes(gw["w_in"], 0, 1).reshape(D, -1)
        main = jnp.concatenate([w_in_f[:, o_v:o_g], w_in_f[:, o_g:o_a], w_in_f[:, o_bg:],
                                w_in_f[:, o_q:o_k], w_in_f[:, o_k:o_v], w_in_f[:, o_p:o_bg]], axis=1)
        alr_w = jnp.pad(w_in_f[:, o_a:o_p], ((0, 0), (0, ALR_PAD - 2 * GATE_RANK)))
        pgf = jnp.swapaxes(gw["pg"].reshape(N_DEV, ng, pg // N_DEV, pg), 0, 1).reshape(ng, pg, pg)
        return dict(main=main, alr=alr_w, go=gw["go"].reshape(D, D), po=gw["po"], out=gw["out"].reshape(D, D),
                    fi=gw["fi"][None], fo=gw["fo"].reshape(1, dff, D), pg=pgf)

    def gather_next(fn, names, l, nxt):
        if l + 1 >= L:
            return fn(None)
        res, outs = fn(_gather_job([shards[k] for k in names], l + 1))
        nxt.update(zip(names, outs))
        return res

    gathered = dict(zip(wkeys, _run_job(_gather_job([shards[k] for k in wkeys], 0), "ag_layer0")))

    dku = w_decay_up.shape[-1]
    small_in = jnp.concatenate([c.reshape(-1), w_decay_up.reshape(-1), b_decay_up.reshape(-1)])
    (small_all,) = _gather_flat([small_in], "ag_small")
    c_all = small_all[:, :D]
    n_wdu = L * 2 * GATE_RANK * dku
    wdu_all = small_all[:, D:D + n_wdu].reshape(N_DEV, L, 2, GATE_RANK, dku)
    wdu_full = jnp.transpose(wdu_all, (1, 2, 3, 0, 4)).reshape(L, 2, GATE_RANK, DK)
    bdu_all = small_all[:, D + n_wdu:].reshape(N_DEV, L, 2, dku)
    bdu_full = jnp.transpose(bdu_all, (1, 2, 0, 3)).reshape(L, 1, 2 * DK)
    wdu_bd = jnp.zeros((L, ALR_PAD, 2 * DK), F32)
    wdu_bd = wdu_bd.at[:, :GATE_RANK, :DK].set(wdu_full[:, 0])
    wdu_bd = wdu_bd.at[:, GATE_RANK:2 * GATE_RANK, DK:].set(wdu_full[:, 1]).astype(BF16)

    ncond = 16
    cond = jnp.concatenate([c_all, c_ctx.reshape(1, D), jnp.zeros((ncond - N_DEV - 1, D), F32)], axis=0)
    s_cond = _silu_rows(cond, "silu_cond")
    wsh = w_ada.shape[-1]
    b_ada_mine = lax.dynamic_slice_in_dim(b_ada, me * wsh, wsh, axis=1)
    mod_part = jnp.stack([_mm(s_cond, w_ada, "nn", F32, "mod_mm", bias=b_ada_mine[l:l + 1], b_pre=(l,))
                          for l in range(L)])
    (mod_all,) = _all_gather([mod_part], "ag_mod")
    mod_all = jnp.swapaxes(mod_all, 1, 2).reshape(L, ncond, N_MOD * D)
    mod_lat = lax.dynamic_slice_in_dim(mod_all, me, 1, axis=1)
    mods = jnp.concatenate([mod_all[:, N_DEV:N_DEV + 1], mod_lat], axis=1).reshape(L, 2, 1, N_MOD * D)
    SH_M, SC_M, GT_M, SH_F, SC_F, GT_F = range(N_MOD)

    xa = jnp.concatenate([ctx[0], x[0]], axis=0)
    vec = lambda a, l: a[l].reshape(1, -1)
    saved = []
    h = _mod_fwd(xa, mods[0], SC_M, SH_M, rb, "mod_fwd")
    weights = []
    for l in range(L):
        W = prepared(gathered)
        weights.append(W)
        gathered = {}
        proj = gather_next(lambda j: _mm(h, W["main"], "nn", F32, "mm_in", job=j), ["w_in"], l, gathered)
        alr = _mm(h, W["alr"], "nn", F32, "mm_alr")
        la = _decay_fwd(alr, wdu_bd[l], bdu_full[l], rb, "decay_fwd")
        o_f, s_f = gather_next(lambda j: _gla_fwd(proj, la, False, rb, D, "gla_fwd_f", job=j), ["go", "out"], l,
                               gathered)
        o_b, s_b = gather_next(lambda j: _gla_fwd(proj, la, True, rb, D, "gla_fwd_b", job=j), ["po", "pg"], l,
                               gathered)
        u = _glaout_fwd(o_f, o_b, proj, vec(gla_norm_gain, l), rb, "glaout_fwd")
        y_gla = _mm(u, W["go"], "nn", F32, "mm_go")
        pd, y0, r = _pool_fwd(proj, pc, W["pg"], vec(pool_scale, l), ctx_len, D, "pool_fwd")
        y_pool = _mm(r, W["po"], "nn", F32, "mm_po", b_shard=True)
        m_ = _merge_fwd(proj, y_gla, y_pool, rb, "merge_fwd")
        mix = _mm(m_, W["out"], "nn", F32, "mm_out")
        x1, h2 = _unit_fwd(alpha, xa, mix, mods[l], GT_M, vec(ln_mix_gain, l), vec(ln_mix_bias, l),
                           (mods[l], SC_F, SH_F), rb, "unit_mix_fwd")
        ff, s_ = gather_next(lambda j: _ffn_in_fwd(h2, W["fi"], 0, "mm_fi_swiglu", job=j), ["fi"], l, gathered)
        ffn = gather_next(lambda j: _mm(s_, W["fo"], "nn", F32, "mm_fo", b_pre=(0,), job=j), ["fo"], l, gathered)
        nxt = (mods[l + 1], SC_M, SH_M) if l + 1 < L else None
        x2, h_next = _unit_fwd(alpha, x1, ffn, mods[l], GT_F, vec(ln_ffn_gain, l), vec(ln_ffn_bias, l),
                               nxt, rb, "unit_ffn_fwd")
        saved.append(dict(xa=xa, h=h, proj=proj, alr=alr, la=la, o_f=o_f, o_b=o_b, s_f=s_f, s_b=s_b, u=u,
                          y_gla=y_gla, pd=pd, y0=y0, r=r, y_pool=y_pool, m=m_, mix=mix, x1=x1, h2=h2, ff=ff,
                          s=s_, ffn=ffn))
        xa, h = x2, h_next

    dxo, loss_part = _loss_head(xa, loss_target[0], rb, "loss_head")
    loss = lax.psum(loss_part[0, 0], ("x", "y", "c"))

    big_params = [("w_in", w_in, m_w_in, v_w_in), ("w_gla_out", w_gla_out, m_w_gla_out, v_w_gla_out),
                  ("w_pool_out", w_pool_out, m_w_pool_out, v_w_pool_out), ("w_out", w_out, m_w_out, v_w_out),
                  ("w_ffn_in", w_ffn_in, m_w_ffn_in, v_w_ffn_in), ("w_ffn_out", w_ffn_out, m_w_ffn_out, v_w_ffn_out),
                  ("w_pool_group", w_pool_group, m_w_pool_group, v_w_pool_group)]
    big_out = {nm: None for nm, _, _, _ in big_params}
    g_small = {k: [None] * L for k in ("gla_gain", "pool_scale", "mix_g", "mix_b", "ffn_g", "ffn_b", "wdu", "bdu")}
    dmods = [None] * L
    dh = None
    sum2 = lambda a: a[0] + a[1]
    rows8 = lambda g: g.reshape(N_DEV, g.shape[0] // N_DEV, g.shape[1])

    def apply_adamw(parts, layer):
        for (nm, w, m, v), gs in zip(big_params, parts):
            R, C = gs.shape[1], gs.shape[2]
            big_out[nm] = _adamw_layer(gs, w.reshape(L, R, C), m.reshape(L, R, C), v.reshape(L, R, C), layer,
                                       big_out[nm], "adamw_" + nm)

    LATE = (0, 1, 2, 3, 6)
    late_chunks = None
    arrived = {}

    def behind(fn, job, positions):
        if job is None:
            return fn(None)
        res, outs = fn(job)
        if positions is None:
            return res, outs
        arrived.update(zip(positions, outs))
        return res

    for l in range(L - 1, -1, -1):
        sv = saved[l]
        W = weights[l]
        nxt = (mods[l + 1], SC_M, SH_M) if l + 1 < L else None
        unit = lambda j: _unit_bwd(alpha, dxo, dh, sv["x1"], sv["ffn"], mods[l], GT_F, vec(ln_ffn_gain, l),
                                   vec(ln_ffn_bias, l), nxt, rb, "unit_ffn_bwd", job=j)
        late_pairs = None
        if late_chunks is None:
            res = unit(None)
        else:
            res, sib = behind(unit, _sibling_job(late_chunks), None)
            late_pairs = _pair_adds(late_chunks, sib, "_late")
        dx1, dffn, d_gtf, d_gf, d_bf, d_scm_n, d_shm_n = res
        if nxt is not None:
            dmods[l + 1]["sc_m"], dmods[l + 1]["sh_m"] = d_scm_n, d_shm_n
        dmods[l] = dict(gt_f=d_gtf)
        g_small["ffn_g"][l], g_small["ffn_b"][l] = sum2(d_gf), sum2(d_bf)
        dff_ = behind(lambda j: _ffn_out_dx(dffn, W["fo"], 0, sv["ff"], "mm_fo_dx_swiglu", job=j),
                      _chip_job(late_pairs[1:]) if late_pairs else None, LATE[1:])
        c_fo = rows8(_mm(sv["s"], dffn, "tn", BF16, "mm_fo_dw"))
        dh2 = behind(lambda j: _mm(dff_, W["fi"], "nt", F32, "mm_fi_dx", b_pre=(0,), b_shard=True, a_half=True,
                                   job=j), _chip_job(late_pairs[:1]) if late_pairs else None, LATE[:1])
        c_fi = _mm(sv["h2"], dff_, "tn", BF16, "mm_fi_dw", b_half=True, out_shard=True)
        if late_pairs:
            apply_adamw([arrived[i] for i in range(len(big_params))], l + 1)
            arrived = {}
        ffn_chunks = [c_fi, c_fo]
        res, sib = behind(lambda j: _unit_bwd(
            alpha, dx1, dh2, sv["xa"], sv["mix"], mods[l], GT_M, vec(ln_mix_gain, l), vec(ln_mix_bias, l),
            (mods[l], SC_F, SH_F), rb, "unit_mix_bwd", job=j), _sibling_job(ffn_chunks), None)
        dxa, dmix, d_gtm, d_gm, d_bm, d_scf, d_shf = res
        ffn_pairs = _pair_adds(ffn_chunks, sib, "_ffn")
        dmods[l].update(gt_m=d_gtm, sc_f=d_scf, sh_f=d_shf)
        g_small["mix_g"][l], g_small["mix_b"][l] = sum2(d_gm), sum2(d_bm)
        dm = _mm(dmix, W["out"], "nt", F32, "mm_out_dx")
        c_out = rows8(_mm(sv["m"], dmix, "tn", BF16, "mm_out_dw"))
        dbg1, dbg2, dyg, dyp = _merge_bwd(dm, sv["proj"], sv["y_gla"], sv["y_pool"], rb, "merge_bwd")
        dr = _mm(dyp, W["po"], "nt", F32, "mm_po_dx", b_shard=True)
        c_po = _mm(sv["r"], dyp, "tn", BF16, "mm_po_dw", out_shard=True)
        dp_, d_ps, g_pgl = _pool_bwd(dr, sv["y0"], sv["pd"], pc, W["pg"], vec(pool_scale, l), ctx_len, D, "pool_bwd")
        g_small["pool_scale"][l] = d_ps
        c_pg = jnp.swapaxes(g_pgl.astype(BF16).reshape(ng, N_DEV, pg // N_DEV, pg), 0, 1).reshape(N_DEV, -1, pg)
        du = _mm(dyg, W["go"], "nt", F32, "mm_go_dx")
        c_go = rows8(_mm(sv["u"], dyg, "tn", BF16, "mm_go_dw"))
        do, dg, d_gg = _glaout_bwd(du, sv["o_f"], sv["o_b"], sv["proj"], vec(gla_norm_gain, l), rb, "glaout_bwd")
        g_small["gla_gain"][l] = sum2(d_gg)
        dq_f, dk_f, dv_f, dla_f = behind(lambda j: _gla_bwd(
            sv["proj"], sv["la"], do, sv["s_f"], False, rb, D, None, "gla_bwd_f", job=j),
            _chip_job(ffn_pairs[:1]), (4,))
        dq, dk, dv, dla_b = behind(lambda j: _gla_bwd(
            sv["proj"], sv["la"], do, sv["s_b"], True, rb, D, (dq_f, dk_f, dv_f), "gla_bwd_b", job=j),
            _chip_job(ffn_pairs[1:]), (5,))
        dalr, g_wdu, g_bdu = _decay_bwd(dla_f, dla_b, sv["alr"], wdu_bd[l], bdu_full[l], rb, "decay_bwd")
        g_small["wdu"][l] = jnp.stack([g_wdu[:GATE_RANK, :DK], g_wdu[GATE_RANK:2 * GATE_RANK, DK:]])
        g_small["bdu"][l] = g_bdu.reshape(2, DK)
        dproj = jnp.concatenate([dv, dg, dbg1, dbg2, dq, dk, dp_], axis=1)
        dh_alr = _mm(dalr, W["alr"], "nt", F32, "mm_alr_dx")
        dh = _mm(dproj, W["main"], "nt", F32, "mm_in_dx", add=dh_alr)
        g_main = _mm(sv["h"], dproj, "tn", BF16, "mm_in_dw")
        g_alr = _mm(sv["h"], dalr, "tn", BF16, "mm_alr_dw")
        g_in = jnp.concatenate(
            [g_main[:, 4 * D:4 * D + DK], g_main[:, 4 * D + DK:5 * D], g_main[:, :D], g_main[:, D:2 * D],
             g_alr[:, :2 * GATE_RANK], g_main[:, 5 * D:], g_main[:, 2 * D:4 * D]], axis=1)
        c_in = jnp.swapaxes(g_in.reshape(D, N_DEV, -1), 0, 1)
        late_chunks = [c_in, c_go, c_po, c_out, c_pg]
        dxo = dxa
    sib = _run_job(_sibling_job(late_chunks), "rs_sibling_last")
    arrived.update(zip(LATE, _run_job(_chip_job(_pair_adds(late_chunks, sib, "_last")), "rs_chips_last")))
    apply_adamw([arrived[i] for i in range(len(big_params))], 0)
    grad_xa, d_scm0, d_shm0 = _mod_bwd(dxo, dh, saved[0]["xa"], mods[0], SC_M, SH_M, rb, "mod_bwd")
    dmods[0]["sc_m"], dmods[0]["sh_m"] = d_scm0, d_shm0
    grad_x = grad_xa[ctx_len:].reshape(1, seq, D)

    order = ("sh_m", "sc_m", "gt_m", "sh_f", "sc_f", "gt_f")
    dmod = jnp.stack([jnp.concatenate([dmods[l][k] for k in order], axis=2) for l in range(L)])
    dmod = dmod.reshape(-1)
    sm = lambda k: jnp.stack([a.reshape(-1) for a in g_small[k]]).reshape(-1)
    small_keys = ("gla_gain", "pool_scale", "mix_g", "mix_b", "ffn_g", "ffn_b", "wdu", "bdu")
    small_part = jnp.concatenate([sm(k) for k in small_keys])
    small_g, dmod_g = _gather_flat([small_part, dmod], "ag_small_grads")
    small_sum = _sum_lead(small_g.reshape(N_DEV, -1, 128), "sum_small").reshape(-1)
    off = 0
    rep = {}
    for k, n in zip(small_keys, (L * D, L * DP, L * D, L * D, L * D, L * D, L * 2 * GATE_RANK * DK, L * 2 * DK)):
        rep[k] = small_sum[off:off + n]
        off += n
    g_wdu_mine = lax.dynamic_slice_in_dim(rep["wdu"].reshape(L, 2, GATE_RANK, DK), me * dku, dku, axis=3)
    g_bdu_mine = lax.dynamic_slice_in_dim(rep["bdu"].reshape(L, 2, DK), me * dku, dku, axis=2)

    dmod_all = dmod_g.reshape(N_DEV, L, 2, N_MOD * D)
    dm_ctx = _sum_lead(dmod_all[:, :, 0].reshape(N_DEV, L, N_MOD * D), "sum_dmod_ctx")
    dm_rows = jnp.concatenate([jnp.swapaxes(dmod_all[:, :, 1], 0, 1), dm_ctx[:, None],
                               jnp.zeros((L, ncond - N_DEV - 1, N_MOD * D), F32)], axis=1)
    g_b_ada = _sum_lead(jnp.swapaxes(dm_rows, 0, 1), "sum_b_ada")
    dm_mine = lax.dynamic_slice_in_dim(dm_rows, me * wsh, wsh, axis=2).astype(BF16)
    g_w_ada = jnp.stack([_mm(s_cond, dm_mine[l], "tn", F32, "ada_dw") for l in range(L)])
    ds_part = _sum_lead(jnp.stack([_mm(dm_mine[l], w_ada, "nt", F32, "ada_dx", b_pre=(l,)) for l in range(L)]),
                        "sum_ds")
    (ds_all,) = _gather_flat([ds_part[N_DEV]], "ag_ds")
    ds_ctx = _sum_lead(ds_all.reshape(N_DEV, 1, D), "sum_ds_ctx")
    g_c_ctx = _silu_grad(c_ctx.reshape(1, D), ds_ctx, "silu_grad").reshape(D)

    one = lambda g: g[None]
    small_table = {
        "c_ctx": (one(g_c_ctx), c_ctx, m_c_ctx, v_c_ctx),
        "w_ada": (one(g_w_ada), w_ada, m_w_ada, v_w_ada),
        "b_ada": (one(g_b_ada), b_ada, m_b_ada, v_b_ada),
        "w_decay_up": (one(g_wdu_mine), w_decay_up, m_w_decay_up, v_w_decay_up),
        "b_decay_up": (one(g_bdu_mine), b_decay_up, m_b_decay_up, v_b_decay_up),
        "gla_norm_gain": (one(rep["gla_gain"].reshape(L, D)), gla_norm_gain, m_gla_norm_gain, v_gla_norm_gain),
        "pool_scale": (one(rep["pool_scale"].reshape(L, DP)), pool_scale, m_pool_scale, v_pool_scale),
        "ln_mix_gain": (one(rep["mix_g"].reshape(L, D)), ln_mix_gain, m_ln_mix_gain, v_ln_mix_gain),
        "ln_mix_bias": (one(rep["mix_b"].reshape(L, D)), ln_mix_bias, m_ln_mix_bias, v_ln_mix_bias),
        "ln_ffn_gain": (one(rep["ffn_g"].reshape(L, D)), ln_ffn_gain, m_ln_ffn_gain, v_ln_ffn_gain),
        "ln_ffn_bias": (one(rep["ffn_b"].reshape(L, D)), ln_ffn_bias, m_ln_ffn_bias, v_ln_ffn_bias),
    }
    big_shapes = {nm: w.shape for nm, w, _, _ in big_params}
    names = ("c_ctx", "w_ada", "b_ada", "w_in", "w_decay_up", "b_decay_up", "gla_norm_gain", "w_pool_group",
             "pool_scale", "w_gla_out", "w_pool_out", "w_out", "ln_mix_gain", "ln_mix_bias", "w_ffn_in", "w_ffn_out",
             "ln_ffn_gain", "ln_ffn_bias")
    grads, deltas, new_m, new_v = [], [], [], []
    for nm in names:
        if nm in small_table:
            res = _adamw_nd(*small_table[nm], "adamw_" + nm)
        else:
            res = [o.reshape(big_shapes[nm]) for o in big_out[nm]]
        for lst, o in zip((grads, deltas, new_m, new_v), res):
            lst.append(o)
    return (loss, grad_x, *grads, *deltas, *new_m, *new_v)
```

```python
import functools
import math

import numpy as np
import jax
import jax.numpy as jnp
from jax import lax
from jax.experimental import pallas as pl
from jax.experimental.pallas import tpu as pltpu

F32 = jnp.float32
BF16 = jnp.bfloat16

N_DEV = 8
N_HEADS = 4
GATE_RANK = 16
GATE_NORM = 16.0
CHUNK = 64
GRID_W = 64
POOL_WINDOWS = (2, 4, 8, 16)
N_MOD = 6
LN_EPS = 1e-5
RMS_EPS = 1e-6
ALR_PAD = 128
POOL_TB = 256
POOL_PAD_ROWS = 8
ADAM_LR = 0.001
ADAM_B1 = 0.9
ADAM_B2 = 0.999
ADAM_EPS = 1e-08
ADAM_WD = 0.01
ADAM_STEP = 10
VMEM_LIMIT = 56 * 1024 * 1024
MESH = pl.DeviceIdType.MESH


def _cparams(sem=None):
    return pltpu.CompilerParams(dimension_semantics=sem, vmem_limit_bytes=VMEM_LIMIT)


def _pick(dim, cap, mult):
    best = None
    for d in range(mult, min(dim, cap) + 1, mult):
        if dim % d == 0:
            best = d
    return best if best is not None else dim


def _sig(x):
    return 1.0 / (1.0 + jnp.exp(-x))


def _silu(x):
    return x * _sig(x)


def _dot(a, b):
    return lax.dot_general(a, b, (((1,), (0,)), ((), ())), preferred_element_type=F32)


def _dot_nt(a, b):
    return lax.dot_general(a, b, (((1,), (1,)), ((), ())), preferred_element_type=F32)


def _dot_tn(a, b):
    return lax.dot_general(a, b, (((0,), (0,)), ((), ())), preferred_element_type=F32)


def _split2(x):
    hi = x.astype(BF16)
    lo = (x - hi.astype(F32)).astype(BF16)
    return hi, lo


def _dot2(m_b, x):
    hi, lo = _split2(x)
    return _dot(m_b, hi) + _dot(m_b, lo)


def _dot3(m_b, x):
    h1 = x.astype(BF16)
    r1 = x - h1.astype(F32)
    h2 = r1.astype(BF16)
    h3 = (r1 - h2.astype(F32)).astype(BF16)
    return _dot(m_b, h1) + _dot(m_b, h2) + _dot(m_b, h3)


def _my_pos():
    return lax.axis_index("x"), lax.axis_index("y"), lax.axis_index("c")


def _all_gather(arrs, name):
    n = len(arrs)
    srcs = [a.reshape((a.shape[0], 1) + a.shape[1:]) for a in arrs]
    outs = [jax.ShapeDtypeStruct((a.shape[0], N_DEV) + a.shape[1:], a.dtype) for a in arrs]

    def body(*refs):
        in_refs, out_refs = refs[:n], refs[n:2 * n]
        send_sems, recv_sems, local_sems = refs[2 * n:]
        x, y, c = _my_pos()
        me, sibling = (x, y, c), (x, y, 1 - c)
        chips = [(1 - x, y), (x, 1 - y), (1 - x, 1 - y)]

        def slot(t, pos):
            return out_refs[t].at[:, pl.ds(4 * pos[0] + 2 * pos[1] + pos[2], 1)]

        def copy(t, k, block, to, src=None):
            return pltpu.make_async_remote_copy(
                src_ref=slot(t, block) if src is None else src, dst_ref=slot(t, block),
                send_sem=send_sems.at[t * 7 + k], recv_sem=recv_sems.at[t * 7 + k],
                device_id=to, device_id_type=MESH)

        mine = [pltpu.make_async_copy(in_refs[t], slot(t, me), local_sems.at[t]) for t in range(n)]
        for cp in mine:
            cp.start()
        first = []
        for t in range(n):
            first.append(copy(t, 0, me, sibling, src=in_refs[t]))
            first += [copy(t, 1 + j, me, (*chip, c), src=in_refs[t]) for j, chip in enumerate(chips)]
        for cp in first:
            cp.start()
        passed = []
        for j, chip in enumerate(chips):
            for t in range(n):
                copy(t, 1 + j, (*chip, c), me).wait_recv()
                fwd = copy(t, 4 + j, (*chip, c), sibling)
                fwd.start()
                passed.append(fwd)
        for t in range(n):
            copy(t, 0, sibling, me).wait_recv()
            for j, chip in enumerate(chips):
                copy(t, 4 + j, (*chip, 1 - c), me).wait_recv()
        for cp in first + passed:
            cp.wait_send()
        for cp in mine:
            cp.wait()

    any_spec = pl.BlockSpec(memory_space=pl.ANY)
    res = pl.pallas_call(
        body, name=name, out_shape=outs,
        in_specs=[any_spec] * n, out_specs=[any_spec] * n,
        scratch_shapes=[pltpu.SemaphoreType.DMA((7 * n,)), pltpu.SemaphoreType.DMA((7 * n,)),
                        pltpu.SemaphoreType.DMA((n,))],
        compiler_params=pltpu.CompilerParams(has_side_effects=True),
    )(*srcs)
    return list(res)


def _gather_flat(vecs, name):
    padded = []
    for v in vecs:
        n = v.shape[0]
        padded.append(jnp.pad(v, (0, -n % 128)).reshape(1, -1, 128))
    res = _all_gather(padded, name)
    return [r.reshape(N_DEV, -1)[:, :v.shape[0]] for r, v in zip(res, vecs)]


N_CHIP = 4


def _comm_call(body, name, arrs, outs, n_sems):
    any_spec = pl.BlockSpec(memory_space=pl.ANY)
    n = len(arrs)
    res = pl.pallas_call(
        body, name=name, out_shape=outs,
        in_specs=[any_spec] * n, out_specs=[any_spec] * len(outs),
        scratch_shapes=[pltpu.SemaphoreType.DMA((s,)) for s in n_sems],
        compiler_params=pltpu.CompilerParams(has_side_effects=True),
    )(*arrs)
    return list(res)


def _sibling_job(arrs):
    n = len(arrs)
    outs = [jax.ShapeDtypeStruct((N_CHIP,) + a.shape[1:], a.dtype) for a in arrs]

    def copies(in_refs, out_refs, sems):
        send_sems, recv_sems = sems
        x, y, c = _my_pos()
        return [pltpu.make_async_remote_copy(
            src_ref=in_refs[t].at[pl.ds(2 * k + (1 - c), 1)], dst_ref=out_refs[t].at[pl.ds(k, 1)],
            send_sem=send_sems.at[t * N_CHIP + k], recv_sem=recv_sems.at[t * N_CHIP + k],
            device_id=(x, y, 1 - c), device_id_type=MESH) for t in range(n) for k in range(N_CHIP)]

    def start(in_refs, out_refs, sems):
        for cp in copies(in_refs, out_refs, sems):
            cp.start()

    def finish(in_refs, out_refs, sems):
        cps = copies(in_refs, out_refs, sems)
        for cp in cps:
            cp.wait_recv()
        for cp in cps:
            cp.wait_send()

    return _Job(arrs, outs, (N_CHIP * n, N_CHIP * n), start, finish)


class _Job:
    def __init__(self, arrs, outs, n_sems, start, finish):
        self.arrs, self.outs, self.n_sems, self.start, self.finish = arrs, outs, n_sems, start, finish


def _gather_job(stacked, l):
    n = len(stacked)
    outs = [jax.ShapeDtypeStruct((N_DEV,) + a.shape[1:], a.dtype) for a in stacked]

    def parts(in_refs, out_refs, sems):
        send_sems, recv_sems, local_sems = sems
        x, y, c = _my_pos()
        me, sibling = (x, y, c), (x, y, 1 - c)
        chips = [(1 - x, y), (x, 1 - y), (1 - x, 1 - y)]
        src = lambda t: in_refs[t].at[pl.ds(l, 1)]

        def slot(t, pos):
            return out_refs[t].at[pl.ds(4 * pos[0] + 2 * pos[1] + pos[2], 1)]

        def copy(t, k, block, to, from_input=False):
            return pltpu.make_async_remote_copy(
                src_ref=src(t) if from_input else slot(t, block), dst_ref=slot(t, block),
                send_sem=send_sems.at[t * 7 + k], recv_sem=recv_sems.at[t * 7 + k],
                device_id=to, device_id_type=MESH)

        mine = [pltpu.make_async_copy(src(t), slot(t, me), local_sems.at[t]) for t in range(n)]
        first = []
        for t in range(n):
            first.append(copy(t, 0, me, sibling, True))
            first += [copy(t, 1 + j, me, (*chip, c), True) for j, chip in enumerate(chips)]
        return me, sibling, chips, copy, mine, first

    def start(in_refs, out_refs, sems):
        _, _, _, _, mine, first = parts(in_refs, out_refs, sems)
        for cp in mine + first:
            cp.start()

    def finish(in_refs, out_refs, sems):
        me, sibling, chips, copy, mine, first = parts(in_refs, out_refs, sems)
        passed = []
        for j, chip in enumerate(chips):
            for t in range(n):
                copy(t, 1 + j, (*chip, me[2]), me).wait_recv()
                fwd = copy(t, 4 + j, (*chip, me[2]), sibling)
                fwd.start()
                passed.append(fwd)
        for t in range(n):
            copy(t, 0, sibling, me).wait_recv()
            for j, chip in enumerate(chips):
                copy(t, 4 + j, (*chip, 1 - me[2]), me).wait_recv()
        for cp in first + passed:
            cp.wait_send()
        for cp in mine:
            cp.wait()

    return _Job(stacked, outs, (7 * n, 7 * n, n), start, finish)


def _chip_job(arrs):
    n = len(arrs)
    outs = [jax.ShapeDtypeStruct(a.shape, a.dtype) for a in arrs]

    def parts(in_refs, out_refs, sems):
        send_sems, recv_sems, local_sems = sems
        x, y, c = _my_pos()
        chip = 2 * x + y
        mine, sends, recvs = [], [], []
        for t in range(n):
            mine.append(pltpu.make_async_copy(in_refs[t].at[pl.ds(chip, 1)], out_refs[t].at[pl.ds(chip, 1)],
                                              local_sems.at[t]))
            for m in range(1, N_CHIP):
                px, py = x ^ (m >> 1), y ^ (m & 1)
                peer = 2 * px + py
                sends.append(pltpu.make_async_remote_copy(
                    src_ref=in_refs[t].at[pl.ds(peer, 1)], dst_ref=out_refs[t].at[pl.ds(chip, 1)],
                    send_sem=send_sems.at[t * 3 + m - 1], recv_sem=recv_sems.at[t * 3 + m - 1],
                    device_id=(px, py, c), device_id_type=MESH))
                recvs.append(pltpu.make_async_remote_copy(
                    src_ref=in_refs[t].at[pl.ds(peer, 1)], dst_ref=out_refs[t].at[pl.ds(peer, 1)],
                    send_sem=send_sems.at[t * 3 + m - 1], recv_sem=recv_sems.at[t * 3 + m - 1],
                    device_id=(x, y, c), device_id_type=MESH))
        return mine, sends, recvs

    def start(in_refs, out_refs, sems):
        mine, sends, _ = parts(in_refs, out_refs, sems)
        for cp in mine + sends:
            cp.start()

    def finish(in_refs, out_refs, sems):
        mine, sends, recvs = parts(in_refs, out_refs, sems)
        for cp in recvs:
            cp.wait_recv()
        for cp in sends:
            cp.wait_send()
        for cp in mine:
            cp.wait()

    return _Job(arrs, outs, (3 * n, 3 * n, n), start, finish)


def _run_job(job, name):
    n = len(job.arrs)

    def body(*refs):
        ins, outs, sems = refs[:n], refs[n:n + len(job.outs)], refs[n + len(job.outs):]
        job.start(ins, outs, sems)
        job.finish(ins, outs, sems)

    return _comm_call(body, name, job.arrs, job.outs, job.n_sems)


def _carry(job, body, grid, in_specs, out_specs, out_shape, scratch_shapes, args):
    out_specs = list(out_specs) if isinstance(out_specs, (list, tuple)) else [out_specs]
    out_shape = list(out_shape) if isinstance(out_shape, (list, tuple)) else [out_shape]
    n_ci, n_co, n_cs = len(in_specs), len(out_specs), len(scratch_shapes)
    n_ji, n_jo = len(job.arrs), len(job.outs)
    any_spec = pl.BlockSpec(memory_space=pl.ANY)
    total = int(np.prod(grid))

    def wrapped(*refs):
        cin, jin = refs[:n_ci], refs[n_ci:n_ci + n_ji]
        o0 = n_ci + n_ji
        cout, jout = refs[o0:o0 + n_co], refs[o0 + n_co:o0 + n_co + n_jo]
        s0 = o0 + n_co + n_jo
        cscr, jsems = refs[s0:s0 + n_cs], refs[s0 + n_cs:]
        step = pl.program_id(0)
        for d in range(1, len(grid)):
            step = step * grid[d] + pl.program_id(d)

        @pl.when(step == 0)
        def _():
            job.start(jin, jout, jsems)

        body(*cin, *cout, *cscr)

        @pl.when(step == total - 1)
        def _():
            job.finish(jin, jout, jsems)

    return (wrapped, list(in_specs) + [any_spec] * n_ji, out_specs + [any_spec] * n_jo,
            out_shape + list(job.outs),
            list(scratch_shapes) + [pltpu.SemaphoreType.DMA((s,)) for s in job.n_sems],
            list(args) + list(job.arrs), n_co)


def _pair_add(g, r, name):
    _, R, C = g.shape
    cpad = -(-C // 128) * 128
    rt = _pick(R, max(16, (1 << 20) // (2 * cpad)), 16)
    cidx = lax.axis_index("c").astype(jnp.int32).reshape(1)

    def body(c_ref, g_ref, r_ref, o_ref):
        o_ref[...] = (g_ref[...].astype(F32) + r_ref[...].astype(F32)).astype(o_ref.dtype)

    return pl.pallas_call(
        body, name=name, out_shape=jax.ShapeDtypeStruct((N_CHIP, R, C), g.dtype),
        grid_spec=pltpu.PrefetchScalarGridSpec(
            num_scalar_prefetch=1, grid=(N_CHIP, R // rt),
            in_specs=[pl.BlockSpec((None, rt, C), lambda k, i, c_ref: (2 * k + c_ref[0], i, 0)),
                      pl.BlockSpec((None, rt, C), lambda k, i, c_ref: (k, i, 0))],
            out_specs=pl.BlockSpec((None, rt, C), lambda k, i, c_ref: (k, i, 0))),
        compiler_params=_cparams(("parallel", "parallel")),
    )(cidx, g, r)


def _pair_adds(chunks, sib, tag):
    return [_pair_add(g, r, "rs_pair_add" + tag) for g, r in zip(chunks, sib)]


def _mm(a, b, mode, out_dtype=F32, name="mm", bias=None, add=None, b_pre=(), b_shard=False,
        a_half=False, b_half=False, out_shard=False, job=None):
    npre = len(b_pre)
    bshape = b.shape[npre:]
    if mode == "nn":
        M, K = a.shape
        if b_shard:
            K2, N = bshape[1], N_DEV * bshape[2]
        else:
            K2, N = bshape
    elif mode == "nt":
        M, K = (a.shape[1], 2 * a.shape[2]) if a_half else a.shape
        if b_shard:
            N, K2 = bshape[1], N_DEV * bshape[2]
        else:
            N, K2 = bshape
    else:
        K, M = a.shape
        K2, N = (b.shape[1], 2 * b.shape[2]) if b_half else bshape
    assert K == K2, (a.shape, b.shape, mode)
    tm = _pick(M, 1100, 16) if mode != "tn" else _pick(M, 1024, 128)
    tn = _pick(N, 1024, 128)
    tk = _pick(K, 2816 if mode == "nt" else 2176, 128)
    if b_shard and mode == "nn":
        tn = bshape[2]
    if b_shard and mode == "nt":
        tk = bshape[2]
    if out_shard:
        tn = N // N_DEV
    nk = K // tk
    none_pre = (None,) * npre
    if mode == "nn":
        a_spec = pl.BlockSpec((tm, tk), lambda i, j, k: (i, k))
        if b_shard:
            b_spec = pl.BlockSpec(none_pre + (None, tk, tn), lambda i, j, k: b_pre + (j, k, 0))
        else:
            b_spec = pl.BlockSpec(none_pre + (tk, tn), lambda i, j, k: b_pre + (k, j))
        dot = _dot
    elif mode == "nt":
        if a_half:
            nkh = a.shape[2] // tk
            a_spec = pl.BlockSpec((None, tm, tk), lambda i, j, k: (k // nkh, i, k % nkh))
        else:
            a_spec = pl.BlockSpec((tm, tk), lambda i, j, k: (i, k))
        if b_shard:
            b_spec = pl.BlockSpec(none_pre + (None, tn, tk), lambda i, j, k: b_pre + (k, j, 0))
        else:
            b_spec = pl.BlockSpec(none_pre + (tn, tk), lambda i, j, k: b_pre + (j, k))
        dot = _dot_nt
    else:
        a_spec = pl.BlockSpec((tk, tm), lambda i, j, k: (k, i))
        if b_half:
            nnh = b.shape[2] // tn
            b_spec = pl.BlockSpec((None, tk, tn), lambda i, j, k: (j // nnh, k, j % nnh))
        else:
            b_spec = pl.BlockSpec(none_pre + (tk, tn), lambda i, j, k: b_pre + (k, j))
        dot = _dot_tn
    in_specs = [a_spec, b_spec]
    args = [a, b]
    if bias is not None:
        in_specs.append(pl.BlockSpec((1, tn), lambda i, j, k: (0, j)))
        args.append(bias)
    if add is not None:
        in_specs.append(pl.BlockSpec((tm, tn), lambda i, j, k: (i, j)))
        args.append(add)
    n_in = len(args)
    if out_shard:
        o_spec = pl.BlockSpec((None, tm, tn), lambda i, j, k: (j, i, 0))
        o_shape = jax.ShapeDtypeStruct((N_DEV, M, tn), out_dtype)
    else:
        o_spec = pl.BlockSpec((tm, tn), lambda i, j, k: (i, j))
        o_shape = jax.ShapeDtypeStruct((M, N), out_dtype)

    def body(*refs):
        a_ref, b_ref = refs[0], refs[1]
        bias_ref = refs[2] if bias is not None else None
        add_ref = refs[n_in - 1] if add is not None else None
        o_ref = refs[n_in]
        p = dot(a_ref[...].astype(BF16), b_ref[...].astype(BF16))

        def finish(acc):
            if bias_ref is not None:
                acc = acc + bias_ref[...]
            if add_ref is not None:
                acc = acc + add_ref[...]
            o_ref[...] = acc.astype(o_ref.dtype)

        if nk == 1:
            finish(p)
        else:
            acc_ref = refs[-1]
            k = pl.program_id(2)

            @pl.when(k == 0)
            def _():
                acc_ref[...] = p

            @pl.when(k > 0)
            def _():
                acc_ref[...] += p

            @pl.when(k == nk - 1)
            def _():
                finish(acc_ref[...])

    grid = (M // tm, N // tn, nk)
    scratch = [pltpu.VMEM((tm, tn), F32)] if nk > 1 else []
    if job is None:
        return pl.pallas_call(
            body, name=name, grid=grid, in_specs=in_specs, out_specs=o_spec, out_shape=o_shape,
            scratch_shapes=scratch, compiler_params=_cparams(("parallel", "parallel", "arbitrary")),
        )(*args)
    return _call_carrying(job, body, name, grid, in_specs, o_spec, o_shape, scratch, args)


def _call_carrying(job, body, name, grid, in_specs, out_specs, out_shape, scratch, args):
    body, in_specs, out_specs, out_shape, scratch, args, n_co = _carry(
        job, body, grid, in_specs, out_specs, out_shape, scratch, args)
    res = pl.pallas_call(
        body, name=name, grid=grid, in_specs=in_specs, out_specs=out_specs, out_shape=out_shape,
        scratch_shapes=scratch, compiler_params=_cparams(("arbitrary",) * len(grid)),
    )(*args)
    own = res[0] if n_co == 1 else list(res[:n_co])
    return own, list(res[n_co:])


def _proj_layout(D):
    DK, DP, R2 = D // 2, D // 2, 2 * GATE_RANK
    return [("q", 0, DK, 4 * D), ("k", DK, DK, 4 * D + DK), ("v", 2 * DK, D, 0), ("g", 2 * DK + D, D, D),
            ("a", 2 * DK + 2 * D, R2, None), ("p", 2 * DK + 2 * D + R2, DP, 5 * D),
            ("bg", 2 * DK + 2 * D + R2 + DP, 2 * D, 2 * D)]


RELAYOUT_ROWS = 64


def _w_in_operands(g, name):
    _, D, n = g.shape
    segs = _proj_layout(D)
    tr = RELAYOUT_ROWS

    def body(g_ref, main_ref, alr_ref):
        shard = [g_ref[j].astype(F32) for j in range(N_DEV)]

        def columns(a, b):
            parts = []
            for j in range(a // n, (b - 1) // n + 1):
                parts.append(shard[j][:, max(a, j * n) - j * n:min(b, (j + 1) * n) - j * n])
            return parts[0] if len(parts) == 1 else jnp.concatenate(parts, axis=1)

        for _, start, width, dst in segs:
            cols = columns(start, start + width)
            if dst is None:
                cols = jnp.concatenate([cols, jnp.zeros((tr, ALR_PAD - width), F32)], axis=1)
                alr_ref[...] = cols.astype(BF16)
            else:
                main_ref[:, dst:dst + width] = cols.astype(BF16)

    return pl.pallas_call(
        body, name=name, grid=(D // tr,),
        in_specs=[pl.BlockSpec((N_DEV, tr, n), lambda i: (0, i, 0))],
        out_specs=[pl.BlockSpec((tr, 11 * D // 2), lambda i: (i, 0)), pl.BlockSpec((tr, ALR_PAD), lambda i: (i, 0))],
        out_shape=[jax.ShapeDtypeStruct((D, 11 * D // 2), BF16), jax.ShapeDtypeStruct((D, ALR_PAD), BF16)],
        compiler_params=_cparams(("parallel",)),
    )(g)


def _w_in_chunks(g_main, g_alr, n, name):
    D = g_main.shape[0]
    segs = _proj_layout(D)
    tr = RELAYOUT_ROWS

    def body(main_ref, alr_ref, o_ref):
        main = main_ref[...].astype(F32)
        alr = alr_ref[...].astype(F32)
        for j in range(N_DEV):
            a, b = j * n, (j + 1) * n
            parts = []
            for _, start, width, dst in segs:
                lo, hi = max(a, start), min(b, start + width)
                if lo >= hi:
                    continue
                src = alr if dst is None else main
                off = 0 if dst is None else dst
                parts.append(src[:, off + lo - start:off + hi - start])
            o_ref[j] = (parts[0] if len(parts) == 1 else jnp.concatenate(parts, axis=1)).astype(BF16)

    return pl.pallas_call(
        body, name=name, grid=(D // tr,),
        in_specs=[pl.BlockSpec((tr, 11 * D // 2), lambda i: (i, 0)), pl.BlockSpec((tr, ALR_PAD), lambda i: (i, 0))],
        out_specs=pl.BlockSpec((N_DEV, tr, n), lambda i: (0, i, 0)),
        out_shape=jax.ShapeDtypeStruct((N_DEV, D, n), BF16),
        compiler_params=_cparams(("parallel",)),
    )(g_main, g_alr)


def _ffn_in_fwd(h2, w_fi, l, name, job=None):
    T, D = h2.shape
    n = w_fi.shape[3]
    nh = N_DEV // 2
    dff = nh * n
    tm = _pick(T, 600, 16)

    def body(a_ref, bg_ref, bu_ref, ff_ref, s_ref):
        a = a_ref[...]
        g = _dot(a, bg_ref[...])
        u = _dot(a, bu_ref[...])
        ff_ref[0] = g
        ff_ref[1] = u
        s_ref[...] = _swiglu_f(g, u).astype(BF16)

    grid = (T // tm, nh)
    in_specs = [pl.BlockSpec((tm, D), lambda i, j: (i, 0)),
                pl.BlockSpec((None, None, D, n), lambda i, j: (l, j, 0, 0)),
                pl.BlockSpec((None, None, D, n), lambda i, j: (l, nh + j, 0, 0))]
    out_specs = [pl.BlockSpec((2, tm, n), lambda i, j: (0, i, j)), pl.BlockSpec((tm, n), lambda i, j: (i, j))]
    out_shape = [jax.ShapeDtypeStruct((2, T, dff), F32), jax.ShapeDtypeStruct((T, dff), BF16)]
    args = (h2, w_fi, w_fi)
    if job is None:
        return pl.pallas_call(
            body, name=name, grid=grid, in_specs=in_specs, out_specs=out_specs, out_shape=out_shape,
            compiler_params=_cparams(("parallel", "parallel")),
        )(*args)
    return _call_carrying(job, body, name, grid, in_specs, out_specs, out_shape, [], args)


def _ffn_out_dx(dffn, w_fo, l, ff, name, job=None):
    T, D = dffn.shape
    dff = ff.shape[2]
    tm = _pick(T, 600, 16)
    tw = _pick(dff, 1408, 128)

    def body(a_ref, b_ref, ff_ref, o_ref):
        ds = _dot_nt(a_ref[...], b_ref[...])
        _, vjp = jax.vjp(_swiglu_f, ff_ref[0], ff_ref[1])
        dg, du = vjp(ds)
        o_ref[0] = dg.astype(BF16)
        o_ref[1] = du.astype(BF16)

    grid = (T // tm, dff // tw)
    in_specs = [pl.BlockSpec((tm, D), lambda i, j: (i, 0)),
                pl.BlockSpec((None, tw, D), lambda i, j: (l, j, 0)),
                pl.BlockSpec((2, tm, tw), lambda i, j: (0, i, j))]
    out_specs = pl.BlockSpec((2, tm, tw), lambda i, j: (0, i, j))
    out_shape = jax.ShapeDtypeStruct((2, T, dff), BF16)
    args = (dffn, w_fo, ff)
    if job is None:
        return pl.pallas_call(
            body, name=name, grid=grid, in_specs=in_specs, out_specs=out_specs, out_shape=out_shape,
            compiler_params=_cparams(("parallel", "parallel")),
        )(*args)
    return _call_carrying(job, body, name, grid, in_specs, out_specs, out_shape, [], args)


def _row(rb, w, col=0):
    return pl.BlockSpec((rb, w), lambda i: (i, col))


def _modspec(d, sec):
    return pl.BlockSpec((None, 1, d), lambda i: (jnp.minimum(i, 1), 0, sec))


def _vec(w):
    return pl.BlockSpec((1, w), lambda i: (0, 0))


def _acc2(w):
    return pl.BlockSpec((None, 1, w), lambda i: (jnp.minimum(i, 1), 0, 0))


def _accum(ref, val):
    i = pl.program_id(0)

    @pl.when(i <= 1)
    def _():
        ref[...] = val

    @pl.when(i > 1)
    def _():
        ref[...] += val


def _acc_shape(w):
    return jax.ShapeDtypeStruct((2, 1, w), F32)


def _mod_f(x, sc, sh):
    return x * (1.0 + sc) + sh


def _mod_fwd(xa, mod, sec_sc, sec_sh, rb, name):
    T, D = xa.shape

    def body(x_ref, sc_ref, sh_ref, h_ref):
        h_ref[...] = _mod_f(x_ref[...], sc_ref[...], sh_ref[...]).astype(BF16)

    return pl.pallas_call(
        body, name=name, grid=(T // rb,),
        in_specs=[_row(rb, D), _modspec(D, sec_sc), _modspec(D, sec_sh)],
        out_specs=_row(rb, D), out_shape=jax.ShapeDtypeStruct((T, D), BF16),
        compiler_params=_cparams(("parallel",)),
    )(xa, mod, mod)


def _mod_bwd(dxa, dh, xa, mod, sec_sc, sec_sh, rb, name):
    T, D = xa.shape

    def body(dxa_ref, dh_ref, x_ref, sc_ref, sh_ref, dx_ref, dsc_ref, dsh_ref):
        _, vjp = jax.vjp(_mod_f, x_ref[...], sc_ref[...], sh_ref[...])
        dx, dsc, dsh = vjp(dh_ref[...])
        dx_ref[...] = dxa_ref[...] + dx
        _accum(dsc_ref, dsc)
        _accum(dsh_ref, dsh)

    return pl.pallas_call(
        body, name=name, grid=(T // rb,),
        in_specs=[_row(rb, D), _row(rb, D), _row(rb, D), _modspec(D, sec_sc), _modspec(D, sec_sh)],
        out_specs=[_row(rb, D), _acc2(D), _acc2(D)],
        out_shape=[jax.ShapeDtypeStruct((T, D), F32), _acc_shape(D), _acc_shape(D)],
        compiler_params=_cparams(("arbitrary",)),
    )(dxa, dh, xa, mod, mod)


def _ln_f(alpha, x, mix, gt, gain, bias):
    z = alpha * x + gt * mix
    mu = jnp.mean(z, axis=-1, keepdims=True)
    zc = z - mu
    var = jnp.mean(zc * zc, axis=-1, keepdims=True)
    return zc * lax.rsqrt(var + LN_EPS) * gain + bias


def _unit_fwd(alpha, x, mix, mod, sec_gt, gain, bias, next_mod, rb, name):
    T, D = x.shape
    has_mod = next_mod is not None

    def body(*refs):
        if has_mod:
            x_ref, mix_ref, gt_ref, g_ref, b_ref, sc_ref, sh_ref, xo_ref, h_ref = refs
        else:
            x_ref, mix_ref, gt_ref, g_ref, b_ref, xo_ref = refs
        xo = _ln_f(alpha, x_ref[...], mix_ref[...], gt_ref[...], g_ref[...], b_ref[...])
        xo_ref[...] = xo
        if has_mod:
            h_ref[...] = _mod_f(xo, sc_ref[...], sh_ref[...]).astype(BF16)

    in_specs = [_row(rb, D), _row(rb, D), _modspec(D, sec_gt), _vec(D), _vec(D)]
    args = [x, mix, mod, gain, bias]
    out_specs = [_row(rb, D)]
    out_shape = [jax.ShapeDtypeStruct((T, D), F32)]
    if has_mod:
        nm, s_sc, s_sh = next_mod
        in_specs += [_modspec(D, s_sc), _modspec(D, s_sh)]
        args += [nm, nm]
        out_specs.append(_row(rb, D))
        out_shape.append(jax.ShapeDtypeStruct((T, D), BF16))
    res = pl.pallas_call(
        body, name=name, grid=(T // rb,), in_specs=in_specs, out_specs=out_specs, out_shape=out_shape,
        compiler_params=_cparams(("parallel",)),
    )(*args)
    return (res[0], res[1]) if has_mod else (res[0], None)


def _unit_bwd(alpha, dxo, dh, x, mix, mod, sec_gt, gain, bias, next_mod, rb, name, job=None):
    T, D = x.shape
    has_mod = next_mod is not None

    def body(*refs):
        if has_mod:
            (dxo_ref, dh_ref, x_ref, mix_ref, gt_ref, g_ref, b_ref, sc_ref, sh_ref,
             dx_ref, dmix_ref, dgt_ref, dg_ref, db_ref, dsc_ref, dsh_ref) = refs
        else:
            (dxo_ref, x_ref, mix_ref, gt_ref, g_ref, b_ref,
             dx_ref, dmix_ref, dgt_ref, dg_ref, db_ref) = refs
        xo, vjp = jax.vjp(functools.partial(_ln_f, alpha), x_ref[...], mix_ref[...], gt_ref[...],
                          g_ref[...], b_ref[...])
        dxo_t = dxo_ref[...]
        if has_mod:
            _, vjp_m = jax.vjp(_mod_f, xo, sc_ref[...], sh_ref[...])
            dxo_m, dsc, dsh = vjp_m(dh_ref[...])
            dxo_t = dxo_t + dxo_m
            _accum(dsc_ref, dsc)
            _accum(dsh_ref, dsh)
        dx, dmix, dgt, dg, db = vjp(dxo_t)
        dx_ref[...] = dx
        dmix_ref[...] = dmix.astype(BF16)
        _accum(dgt_ref, dgt)
        _accum(dg_ref, dg)
        _accum(db_ref, db)

    in_specs = [_row(rb, D)]
    args = [dxo]
    if has_mod:
        in_specs.append(_row(rb, D))
        args.append(dh)
    in_specs += [_row(rb, D), _row(rb, D), _modspec(D, sec_gt), _vec(D), _vec(D)]
    args += [x, mix, mod, gain, bias]
    out_specs = [_row(rb, D), _row(rb, D), _acc2(D), _acc2(D), _acc2(D)]
    out_shape = [jax.ShapeDtypeStruct((T, D), F32), jax.ShapeDtypeStruct((T, D), BF16),
                 _acc_shape(D), _acc_shape(D), _acc_shape(D)]
    if has_mod:
        nm, s_sc, s_sh = next_mod
        in_specs += [_modspec(D, s_sc), _modspec(D, s_sh)]
        args += [nm, nm]
        out_specs += [_acc2(D), _acc2(D)]
        out_shape += [_acc_shape(D), _acc_shape(D)]
    if job is None:
        res = pl.pallas_call(
            body, name=name, grid=(T // rb,), in_specs=in_specs, out_specs=out_specs, out_shape=out_shape,
            compiler_params=_cparams(("arbitrary",)),
        )(*args)
        job_res = None
    else:
        res, job_res = _call_carrying(job, body, name, (T // rb,), in_specs, out_specs, out_shape, [], args)
    res = list(res) if has_mod else list(res) + [None, None]
    return res if job is None else (res, job_res)


def _log_sigmoid(z):
    return jnp.minimum(z, 0.0) - jnp.log(1.0 + jnp.exp(-jnp.abs(z)))


def _decay_fwd(alr, wdu, bdu, rb, name):
    T = alr.shape[0]
    W = wdu.shape[1]

    def body(a_ref, w_ref, b_ref, la_ref):
        z = _dot(a_ref[...].astype(BF16), w_ref[...]) + b_ref[...]
        la_ref[...] = _log_sigmoid(z) * (1.0 / GATE_NORM)

    return pl.pallas_call(
        body, name=name, grid=(T // rb,),
        in_specs=[_row(rb, ALR_PAD), pl.BlockSpec((ALR_PAD, W), lambda i: (0, 0)), _vec(W)],
        out_specs=_row(rb, W), out_shape=jax.ShapeDtypeStruct((T, W), F32),
        compiler_params=_cparams(("parallel",)),
    )(alr, wdu, bdu)


def _decay_bwd(dla_f, dla_b, alr, wdu, bdu, rb, name):
    T = alr.shape[0]
    W = wdu.shape[1]
    DK = W // 2

    def body(df_ref, db_ref, a_ref, w_ref, b_ref, dalr_ref, gw_ref, gb_ref):
        i = pl.program_id(0)
        ab = a_ref[...].astype(BF16)
        z = _dot(ab, w_ref[...]) + b_ref[...]
        dla = jnp.concatenate([df_ref[...], db_ref[...]], axis=1)
        dz = dla * _sig(-z) * (1.0 / GATE_NORM)
        dzb = dz.astype(BF16)
        dalr_ref[...] = _dot_nt(dzb, w_ref[...]).astype(BF16)
        gw = _dot_tn(ab, dzb)
        gb = jnp.sum(dz, axis=0, keepdims=True)

        @pl.when(i == 0)
        def _():
            gw_ref[...] = gw
            gb_ref[...] = gb

        @pl.when(i > 0)
        def _():
            gw_ref[...] += gw
            gb_ref[...] += gb

    return pl.pallas_call(
        body, name=name, grid=(T // rb,),
        in_specs=[_row(rb, DK), _row(rb, DK), _row(rb, ALR_PAD), pl.BlockSpec((ALR_PAD, W), lambda i: (0, 0)), _vec(W)],
        out_specs=[_row(rb, ALR_PAD), pl.BlockSpec((ALR_PAD, W), lambda i: (0, 0)), _vec(W)],
        out_shape=[jax.ShapeDtypeStruct((T, ALR_PAD), BF16), jax.ShapeDtypeStruct((ALR_PAD, W), F32),
                   jax.ShapeDtypeStruct((1, W), F32)],
        compiler_params=_cparams(("arbitrary",)),
    )(dla_f, dla_b, alr, wdu, bdu)


def _tri(rev, ncb):
    m = np.tril(np.ones((CHUNK, CHUNK), np.float32))
    return jnp.asarray(np.kron(np.eye(ncb, dtype=np.float32), m.T if rev else m), BF16)


def _gla_block_common(q_ref, k_ref, v_ref, la_ref, tri_ref, ck, cv, rev, scale_q, ncb):
    mid = CHUNK // 2 if rev else CHUNK // 2 - 1
    last_i = 0 if rev else CHUNK - 1
    rb = ncb * CHUNK
    hk = ck.stop - ck.start
    q = q_ref[:, ck] * scale_q
    k = k_ref[:, ck]
    v = v_ref[:, cv]
    cum = _dot3(tri_ref[...], la_ref[:, ck])
    per_chunk = lambda i: jnp.concatenate(
        [jnp.broadcast_to(cum[c * CHUNK + i:c * CHUNK + i + 1, :], (CHUNK, hk)) for c in range(ncb)], axis=0)
    ref, last = per_chunk(mid), per_chunk(last_i)
    e_q = jnp.exp(cum - ref)
    e_k = jnp.exp(ref - cum)
    e_c = jnp.exp(cum)
    e_s = jnp.exp(last - cum)
    e_l = [jnp.exp(cum[c * CHUNK + last_i:c * CHUNK + last_i + 1, :]) for c in range(ncb)]
    ri = lax.broadcasted_iota(jnp.int32, (rb, rb), 0)
    ci = lax.broadcasted_iota(jnp.int32, (rb, rb), 1)
    mask = (ri // CHUNK == ci // CHUNK) & ((ci >= ri) if rev else (ci <= ri))
    return q, k, v, e_q, e_k, e_c, e_s, e_l, mask, last_i


GLA_HEADS_PER_STEP = 1


def _gla_specs(rb, hk, hv, D, rbmap, rev):
    hp = GLA_HEADS_PER_STEP
    q_col0 = 4 * D // (hp * hk)
    k_col0 = q_col0 + N_HEADS // hp
    la_col0 = N_HEADS // hp if rev else 0
    return [
        pl.BlockSpec((rb, hp * hk), lambda h, i: (rbmap(i), q_col0 + h)),
        pl.BlockSpec((rb, hp * hk), lambda h, i: (rbmap(i), k_col0 + h)),
        pl.BlockSpec((rb, hp * hv), lambda h, i: (rbmap(i), h)),
        pl.BlockSpec((rb, hp * hk), lambda h, i: (rbmap(i), la_col0 + h)),
        pl.BlockSpec((rb, rb), lambda h, i: (0, 0)),
    ]


def _gla_call(job, body, name, grid, in_specs, out_specs, out_shape, scratch, args):
    if job is None:
        return pl.pallas_call(
            body, name=name, grid=grid, in_specs=in_specs, out_specs=out_specs, out_shape=out_shape,
            scratch_shapes=scratch, compiler_params=_cparams(("parallel", "arbitrary")),
        )(*args)
    return _call_carrying(job, body, name, grid, in_specs, out_specs, out_shape, scratch, args)


def _gla_fwd(proj, la, rev, rb, D, name, job=None):
    T = proj.shape[0]
    nb = T // rb
    ncb = rb // CHUNK
    hp = GLA_HEADS_PER_STEP
    hk, hv = D // 2 // N_HEADS, D // N_HEADS
    scale_q = float(hk) ** -0.5
    rbmap = (lambda i: jnp.where(i == 0, 0, nb - i)) if rev else (lambda i: i)

    def body(q_ref, k_ref, v_ref, la_ref, tri_ref, o_ref, s_ref, st_ref):
        @pl.when(pl.program_id(1) == 0)
        def _():
            st_ref[...] = jnp.zeros_like(st_ref)

        order = range(ncb - 1, -1, -1) if rev else range(ncb)
        for hh in range(hp):
            ck, cv = slice(hh * hk, (hh + 1) * hk), slice(hh * hv, (hh + 1) * hv)
            q, k, v, e_q, e_k, e_c, e_s, e_l, mask, _ = _gla_block_common(
                q_ref, k_ref, v_ref, la_ref, tri_ref, ck, cv, rev, scale_q, ncb)
            vb = v.astype(BF16)
            a = jnp.where(mask, _dot_nt((q * e_q).astype(BF16), (k * e_k).astype(BF16)), 0.0)
            o_intra = _dot(a.astype(BF16), vb)
            qc = (q * e_c).astype(BF16)
            ks = (k * e_s).astype(BF16)
            st = st_ref[hh]
            for cc in order:
                rows = slice(cc * CHUNK, (cc + 1) * CHUNK)
                s_ref[hh, cc] = st
                o_ref[rows, cv] = o_intra[rows] + _dot_nt(qc[rows], st.astype(BF16))
                st = st * e_l[cc] + _dot_tn(vb[rows], ks[rows])
            st_ref[hh] = st

    return _gla_call(
        job, body, name, (N_HEADS // hp, nb), _gla_specs(rb, hk, hv, D, rbmap, rev),
        [pl.BlockSpec((rb, hp * hv), lambda h, i: (rbmap(i), h)),
         pl.BlockSpec((hp, ncb, hv, hk), lambda h, i: (h, rbmap(i), 0, 0))],
        [jax.ShapeDtypeStruct((T, D), F32), jax.ShapeDtypeStruct((N_HEADS, T // CHUNK, hv, hk), F32)],
        [pltpu.VMEM((hp, hv, hk), F32)], (proj, proj, proj, la, _tri(rev, ncb)))


def _gla_bwd(proj, la, do, states, rev, rb, D, prev, name, job=None):
    T = proj.shape[0]
    nb = T // rb
    ncb = rb // CHUNK
    hp = GLA_HEADS_PER_STEP
    hk, hv = D // 2 // N_HEADS, D // N_HEADS
    DK = D // 2
    scale_q = float(hk) ** -0.5
    if rev:
        rbmap = lambda i: jnp.where(i == nb - 1, 0, i + 1)
    else:
        rbmap = lambda i: nb - 1 - i
    has_prev = prev is not None
    out_dt = BF16 if has_prev else F32

    def body(*refs):
        q_ref, k_ref, v_ref, la_ref, tri_ref, trit_ref, do_ref, s_ref = refs[:8]
        n_in = 11 if has_prev else 8
        pq_ref, pk_ref, pv_ref = refs[8:11] if has_prev else (None, None, None)
        dq_ref, dk_ref, dv_ref, dla_ref, ds_ref = refs[n_in:]

        @pl.when(pl.program_id(1) == 0)
        def _():
            ds_ref[...] = jnp.zeros_like(ds_ref)

        order = range(ncb) if rev else range(ncb - 1, -1, -1)
        for hh in range(hp):
            ck, cv = slice(hh * hk, (hh + 1) * hk), slice(hh * hv, (hh + 1) * hv)
            q, k, v, e_q, e_k, e_c, e_s, e_l, mask, last_i = _gla_block_common(
                q_ref, k_ref, v_ref, la_ref, tri_ref, ck, cv, rev, scale_q, ncb)
            vb = v.astype(BF16)
            qi = (q * e_q).astype(BF16)
            ki = (k * e_k).astype(BF16)
            qc = (q * e_c).astype(BF16)
            ks = (k * e_s).astype(BF16)
            a = jnp.where(mask, _dot_nt(qi, ki), 0.0).astype(BF16)
            dob = do_ref[:, cv].astype(BF16)
            da = jnp.where(mask, _dot_nt(dob, vb), 0.0).astype(BF16)
            dv_intra = _dot_tn(a, dob)
            dq_intra = _dot(da, ki) * e_q
            dk_intra = _dot_tn(da, qi) * e_k
            rowi = lax.broadcasted_iota(jnp.int32, (CHUNK, hk), 0)
            dst = ds_ref[hh]
            for cc in order:
                rows = slice(cc * CHUNK, (cc + 1) * CHUNK)
                st0 = s_ref[hh, cc]
                dstb = dst.astype(BF16)
                dv = dv_intra[rows] + _dot_nt(ks[rows], dstb)
                dk_inter = _dot(vb[rows], dstb) * e_s[rows]
                dq_s = dq_intra[rows] + _dot(dob[rows], st0.astype(BF16)) * e_c[rows]
                dk = dk_intra[rows] + dk_inter
                extra = (jnp.sum(k[rows] * dk_inter, axis=0, keepdims=True)
                         + e_l[cc] * jnp.sum(dst * st0, axis=0, keepdims=True))
                dla_ref[rows, ck] = q[rows] * dq_s - k[rows] * dk + jnp.where(rowi == last_i, extra, 0.0)
                dq = dq_s * scale_q
                if has_prev:
                    dq = dq + pq_ref[rows, ck]
                    dk = dk + pk_ref[rows, ck]
                    dv = dv + pv_ref[rows, cv]
                dq_ref[rows, ck] = dq.astype(out_dt)
                dk_ref[rows, ck] = dk.astype(out_dt)
                dv_ref[rows, cv] = dv.astype(out_dt)
                dst = dst * e_l[cc] + _dot_tn(dob[rows], qc[rows])
            ds_ref[hh] = dst
            dla_ref[:, ck] = _dot3(trit_ref[...], dla_ref[:, ck])

    in_specs = _gla_specs(rb, hk, hv, D, rbmap, rev)
    in_specs += [pl.BlockSpec((rb, rb), lambda h, i: (0, 0)),
                 pl.BlockSpec((rb, hp * hv), lambda h, i: (rbmap(i), h)),
                 pl.BlockSpec((hp, ncb, hv, hk), lambda h, i: (h, rbmap(i), 0, 0))]
    args = [proj, proj, proj, la, _tri(rev, ncb), _tri(not rev, ncb), do, states]
    hk_spec = pl.BlockSpec((rb, hp * hk), lambda h, i: (rbmap(i), h))
    hv_spec = pl.BlockSpec((rb, hp * hv), lambda h, i: (rbmap(i), h))
    if has_prev:
        in_specs += [hk_spec, hk_spec, hv_spec]
        args += list(prev)
    return _gla_call(
        job, body, name, (N_HEADS // hp, nb), in_specs, [hk_spec, hk_spec, hv_spec, hk_spec],
        [jax.ShapeDtypeStruct((T, DK), out_dt), jax.ShapeDtypeStruct((T, DK), out_dt),
         jax.ShapeDtypeStruct((T, D), out_dt), jax.ShapeDtypeStruct((T, DK), F32)],
        [pltpu.VMEM((hp, hv, hk), F32)], args)


def _glaout_f(of, ob, g, gain):
    o = of + ob
    n = o * lax.rsqrt(jnp.mean(o * o, axis=-1, keepdims=True) + RMS_EPS)
    return n * gain * _silu(g)


def _glaout_fwd(o_f, o_b, proj, gain, rb, name):
    T, D = o_f.shape
    hv = D // N_HEADS

    def body(of_ref, ob_ref, g_ref, gn_ref, u_ref):
        for h in range(N_HEADS):
            cs = slice(h * hv, (h + 1) * hv)
            u_ref[:, cs] = _glaout_f(of_ref[:, cs], ob_ref[:, cs], g_ref[:, cs], gn_ref[:, cs]).astype(BF16)

    return pl.pallas_call(
        body, name=name, grid=(T // rb,),
        in_specs=[_row(rb, D), _row(rb, D), _row(rb, D, 1), _vec(D)],
        out_specs=_row(rb, D), out_shape=jax.ShapeDtypeStruct((T, D), BF16),
        compiler_params=_cparams(("parallel",)),
    )(o_f, o_b, proj, gain)


def _glaout_bwd(du, o_f, o_b, proj, gain, rb, name):
    T, D = o_f.shape
    hv = D // N_HEADS

    def body(du_ref, of_ref, ob_ref, g_ref, gn_ref, do_ref, dg_ref, dgn_ref, tmp_ref):
        for h in range(N_HEADS):
            cs = slice(h * hv, (h + 1) * hv)
            _, vjp = jax.vjp(_glaout_f, of_ref[:, cs], ob_ref[:, cs], g_ref[:, cs], gn_ref[:, cs])
            d_of, _, dg, dgn = vjp(du_ref[:, cs])
            do_ref[:, cs] = d_of
            dg_ref[:, cs] = dg.astype(BF16)
            tmp_ref[:, cs] = dgn
        _accum(dgn_ref, tmp_ref[...])

    return pl.pallas_call(
        body, name=name, grid=(T // rb,),
        in_specs=[_row(rb, D), _row(rb, D), _row(rb, D), _row(rb, D, 1), _vec(D)],
        out_specs=[_row(rb, D), _row(rb, D), _acc2(D)],
        out_shape=[jax.ShapeDtypeStruct((T, D), F32), jax.ShapeDtypeStruct((T, D), BF16), _acc_shape(D)],
        scratch_shapes=[pltpu.VMEM((1, D), F32)],
        compiler_params=_cparams(("arbitrary",)),
    )(du, o_f, o_b, proj, gain)


def _merge_f(bg1, bg2, yg, yp):
    return _sig(bg1) * yg + _sig(bg2) * yp


def _merge_fwd(proj, y_gla, y_pool, rb, name):
    T, D = y_gla.shape

    def body(b1_ref, b2_ref, yg_ref, yp_ref, m_ref):
        m_ref[...] = _merge_f(b1_ref[...], b2_ref[...], yg_ref[...], yp_ref[...]).astype(BF16)

    return pl.pallas_call(
        body, name=name, grid=(T // rb,),
        in_specs=[_row(rb, D, 2), _row(rb, D, 3), _row(rb, D), _row(rb, D)],
        out_specs=_row(rb, D), out_shape=jax.ShapeDtypeStruct((T, D), BF16),
        compiler_params=_cparams(("parallel",)),
    )(proj, proj, y_gla, y_pool)


def _merge_bwd(dm, proj, y_gla, y_pool, rb, name):
    T, D = y_gla.shape

    def body(dm_ref, b1_ref, b2_ref, yg_ref, yp_ref, d1_ref, d2_ref, dyg_ref, dyp_ref):
        _, vjp = jax.vjp(_merge_f, b1_ref[...], b2_ref[...], yg_ref[...], yp_ref[...])
        d1, d2, dyg, dyp = vjp(dm_ref[...])
        d1_ref[...] = d1.astype(BF16)
        d2_ref[...] = d2.astype(BF16)
        dyg_ref[...] = dyg.astype(BF16)
        dyp_ref[...] = dyp.astype(BF16)

    return pl.pallas_call(
        body, name=name, grid=(T // rb,),
        in_specs=[_row(rb, D), _row(rb, D, 2), _row(rb, D, 3), _row(rb, D), _row(rb, D)],
        out_specs=[_row(rb, D)] * 4, out_shape=[jax.ShapeDtypeStruct((T, D), BF16)] * 4,
        compiler_params=_cparams(("parallel",)),
    )(dm, proj, proj, y_gla, y_pool)


def _swiglu_f(gate, up):
    return _silu(gate) * up


def _pool_consts(ctx_len, seq):
    rows = seq // GRID_W
    reps = POOL_TB // GRID_W
    mw, bc, cw, ch, cc = [], [], [], [], []
    for w in POOL_WINDOWS:
        lo, hi = w // 2, w - w // 2 - 1

        def band(n):
            i = np.arange(n)[:, None]
            j = np.arange(n)[None, :]
            return ((j - i >= -lo) & (j - i <= hi)).astype(np.float32)

        def count(n):
            i = np.arange(n)
            return (np.minimum(i + hi + 1, n) - np.maximum(i - lo, 0)).astype(np.float32)

        mw.append(np.kron(np.eye(reps, dtype=np.float32), band(GRID_W)))
        bc.append(band(ctx_len))
        cw.append(np.tile(count(GRID_W), reps)[:, None])
        ch.append(np.repeat(count(rows), GRID_W)[:, None])
        cc.append(count(ctx_len)[:, None])
    mw, bc = np.stack(mw), np.stack(bc)
    return dict(
        mw=jnp.asarray(mw, BF16), mwt=jnp.asarray(mw.transpose(0, 2, 1), BF16),
        bc=jnp.asarray(bc, BF16), bct=jnp.asarray(bc.transpose(0, 2, 1), BF16),
        cw=jnp.asarray(np.stack(cw)), ch=jnp.asarray(np.stack(ch)), cc=jnp.asarray(np.stack(cc)))


def _gspec(*shape):
    nd = len(shape)
    return pl.BlockSpec((None,) + tuple(shape), lambda g: (g,) + (0,) * nd)


def _pool_fwd(proj, pc, wg, scale, ctx_len, D, name):
    T = proj.shape[0]
    seq = T - ctx_len
    dp = D // 2
    pg = dp // len(POOL_WINDOWS)
    nblk = seq // POOL_TB
    padt = POOL_PAD_ROWS * GRID_W
    p_col0 = 5 * D // pg

    def body(p_ref, mw_ref, bc_ref, cw_ref, ch_ref, cc_ref, wg_ref, sc_ref, pd_ref, y0_ref, r_ref, pad_ref):
        g = pl.program_id(0)

        def tail(rows, mean, x):
            pdb = (mean - x).astype(BF16)
            y0 = _dot(pdb, wg_ref[...])
            pd_ref[rows, :] = pdb
            y0_ref[rows, :] = y0
            r_ref[rows, :] = (y0 * sc_ref[...]).astype(BF16)

        xc = p_ref[0:ctx_len, :]
        tail(slice(0, ctx_len), _dot2(bc_ref[...], xc) / cc_ref[...], xc)

        pad_ref[0:padt, :] = jnp.zeros((padt, pg), F32)
        pad_ref[padt + seq:, :] = jnp.zeros((padt, pg), F32)

        def wpass(b, carry):
            rows = pl.ds(pl.multiple_of(ctx_len + b * POOL_TB, CHUNK), POOL_TB)
            dst = pl.ds(pl.multiple_of(padt + b * POOL_TB, CHUNK), POOL_TB)
            pad_ref[dst, :] = _dot2(mw_ref[...], p_ref[rows, :]) / cw_ref[...]
            return carry

        lax.fori_loop(0, nblk, wpass, 0)

        for gi, w in enumerate(POOL_WINDOWS):
            lo, hi = w // 2, w - w // 2 - 1

            @pl.when(g == gi)
            def _():
                def hpass(b, carry):
                    acc = jnp.zeros((POOL_TB, pg), F32)
                    for d in range(-lo, hi + 1):
                        src = pl.ds(pl.multiple_of(padt + b * POOL_TB + d * GRID_W, CHUNK), POOL_TB)
                        acc = acc + pad_ref[src, :]
                    mean = acc / ch_ref[pl.ds(pl.multiple_of(b * POOL_TB, CHUNK), POOL_TB), :]
                    rows = pl.ds(pl.multiple_of(ctx_len + b * POOL_TB, CHUNK), POOL_TB)
                    tail(rows, mean, p_ref[rows, :])
                    return carry

                lax.fori_loop(0, nblk, hpass, 0)

    col = lambda g: (0, g)
    return pl.pallas_call(
        body, name=name, grid=(len(POOL_WINDOWS),),
        in_specs=[pl.BlockSpec((T, pg), lambda g: (0, p_col0 + g)),
                  _gspec(POOL_TB, POOL_TB), _gspec(ctx_len, ctx_len), _gspec(POOL_TB, 1), _gspec(seq, 1),
                  _gspec(ctx_len, 1), _gspec(pg, pg), pl.BlockSpec((1, pg), col)],
        out_specs=[pl.BlockSpec((T, pg), col)] * 3,
        out_shape=[jax.ShapeDtypeStruct((T, dp), BF16), jax.ShapeDtypeStruct((T, dp), F32),
                   jax.ShapeDtypeStruct((T, dp), BF16)],
        scratch_shapes=[pltpu.VMEM((seq + 2 * padt, pg), F32)],
        compiler_params=_cparams(("arbitrary",)),
    )(proj, pc["mw"], pc["bc"], pc["cw"], pc["ch"], pc["cc"], wg, scale)


def _pool_bwd(dr, y0, pd, pc, wg, scale, ctx_len, D, name):
    T = dr.shape[0]
    seq = T - ctx_len
    dp = D // 2
    ng = len(POOL_WINDOWS)
    pg = dp // ng
    nblk = seq // POOL_TB
    padt = POOL_PAD_ROWS * GRID_W

    def body(dr_ref, y0_ref, pd_ref, mwt_ref, bct_ref, cw_ref, ch_ref, cc_ref, wg_ref, sc_ref,
             dp_ref, dsc_ref, gwg_ref, pad_ref, dpd_ref):
        g = pl.program_id(0)
        dsc_ref[...] = jnp.zeros_like(dsc_ref)
        gwg_ref[...] = jnp.zeros_like(gwg_ref)

        def head(rows):
            drv = dr_ref[rows, :]
            dsc_ref[...] += jnp.sum(drv * y0_ref[rows, :], axis=0, keepdims=True)
            dy0 = (drv * sc_ref[...]).astype(BF16)
            gwg_ref[...] += _dot_tn(pd_ref[rows, :], dy0)
            return _dot_nt(dy0, wg_ref[...])

        crow = slice(0, ctx_len)
        dpd_c = head(crow)
        dp_ref[crow, :] = (_dot2(bct_ref[...], dpd_c / cc_ref[...]) - dpd_c).astype(BF16)

        pad_ref[0:padt, :] = jnp.zeros((padt, pg), F32)
        pad_ref[padt + seq:, :] = jnp.zeros((padt, pg), F32)

        def first(b, carry):
            rows = pl.ds(pl.multiple_of(ctx_len + b * POOL_TB, CHUNK), POOL_TB)
            lrows = pl.ds(pl.multiple_of(b * POOL_TB, CHUNK), POOL_TB)
            dst = pl.ds(pl.multiple_of(padt + b * POOL_TB, CHUNK), POOL_TB)
            dpd = head(rows)
            dpd_ref[lrows, :] = dpd
            pad_ref[dst, :] = dpd / ch_ref[lrows, :]
            return carry

        lax.fori_loop(0, nblk, first, 0)

        for gi, w in enumerate(POOL_WINDOWS):
            lo, hi = w // 2, w - w // 2 - 1

            @pl.when(g == gi)
            def _():
                def second(b, carry):
                    acc = jnp.zeros((POOL_TB, pg), F32)
                    for d in range(-hi, lo + 1):
                        src = pl.ds(pl.multiple_of(padt + b * POOL_TB + d * GRID_W, CHUNK), POOL_TB)
                        acc = acc + pad_ref[src, :]
                    rows = pl.ds(pl.multiple_of(ctx_len + b * POOL_TB, CHUNK), POOL_TB)
                    lrows = pl.ds(pl.multiple_of(b * POOL_TB, CHUNK), POOL_TB)
                    dx = _dot2(mwt_ref[...], acc / cw_ref[...]) - dpd_ref[lrows, :]
                    dp_ref[rows, :] = dx.astype(BF16)
                    return carry

                lax.fori_loop(0, nblk, second, 0)

    col = lambda g: (0, g)
    return pl.pallas_call(
        body, name=name, grid=(ng,),
        in_specs=[pl.BlockSpec((T, pg), col), pl.BlockSpec((T, pg), col), pl.BlockSpec((T, pg), col),
                  _gspec(POOL_TB, POOL_TB), _gspec(ctx_len, ctx_len), _gspec(POOL_TB, 1), _gspec(seq, 1),
                  _gspec(ctx_len, 1), _gspec(pg, pg), pl.BlockSpec((1, pg), col)],
        out_specs=[pl.BlockSpec((T, pg), col), pl.BlockSpec((1, pg), col), _gspec(pg, pg)],
        out_shape=[jax.ShapeDtypeStruct((T, dp), BF16), jax.ShapeDtypeStruct((1, dp), F32),
                   jax.ShapeDtypeStruct((ng, pg, pg), F32)],
        scratch_shapes=[pltpu.VMEM((seq + 2 * padt, pg), F32), pltpu.VMEM((seq, pg), F32)],
        compiler_params=_cparams(("arbitrary",)),
    )(dr, y0, pd, pc["mwt"], pc["bct"], pc["cw"], pc["ch"], pc["cc"], wg, scale)


def _loss_head(x2, target, rb, name):
    T, D = x2.shape

    def body(y_ref, t_ref, dy_ref, l_ref):
        i = pl.program_id(0)

        @pl.when(i == 0)
        def _():
            dy_ref[...] = jnp.zeros_like(dy_ref)
            l_ref[...] = jnp.zeros_like(l_ref)

        @pl.when(i > 0)
        def _():
            e = y_ref[...] - t_ref[...]
            dy_ref[...] = e * (1.0 / D)
            l_ref[...] += 0.5 * jnp.sum(jnp.mean(e * e, axis=-1, keepdims=True), axis=0, keepdims=True)

    return pl.pallas_call(
        body, name=name, grid=(T // rb,),
        in_specs=[_row(rb, D), pl.BlockSpec((rb, D), lambda i: (jnp.maximum(i - 1, 0), 0))],
        out_specs=[_row(rb, D), pl.BlockSpec((8, 128), lambda i: (0, 0))],
        out_shape=[jax.ShapeDtypeStruct((T, D), F32), jax.ShapeDtypeStruct((8, 128), F32)],
        compiler_params=_cparams(("arbitrary",)),
    )(x2, target)


def _sum_lead(x, name):
    S, R, C = x.shape

    def body(x_ref, o_ref):
        acc = x_ref[0]
        for s in range(1, S):
            acc = acc + x_ref[s]
        o_ref[...] = acc

    return pl.pallas_call(
        body, name=name, out_shape=jax.ShapeDtypeStruct((R, C), F32),
        compiler_params=_cparams(),
    )(x)


def _silu_rows(cond, name):
    def body(c_ref, o_ref):
        o_ref[...] = _silu(c_ref[...]).astype(BF16)

    return pl.pallas_call(body, name=name, out_shape=jax.ShapeDtypeStruct(cond.shape, BF16),
                          compiler_params=_cparams())(cond)


def _silu_grad(cond, ds, name):
    def body(c_ref, d_ref, o_ref):
        _, vjp = jax.vjp(_silu, c_ref[...])
        o_ref[...] = vjp(d_ref[...])[0]

    return pl.pallas_call(body, name=name, out_shape=jax.ShapeDtypeStruct(cond.shape, F32),
                          compiler_params=_cparams())(cond, ds)


def _adamw_math(g, w_ref, m_ref, v_ref, go_ref, d_ref, mo_ref, vo_ref):
    c1 = 1.0 / (1.0 - ADAM_B1 ** ADAM_STEP)
    c2 = 1.0 / (1.0 - ADAM_B2 ** ADAM_STEP)
    mn = ADAM_B1 * m_ref[...] + (1.0 - ADAM_B1) * g
    vn = ADAM_B2 * v_ref[...] + (1.0 - ADAM_B2) * (g * g)
    go_ref[...] = g
    mo_ref[...] = mn
    vo_ref[...] = vn
    d_ref[...] = -ADAM_LR * ((mn * c1) / (jnp.sqrt(vn * c2) + ADAM_EPS) + ADAM_WD * w_ref[...])


def _adamw(gs, w, m, v, name):
    S, R, C = gs.shape
    cpad = -(-C // 128) * 128
    rt = _pick(R, max(16, (1 << 20) // (4 * cpad)), 16)

    def body(g_ref, w_ref, m_ref, v_ref, go_ref, d_ref, mo_ref, vo_ref):
        g = g_ref[0].astype(F32)
        for s in range(1, S):
            g = g + g_ref[s].astype(F32)
        _adamw_math(g, w_ref, m_ref, v_ref, go_ref, d_ref, mo_ref, vo_ref)

    blk = pl.BlockSpec((rt, C), lambda i: (i, 0))
    return pl.pallas_call(
        body, name=name, grid=(R // rt,),
        in_specs=[pl.BlockSpec((S, rt, C), lambda i: (0, i, 0)), blk, blk, blk],
        out_specs=[blk] * 4, out_shape=[jax.ShapeDtypeStruct((R, C), F32)] * 4,
        compiler_params=_cparams(("parallel",)),
    )(gs, w, m, v)


def _adamw_layer(gs, w, m, v, l, prev, name):
    S, R, C = gs.shape
    L = w.shape[0]
    cpad = -(-C // 128) * 128
    rt = _pick(R, max(16, (1 << 20) // (4 * cpad)), 16)

    def body(*refs):
        g_ref, w_ref, m_ref, v_ref = refs[:4]
        go_ref, d_ref, mo_ref, vo_ref = refs[-4:]
        g = g_ref[0].astype(F32)
        for s in range(1, S):
            g = g + g_ref[s].astype(F32)
        _adamw_math(g, w_ref, m_ref, v_ref, go_ref, d_ref, mo_ref, vo_ref)

    blk = pl.BlockSpec((None, rt, C), lambda i: (l, i, 0))
    in_specs = [pl.BlockSpec((S, rt, C), lambda i: (0, i, 0)), blk, blk, blk]
    args = [gs, w, m, v]
    aliases = {}
    if prev is not None:
        in_specs += [pl.BlockSpec(memory_space=pl.ANY)] * 4
        args += list(prev)
        aliases = {4 + q: q for q in range(4)}
    return pl.pallas_call(
        body, name=name, grid=(R // rt,), in_specs=in_specs,
        out_specs=[blk] * 4, out_shape=[jax.ShapeDtypeStruct((L, R, C), F32)] * 4,
        input_output_aliases=aliases,
        compiler_params=_cparams(("parallel",)),
    )(*args)


def _adamw_nd(gs, w, m, v, name):
    shp = w.shape
    if len(shp) == 1:
        r, c = 1, shp[0]
    else:
        r, c = int(np.prod(shp[:-1])), shp[-1]
    outs = _adamw(gs.reshape(gs.shape[0], r, c), w.reshape(r, c), m.reshape(r, c), v.reshape(r, c), name)
    return [o.reshape(shp) for o in outs]


def kernel(x, c, ctx, c_ctx, w_ada, b_ada, w_in, w_decay_up, b_decay_up, gla_norm_gain, w_pool_group, pool_scale, w_gla_out, w_pool_out, w_out, ln_mix_gain, ln_mix_bias, w_ffn_in, w_ffn_out, ln_ffn_gain, ln_ffn_bias, loss_target, m_c_ctx, m_w_ada, m_b_ada, m_w_in, m_w_decay_up, m_b_decay_up, m_gla_norm_gain, m_w_pool_group, m_pool_scale, m_w_gla_out, m_w_pool_out, m_w_out, m_ln_mix_gain, m_ln_mix_bias, m_w_ffn_in, m_w_ffn_out, m_ln_ffn_gain, m_ln_ffn_bias, v_c_ctx, v_w_ada, v_b_ada, v_w_in, v_w_decay_up, v_b_decay_up, v_gla_norm_gain, v_w_pool_group, v_pool_scale, v_w_gla_out, v_w_pool_out, v_w_out, v_ln_mix_gain, v_ln_mix_bias, v_w_ffn_in, v_w_ffn_out, v_ln_ffn_gain, v_ln_ffn_bias):
    L, D = w_ada.shape[0], w_ada.shape[1]
    seq, ctx_len = x.shape[1], ctx.shape[1]
    T = seq + ctx_len
    rb = ctx_len
    DK = D // 2
    DP = D // 2
    ng = len(POOL_WINDOWS)
    pg = DP // ng
    dff = w_ffn_out.shape[1] * N_DEV
    alpha = (2.0 * L) ** 0.25
    assert seq % rb == 0 and rb % CHUNK == 0 and seq % POOL_TB == 0 and ctx_len % 8 == 0
    xi, yi, ci = _my_pos()
    me = 4 * xi + 2 * yi + ci
    pc = _pool_consts(ctx_len, seq)

    shards = dict(w_in=w_in.astype(BF16), go=w_gla_out.astype(BF16), po=w_pool_out.astype(BF16),
                  out=w_out.astype(BF16), fi=w_ffn_in.astype(BF16), fo=w_ffn_out.astype(BF16),
                  pg=w_pool_group.astype(BF16).reshape(L, ng * pg // N_DEV, pg))
    wkeys = ("w_in", "go", "po", "out", "fi", "fo", "pg")

    def prepared(gw):
        main, alr_w = _w_in_operands(gw["w_in"], "w_in_operands")
        pgf = jnp.swapaxes(gw["pg"].reshape(N_DEV, ng, pg // N_DEV, pg), 0, 1).reshape(ng, pg, pg)
        return dict(main=main, alr=alr_w, go=gw["go"].reshape(D, D), po=gw["po"], out=gw["out"].reshape(D, D),
                    fi=gw["fi"][None], fo=gw["fo"].reshape(1, dff, D), pg=pgf)

    def gather_next(fn, names, l, nxt):
        if l + 1 >= L:
            return fn(None)
        res, outs = fn(_gather_job([shards[k] for k in names], l + 1))
        nxt.update(zip(names, outs))
        return res

    gathered = dict(zip(wkeys, _run_job(_gather_job([shards[k] for k in wkeys], 0), "ag_layer0")))

    dku = w_decay_up.shape[-1]
    small_in = jnp.concatenate([c.reshape(-1), w_decay_up.reshape(-1), b_decay_up.reshape(-1)])
    (small_all,) = _gather_flat([small_in], "ag_small")
    c_all = small_all[:, :D]
    n_wdu = L * 2 * GATE_RANK * dku
    wdu_all = small_all[:, D:D + n_wdu].reshape(N_DEV, L, 2, GATE_RANK, dku)
    wdu_full = jnp.transpose(wdu_all, (1, 2, 3, 0, 4)).reshape(L, 2, GATE_RANK, DK)
    bdu_all = small_all[:, D + n_wdu:].reshape(N_DEV, L, 2, dku)
    bdu_full = jnp.transpose(bdu_all, (1, 2, 0, 3)).reshape(L, 1, 2 * DK)
    wdu_bd = jnp.zeros((L, ALR_PAD, 2 * DK), F32)
    wdu_bd = wdu_bd.at[:, :GATE_RANK, :DK].set(wdu_full[:, 0])
    wdu_bd = wdu_bd.at[:, GATE_RANK:2 * GATE_RANK, DK:].set(wdu_full[:, 1]).astype(BF16)

    ncond = 16
    cond = jnp.concatenate([c_all, c_ctx.reshape(1, D), jnp.zeros((ncond - N_DEV - 1, D), F32)], axis=0)
    s_cond = _silu_rows(cond, "silu_cond")
    wsh = w_ada.shape[-1]
    b_ada_mine = lax.dynamic_slice_in_dim(b_ada, me * wsh, wsh, axis=1)
    mod_part = jnp.stack([_mm(s_cond, w_ada, "nn", F32, "mod_mm", bias=b_ada_mine[l:l + 1], b_pre=(l,))
                          for l in range(L)])
    (mod_all,) = _all_gather([mod_part], "ag_mod")
    mod_all = jnp.swapaxes(mod_all, 1, 2).reshape(L, ncond, N_MOD * D)
    mod_lat = lax.dynamic_slice_in_dim(mod_all, me, 1, axis=1)
    mods = jnp.concatenate([mod_all[:, N_DEV:N_DEV + 1], mod_lat], axis=1).reshape(L, 2, 1, N_MOD * D)
    SH_M, SC_M, GT_M, SH_F, SC_F, GT_F = range(N_MOD)

    xa = jnp.concatenate([ctx[0], x[0]], axis=0)
    vec = lambda a, l: a[l].reshape(1, -1)
    saved = []
    h = _mod_fwd(xa, mods[0], SC_M, SH_M, rb, "mod_fwd")
    weights = []
    for l in range(L):
        W = prepared(gathered)
        weights.append(W)
        gathered = {}
        proj = gather_next(lambda j: _mm(h, W["main"], "nn", F32, "mm_in", job=j), ["w_in"], l, gathered)
        alr = _mm(h, W["alr"], "nn", F32, "mm_alr")
        la = _decay_fwd(alr, wdu_bd[l], bdu_full[l], rb, "decay_fwd")
        o_f, s_f = gather_next(lambda j: _gla_fwd(proj, la, False, rb, D, "gla_fwd_f", job=j), ["go", "out"], l,
                               gathered)
        o_b, s_b = gather_next(lambda j: _gla_fwd(proj, la, True, rb, D, "gla_fwd_b", job=j), ["po", "pg"], l,
                               gathered)
        u = _glaout_fwd(o_f, o_b, proj, vec(gla_norm_gain, l), rb, "glaout_fwd")
        y_gla = _mm(u, W["go"], "nn", F32, "mm_go")
        pd, y0, r = _pool_fwd(proj, pc, W["pg"], vec(pool_scale, l), ctx_len, D, "pool_fwd")
        y_pool = _mm(r, W["po"], "nn", F32, "mm_po", b_shard=True)
        m_ = _merge_fwd(proj, y_gla, y_pool, rb, "merge_fwd")
        mix = _mm(m_, W["out"], "nn", F32, "mm_out")
        x1, h2 = _unit_fwd(alpha, xa, mix, mods[l], GT_M, vec(ln_mix_gain, l), vec(ln_mix_bias, l),
                           (mods[l], SC_F, SH_F), rb, "unit_mix_fwd")
        ff, s_ = gather_next(lambda j: _ffn_in_fwd(h2, W["fi"], 0, "mm_fi_swiglu", job=j), ["fi"], l, gathered)
        ffn = gather_next(lambda j: _mm(s_, W["fo"], "nn", F32, "mm_fo", b_pre=(0,), job=j), ["fo"], l, gathered)
        nxt = (mods[l + 1], SC_M, SH_M) if l + 1 < L else None
        x2, h_next = _unit_fwd(alpha, x1, ffn, mods[l], GT_F, vec(ln_ffn_gain, l), vec(ln_ffn_bias, l),
                               nxt, rb, "unit_ffn_fwd")
        saved.append(dict(xa=xa, h=h, proj=proj, alr=alr, la=la, o_f=o_f, o_b=o_b, s_f=s_f, s_b=s_b, u=u,
                          y_gla=y_gla, pd=pd, y0=y0, r=r, y_pool=y_pool, m=m_, mix=mix, x1=x1, h2=h2, ff=ff,
                          s=s_, ffn=ffn))
        xa, h = x2, h_next

    dxo, loss_part = _loss_head(xa, loss_target[0], rb, "loss_head")
    loss = lax.psum(loss_part[0, 0], ("x", "y", "c"))

    big_params = [("w_in", w_in, m_w_in, v_w_in), ("w_gla_out", w_gla_out, m_w_gla_out, v_w_gla_out),
                  ("w_pool_out", w_pool_out, m_w_pool_out, v_w_pool_out), ("w_out", w_out, m_w_out, v_w_out),
                  ("w_ffn_in", w_ffn_in, m_w_ffn_in, v_w_ffn_in), ("w_ffn_out", w_ffn_out, m_w_ffn_out, v_w_ffn_out),
                  ("w_pool_group", w_pool_group, m_w_pool_group, v_w_pool_group)]
    big_out = {nm: None for nm, _, _, _ in big_params}
    g_small = {k: [None] * L for k in ("gla_gain", "pool_scale", "mix_g", "mix_b", "ffn_g", "ffn_b", "wdu", "bdu")}
    dmods = [None] * L
    dh = None
    sum2 = lambda a: a[0] + a[1]
    rows8 = lambda g: g.reshape(N_DEV, g.shape[0] // N_DEV, g.shape[1])

    def apply_adamw(parts, layer):
        for (nm, w, m, v), gs in zip(big_params, parts):
            R, C = gs.shape[1], gs.shape[2]
            big_out[nm] = _adamw_layer(gs, w.reshape(L, R, C), m.reshape(L, R, C), v.reshape(L, R, C), layer,
                                       big_out[nm], "adamw_" + nm)

    LATE = (0, 1, 2, 3, 6)
    late_chunks = None
    arrived = {}

    def behind(fn, job, positions):
        if job is None:
            return fn(None)
        res, outs = fn(job)
        if positions is None:
            return res, outs
        arrived.update(zip(positions, outs))
        return res

    for l in range(L - 1, -1, -1):
        sv = saved[l]
        W = weights[l]
        nxt = (mods[l + 1], SC_M, SH_M) if l + 1 < L else None
        unit = lambda j: _unit_bwd(alpha, dxo, dh, sv["x1"], sv["ffn"], mods[l], GT_F, vec(ln_ffn_gain, l),
                                   vec(ln_ffn_bias, l), nxt, rb, "unit_ffn_bwd", job=j)
        late_pairs = None
        if late_chunks is None:
            res = unit(None)
        else:
            res, sib = behind(unit, _sibling_job(late_chunks), None)
            late_pairs = _pair_adds(late_chunks, sib, "_late")
        dx1, dffn, d_gtf, d_gf, d_bf, d_scm_n, d_shm_n = res
        if nxt is not None:
            dmods[l + 1]["sc_m"], dmods[l + 1]["sh_m"] = d_scm_n, d_shm_n
        dmods[l] = dict(gt_f=d_gtf)
        g_small["ffn_g"][l], g_small["ffn_b"][l] = sum2(d_gf), sum2(d_bf)
        dff_ = behind(lambda j: _ffn_out_dx(dffn, W["fo"], 0, sv["ff"], "mm_fo_dx_swiglu", job=j),
                      _chip_job(late_pairs[1:]) if late_pairs else None, LATE[1:])
        c_fo = rows8(_mm(sv["s"], dffn, "tn", BF16, "mm_fo_dw"))
        dh2 = behind(lambda j: _mm(dff_, W["fi"], "nt", F32, "mm_fi_dx", b_pre=(0,), b_shard=True, a_half=True,
                                   job=j), _chip_job(late_pairs[:1]) if late_pairs else None, LATE[:1])
        c_fi = _mm(sv["h2"], dff_, "tn", BF16, "mm_fi_dw", b_half=True, out_shard=True)
        if late_pairs:
            apply_adamw([arrived[i] for i in range(len(big_params))], l + 1)
            arrived = {}
        ffn_chunks = [c_fi, c_fo]
        res, sib = behind(lambda j: _unit_bwd(
            alpha, dx1, dh2, sv["xa"], sv["mix"], mods[l], GT_M, vec(ln_mix_gain, l), vec(ln_mix_bias, l),
            (mods[l], SC_F, SH_F), rb, "unit_mix_bwd", job=j), _sibling_job(ffn_chunks), None)
        dxa, dmix, d_gtm, d_gm, d_bm, d_scf, d_shf = res
        ffn_pairs = _pair_adds(ffn_chunks, sib, "_ffn")
        dmods[l].update(gt_m=d_gtm, sc_f=d_scf, sh_f=d_shf)
        g_small["mix_g"][l], g_small["mix_b"][l] = sum2(d_gm), sum2(d_bm)
        dm = _mm(dmix, W["out"], "nt", F32, "mm_out_dx")
        c_out = rows8(_mm(sv["m"], dmix, "tn", BF16, "mm_out_dw"))
        dbg1, dbg2, dyg, dyp = _merge_bwd(dm, sv["proj"], sv["y_gla"], sv["y_pool"], rb, "merge_bwd")
        dr = _mm(dyp, W["po"], "nt", F32, "mm_po_dx", b_shard=True)
        c_po = _mm(sv["r"], dyp, "tn", BF16, "mm_po_dw", out_shard=True)
        dp_, d_ps, g_pgl = _pool_bwd(dr, sv["y0"], sv["pd"], pc, W["pg"], vec(pool_scale, l), ctx_len, D, "pool_bwd")
        g_small["pool_scale"][l] = d_ps
        c_pg = jnp.swapaxes(g_pgl.astype(BF16).reshape(ng, N_DEV, pg // N_DEV, pg), 0, 1).reshape(N_DEV, -1, pg)
        du = _mm(dyg, W["go"], "nt", F32, "mm_go_dx")
        c_go = rows8(_mm(sv["u"], dyg, "tn", BF16, "mm_go_dw"))
        do, dg, d_gg = _glaout_bwd(du, sv["o_f"], sv["o_b"], sv["proj"], vec(gla_norm_gain, l), rb, "glaout_bwd")
        g_small["gla_gain"][l] = sum2(d_gg)
        dq_f, dk_f, dv_f, dla_f = behind(lambda j: _gla_bwd(
            sv["proj"], sv["la"], do, sv["s_f"], False, rb, D, None, "gla_bwd_f", job=j),
            _chip_job(ffn_pairs[:1]), (4,))
        dq, dk, dv, dla_b = behind(lambda j: _gla_bwd(
            sv["proj"], sv["la"], do, sv["s_b"], True, rb, D, (dq_f, dk_f, dv_f), "gla_bwd_b", job=j),
            _chip_job(ffn_pairs[1:]), (5,))
        dalr, g_wdu, g_bdu = _decay_bwd(dla_f, dla_b, sv["alr"], wdu_bd[l], bdu_full[l], rb, "decay_bwd")
        g_small["wdu"][l] = jnp.stack([g_wdu[:GATE_RANK, :DK], g_wdu[GATE_RANK:2 * GATE_RANK, DK:]])
        g_small["bdu"][l] = g_bdu.reshape(2, DK)
        dproj = jnp.concatenate([dv, dg, dbg1, dbg2, dq, dk, dp_], axis=1)
        dh_alr = _mm(dalr, W["alr"], "nt", F32, "mm_alr_dx")
        dh = _mm(dproj, W["main"], "nt", F32, "mm_in_dx", add=dh_alr)
        g_main = _mm(sv["h"], dproj, "tn", BF16, "mm_in_dw")
        g_alr = _mm(sv["h"], dalr, "tn", BF16, "mm_alr_dw")
        c_in = _w_in_chunks(g_main, g_alr, w_in.shape[2], "w_in_chunks")
        late_chunks = [c_in, c_go, c_po, c_out, c_pg]
        dxo = dxa
    sib = _run_job(_sibling_job(late_chunks), "rs_sibling_last")
    arrived.update(zip(LATE, _run_job(_chip_job(_pair_adds(late_chunks, sib, "_last")), "rs_chips_last")))
    apply_adamw([arrived[i] for i in range(len(big_params))], 0)
    grad_xa, d_scm0, d_shm0 = _mod_bwd(dxo, dh, saved[0]["xa"], mods[0], SC_M, SH_M, rb, "mod_bwd")
    dmods[0]["sc_m"], dmods[0]["sh_m"] = d_scm0, d_shm0
    grad_x = grad_xa[ctx_len:].reshape(1, seq, D)

    order = ("sh_m", "sc_m", "gt_m", "sh_f", "sc_f", "gt_f")
    dmod = jnp.stack([jnp.concatenate([dmods[l][k] for k in order], axis=2) for l in range(L)])
    dmod = dmod.reshape(-1)
    sm = lambda k: jnp.stack([a.reshape(-1) for a in g_small[k]]).reshape(-1)
    small_keys = ("gla_gain", "pool_scale", "mix_g", "mix_b", "ffn_g", "ffn_b", "wdu", "bdu")
    small_part = jnp.concatenate([sm(k) for k in small_keys])
    small_g, dmod_g = _gather_flat([small_part, dmod], "ag_small_grads")
    small_sum = _sum_lead(small_g.reshape(N_DEV, -1, 128), "sum_small").reshape(-1)
    off = 0
    rep = {}
    for k, n in zip(small_keys, (L * D, L * DP, L * D, L * D, L * D, L * D, L * 2 * GATE_RANK * DK, L * 2 * DK)):
        rep[k] = small_sum[off:off + n]
        off += n
    g_wdu_mine = lax.dynamic_slice_in_dim(rep["wdu"].reshape(L, 2, GATE_RANK, DK), me * dku, dku, axis=3)
    g_bdu_mine = lax.dynamic_slice_in_dim(rep["bdu"].reshape(L, 2, DK), me * dku, dku, axis=2)

    dmod_all = dmod_g.reshape(N_DEV, L, 2, N_MOD * D)
    dm_ctx = _sum_lead(dmod_all[:, :, 0].reshape(N_DEV, L, N_MOD * D), "sum_dmod_ctx")
    dm_rows = jnp.concatenate([jnp.swapaxes(dmod_all[:, :, 1], 0, 1), dm_ctx[:, None],
                               jnp.zeros((L, ncond - N_DEV - 1, N_MOD * D), F32)], axis=1)
    g_b_ada = _sum_lead(jnp.swapaxes(dm_rows, 0, 1), "sum_b_ada")
    dm_mine = lax.dynamic_slice_in_dim(dm_rows, me * wsh, wsh, axis=2).astype(BF16)
    g_w_ada = jnp.stack([_mm(s_cond, dm_mine[l], "tn", F32, "ada_dw") for l in range(L)])
    ds_part = _sum_lead(jnp.stack([_mm(dm_mine[l], w_ada, "nt", F32, "ada_dx", b_pre=(l,)) for l in range(L)]),
                        "sum_ds")
    (ds_all,) = _gather_flat([ds_part[N_DEV]], "ag_ds")
    ds_ctx = _sum_lead(ds_all.reshape(N_DEV, 1, D), "sum_ds_ctx")
    g_c_ctx = _silu_grad(c_ctx.reshape(1, D), ds_ctx, "silu_grad").reshape(D)

    one = lambda g: g[None]
    small_table = {
        "c_ctx": (one(g_c_ctx), c_ctx, m_c_ctx, v_c_ctx),
        "w_ada": (one(g_w_ada), w_ada, m_w_ada, v_w_ada),
        "b_ada": (one(g_b_ada), b_ada, m_b_ada, v_b_ada),
        "w_decay_up": (one(g_wdu_mine), w_decay_up, m_w_decay_up, v_w_decay_up),
        "b_decay_up": (one(g_bdu_mine), b_decay_up, m_b_decay_up, v_b_decay_up),
        "gla_norm_gain": (one(rep["gla_gain"].reshape(L, D)), gla_norm_gain, m_gla_norm_gain, v_gla_norm_gain),
        "pool_scale": (one(rep["pool_scale"].reshape(L, DP)), pool_scale, m_pool_scale, v_pool_scale),
        "ln_mix_gain": (one(rep["mix_g"].reshape(L, D)), ln_mix_gain, m_ln_mix_gain, v_ln_mix_gain),
        "ln_mix_bias": (one(rep["mix_b"].reshape(L, D)), ln_mix_bias, m_ln_mix_bias, v_ln_mix_bias),
        "ln_ffn_gain": (one(rep["ffn_g"].reshape(L, D)), ln_ffn_gain, m_ln_ffn_gain, v_ln_ffn_gain),
        "ln_ffn_bias": (one(rep["ffn_b"].reshape(L, D)), ln_ffn_bias, m_ln_ffn_bias, v_ln_ffn_bias),
    }
    big_shapes = {nm: w.shape for nm, w, _, _ in big_params}
    names = ("c_ctx", "w_ada", "b_ada", "w_in", "w_decay_up", "b_decay_up", "gla_norm_gain", "w_pool_group",
             "pool_scale", "w_gla_out", "w_pool_out", "w_out", "ln_mix_gain", "ln_mix_bias", "w_ffn_in", "w_ffn_out",
             "ln_ffn_gain", "ln_ffn_bias")
    grads, deltas, new_m, new_v = [], [], [], []
    for nm in names:
        if nm in small_table:
            res = _adamw_nd(*small_table[nm], "adamw_" + nm)
        else:
            res = [o.reshape(big_shapes[nm]) for o in big_out[nm]]
        for lst, o in zip((grads, deltas, new_m, new_v), res):
            lst.append(o)
    return (loss, grad_x, *grads, *deltas, *new_m, *new_v)
```

```python
import functools
import math

import numpy as np
import jax
import jax.numpy as jnp
from jax import lax
from jax.experimental import pallas as pl
from jax.experimental.pallas import tpu as pltpu

F32 = jnp.float32
BF16 = jnp.bfloat16

N_DEV = 8
N_HEADS = 4
GATE_RANK = 16
GATE_NORM = 16.0
CHUNK = 64
GRID_W = 64
POOL_WINDOWS = (2, 4, 8, 16)
N_MOD = 6
LN_EPS = 1e-5
RMS_EPS = 1e-6
ALR_PAD = 128
POOL_TB = 256
POOL_PAD_ROWS = 8
ADAM_LR = 0.001
ADAM_B1 = 0.9
ADAM_B2 = 0.999
ADAM_EPS = 1e-08
ADAM_WD = 0.01
ADAM_STEP = 10
VMEM_LIMIT = 56 * 1024 * 1024
MESH = pl.DeviceIdType.MESH


def _cparams(sem=None):
    return pltpu.CompilerParams(dimension_semantics=sem, vmem_limit_bytes=VMEM_LIMIT)


def _pick(dim, cap, mult):
    best = None
    for d in range(mult, min(dim, cap) + 1, mult):
        if dim % d == 0:
            best = d
    return best if best is not None else dim


def _sig(x):
    return 1.0 / (1.0 + jnp.exp(-x))


def _silu(x):
    return x * _sig(x)


def _dot(a, b):
    return lax.dot_general(a, b, (((1,), (0,)), ((), ())), preferred_element_type=F32)


def _dot_nt(a, b):
    return lax.dot_general(a, b, (((1,), (1,)), ((), ())), preferred_element_type=F32)


def _dot_tn(a, b):
    return lax.dot_general(a, b, (((0,), (0,)), ((), ())), preferred_element_type=F32)


def _split2(x):
    hi = x.astype(BF16)
    lo = (x - hi.astype(F32)).astype(BF16)
    return hi, lo


def _dot2(m_b, x):
    hi, lo = _split2(x)
    return _dot(m_b, hi) + _dot(m_b, lo)


def _dot3(m_b, x):
    h1 = x.astype(BF16)
    r1 = x - h1.astype(F32)
    h2 = r1.astype(BF16)
    h3 = (r1 - h2.astype(F32)).astype(BF16)
    return _dot(m_b, h1) + _dot(m_b, h2) + _dot(m_b, h3)


def _my_pos():
    return lax.axis_index("x"), lax.axis_index("y"), lax.axis_index("c")


def _all_gather(arrs, name):
    n = len(arrs)
    srcs = [a.reshape((a.shape[0], 1) + a.shape[1:]) for a in arrs]
    outs = [jax.ShapeDtypeStruct((a.shape[0], N_DEV) + a.shape[1:], a.dtype) for a in arrs]

    def body(*refs):
        in_refs, out_refs = refs[:n], refs[n:2 * n]
        send_sems, recv_sems, local_sems = refs[2 * n:]
        x, y, c = _my_pos()
        me, sibling = (x, y, c), (x, y, 1 - c)
        chips = [(1 - x, y), (x, 1 - y), (1 - x, 1 - y)]

        def slot(t, pos):
            return out_refs[t].at[:, pl.ds(4 * pos[0] + 2 * pos[1] + pos[2], 1)]

        def copy(t, k, block, to, src=None):
            return pltpu.make_async_remote_copy(
                src_ref=slot(t, block) if src is None else src, dst_ref=slot(t, block),
                send_sem=send_sems.at[t * 7 + k], recv_sem=recv_sems.at[t * 7 + k],
                device_id=to, device_id_type=MESH)

        mine = [pltpu.make_async_copy(in_refs[t], slot(t, me), local_sems.at[t]) for t in range(n)]
        for cp in mine:
            cp.start()
        first = []
        for t in range(n):
            first.append(copy(t, 0, me, sibling, src=in_refs[t]))
            first += [copy(t, 1 + j, me, (*chip, c), src=in_refs[t]) for j, chip in enumerate(chips)]
        for cp in first:
            cp.start()
        passed = []
        for j, chip in enumerate(chips):
            for t in range(n):
                copy(t, 1 + j, (*chip, c), me).wait_recv()
                fwd = copy(t, 4 + j, (*chip, c), sibling)
                fwd.start()
                passed.append(fwd)
        for t in range(n):
            copy(t, 0, sibling, me).wait_recv()
            for j, chip in enumerate(chips):
                copy(t, 4 + j, (*chip, 1 - c), me).wait_recv()
        for cp in first + passed:
            cp.wait_send()
        for cp in mine:
            cp.wait()

    any_spec = pl.BlockSpec(memory_space=pl.ANY)
    res = pl.pallas_call(
        body, name=name, out_shape=outs,
        in_specs=[any_spec] * n, out_specs=[any_spec] * n,
        scratch_shapes=[pltpu.SemaphoreType.DMA((7 * n,)), pltpu.SemaphoreType.DMA((7 * n,)),
                        pltpu.SemaphoreType.DMA((n,))],
        compiler_params=pltpu.CompilerParams(has_side_effects=True),
    )(*srcs)
    return list(res)


def _gather_flat(vecs, name):
    padded = []
    for v in vecs:
        n = v.shape[0]
        padded.append(jnp.pad(v, (0, -n % 128)).reshape(1, -1, 128))
    res = _all_gather(padded, name)
    return [r.reshape(N_DEV, -1)[:, :v.shape[0]] for r, v in zip(res, vecs)]


N_CHIP = 4


def _comm_call(body, name, arrs, outs, n_sems):
    any_spec = pl.BlockSpec(memory_space=pl.ANY)
    n = len(arrs)
    res = pl.pallas_call(
        body, name=name, out_shape=outs,
        in_specs=[any_spec] * n, out_specs=[any_spec] * len(outs),
        scratch_shapes=[pltpu.SemaphoreType.DMA((s,)) for s in n_sems],
        compiler_params=pltpu.CompilerParams(has_side_effects=True),
    )(*arrs)
    return list(res)


def _sibling_job(arrs):
    n = len(arrs)
    outs = [jax.ShapeDtypeStruct((N_CHIP,) + a.shape[1:], a.dtype) for a in arrs]

    def copies(in_refs, out_refs, sems):
        send_sems, recv_sems = sems
        x, y, c = _my_pos()
        return [pltpu.make_async_remote_copy(
            src_ref=in_refs[t].at[pl.ds(2 * k + (1 - c), 1)], dst_ref=out_refs[t].at[pl.ds(k, 1)],
            send_sem=send_sems.at[t * N_CHIP + k], recv_sem=recv_sems.at[t * N_CHIP + k],
            device_id=(x, y, 1 - c), device_id_type=MESH) for t in range(n) for k in range(N_CHIP)]

    def start(in_refs, out_refs, sems):
        for cp in copies(in_refs, out_refs, sems):
            cp.start()

    def finish(in_refs, out_refs, sems):
        cps = copies(in_refs, out_refs, sems)
        for cp in cps:
            cp.wait_recv()
        for cp in cps:
            cp.wait_send()

    return _Job(arrs, outs, (N_CHIP * n, N_CHIP * n), start, finish)


class _Job:
    def __init__(self, arrs, outs, n_sems, start, finish):
        self.arrs, self.outs, self.n_sems, self.start, self.finish = arrs, outs, n_sems, start, finish


def _gather_job(stacked, l):
    n = len(stacked)
    outs = [jax.ShapeDtypeStruct((N_DEV,) + a.shape[1:], a.dtype) for a in stacked]

    def parts(in_refs, out_refs, sems):
        send_sems, recv_sems, local_sems = sems
        x, y, c = _my_pos()
        me, sibling = (x, y, c), (x, y, 1 - c)
        chips = [(1 - x, y), (x, 1 - y), (1 - x, 1 - y)]
        src = lambda t: in_refs[t].at[pl.ds(l, 1)]

        def slot(t, pos):
            return out_refs[t].at[pl.ds(4 * pos[0] + 2 * pos[1] + pos[2], 1)]

        def copy(t, k, block, to, from_input=False):
            return pltpu.make_async_remote_copy(
                src_ref=src(t) if from_input else slot(t, block), dst_ref=slot(t, block),
                send_sem=send_sems.at[t * 7 + k], recv_sem=recv_sems.at[t * 7 + k],
                device_id=to, device_id_type=MESH)

        mine = [pltpu.make_async_copy(src(t), slot(t, me), local_sems.at[t]) for t in range(n)]
        first = []
        for t in range(n):
            first.append(copy(t, 0, me, sibling, True))
            first += [copy(t, 1 + j, me, (*chip, c), True) for j, chip in enumerate(chips)]
        return me, sibling, chips, copy, mine, first

    def start(in_refs, out_refs, sems):
        _, _, _, _, mine, first = parts(in_refs, out_refs, sems)
        for cp in mine + first:
            cp.start()

    def finish(in_refs, out_refs, sems):
        me, sibling, chips, copy, mine, first = parts(in_refs, out_refs, sems)
        passed = []
        for j, chip in enumerate(chips):
            for t in range(n):
                copy(t, 1 + j, (*chip, me[2]), me).wait_recv()
                fwd = copy(t, 4 + j, (*chip, me[2]), sibling)
                fwd.start()
                passed.append(fwd)
        for t in range(n):
            copy(t, 0, sibling, me).wait_recv()
            for j, chip in enumerate(chips):
                copy(t, 4 + j, (*chip, 1 - me[2]), me).wait_recv()
        for cp in first + passed:
            cp.wait_send()
        for cp in mine:
            cp.wait()

    return _Job(stacked, outs, (7 * n, 7 * n, n), start, finish)


def _chip_job(arrs):
    n = len(arrs)
    outs = [jax.ShapeDtypeStruct(a.shape, a.dtype) for a in arrs]

    def parts(in_refs, out_refs, sems):
        send_sems, recv_sems, local_sems = sems
        x, y, c = _my_pos()
        chip = 2 * x + y
        mine, sends, recvs = [], [], []
        for t in range(n):
            mine.append(pltpu.make_async_copy(in_refs[t].at[pl.ds(chip, 1)], out_refs[t].at[pl.ds(chip, 1)],
                                              local_sems.at[t]))
            for m in range(1, N_CHIP):
                px, py = x ^ (m >> 1), y ^ (m & 1)
                peer = 2 * px + py
                sends.append(pltpu.make_async_remote_copy(
                    src_ref=in_refs[t].at[pl.ds(peer, 1)], dst_ref=out_refs[t].at[pl.ds(chip, 1)],
                    send_sem=send_sems.at[t * 3 + m - 1], recv_sem=recv_sems.at[t * 3 + m - 1],
                    device_id=(px, py, c), device_id_type=MESH))
                recvs.append(pltpu.make_async_remote_copy(
                    src_ref=in_refs[t].at[pl.ds(peer, 1)], dst_ref=out_refs[t].at[pl.ds(peer, 1)],
                    send_sem=send_sems.at[t * 3 + m - 1], recv_sem=recv_sems.at[t * 3 + m - 1],
                    device_id=(x, y, c), device_id_type=MESH))
        return mine, sends, recvs

    def start(in_refs, out_refs, sems):
        mine, sends, _ = parts(in_refs, out_refs, sems)
        for cp in mine + sends:
            cp.start()

    def finish(in_refs, out_refs, sems):
        mine, sends, recvs = parts(in_refs, out_refs, sems)
        for cp in recvs:
            cp.wait_recv()
        for cp in sends:
            cp.wait_send()
        for cp in mine:
            cp.wait()

    return _Job(arrs, outs, (3 * n, 3 * n, n), start, finish)


def _run_job(job, name):
    n = len(job.arrs)

    def body(*refs):
        ins, outs, sems = refs[:n], refs[n:n + len(job.outs)], refs[n + len(job.outs):]
        job.start(ins, outs, sems)
        job.finish(ins, outs, sems)

    return _comm_call(body, name, job.arrs, job.outs, job.n_sems)


def _carry(job, body, grid, in_specs, out_specs, out_shape, scratch_shapes, args):
    out_specs = list(out_specs) if isinstance(out_specs, (list, tuple)) else [out_specs]
    out_shape = list(out_shape) if isinstance(out_shape, (list, tuple)) else [out_shape]
    n_ci, n_co, n_cs = len(in_specs), len(out_specs), len(scratch_shapes)
    n_ji, n_jo = len(job.arrs), len(job.outs)
    any_spec = pl.BlockSpec(memory_space=pl.ANY)
    total = int(np.prod(grid))

    def wrapped(*refs):
        cin, jin = refs[:n_ci], refs[n_ci:n_ci + n_ji]
        o0 = n_ci + n_ji
        cout, jout = refs[o0:o0 + n_co], refs[o0 + n_co:o0 + n_co + n_jo]
        s0 = o0 + n_co + n_jo
        cscr, jsems = refs[s0:s0 + n_cs], refs[s0 + n_cs:]
        step = pl.program_id(0)
        for d in range(1, len(grid)):
            step = step * grid[d] + pl.program_id(d)

        @pl.when(step == 0)
        def _():
            job.start(jin, jout, jsems)

        body(*cin, *cout, *cscr)

        @pl.when(step == total - 1)
        def _():
            job.finish(jin, jout, jsems)

    return (wrapped, list(in_specs) + [any_spec] * n_ji, out_specs + [any_spec] * n_jo,
            out_shape + list(job.outs),
            list(scratch_shapes) + [pltpu.SemaphoreType.DMA((s,)) for s in job.n_sems],
            list(args) + list(job.arrs), n_co)


def _pair_add(g, r, name):
    _, R, C = g.shape
    cpad = -(-C // 128) * 128
    rt = _pick(R, max(16, (1 << 20) // (2 * cpad)), 16)
    cidx = lax.axis_index("c").astype(jnp.int32).reshape(1)

    def body(c_ref, g_ref, r_ref, o_ref):
        o_ref[...] = (g_ref[...].astype(F32) + r_ref[...].astype(F32)).astype(o_ref.dtype)

    return pl.pallas_call(
        body, name=name, out_shape=jax.ShapeDtypeStruct((N_CHIP, R, C), g.dtype),
        grid_spec=pltpu.PrefetchScalarGridSpec(
            num_scalar_prefetch=1, grid=(N_CHIP, R // rt),
            in_specs=[pl.BlockSpec((None, rt, C), lambda k, i, c_ref: (2 * k + c_ref[0], i, 0)),
                      pl.BlockSpec((None, rt, C), lambda k, i, c_ref: (k, i, 0))],
            out_specs=pl.BlockSpec((None, rt, C), lambda k, i, c_ref: (k, i, 0))),
        compiler_params=_cparams(("parallel", "parallel")),
    )(cidx, g, r)


def _pair_adds(chunks, sib, tag):
    return [_pair_add(g, r, "rs_pair_add" + tag) for g, r in zip(chunks, sib)]


def _mm(a, b, mode, out_dtype=F32, name="mm", bias=None, add=None, b_pre=(), b_shard=False,
        a_half=False, b_half=False, out_shard=False, job=None):
    npre = len(b_pre)
    bshape = b.shape[npre:]
    if mode == "nn":
        M, K = a.shape
        if b_shard:
            K2, N = bshape[1], N_DEV * bshape[2]
        else:
            K2, N = bshape
    elif mode == "nt":
        M, K = (a.shape[1], 2 * a.shape[2]) if a_half else a.shape
        if b_shard:
            N, K2 = bshape[1], N_DEV * bshape[2]
        else:
            N, K2 = bshape
    else:
        K, M = a.shape
        K2, N = (b.shape[1], 2 * b.shape[2]) if b_half else bshape
    assert K == K2, (a.shape, b.shape, mode)
    tm = _pick(M, 1100, 16) if mode != "tn" else _pick(M, 1024, 128)
    tn = _pick(N, 1024, 128)
    tk = _pick(K, 2816 if mode == "nt" else 2176, 128)
    if b_shard and mode == "nn":
        tn = bshape[2]
    sps = 1
    if b_shard and mode == "nt":
        ns = bshape[2]
        sps = 2 if ns % 128 == 0 and (not a_half or (a.shape[2] // ns) % 2 == 0) else 1
        tk = sps * ns
    if out_shard:
        tn = N // N_DEV
    nk = K // tk
    none_pre = (None,) * npre
    if mode == "nn":
        a_spec = pl.BlockSpec((tm, tk), lambda i, j, k: (i, k))
        if b_shard:
            b_spec = pl.BlockSpec(none_pre + (None, tk, tn), lambda i, j, k: b_pre + (j, k, 0))
        else:
            b_spec = pl.BlockSpec(none_pre + (tk, tn), lambda i, j, k: b_pre + (k, j))
        dot = _dot
    elif mode == "nt":
        if a_half:
            nkh = a.shape[2] // tk
            a_spec = pl.BlockSpec((None, tm, tk), lambda i, j, k: (k // nkh, i, k % nkh))
        else:
            a_spec = pl.BlockSpec((tm, tk), lambda i, j, k: (i, k))
        if b_shard:
            b_spec = pl.BlockSpec(none_pre + (sps, tn, tk // sps), lambda i, j, k: b_pre + (k, j, 0))
        else:
            b_spec = pl.BlockSpec(none_pre + (tn, tk), lambda i, j, k: b_pre + (j, k))
        dot = _dot_nt
        if b_shard:
            def dot(a_blk, b_blk):
                ns_ = tk // sps
                p = _dot_nt(a_blk[:, :ns_], b_blk[0])
                for s in range(1, sps):
                    p = p + _dot_nt(a_blk[:, s * ns_:(s + 1) * ns_], b_blk[s])
                return p
    else:
        a_spec = pl.BlockSpec((tk, tm), lambda i, j, k: (k, i))
        if b_half:
            nnh = b.shape[2] // tn
            b_spec = pl.BlockSpec((None, tk, tn), lambda i, j, k: (j // nnh, k, j % nnh))
        else:
            b_spec = pl.BlockSpec(none_pre + (tk, tn), lambda i, j, k: b_pre + (k, j))
        dot = _dot_tn
    in_specs = [a_spec, b_spec]
    args = [a, b]
    if bias is not None:
        in_specs.append(pl.BlockSpec((1, tn), lambda i, j, k: (0, j)))
        args.append(bias)
    if add is not None:
        in_specs.append(pl.BlockSpec((tm, tn), lambda i, j, k: (i, j)))
        args.append(add)
    n_in = len(args)
    if out_shard:
        o_spec = pl.BlockSpec((None, tm, tn), lambda i, j, k: (j, i, 0))
        o_shape = jax.ShapeDtypeStruct((N_DEV, M, tn), out_dtype)
    else:
        o_spec = pl.BlockSpec((tm, tn), lambda i, j, k: (i, j))
        o_shape = jax.ShapeDtypeStruct((M, N), out_dtype)

    def body(*refs):
        a_ref, b_ref = refs[0], refs[1]
        bias_ref = refs[2] if bias is not None else None
        add_ref = refs[n_in - 1] if add is not None else None
        o_ref = refs[n_in]
        p = dot(a_ref[...].astype(BF16), b_ref[...].astype(BF16))

        def finish(acc):
            if bias_ref is not None:
                acc = acc + bias_ref[...]
            if add_ref is not None:
                acc = acc + add_ref[...]
            o_ref[...] = acc.astype(o_ref.dtype)

        if nk == 1:
            finish(p)
        else:
            acc_ref = refs[-1]
            k = pl.program_id(2)

            @pl.when(k == 0)
            def _():
                acc_ref[...] = p

            @pl.when(k > 0)
            def _():
                acc_ref[...] += p

            @pl.when(k == nk - 1)
            def _():
                finish(acc_ref[...])

    grid = (M // tm, N // tn, nk)
    scratch = [pltpu.VMEM((tm, tn), F32)] if nk > 1 else []
    if job is None:
        return pl.pallas_call(
            body, name=name, grid=grid, in_specs=in_specs, out_specs=o_spec, out_shape=o_shape,
            scratch_shapes=scratch, compiler_params=_cparams(("parallel", "parallel", "arbitrary")),
        )(*args)
    return _call_carrying(job, body, name, grid, in_specs, o_spec, o_shape, scratch, args)


def _call_carrying(job, body, name, grid, in_specs, out_specs, out_shape, scratch, args):
    body, in_specs, out_specs, out_shape, scratch, args, n_co = _carry(
        job, body, grid, in_specs, out_specs, out_shape, scratch, args)
    res = pl.pallas_call(
        body, name=name, grid=grid, in_specs=in_specs, out_specs=out_specs, out_shape=out_shape,
        scratch_shapes=scratch, compiler_params=_cparams(("arbitrary",) * len(grid)),
    )(*args)
    own = res[0] if n_co == 1 else list(res[:n_co])
    return own, list(res[n_co:])


def _proj_layout(D):
    DK, DP, R2 = D // 2, D // 2, 2 * GATE_RANK
    return [("q", 0, DK, 4 * D), ("k", DK, DK, 4 * D + DK), ("v", 2 * DK, D, 0), ("g", 2 * DK + D, D, D),
            ("a", 2 * DK + 2 * D, R2, None), ("p", 2 * DK + 2 * D + R2, DP, 5 * D),
            ("bg", 2 * DK + 2 * D + R2 + DP, 2 * D, 2 * D)]


RELAYOUT_ROWS = 64


def _w_in_operands(g, name):
    _, D, n = g.shape
    segs = _proj_layout(D)
    tr = RELAYOUT_ROWS

    def body(g_ref, main_ref, alr_ref):
        shard = [g_ref[j].astype(F32) for j in range(N_DEV)]

        def columns(a, b):
            parts = []
            for j in range(a // n, (b - 1) // n + 1):
                parts.append(shard[j][:, max(a, j * n) - j * n:min(b, (j + 1) * n) - j * n])
            return parts[0] if len(parts) == 1 else jnp.concatenate(parts, axis=1)

        for _, start, width, dst in segs:
            cols = columns(start, start + width)
            if dst is None:
                cols = jnp.concatenate([cols, jnp.zeros((tr, ALR_PAD - width), F32)], axis=1)
                alr_ref[...] = cols.astype(BF16)
            else:
                main_ref[:, dst:dst + width] = cols.astype(BF16)

    return pl.pallas_call(
        body, name=name, grid=(D // tr,),
        in_specs=[pl.BlockSpec((N_DEV, tr, n), lambda i: (0, i, 0))],
        out_specs=[pl.BlockSpec((tr, 11 * D // 2), lambda i: (i, 0)), pl.BlockSpec((tr, ALR_PAD), lambda i: (i, 0))],
        out_shape=[jax.ShapeDtypeStruct((D, 11 * D // 2), BF16), jax.ShapeDtypeStruct((D, ALR_PAD), BF16)],
        compiler_params=_cparams(("parallel",)),
    )(g)


def _w_in_chunks(g_main, g_alr, n, name):
    D = g_main.shape[0]
    segs = _proj_layout(D)
    tr = RELAYOUT_ROWS

    def body(main_ref, alr_ref, o_ref):
        main = main_ref[...].astype(F32)
        alr = alr_ref[...].astype(F32)
        for j in range(N_DEV):
            a, b = j * n, (j + 1) * n
            parts = []
            for _, start, width, dst in segs:
                lo, hi = max(a, start), min(b, start + width)
                if lo >= hi:
                    continue
                src = alr if dst is None else main
                off = 0 if dst is None else dst
                parts.append(src[:, off + lo - start:off + hi - start])
            o_ref[j] = (parts[0] if len(parts) == 1 else jnp.concatenate(parts, axis=1)).astype(BF16)

    return pl.pallas_call(
        body, name=name, grid=(D // tr,),
        in_specs=[pl.BlockSpec((tr, 11 * D // 2), lambda i: (i, 0)), pl.BlockSpec((tr, ALR_PAD), lambda i: (i, 0))],
        out_specs=pl.BlockSpec((N_DEV, tr, n), lambda i: (0, i, 0)),
        out_shape=jax.ShapeDtypeStruct((N_DEV, D, n), BF16),
        compiler_params=_cparams(("parallel",)),
    )(g_main, g_alr)


def _ffn_in_fwd(h2, w_fi, l, name, job=None):
    T, D = h2.shape
    n = w_fi.shape[3]
    nh = N_DEV // 2
    dff = nh * n
    tm = _pick(T, 600, 16)

    def body(a_ref, bg_ref, bu_ref, ff_ref, s_ref):
        a = a_ref[...]
        g = _dot(a, bg_ref[...])
        u = _dot(a, bu_ref[...])
        ff_ref[0] = g
        ff_ref[1] = u
        s_ref[...] = _swiglu_f(g, u).astype(BF16)

    grid = (T // tm, nh)
    in_specs = [pl.BlockSpec((tm, D), lambda i, j: (i, 0)),
                pl.BlockSpec((None, None, D, n), lambda i, j: (l, j, 0, 0)),
                pl.BlockSpec((None, None, D, n), lambda i, j: (l, nh + j, 0, 0))]
    out_specs = [pl.BlockSpec((2, tm, n), lambda i, j: (0, i, j)), pl.BlockSpec((tm, n), lambda i, j: (i, j))]
    out_shape = [jax.ShapeDtypeStruct((2, T, dff), F32), jax.ShapeDtypeStruct((T, dff), BF16)]
    args = (h2, w_fi, w_fi)
    if job is None:
        return pl.pallas_call(
            body, name=name, grid=grid, in_specs=in_specs, out_specs=out_specs, out_shape=out_shape,
            compiler_params=_cparams(("parallel", "parallel")),
        )(*args)
    return _call_carrying(job, body, name, grid, in_specs, out_specs, out_shape, [], args)


def _ffn_out_dx(dffn, w_fo, l, ff, name, job=None):
    T, D = dffn.shape
    dff = ff.shape[2]
    tm = _pick(T, 600, 16)
    tw = _pick(dff, 1408, 128)

    def body(a_ref, b_ref, ff_ref, o_ref):
        ds = _dot_nt(a_ref[...], b_ref[...])
        _, vjp = jax.vjp(_swiglu_f, ff_ref[0], ff_ref[1])
        dg, du = vjp(ds)
        o_ref[0] = dg.astype(BF16)
        o_ref[1] = du.astype(BF16)

    grid = (T // tm, dff // tw)
    in_specs = [pl.BlockSpec((tm, D), lambda i, j: (i, 0)),
                pl.BlockSpec((None, tw, D), lambda i, j: (l, j, 0)),
                pl.BlockSpec((2, tm, tw), lambda i, j: (0, i, j))]
    out_specs = pl.BlockSpec((2, tm, tw), lambda i, j: (0, i, j))
    out_shape = jax.ShapeDtypeStruct((2, T, dff), BF16)
    args = (dffn, w_fo, ff)
    if job is None:
        return pl.pallas_call(
            body, name=name, grid=grid, in_specs=in_specs, out_specs=out_specs, out_shape=out_shape,
            compiler_params=_cparams(("parallel", "parallel")),
        )(*args)
    return _call_carrying(job, body, name, grid, in_specs, out_specs, out_shape, [], args)


def _row(rb, w, col=0):
    return pl.BlockSpec((rb, w), lambda i: (i, col))


def _modspec(d, sec):
    return pl.BlockSpec((None, 1, d), lambda i: (jnp.minimum(i, 1), 0, sec))


def _vec(w):
    return pl.BlockSpec((1, w), lambda i: (0, 0))


def _acc2(w):
    return pl.BlockSpec((None, 1, w), lambda i: (jnp.minimum(i, 1), 0, 0))


def _accum(ref, val):
    i = pl.program_id(0)

    @pl.when(i <= 1)
    def _():
        ref[...] = val

    @pl.when(i > 1)
    def _():
        ref[...] += val


def _acc_shape(w):
    return jax.ShapeDtypeStruct((2, 1, w), F32)


def _mod_f(x, sc, sh):
    return x * (1.0 + sc) + sh


def _mod_fwd(xa, mod, sec_sc, sec_sh, rb, name):
    T, D = xa.shape

    def body(x_ref, sc_ref, sh_ref, h_ref):
        h_ref[...] = _mod_f(x_ref[...], sc_ref[...], sh_ref[...]).astype(BF16)

    return pl.pallas_call(
        body, name=name, grid=(T // rb,),
        in_specs=[_row(rb, D), _modspec(D, sec_sc), _modspec(D, sec_sh)],
        out_specs=_row(rb, D), out_shape=jax.ShapeDtypeStruct((T, D), BF16),
        compiler_params=_cparams(("parallel",)),
    )(xa, mod, mod)


def _mod_bwd(dxa, dh, xa, mod, sec_sc, sec_sh, rb, name):
    T, D = xa.shape

    def body(dxa_ref, dh_ref, x_ref, sc_ref, sh_ref, dx_ref, dsc_ref, dsh_ref):
        _, vjp = jax.vjp(_mod_f, x_ref[...], sc_ref[...], sh_ref[...])
        dx, dsc, dsh = vjp(dh_ref[...])
        dx_ref[...] = dxa_ref[...] + dx
        _accum(dsc_ref, dsc)
        _accum(dsh_ref, dsh)

    return pl.pallas_call(
        body, name=name, grid=(T // rb,),
        in_specs=[_row(rb, D), _row(rb, D), _row(rb, D), _modspec(D, sec_sc), _modspec(D, sec_sh)],
        out_specs=[_row(rb, D), _acc2(D), _acc2(D)],
        out_shape=[jax.ShapeDtypeStruct((T, D), F32), _acc_shape(D), _acc_shape(D)],
        compiler_params=_cparams(("arbitrary",)),
    )(dxa, dh, xa, mod, mod)


def _ln_f(alpha, x, mix, gt, gain, bias):
    z = alpha * x + gt * mix
    mu = jnp.mean(z, axis=-1, keepdims=True)
    zc = z - mu
    var = jnp.mean(zc * zc, axis=-1, keepdims=True)
    return zc * lax.rsqrt(var + LN_EPS) * gain + bias


def _unit_fwd(alpha, x, mix, mod, sec_gt, gain, bias, next_mod, rb, name):
    T, D = x.shape
    has_mod = next_mod is not None

    def body(*refs):
        if has_mod:
            x_ref, mix_ref, gt_ref, g_ref, b_ref, sc_ref, sh_ref, xo_ref, h_ref = refs
        else:
            x_ref, mix_ref, gt_ref, g_ref, b_ref, xo_ref = refs
        xo = _ln_f(alpha, x_ref[...], mix_ref[...], gt_ref[...], g_ref[...], b_ref[...])
        xo_ref[...] = xo
        if has_mod:
            h_ref[...] = _mod_f(xo, sc_ref[...], sh_ref[...]).astype(BF16)

    in_specs = [_row(rb, D), _row(rb, D), _modspec(D, sec_gt), _vec(D), _vec(D)]
    args = [x, mix, mod, gain, bias]
    out_specs = [_row(rb, D)]
    out_shape = [jax.ShapeDtypeStruct((T, D), F32)]
    if has_mod:
        nm, s_sc, s_sh = next_mod
        in_specs += [_modspec(D, s_sc), _modspec(D, s_sh)]
        args += [nm, nm]
        out_specs.append(_row(rb, D))
        out_shape.append(jax.ShapeDtypeStruct((T, D), BF16))
    res = pl.pallas_call(
        body, name=name, grid=(T // rb,), in_specs=in_specs, out_specs=out_specs, out_shape=out_shape,
        compiler_params=_cparams(("parallel",)),
    )(*args)
    return (res[0], res[1]) if has_mod else (res[0], None)


def _unit_bwd(alpha, dxo, dh, x, mix, mod, sec_gt, gain, bias, next_mod, rb, name, job=None):
    T, D = x.shape
    has_mod = next_mod is not None

    def body(*refs):
        if has_mod:
            (dxo_ref, dh_ref, x_ref, mix_ref, gt_ref, g_ref, b_ref, sc_ref, sh_ref,
             dx_ref, dmix_ref, dgt_ref, dg_ref, db_ref, dsc_ref, dsh_ref) = refs
        else:
            (dxo_ref, x_ref, mix_ref, gt_ref, g_ref, b_ref,
             dx_ref, dmix_ref, dgt_ref, dg_ref, db_ref) = refs
        xo, vjp = jax.vjp(functools.partial(_ln_f, alpha), x_ref[...], mix_ref[...], gt_ref[...],
                          g_ref[...], b_ref[...])
        dxo_t = dxo_ref[...]
        if has_mod:
            _, vjp_m = jax.vjp(_mod_f, xo, sc_ref[...], sh_ref[...])
            dxo_m, dsc, dsh = vjp_m(dh_ref[...])
            dxo_t = dxo_t + dxo_m
            _accum(dsc_ref, dsc)
            _accum(dsh_ref, dsh)
        dx, dmix, dgt, dg, db = vjp(dxo_t)
        dx_ref[...] = dx
        dmix_ref[...] = dmix.astype(BF16)
        _accum(dgt_ref, dgt)
        _accum(dg_ref, dg)
        _accum(db_ref, db)

    in_specs = [_row(rb, D)]
    args = [dxo]
    if has_mod:
        in_specs.append(_row(rb, D))
        args.append(dh)
    in_specs += [_row(rb, D), _row(rb, D), _modspec(D, sec_gt), _vec(D), _vec(D)]
    args += [x, mix, mod, gain, bias]
    out_specs = [_row(rb, D), _row(rb, D), _acc2(D), _acc2(D), _acc2(D)]
    out_shape = [jax.ShapeDtypeStruct((T, D), F32), jax.ShapeDtypeStruct((T, D), BF16),
                 _acc_shape(D), _acc_shape(D), _acc_shape(D)]
    if has_mod:
        nm, s_sc, s_sh = next_mod
        in_specs += [_modspec(D, s_sc), _modspec(D, s_sh)]
        args += [nm, nm]
        out_specs += [_acc2(D), _acc2(D)]
        out_shape += [_acc_shape(D), _acc_shape(D)]
    if job is None:
        res = pl.pallas_call(
            body, name=name, grid=(T // rb,), in_specs=in_specs, out_specs=out_specs, out_shape=out_shape,
            compiler_params=_cparams(("arbitrary",)),
        )(*args)
        job_res = None
    else:
        res, job_res = _call_carrying(job, body, name, (T // rb,), in_specs, out_specs, out_shape, [], args)
    res = list(res) if has_mod else list(res) + [None, None]
    return res if job is None else (res, job_res)


def _log_sigmoid(z):
    return jnp.minimum(z, 0.0) - jnp.log(1.0 + jnp.exp(-jnp.abs(z)))


def _decay_fwd(alr, wdu, bdu, rb, name):
    T = alr.shape[0]
    W = wdu.shape[1]

    def body(a_ref, w_ref, b_ref, la_ref):
        z = _dot(a_ref[...].astype(BF16), w_ref[...]) + b_ref[...]
        la_ref[...] = _log_sigmoid(z) * (1.0 / GATE_NORM)

    return pl.pallas_call(
        body, name=name, grid=(T // rb,),
        in_specs=[_row(rb, ALR_PAD), pl.BlockSpec((ALR_PAD, W), lambda i: (0, 0)), _vec(W)],
        out_specs=_row(rb, W), out_shape=jax.ShapeDtypeStruct((T, W), F32),
        compiler_params=_cparams(("parallel",)),
    )(alr, wdu, bdu)


def _decay_bwd(dla_f, dla_b, alr, wdu, bdu, rb, name):
    T = alr.shape[0]
    W = wdu.shape[1]
    DK = W // 2

    def body(df_ref, db_ref, a_ref, w_ref, b_ref, dalr_ref, gw_ref, gb_ref):
        i = pl.program_id(0)
        ab = a_ref[...].astype(BF16)
        z = _dot(ab, w_ref[...]) + b_ref[...]
        dla = jnp.concatenate([df_ref[...], db_ref[...]], axis=1)
        dz = dla * _sig(-z) * (1.0 / GATE_NORM)
        dzb = dz.astype(BF16)
        dalr_ref[...] = _dot_nt(dzb, w_ref[...]).astype(BF16)
        gw = _dot_tn(ab, dzb)
        gb = jnp.sum(dz, axis=0, keepdims=True)

        @pl.when(i == 0)
        def _():
            gw_ref[...] = gw
            gb_ref[...] = gb

        @pl.when(i > 0)
        def _():
            gw_ref[...] += gw
            gb_ref[...] += gb

    return pl.pallas_call(
        body, name=name, grid=(T // rb,),
        in_specs=[_row(rb, DK), _row(rb, DK), _row(rb, ALR_PAD), pl.BlockSpec((ALR_PAD, W), lambda i: (0, 0)), _vec(W)],
        out_specs=[_row(rb, ALR_PAD), pl.BlockSpec((ALR_PAD, W), lambda i: (0, 0)), _vec(W)],
        out_shape=[jax.ShapeDtypeStruct((T, ALR_PAD), BF16), jax.ShapeDtypeStruct((ALR_PAD, W), F32),
                   jax.ShapeDtypeStruct((1, W), F32)],
        compiler_params=_cparams(("arbitrary",)),
    )(dla_f, dla_b, alr, wdu, bdu)


def _tri(rev, ncb):
    m = np.tril(np.ones((CHUNK, CHUNK), np.float32))
    return jnp.asarray(np.kron(np.eye(ncb, dtype=np.float32), m.T if rev else m), BF16)


def _gla_block_common(q_ref, k_ref, v_ref, la_ref, tri_ref, ck, cv, rev, scale_q, ncb):
    mid = CHUNK // 2 if rev else CHUNK // 2 - 1
    last_i = 0 if rev else CHUNK - 1
    rb = ncb * CHUNK
    hk = ck.stop - ck.start
    q = q_ref[:, ck] * scale_q
    k = k_ref[:, ck]
    v = v_ref[:, cv]
    cum = _dot3(tri_ref[...], la_ref[:, ck])
    per_chunk = lambda i: jnp.concatenate(
        [jnp.broadcast_to(cum[c * CHUNK + i:c * CHUNK + i + 1, :], (CHUNK, hk)) for c in range(ncb)], axis=0)
    ref, last = per_chunk(mid), per_chunk(last_i)
    e_q = jnp.exp(cum - ref)
    e_k = jnp.exp(ref - cum)
    e_c = jnp.exp(cum)
    e_s = jnp.exp(last - cum)
    e_l = [jnp.exp(cum[c * CHUNK + last_i:c * CHUNK + last_i + 1, :]) for c in range(ncb)]
    ri = lax.broadcasted_iota(jnp.int32, (rb, rb), 0)
    ci = lax.broadcasted_iota(jnp.int32, (rb, rb), 1)
    mask = (ri // CHUNK == ci // CHUNK) & ((ci >= ri) if rev else (ci <= ri))
    return q, k, v, e_q, e_k, e_c, e_s, e_l, mask, last_i


GLA_HEADS_PER_STEP = 1


def _gla_specs(rb, hk, hv, D, rbmap, rev):
    hp = GLA_HEADS_PER_STEP
    q_col0 = 4 * D // (hp * hk)
    k_col0 = q_col0 + N_HEADS // hp
    la_col0 = N_HEADS // hp if rev else 0
    return [
        pl.BlockSpec((rb, hp * hk), lambda h, i: (rbmap(i), q_col0 + h)),
        pl.BlockSpec((rb, hp * hk), lambda h, i: (rbmap(i), k_col0 + h)),
        pl.BlockSpec((rb, hp * hv), lambda h, i: (rbmap(i), h)),
        pl.BlockSpec((rb, hp * hk), lambda h, i: (rbmap(i), la_col0 + h)),
        pl.BlockSpec((rb, rb), lambda h, i: (0, 0)),
    ]


def _gla_call(job, body, name, grid, in_specs, out_specs, out_shape, scratch, args):
    if job is None:
        return pl.pallas_call(
            body, name=name, grid=grid, in_specs=in_specs, out_specs=out_specs, out_shape=out_shape,
            scratch_shapes=scratch, compiler_params=_cparams(("parallel", "arbitrary")),
        )(*args)
    return _call_carrying(job, body, name, grid, in_specs, out_specs, out_shape, scratch, args)


def _gla_fwd(proj, la, rev, rb, D, name, job=None):
    T = proj.shape[0]
    nb = T // rb
    ncb = rb // CHUNK
    hp = GLA_HEADS_PER_STEP
    hk, hv = D // 2 // N_HEADS, D // N_HEADS
    scale_q = float(hk) ** -0.5
    rbmap = (lambda i: jnp.where(i == 0, 0, nb - i)) if rev else (lambda i: i)

    def body(q_ref, k_ref, v_ref, la_ref, tri_ref, o_ref, s_ref, st_ref):
        @pl.when(pl.program_id(1) == 0)
        def _():
            st_ref[...] = jnp.zeros_like(st_ref)

        order = range(ncb - 1, -1, -1) if rev else range(ncb)
        for hh in range(hp):
            ck, cv = slice(hh * hk, (hh + 1) * hk), slice(hh * hv, (hh + 1) * hv)
            q, k, v, e_q, e_k, e_c, e_s, e_l, mask, _ = _gla_block_common(
                q_ref, k_ref, v_ref, la_ref, tri_ref, ck, cv, rev, scale_q, ncb)
            vb = v.astype(BF16)
            a = jnp.where(mask, _dot_nt((q * e_q).astype(BF16), (k * e_k).astype(BF16)), 0.0)
            o_intra = _dot(a.astype(BF16), vb)
            qc = (q * e_c).astype(BF16)
            ks = (k * e_s).astype(BF16)
            st = st_ref[hh]
            for cc in order:
                rows = slice(cc * CHUNK, (cc + 1) * CHUNK)
                s_ref[hh, cc] = st
                o_ref[rows, cv] = o_intra[rows] + _dot_nt(qc[rows], st.astype(BF16))
                st = st * e_l[cc] + _dot_tn(vb[rows], ks[rows])
            st_ref[hh] = st

    return _gla_call(
        job, body, name, (N_HEADS // hp, nb), _gla_specs(rb, hk, hv, D, rbmap, rev),
        [pl.BlockSpec((rb, hp * hv), lambda h, i: (rbmap(i), h)),
         pl.BlockSpec((hp, ncb, hv, hk), lambda h, i: (h, rbmap(i), 0, 0))],
        [jax.ShapeDtypeStruct((T, D), F32), jax.ShapeDtypeStruct((N_HEADS, T // CHUNK, hv, hk), F32)],
        [pltpu.VMEM((hp, hv, hk), F32)], (proj, proj, proj, la, _tri(rev, ncb)))


def _gla_bwd(proj, la, do, states, rev, rb, D, prev, name, job=None):
    T = proj.shape[0]
    nb = T // rb
    ncb = rb // CHUNK
    hp = GLA_HEADS_PER_STEP
    hk, hv = D // 2 // N_HEADS, D // N_HEADS
    DK = D // 2
    scale_q = float(hk) ** -0.5
    if rev:
        rbmap = lambda i: jnp.where(i == nb - 1, 0, i + 1)
    else:
        rbmap = lambda i: nb - 1 - i
    has_prev = prev is not None
    out_dt = BF16 if has_prev else F32

    def body(*refs):
        q_ref, k_ref, v_ref, la_ref, tri_ref, trit_ref, do_ref, s_ref = refs[:8]
        n_in = 11 if has_prev else 8
        pq_ref, pk_ref, pv_ref = refs[8:11] if has_prev else (None, None, None)
        dq_ref, dk_ref, dv_ref, dla_ref, ds_ref = refs[n_in:]

        @pl.when(pl.program_id(1) == 0)
        def _():
            ds_ref[...] = jnp.zeros_like(ds_ref)

        order = range(ncb) if rev else range(ncb - 1, -1, -1)
        for hh in range(hp):
            ck, cv = slice(hh * hk, (hh + 1) * hk), slice(hh * hv, (hh + 1) * hv)
            q, k, v, e_q, e_k, e_c, e_s, e_l, mask, last_i = _gla_block_common(
                q_ref, k_ref, v_ref, la_ref, tri_ref, ck, cv, rev, scale_q, ncb)
            vb = v.astype(BF16)
            qi = (q * e_q).astype(BF16)
            ki = (k * e_k).astype(BF16)
            qc = (q * e_c).astype(BF16)
            ks = (k * e_s).astype(BF16)
            a = jnp.where(mask, _dot_nt(qi, ki), 0.0).astype(BF16)
            dob = do_ref[:, cv].astype(BF16)
            da = jnp.where(mask, _dot_nt(dob, vb), 0.0).astype(BF16)
            dv_intra = _dot_tn(a, dob)
            dq_intra = _dot(da, ki) * e_q
            dk_intra = _dot_tn(da, qi) * e_k
            rowi = lax.broadcasted_iota(jnp.int32, (CHUNK, hk), 0)
            dst = ds_ref[hh]
            for cc in order:
                rows = slice(cc * CHUNK, (cc + 1) * CHUNK)
                st0 = s_ref[hh, cc]
                dstb = dst.astype(BF16)
                dv = dv_intra[rows] + _dot_nt(ks[rows], dstb)
                dk_inter = _dot(vb[rows], dstb) * e_s[rows]
                dq_s = dq_intra[rows] + _dot(dob[rows], st0.astype(BF16)) * e_c[rows]
                dk = dk_intra[rows] + dk_inter
                extra = (jnp.sum(k[rows] * dk_inter, axis=0, keepdims=True)
                         + e_l[cc] * jnp.sum(dst * st0, axis=0, keepdims=True))
                dla_ref[rows, ck] = q[rows] * dq_s - k[rows] * dk + jnp.where(rowi == last_i, extra, 0.0)
                dq = dq_s * scale_q
                if has_prev:
                    dq = dq + pq_ref[rows, ck]
                    dk = dk + pk_ref[rows, ck]
                    dv = dv + pv_ref[rows, cv]
                dq_ref[rows, ck] = dq.astype(out_dt)
                dk_ref[rows, ck] = dk.astype(out_dt)
                dv_ref[rows, cv] = dv.astype(out_dt)
                dst = dst * e_l[cc] + _dot_tn(dob[rows], qc[rows])
            ds_ref[hh] = dst
            dla_ref[:, ck] = _dot3(trit_ref[...], dla_ref[:, ck])

    in_specs = _gla_specs(rb, hk, hv, D, rbmap, rev)
    in_specs += [pl.BlockSpec((rb, rb), lambda h, i: (0, 0)),
                 pl.BlockSpec((rb, hp * hv), lambda h, i: (rbmap(i), h)),
                 pl.BlockSpec((hp, ncb, hv, hk), lambda h, i: (h, rbmap(i), 0, 0))]
    args = [proj, proj, proj, la, _tri(rev, ncb), _tri(not rev, ncb), do, states]
    hk_spec = pl.BlockSpec((rb, hp * hk), lambda h, i: (rbmap(i), h))
    hv_spec = pl.BlockSpec((rb, hp * hv), lambda h, i: (rbmap(i), h))
    if has_prev:
        in_specs += [hk_spec, hk_spec, hv_spec]
        args += list(prev)
    return _gla_call(
        job, body, name, (N_HEADS // hp, nb), in_specs, [hk_spec, hk_spec, hv_spec, hk_spec],
        [jax.ShapeDtypeStruct((T, DK), out_dt), jax.ShapeDtypeStruct((T, DK), out_dt),
         jax.ShapeDtypeStruct((T, D), out_dt), jax.ShapeDtypeStruct((T, DK), F32)],
        [pltpu.VMEM((hp, hv, hk), F32)], args)


def _glaout_f(of, ob, g, gain):
    o = of + ob
    n = o * lax.rsqrt(jnp.mean(o * o, axis=-1, keepdims=True) + RMS_EPS)
    return n * gain * _silu(g)


def _glaout_fwd(o_f, o_b, proj, gain, rb, name):
    T, D = o_f.shape
    hv = D // N_HEADS

    def body(of_ref, ob_ref, g_ref, gn_ref, u_ref):
        for h in range(N_HEADS):
            cs = slice(h * hv, (h + 1) * hv)
            u_ref[:, cs] = _glaout_f(of_ref[:, cs], ob_ref[:, cs], g_ref[:, cs], gn_ref[:, cs]).astype(BF16)

    return pl.pallas_call(
        body, name=name, grid=(T // rb,),
        in_specs=[_row(rb, D), _row(rb, D), _row(rb, D, 1), _vec(D)],
        out_specs=_row(rb, D), out_shape=jax.ShapeDtypeStruct((T, D), BF16),
        compiler_params=_cparams(("parallel",)),
    )(o_f, o_b, proj, gain)


def _glaout_bwd(du, o_f, o_b, proj, gain, rb, name):
    T, D = o_f.shape
    hv = D // N_HEADS

    def body(du_ref, of_ref, ob_ref, g_ref, gn_ref, do_ref, dg_ref, dgn_ref, tmp_ref):
        for h in range(N_HEADS):
            cs = slice(h * hv, (h + 1) * hv)
            _, vjp = jax.vjp(_glaout_f, of_ref[:, cs], ob_ref[:, cs], g_ref[:, cs], gn_ref[:, cs])
            d_of, _, dg, dgn = vjp(du_ref[:, cs])
            do_ref[:, cs] = d_of
            dg_ref[:, cs] = dg.astype(BF16)
            tmp_ref[:, cs] = dgn
        _accum(dgn_ref, tmp_ref[...])

    return pl.pallas_call(
        body, name=name, grid=(T // rb,),
        in_specs=[_row(rb, D), _row(rb, D), _row(rb, D), _row(rb, D, 1), _vec(D)],
        out_specs=[_row(rb, D), _row(rb, D), _acc2(D)],
        out_shape=[jax.ShapeDtypeStruct((T, D), F32), jax.ShapeDtypeStruct((T, D), BF16), _acc_shape(D)],
        scratch_shapes=[pltpu.VMEM((1, D), F32)],
        compiler_params=_cparams(("arbitrary",)),
    )(du, o_f, o_b, proj, gain)


def _merge_f(bg1, bg2, yg, yp):
    return _sig(bg1) * yg + _sig(bg2) * yp


def _merge_fwd(proj, y_gla, y_pool, rb, name):
    T, D = y_gla.shape

    def body(b1_ref, b2_ref, yg_ref, yp_ref, m_ref):
        m_ref[...] = _merge_f(b1_ref[...], b2_ref[...], yg_ref[...], yp_ref[...]).astype(BF16)

    return pl.pallas_call(
        body, name=name, grid=(T // rb,),
        in_specs=[_row(rb, D, 2), _row(rb, D, 3), _row(rb, D), _row(rb, D)],
        out_specs=_row(rb, D), out_shape=jax.ShapeDtypeStruct((T, D), BF16),
        compiler_params=_cparams(("parallel",)),
    )(proj, proj, y_gla, y_pool)


def _merge_bwd(dm, proj, y_gla, y_pool, rb, name):
    T, D = y_gla.shape

    def body(dm_ref, b1_ref, b2_ref, yg_ref, yp_ref, d1_ref, d2_ref, dyg_ref, dyp_ref):
        _, vjp = jax.vjp(_merge_f, b1_ref[...], b2_ref[...], yg_ref[...], yp_ref[...])
        d1, d2, dyg, dyp = vjp(dm_ref[...])
        d1_ref[...] = d1.astype(BF16)
        d2_ref[...] = d2.astype(BF16)
        dyg_ref[...] = dyg.astype(BF16)
        dyp_ref[...] = dyp.astype(BF16)

    return pl.pallas_call(
        body, name=name, grid=(T // rb,),
        in_specs=[_row(rb, D), _row(rb, D, 2), _row(rb, D, 3), _row(rb, D), _row(rb, D)],
        out_specs=[_row(rb, D)] * 4, out_shape=[jax.ShapeDtypeStruct((T, D), BF16)] * 4,
        compiler_params=_cparams(("parallel",)),
    )(dm, proj, proj, y_gla, y_pool)


def _swiglu_f(gate, up):
    return _silu(gate) * up


def _pool_consts(ctx_len, seq):
    rows = seq // GRID_W
    reps = POOL_TB // GRID_W
    mw, bc, cw, ch, cc = [], [], [], [], []
    for w in POOL_WINDOWS:
        lo, hi = w // 2, w - w // 2 - 1

        def band(n):
            i = np.arange(n)[:, None]
            j = np.arange(n)[None, :]
            return ((j - i >= -lo) & (j - i <= hi)).astype(np.float32)

        def count(n):
            i = np.arange(n)
            return (np.minimum(i + hi + 1, n) - np.maximum(i - lo, 0)).astype(np.float32)

        mw.append(np.kron(np.eye(reps, dtype=np.float32), band(GRID_W)))
        bc.append(band(ctx_len))
        cw.append(np.tile(count(GRID_W), reps)[:, None])
        ch.append(np.repeat(count(rows), GRID_W)[:, None])
        cc.append(count(ctx_len)[:, None])
    mw, bc = np.stack(mw), np.stack(bc)
    return dict(
        mw=jnp.asarray(mw, BF16), mwt=jnp.asarray(mw.transpose(0, 2, 1), BF16),
        bc=jnp.asarray(bc, BF16), bct=jnp.asarray(bc.transpose(0, 2, 1), BF16),
        cw=jnp.asarray(np.stack(cw)), ch=jnp.asarray(np.stack(ch)), cc=jnp.asarray(np.stack(cc)))


def _gspec(*shape):
    nd = len(shape)
    return pl.BlockSpec((None,) + tuple(shape), lambda g: (g,) + (0,) * nd)


def _pool_fwd(proj, pc, wg, scale, ctx_len, D, name):
    T = proj.shape[0]
    seq = T - ctx_len
    dp = D // 2
    pg = dp // len(POOL_WINDOWS)
    nblk = seq // POOL_TB
    padt = POOL_PAD_ROWS * GRID_W
    p_col0 = 5 * D // pg

    def body(p_ref, mw_ref, bc_ref, cw_ref, ch_ref, cc_ref, wg_ref, sc_ref, pd_ref, y0_ref, r_ref, pad_ref):
        g = pl.program_id(0)

        def tail(rows, mean, x):
            pdb = (mean - x).astype(BF16)
            y0 = _dot(pdb, wg_ref[...])
            pd_ref[rows, :] = pdb
            y0_ref[rows, :] = y0
            r_ref[rows, :] = (y0 * sc_ref[...]).astype(BF16)

        xc = p_ref[0:ctx_len, :]
        tail(slice(0, ctx_len), _dot2(bc_ref[...], xc) / cc_ref[...], xc)

        pad_ref[0:padt, :] = jnp.zeros((padt, pg), F32)
        pad_ref[padt + seq:, :] = jnp.zeros((padt, pg), F32)

        def wpass(b, carry):
            rows = pl.ds(pl.multiple_of(ctx_len + b * POOL_TB, CHUNK), POOL_TB)
            dst = pl.ds(pl.multiple_of(padt + b * POOL_TB, CHUNK), POOL_TB)
            pad_ref[dst, :] = _dot2(mw_ref[...], p_ref[rows, :]) / cw_ref[...]
            return carry

        lax.fori_loop(0, nblk, wpass, 0)

        for gi, w in enumerate(POOL_WINDOWS):
            lo, hi = w // 2, w - w // 2 - 1

            @pl.when(g == gi)
            def _():
                def hpass(b, carry):
                    acc = jnp.zeros((POOL_TB, pg), F32)
                    for d in range(-lo, hi + 1):
                        src = pl.ds(pl.multiple_of(padt + b * POOL_TB + d * GRID_W, CHUNK), POOL_TB)
                        acc = acc + pad_ref[src, :]
                    mean = acc / ch_ref[pl.ds(pl.multiple_of(b * POOL_TB, CHUNK), POOL_TB), :]
                    rows = pl.ds(pl.multiple_of(ctx_len + b * POOL_TB, CHUNK), POOL_TB)
                    tail(rows, mean, p_ref[rows, :])
                    return carry

                lax.fori_loop(0, nblk, hpass, 0)

    col = lambda g: (0, g)
    return pl.pallas_call(
        body, name=name, grid=(len(POOL_WINDOWS),),
        in_specs=[pl.BlockSpec((T, pg), lambda g: (0, p_col0 + g)),
                  _gspec(POOL_TB, POOL_TB), _gspec(ctx_len, ctx_len), _gspec(POOL_TB, 1), _gspec(seq, 1),
                  _gspec(ctx_len, 1), _gspec(pg, pg), pl.BlockSpec((1, pg), col)],
        out_specs=[pl.BlockSpec((T, pg), col)] * 3,
        out_shape=[jax.ShapeDtypeStruct((T, dp), BF16), jax.ShapeDtypeStruct((T, dp), F32),
                   jax.ShapeDtypeStruct((T, dp), BF16)],
        scratch_shapes=[pltpu.VMEM((seq + 2 * padt, pg), F32)],
        compiler_params=_cparams(("arbitrary",)),
    )(proj, pc["mw"], pc["bc"], pc["cw"], pc["ch"], pc["cc"], wg, scale)


def _pool_bwd(dr, y0, pd, pc, wg, scale, ctx_len, D, name):
    T = dr.shape[0]
    seq = T - ctx_len
    dp = D // 2
    ng = len(POOL_WINDOWS)
    pg = dp // ng
    nblk = seq // POOL_TB
    padt = POOL_PAD_ROWS * GRID_W

    def body(dr_ref, y0_ref, pd_ref, mwt_ref, bct_ref, cw_ref, ch_ref, cc_ref, wg_ref, sc_ref,
             dp_ref, dsc_ref, gwg_ref, pad_ref, dpd_ref):
        g = pl.program_id(0)
        dsc_ref[...] = jnp.zeros_like(dsc_ref)
        gwg_ref[...] = jnp.zeros_like(gwg_ref)

        def head(rows):
            drv = dr_ref[rows, :]
            dsc_ref[...] += jnp.sum(drv * y0_ref[rows, :], axis=0, keepdims=True)
            dy0 = (drv * sc_ref[...]).astype(BF16)
            gwg_ref[...] += _dot_tn(pd_ref[rows, :], dy0)
            return _dot_nt(dy0, wg_ref[...])

        crow = slice(0, ctx_len)
        dpd_c = head(crow)
        dp_ref[crow, :] = (_dot2(bct_ref[...], dpd_c / cc_ref[...]) - dpd_c).astype(BF16)

        pad_ref[0:padt, :] = jnp.zeros((padt, pg), F32)
        pad_ref[padt + seq:, :] = jnp.zeros((padt, pg), F32)

        def first(b, carry):
            rows = pl.ds(pl.multiple_of(ctx_len + b * POOL_TB, CHUNK), POOL_TB)
            lrows = pl.ds(pl.multiple_of(b * POOL_TB, CHUNK), POOL_TB)
            dst = pl.ds(pl.multiple_of(padt + b * POOL_TB, CHUNK), POOL_TB)
            dpd = head(rows)
            dpd_ref[lrows, :] = dpd
            pad_ref[dst, :] = dpd / ch_ref[lrows, :]
            return carry

        lax.fori_loop(0, nblk, first, 0)

        for gi, w in enumerate(POOL_WINDOWS):
            lo, hi = w // 2, w - w // 2 - 1

            @pl.when(g == gi)
            def _():
                def second(b, carry):
                    acc = jnp.zeros((POOL_TB, pg), F32)
                    for d in range(-hi, lo + 1):
                        src = pl.ds(pl.multiple_of(padt + b * POOL_TB + d * GRID_W, CHUNK), POOL_TB)
                        acc = acc + pad_ref[src, :]
                    rows = pl.ds(pl.multiple_of(ctx_len + b * POOL_TB, CHUNK), POOL_TB)
                    lrows = pl.ds(pl.multiple_of(b * POOL_TB, CHUNK), POOL_TB)
                    dx = _dot2(mwt_ref[...], acc / cw_ref[...]) - dpd_ref[lrows, :]
                    dp_ref[rows, :] = dx.astype(BF16)
                    return carry

                lax.fori_loop(0, nblk, second, 0)

    col = lambda g: (0, g)
    return pl.pallas_call(
        body, name=name, grid=(ng,),
        in_specs=[pl.BlockSpec((T, pg), col), pl.BlockSpec((T, pg), col), pl.BlockSpec((T, pg), col),
                  _gspec(POOL_TB, POOL_TB), _gspec(ctx_len, ctx_len), _gspec(POOL_TB, 1), _gspec(seq, 1),
                  _gspec(ctx_len, 1), _gspec(pg, pg), pl.BlockSpec((1, pg), col)],
        out_specs=[pl.BlockSpec((T, pg), col), pl.BlockSpec((1, pg), col), _gspec(pg, pg)],
        out_shape=[jax.ShapeDtypeStruct((T, dp), BF16), jax.ShapeDtypeStruct((1, dp), F32),
                   jax.ShapeDtypeStruct((ng, pg, pg), F32)],
        scratch_shapes=[pltpu.VMEM((seq + 2 * padt, pg), F32), pltpu.VMEM((seq, pg), F32)],
        compiler_params=_cparams(("arbitrary",)),
    )(dr, y0, pd, pc["mwt"], pc["bct"], pc["cw"], pc["ch"], pc["cc"], wg, scale)


def _loss_head(x2, target, rb, name):
    T, D = x2.shape

    def body(y_ref, t_ref, dy_ref, l_ref):
        i = pl.program_id(0)

        @pl.when(i == 0)
        def _():
            dy_ref[...] = jnp.zeros_like(dy_ref)
            l_ref[...] = jnp.zeros_like(l_ref)

        @pl.when(i > 0)
        def _():
            e = y_ref[...] - t_ref[...]
            dy_ref[...] = e * (1.0 / D)
            l_ref[...] += 0.5 * jnp.sum(jnp.mean(e * e, axis=-1, keepdims=True), axis=0, keepdims=True)

    return pl.pallas_call(
        body, name=name, grid=(T // rb,),
        in_specs=[_row(rb, D), pl.BlockSpec((rb, D), lambda i: (jnp.maximum(i - 1, 0), 0))],
        out_specs=[_row(rb, D), pl.BlockSpec((8, 128), lambda i: (0, 0))],
        out_shape=[jax.ShapeDtypeStruct((T, D), F32), jax.ShapeDtypeStruct((8, 128), F32)],
        compiler_params=_cparams(("arbitrary",)),
    )(x2, target)


def _sum_lead(x, name):
    S, R, C = x.shape

    def body(x_ref, o_ref):
        acc = x_ref[0]
        for s in range(1, S):
            acc = acc + x_ref[s]
        o_ref[...] = acc

    return pl.pallas_call(
        body, name=name, out_shape=jax.ShapeDtypeStruct((R, C), F32),
        compiler_params=_cparams(),
    )(x)


def _silu_rows(cond, name):
    def body(c_ref, o_ref):
        o_ref[...] = _silu(c_ref[...]).astype(BF16)

    return pl.pallas_call(body, name=name, out_shape=jax.ShapeDtypeStruct(cond.shape, BF16),
                          compiler_params=_cparams())(cond)


def _silu_grad(cond, ds, name):
    def body(c_ref, d_ref, o_ref):
        _, vjp = jax.vjp(_silu, c_ref[...])
        o_ref[...] = vjp(d_ref[...])[0]

    return pl.pallas_call(body, name=name, out_shape=jax.ShapeDtypeStruct(cond.shape, F32),
                          compiler_params=_cparams())(cond, ds)


def _adamw_math(g, w_ref, m_ref, v_ref, go_ref, d_ref, mo_ref, vo_ref):
    c1 = 1.0 / (1.0 - ADAM_B1 ** ADAM_STEP)
    c2 = 1.0 / (1.0 - ADAM_B2 ** ADAM_STEP)
    mn = ADAM_B1 * m_ref[...] + (1.0 - ADAM_B1) * g
    vn = ADAM_B2 * v_ref[...] + (1.0 - ADAM_B2) * (g * g)
    go_ref[...] = g
    mo_ref[...] = mn
    vo_ref[...] = vn
    d_ref[...] = -ADAM_LR * ((mn * c1) / (jnp.sqrt(vn * c2) + ADAM_EPS) + ADAM_WD * w_ref[...])


def _adamw(gs, w, m, v, name, job=None):
    S, R, C = gs.shape
    cpad = -(-C // 128) * 128
    rt = _pick(R, max(16, (1 << 20) // (4 * cpad)), 16)

    def body(g_ref, w_ref, m_ref, v_ref, go_ref, d_ref, mo_ref, vo_ref):
        g = g_ref[0].astype(F32)
        for s in range(1, S):
            g = g + g_ref[s].astype(F32)
        _adamw_math(g, w_ref, m_ref, v_ref, go_ref, d_ref, mo_ref, vo_ref)

    blk = pl.BlockSpec((rt, C), lambda i: (i, 0))
    in_specs = [pl.BlockSpec((S, rt, C), lambda i: (0, i, 0)), blk, blk, blk]
    out_shape = [jax.ShapeDtypeStruct((R, C), F32)] * 4
    if job is None:
        return pl.pallas_call(
            body, name=name, grid=(R // rt,), in_specs=in_specs, out_specs=[blk] * 4, out_shape=out_shape,
            compiler_params=_cparams(("parallel",)),
        )(gs, w, m, v)
    return _call_carrying(job, body, name, (R // rt,), in_specs, [blk] * 4, out_shape, [], (gs, w, m, v))


def _adamw_layer(gs, w, m, v, l, prev, name):
    S, R, C = gs.shape
    L = w.shape[0]
    cpad = -(-C // 128) * 128
    rt = _pick(R, max(16, (1 << 20) // (4 * cpad)), 16)

    def body(*refs):
        g_ref, w_ref, m_ref, v_ref = refs[:4]
        go_ref, d_ref, mo_ref, vo_ref = refs[-4:]
        g = g_ref[0].astype(F32)
        for s in range(1, S):
            g = g + g_ref[s].astype(F32)
        _adamw_math(g, w_ref, m_ref, v_ref, go_ref, d_ref, mo_ref, vo_ref)

    blk = pl.BlockSpec((None, rt, C), lambda i: (l, i, 0))
    in_specs = [pl.BlockSpec((S, rt, C), lambda i: (0, i, 0)), blk, blk, blk]
    args = [gs, w, m, v]
    aliases = {}
    if prev is not None:
        in_specs += [pl.BlockSpec(memory_space=pl.ANY)] * 4
        args += list(prev)
        aliases = {4 + q: q for q in range(4)}
    return pl.pallas_call(
        body, name=name, grid=(R // rt,), in_specs=in_specs,
        out_specs=[blk] * 4, out_shape=[jax.ShapeDtypeStruct((L, R, C), F32)] * 4,
        input_output_aliases=aliases,
        compiler_params=_cparams(("parallel",)),
    )(*args)


def _adamw_nd(gs, w, m, v, name, job=None):
    shp = w.shape
    if len(shp) == 1:
        r, c = 1, shp[0]
    else:
        r, c = int(np.prod(shp[:-1])), shp[-1]
    outs = _adamw(gs.reshape(gs.shape[0], r, c), w.reshape(r, c), m.reshape(r, c), v.reshape(r, c), name, job)
    if job is None:
        return [o.reshape(shp) for o in outs]
    return [o.reshape(shp) for o in outs[0]], outs[1]


def kernel(x, c, ctx, c_ctx, w_ada, b_ada, w_in, w_decay_up, b_decay_up, gla_norm_gain, w_pool_group, pool_scale, w_gla_out, w_pool_out, w_out, ln_mix_gain, ln_mix_bias, w_ffn_in, w_ffn_out, ln_ffn_gain, ln_ffn_bias, loss_target, m_c_ctx, m_w_ada, m_b_ada, m_w_in, m_w_decay_up, m_b_decay_up, m_gla_norm_gain, m_w_pool_group, m_pool_scale, m_w_gla_out, m_w_pool_out, m_w_out, m_ln_mix_gain, m_ln_mix_bias, m_w_ffn_in, m_w_ffn_out, m_ln_ffn_gain, m_ln_ffn_bias, v_c_ctx, v_w_ada, v_b_ada, v_w_in, v_w_decay_up, v_b_decay_up, v_gla_norm_gain, v_w_pool_group, v_pool_scale, v_w_gla_out, v_w_pool_out, v_w_out, v_ln_mix_gain, v_ln_mix_bias, v_w_ffn_in, v_w_ffn_out, v_ln_ffn_gain, v_ln_ffn_bias):
    L, D = w_ada.shape[0], w_ada.shape[1]
    seq, ctx_len = x.shape[1], ctx.shape[1]
    T = seq + ctx_len
    rb = ctx_len
    DK = D // 2
    DP = D // 2
    ng = len(POOL_WINDOWS)
    pg = DP // ng
    dff = w_ffn_out.shape[1] * N_DEV
    alpha = (2.0 * L) ** 0.25
    assert seq % rb == 0 and rb % CHUNK == 0 and seq % POOL_TB == 0 and ctx_len % 8 == 0
    xi, yi, ci = _my_pos()
    me = 4 * xi + 2 * yi + ci
    pc = _pool_consts(ctx_len, seq)

    shards = dict(w_in=w_in.astype(BF16), go=w_gla_out.astype(BF16), po=w_pool_out.astype(BF16),
                  out=w_out.astype(BF16), fi=w_ffn_in.astype(BF16), fo=w_ffn_out.astype(BF16),
                  pg=w_pool_group.astype(BF16).reshape(L, ng * pg // N_DEV, pg))
    wkeys = ("w_in", "go", "po", "out", "fi", "fo", "pg")

    def prepared(gw):
        main, alr_w = _w_in_operands(gw["w_in"], "w_in_operands")
        pgf = jnp.swapaxes(gw["pg"].reshape(N_DEV, ng, pg // N_DEV, pg), 0, 1).reshape(ng, pg, pg)
        return dict(main=main, alr=alr_w, go=gw["go"].reshape(D, D), po=gw["po"], out=gw["out"].reshape(D, D),
                    fi=gw["fi"][None], fo=gw["fo"].reshape(1, dff, D), pg=pgf)

    def gather_next(fn, names, l, nxt):
        if l + 1 >= L:
            return fn(None)
        res, outs = fn(_gather_job([shards[k] for k in names], l + 1))
        nxt.update(zip(names, outs))
        return res

    gathered = dict(zip(wkeys, _run_job(_gather_job([shards[k] for k in wkeys], 0), "ag_layer0")))

    dku = w_decay_up.shape[-1]
    small_in = jnp.concatenate([c.reshape(-1), w_decay_up.reshape(-1), b_decay_up.reshape(-1)])
    (small_all,) = _gather_flat([small_in], "ag_small")
    c_all = small_all[:, :D]
    n_wdu = L * 2 * GATE_RANK * dku
    wdu_all = small_all[:, D:D + n_wdu].reshape(N_DEV, L, 2, GATE_RANK, dku)
    wdu_full = jnp.transpose(wdu_all, (1, 2, 3, 0, 4)).reshape(L, 2, GATE_RANK, DK)
    bdu_all = small_all[:, D + n_wdu:].reshape(N_DEV, L, 2, dku)
    bdu_full = jnp.transpose(bdu_all, (1, 2, 0, 3)).reshape(L, 1, 2 * DK)
    wdu_bd = jnp.zeros((L, ALR_PAD, 2 * DK), F32)
    wdu_bd = wdu_bd.at[:, :GATE_RANK, :DK].set(wdu_full[:, 0])
    wdu_bd = wdu_bd.at[:, GATE_RANK:2 * GATE_RANK, DK:].set(wdu_full[:, 1]).astype(BF16)

    ncond = 16
    cond = jnp.concatenate([c_all, c_ctx.reshape(1, D), jnp.zeros((ncond - N_DEV - 1, D), F32)], axis=0)
    s_cond = _silu_rows(cond, "silu_cond")
    wsh = w_ada.shape[-1]
    b_ada_mine = lax.dynamic_slice_in_dim(b_ada, me * wsh, wsh, axis=1)
    mod_part = jnp.stack([_mm(s_cond, w_ada, "nn", F32, "mod_mm", bias=b_ada_mine[l:l + 1], b_pre=(l,))
                          for l in range(L)])
    (mod_all,) = _all_gather([mod_part], "ag_mod")
    mod_all = jnp.swapaxes(mod_all, 1, 2).reshape(L, ncond, N_MOD * D)
    mod_lat = lax.dynamic_slice_in_dim(mod_all, me, 1, axis=1)
    mods = jnp.concatenate([mod_all[:, N_DEV:N_DEV + 1], mod_lat], axis=1).reshape(L, 2, 1, N_MOD * D)
    SH_M, SC_M, GT_M, SH_F, SC_F, GT_F = range(N_MOD)

    xa = jnp.concatenate([ctx[0], x[0]], axis=0)
    vec = lambda a, l: a[l].reshape(1, -1)
    saved = []
    h = _mod_fwd(xa, mods[0], SC_M, SH_M, rb, "mod_fwd")
    weights = []
    for l in range(L):
        W = prepared(gathered)
        weights.append(W)
        gathered = {}
        proj = gather_next(lambda j: _mm(h, W["main"], "nn", F32, "mm_in", job=j), ["w_in"], l, gathered)
        alr = _mm(h, W["alr"], "nn", F32, "mm_alr")
        la = _decay_fwd(alr, wdu_bd[l], bdu_full[l], rb, "decay_fwd")
        o_f, s_f = gather_next(lambda j: _gla_fwd(proj, la, False, rb, D, "gla_fwd_f", job=j), ["go", "out"], l,
                               gathered)
        o_b, s_b = gather_next(lambda j: _gla_fwd(proj, la, True, rb, D, "gla_fwd_b", job=j), ["po", "pg"], l,
                               gathered)
        u = _glaout_fwd(o_f, o_b, proj, vec(gla_norm_gain, l), rb, "glaout_fwd")
        y_gla = _mm(u, W["go"], "nn", F32, "mm_go")
        pd, y0, r = _pool_fwd(proj, pc, W["pg"], vec(pool_scale, l), ctx_len, D, "pool_fwd")
        y_pool = _mm(r, W["po"], "nn", F32, "mm_po", b_shard=True)
        m_ = _merge_fwd(proj, y_gla, y_pool, rb, "merge_fwd")
        mix = _mm(m_, W["out"], "nn", F32, "mm_out")
        x1, h2 = _unit_fwd(alpha, xa, mix, mods[l], GT_M, vec(ln_mix_gain, l), vec(ln_mix_bias, l),
                           (mods[l], SC_F, SH_F), rb, "unit_mix_fwd")
        ff, s_ = gather_next(lambda j: _ffn_in_fwd(h2, W["fi"], 0, "mm_fi_swiglu", job=j), ["fi"], l, gathered)
        ffn = gather_next(lambda j: _mm(s_, W["fo"], "nn", F32, "mm_fo", b_pre=(0,), job=j), ["fo"], l, gathered)
        nxt = (mods[l + 1], SC_M, SH_M) if l + 1 < L else None
        x2, h_next = _unit_fwd(alpha, x1, ffn, mods[l], GT_F, vec(ln_ffn_gain, l), vec(ln_ffn_bias, l),
                               nxt, rb, "unit_ffn_fwd")
        saved.append(dict(xa=xa, h=h, proj=proj, alr=alr, la=la, o_f=o_f, o_b=o_b, s_f=s_f, s_b=s_b, u=u,
                          y_gla=y_gla, pd=pd, y0=y0, r=r, y_pool=y_pool, m=m_, mix=mix, x1=x1, h2=h2, ff=ff,
                          s=s_, ffn=ffn))
        xa, h = x2, h_next

    dxo, loss_part = _loss_head(xa, loss_target[0], rb, "loss_head")
    loss = lax.psum(loss_part[0, 0], ("x", "y", "c"))

    big_params = [("w_in", w_in, m_w_in, v_w_in), ("w_gla_out", w_gla_out, m_w_gla_out, v_w_gla_out),
                  ("w_pool_out", w_pool_out, m_w_pool_out, v_w_pool_out), ("w_out", w_out, m_w_out, v_w_out),
                  ("w_ffn_in", w_ffn_in, m_w_ffn_in, v_w_ffn_in), ("w_ffn_out", w_ffn_out, m_w_ffn_out, v_w_ffn_out),
                  ("w_pool_group", w_pool_group, m_w_pool_group, v_w_pool_group)]
    big_out = {nm: None for nm, _, _, _ in big_params}
    g_small = {k: [None] * L for k in ("gla_gain", "pool_scale", "mix_g", "mix_b", "ffn_g", "ffn_b", "wdu", "bdu")}
    dmods = [None] * L
    dh = None
    sum2 = lambda a: a[0] + a[1]
    rows8 = lambda g: g.reshape(N_DEV, g.shape[0] // N_DEV, g.shape[1])

    def apply_adamw(parts, layer):
        for (nm, w, m, v), gs in zip(big_params, parts):
            R, C = gs.shape[1], gs.shape[2]
            big_out[nm] = _adamw_layer(gs, w.reshape(L, R, C), m.reshape(L, R, C), v.reshape(L, R, C), layer,
                                       big_out[nm], "adamw_" + nm)

    LATE = (0, 1, 2, 3, 6)
    late_chunks = None
    arrived = {}

    def behind(fn, job, positions):
        if job is None:
            return fn(None)
        res, outs = fn(job)
        if positions is None:
            return res, outs
        arrived.update(zip(positions, outs))
        return res

    for l in range(L - 1, -1, -1):
        sv = saved[l]
        W = weights[l]
        nxt = (mods[l + 1], SC_M, SH_M) if l + 1 < L else None
        unit = lambda j: _unit_bwd(alpha, dxo, dh, sv["x1"], sv["ffn"], mods[l], GT_F, vec(ln_ffn_gain, l),
                                   vec(ln_ffn_bias, l), nxt, rb, "unit_ffn_bwd", job=j)
        late_pairs = None
        if late_chunks is None:
            res = unit(None)
        else:
            res, sib = behind(unit, _sibling_job(late_chunks), None)
            late_pairs = _pair_adds(late_chunks, sib, "_late")
        dx1, dffn, d_gtf, d_gf, d_bf, d_scm_n, d_shm_n = res
        if nxt is not None:
            dmods[l + 1]["sc_m"], dmods[l + 1]["sh_m"] = d_scm_n, d_shm_n
        dmods[l] = dict(gt_f=d_gtf)
        g_small["ffn_g"][l], g_small["ffn_b"][l] = sum2(d_gf), sum2(d_bf)
        dff_ = behind(lambda j: _ffn_out_dx(dffn, W["fo"], 0, sv["ff"], "mm_fo_dx_swiglu", job=j),
                      _chip_job(late_pairs[1:]) if late_pairs else None, LATE[1:])
        c_fo = rows8(_mm(sv["s"], dffn, "tn", BF16, "mm_fo_dw"))
        dh2 = behind(lambda j: _mm(dff_, W["fi"], "nt", F32, "mm_fi_dx", b_pre=(0,), b_shard=True, a_half=True,
                                   job=j), _chip_job(late_pairs[:1]) if late_pairs else None, LATE[:1])
        c_fi = _mm(sv["h2"], dff_, "tn", BF16, "mm_fi_dw", b_half=True, out_shard=True)
        if late_pairs:
            apply_adamw([arrived[i] for i in range(len(big_params))], l + 1)
            arrived = {}
        ffn_chunks = [c_fi, c_fo]
        res, sib = behind(lambda j: _unit_bwd(
            alpha, dx1, dh2, sv["xa"], sv["mix"], mods[l], GT_M, vec(ln_mix_gain, l), vec(ln_mix_bias, l),
            (mods[l], SC_F, SH_F), rb, "unit_mix_bwd", job=j), _sibling_job(ffn_chunks), None)
        dxa, dmix, d_gtm, d_gm, d_bm, d_scf, d_shf = res
        ffn_pairs = _pair_adds(ffn_chunks, sib, "_ffn")
        dmods[l].update(gt_m=d_gtm, sc_f=d_scf, sh_f=d_shf)
        g_small["mix_g"][l], g_small["mix_b"][l] = sum2(d_gm), sum2(d_bm)
        dm = _mm(dmix, W["out"], "nt", F32, "mm_out_dx")
        c_out = rows8(_mm(sv["m"], dmix, "tn", BF16, "mm_out_dw"))
        dbg1, dbg2, dyg, dyp = _merge_bwd(dm, sv["proj"], sv["y_gla"], sv["y_pool"], rb, "merge_bwd")
        dr = _mm(dyp, W["po"], "nt", F32, "mm_po_dx", b_shard=True)
        c_po = _mm(sv["r"], dyp, "tn", BF16, "mm_po_dw", out_shard=True)
        dp_, d_ps, g_pgl = _pool_bwd(dr, sv["y0"], sv["pd"], pc, W["pg"], vec(pool_scale, l), ctx_len, D, "pool_bwd")
        g_small["pool_scale"][l] = d_ps
        c_pg = jnp.swapaxes(g_pgl.astype(BF16).reshape(ng, N_DEV, pg // N_DEV, pg), 0, 1).reshape(N_DEV, -1, pg)
        du = _mm(dyg, W["go"], "nt", F32, "mm_go_dx")
        c_go = rows8(_mm(sv["u"], dyg, "tn", BF16, "mm_go_dw"))
        do, dg, d_gg = _glaout_bwd(du, sv["o_f"], sv["o_b"], sv["proj"], vec(gla_norm_gain, l), rb, "glaout_bwd")
        g_small["gla_gain"][l] = sum2(d_gg)
        dq_f, dk_f, dv_f, dla_f = behind(lambda j: _gla_bwd(
            sv["proj"], sv["la"], do, sv["s_f"], False, rb, D, None, "gla_bwd_f", job=j),
            _chip_job(ffn_pairs[:1]), (4,))
        dq, dk, dv, dla_b = behind(lambda j: _gla_bwd(
            sv["proj"], sv["la"], do, sv["s_b"], True, rb, D, (dq_f, dk_f, dv_f), "gla_bwd_b", job=j),
            _chip_job(ffn_pairs[1:]), (5,))
        dalr, g_wdu, g_bdu = _decay_bwd(dla_f, dla_b, sv["alr"], wdu_bd[l], bdu_full[l], rb, "decay_bwd")
        g_small["wdu"][l] = jnp.stack([g_wdu[:GATE_RANK, :DK], g_wdu[GATE_RANK:2 * GATE_RANK, DK:]])
        g_small["bdu"][l] = g_bdu.reshape(2, DK)
        dproj = jnp.concatenate([dv, dg, dbg1, dbg2, dq, dk, dp_], axis=1)
        dh_alr = _mm(dalr, W["alr"], "nt", F32, "mm_alr_dx")
        dh = _mm(dproj, W["main"], "nt", F32, "mm_in_dx", add=dh_alr)
        g_main = _mm(sv["h"], dproj, "tn", BF16, "mm_in_dw")
        g_alr = _mm(sv["h"], dalr, "tn", BF16, "mm_alr_dw")
        c_in = _w_in_chunks(g_main, g_alr, w_in.shape[2], "w_in_chunks")
        late_chunks = [c_in, c_go, c_po, c_out, c_pg]
        dxo = dxa
    sib = _run_job(_sibling_job(late_chunks), "rs_sibling_last")
    last_pairs = _pair_adds(late_chunks, sib, "_last")
    grad_xa, d_scm0, d_shm0 = _mod_bwd(dxo, dh, saved[0]["xa"], mods[0], SC_M, SH_M, rb, "mod_bwd")
    dmods[0]["sc_m"], dmods[0]["sh_m"] = d_scm0, d_shm0
    grad_x = grad_xa[ctx_len:].reshape(1, seq, D)

    order = ("sh_m", "sc_m", "gt_m", "sh_f", "sc_f", "gt_f")
    dmod = jnp.stack([jnp.concatenate([dmods[l][k] for k in order], axis=2) for l in range(L)])
    dmod = dmod.reshape(-1)
    sm = lambda k: jnp.stack([a.reshape(-1) for a in g_small[k]]).reshape(-1)
    small_keys = ("gla_gain", "pool_scale", "mix_g", "mix_b", "ffn_g", "ffn_b", "wdu", "bdu")
    small_part = jnp.concatenate([sm(k) for k in small_keys])
    small_g, dmod_g = _gather_flat([small_part, dmod], "ag_small_grads")
    small_sum = _sum_lead(small_g.reshape(N_DEV, -1, 128), "sum_small").reshape(-1)
    off = 0
    rep = {}
    for k, n in zip(small_keys, (L * D, L * DP, L * D, L * D, L * D, L * D, L * 2 * GATE_RANK * DK, L * 2 * DK)):
        rep[k] = small_sum[off:off + n]
        off += n
    g_wdu_mine = lax.dynamic_slice_in_dim(rep["wdu"].reshape(L, 2, GATE_RANK, DK), me * dku, dku, axis=3)
    g_bdu_mine = lax.dynamic_slice_in_dim(rep["bdu"].reshape(L, 2, DK), me * dku, dku, axis=2)

    dmod_all = dmod_g.reshape(N_DEV, L, 2, N_MOD * D)
    dm_ctx = _sum_lead(dmod_all[:, :, 0].reshape(N_DEV, L, N_MOD * D), "sum_dmod_ctx")
    dm_rows = jnp.concatenate([jnp.swapaxes(dmod_all[:, :, 1], 0, 1), dm_ctx[:, None],
                               jnp.zeros((L, ncond - N_DEV - 1, N_MOD * D), F32)], axis=1)
    g_b_ada = _sum_lead(jnp.swapaxes(dm_rows, 0, 1), "sum_b_ada")
    dm_mine = lax.dynamic_slice_in_dim(dm_rows, me * wsh, wsh, axis=2).astype(BF16)
    g_w_ada = jnp.stack([_mm(s_cond, dm_mine[l], "tn", F32, "ada_dw") for l in range(L)])
    ds_part = _sum_lead(jnp.stack([_mm(dm_mine[l], w_ada, "nt", F32, "ada_dx", b_pre=(l,)) for l in range(L)]),
                        "sum_ds")
    (ds_all,) = _gather_flat([ds_part[N_DEV]], "ag_ds")
    ds_ctx = _sum_lead(ds_all.reshape(N_DEV, 1, D), "sum_ds_ctx")
    g_c_ctx = _silu_grad(c_ctx.reshape(1, D), ds_ctx, "silu_grad").reshape(D)

    one = lambda g: g[None]
    small_table = {
        "c_ctx": (one(g_c_ctx), c_ctx, m_c_ctx, v_c_ctx),
        "w_ada": (one(g_w_ada), w_ada, m_w_ada, v_w_ada),
        "b_ada": (one(g_b_ada), b_ada, m_b_ada, v_b_ada),
        "w_decay_up": (one(g_wdu_mine), w_decay_up, m_w_decay_up, v_w_decay_up),
        "b_decay_up": (one(g_bdu_mine), b_decay_up, m_b_decay_up, v_b_decay_up),
        "gla_norm_gain": (one(rep["gla_gain"].reshape(L, D)), gla_norm_gain, m_gla_norm_gain, v_gla_norm_gain),
        "pool_scale": (one(rep["pool_scale"].reshape(L, DP)), pool_scale, m_pool_scale, v_pool_scale),
        "ln_mix_gain": (one(rep["mix_g"].reshape(L, D)), ln_mix_gain, m_ln_mix_gain, v_ln_mix_gain),
        "ln_mix_bias": (one(rep["mix_b"].reshape(L, D)), ln_mix_bias, m_ln_mix_bias, v_ln_mix_bias),
        "ln_ffn_gain": (one(rep["ffn_g"].reshape(L, D)), ln_ffn_gain, m_ln_ffn_gain, v_ln_ffn_gain),
        "ln_ffn_bias": (one(rep["ffn_b"].reshape(L, D)), ln_ffn_bias, m_ln_ffn_bias, v_ln_ffn_bias),
    }
    big_shapes = {nm: w.shape for nm, w, _, _ in big_params}
    names = ("c_ctx", "w_ada", "b_ada", "w_in", "w_decay_up", "b_decay_up", "gla_norm_gain", "w_pool_group",
             "pool_scale", "w_gla_out", "w_pool_out", "w_out", "ln_mix_gain", "ln_mix_bias", "w_ffn_in", "w_ffn_out",
             "ln_ffn_gain", "ln_ffn_bias")
    ada_res, last_parts = _adamw_nd(*small_table["w_ada"], "adamw_w_ada", job=_chip_job(last_pairs))
    arrived.update(zip(LATE, last_parts))
    apply_adamw([arrived[i] for i in range(len(big_params))], 0)
    grads, deltas, new_m, new_v = [], [], [], []
    for nm in names:
        if nm == "w_ada":
            res = ada_res
        elif nm in small_table:
            res = _adamw_nd(*small_table[nm], "adamw_" + nm)
        else:
            res = [o.reshape(big_shapes[nm]) for o in big_out[nm]]
        for lst, o in zip((grads, deltas, new_m, new_v), res):
            lst.append(o)
    return (loss, grad_x, *grads, *deltas, *new_m, *new_v)
```

```python
import functools
import math

import numpy as np
import jax
import jax.numpy as jnp
from jax import lax
from jax.experimental import pallas as pl
from jax.experimental.pallas import tpu as pltpu

F32 = jnp.float32
BF16 = jnp.bfloat16

N_DEV = 8
N_HEADS = 4
GATE_RANK = 16
GATE_NORM = 16.0
CHUNK = 64
GRID_W = 64
POOL_WINDOWS = (2, 4, 8, 16)
N_MOD = 6
LN_EPS = 1e-5
RMS_EPS = 1e-6
ALR_PAD = 128
POOL_TB = 256
POOL_PAD_ROWS = 8
ADAM_LR = 0.001
ADAM_B1 = 0.9
ADAM_B2 = 0.999
ADAM_EPS = 1e-08
ADAM_WD = 0.01
ADAM_STEP = 10
VMEM_LIMIT = 56 * 1024 * 1024
MESH = pl.DeviceIdType.MESH


def _cparams(sem=None):
    return pltpu.CompilerParams(dimension_semantics=sem, vmem_limit_bytes=VMEM_LIMIT)


def _pick(dim, cap, mult):
    best = None
    for d in range(mult, min(dim, cap) + 1, mult):
        if dim % d == 0:
            best = d
    return best if best is not None else dim


def _sig(x):
    return 1.0 / (1.0 + jnp.exp(-x))


def _silu(x):
    return x * _sig(x)


def _dot(a, b):
    return lax.dot_general(a, b, (((1,), (0,)), ((), ())), preferred_element_type=F32)


def _dot_nt(a, b):
    return lax.dot_general(a, b, (((1,), (1,)), ((), ())), preferred_element_type=F32)


def _dot_tn(a, b):
    return lax.dot_general(a, b, (((0,), (0,)), ((), ())), preferred_element_type=F32)


def _split2(x):
    hi = x.astype(BF16)
    lo = (x - hi.astype(F32)).astype(BF16)
    return hi, lo


def _dot2(m_b, x):
    hi, lo = _split2(x)
    return _dot(m_b, hi) + _dot(m_b, lo)


def _dot3(m_b, x):
    h1 = x.astype(BF16)
    r1 = x - h1.astype(F32)
    h2 = r1.astype(BF16)
    h3 = (r1 - h2.astype(F32)).astype(BF16)
    return _dot(m_b, h1) + _dot(m_b, h2) + _dot(m_b, h3)


def _my_pos():
    return lax.axis_index("x"), lax.axis_index("y"), lax.axis_index("c")


def _all_gather(arrs, name):
    n = len(arrs)
    srcs = [a.reshape((a.shape[0], 1) + a.shape[1:]) for a in arrs]
    outs = [jax.ShapeDtypeStruct((a.shape[0], N_DEV) + a.shape[1:], a.dtype) for a in arrs]

    def body(*refs):
        in_refs, out_refs = refs[:n], refs[n:2 * n]
        send_sems, recv_sems, local_sems = refs[2 * n:]
        x, y, c = _my_pos()
        me, sibling = (x, y, c), (x, y, 1 - c)
        chips = [(1 - x, y), (x, 1 - y), (1 - x, 1 - y)]

        def slot(t, pos):
            return out_refs[t].at[:, pl.ds(4 * pos[0] + 2 * pos[1] + pos[2], 1)]

        def copy(t, k, block, to, src=None):
            return pltpu.make_async_remote_copy(
                src_ref=slot(t, block) if src is None else src, dst_ref=slot(t, block),
                send_sem=send_sems.at[t * 7 + k], recv_sem=recv_sems.at[t * 7 + k],
                device_id=to, device_id_type=MESH)

        mine = [pltpu.make_async_copy(in_refs[t], slot(t, me), local_sems.at[t]) for t in range(n)]
        for cp in mine:
            cp.start()
        first = []
        for t in range(n):
            first.append(copy(t, 0, me, sibling, src=in_refs[t]))
            first += [copy(t, 1 + j, me, (*chip, c), src=in_refs[t]) for j, chip in enumerate(chips)]
        for cp in first:
            cp.start()
        passed = []
        for j, chip in enumerate(chips):
            for t in range(n):
                copy(t, 1 + j, (*chip, c), me).wait_recv()
                fwd = copy(t, 4 + j, (*chip, c), sibling)
                fwd.start()
                passed.append(fwd)
        for t in range(n):
            copy(t, 0, sibling, me).wait_recv()
            for j, chip in enumerate(chips):
                copy(t, 4 + j, (*chip, 1 - c), me).wait_recv()
        for cp in first + passed:
            cp.wait_send()
        for cp in mine:
            cp.wait()

    any_spec = pl.BlockSpec(memory_space=pl.ANY)
    res = pl.pallas_call(
        body, name=name, out_shape=outs,
        in_specs=[any_spec] * n, out_specs=[any_spec] * n,
        scratch_shapes=[pltpu.SemaphoreType.DMA((7 * n,)), pltpu.SemaphoreType.DMA((7 * n,)),
                        pltpu.SemaphoreType.DMA((n,))],
        compiler_params=pltpu.CompilerParams(has_side_effects=True),
    )(*srcs)
    return list(res)


def _gather_flat(vecs, name):
    padded = []
    for v in vecs:
        n = v.shape[0]
        padded.append(jnp.pad(v, (0, -n % 128)).reshape(1, -1, 128))
    res = _all_gather(padded, name)
    return [r.reshape(N_DEV, -1)[:, :v.shape[0]] for r, v in zip(res, vecs)]


N_CHIP = 4


def _comm_call(body, name, arrs, outs, n_sems):
    any_spec = pl.BlockSpec(memory_space=pl.ANY)
    n = len(arrs)
    res = pl.pallas_call(
        body, name=name, out_shape=outs,
        in_specs=[any_spec] * n, out_specs=[any_spec] * len(outs),
        scratch_shapes=[pltpu.SemaphoreType.DMA((s,)) for s in n_sems],
        compiler_params=pltpu.CompilerParams(has_side_effects=True),
    )(*arrs)
    return list(res)


def _sibling_job(arrs):
    n = len(arrs)
    outs = [jax.ShapeDtypeStruct((N_CHIP,) + a.shape[1:], a.dtype) for a in arrs]

    def copies(in_refs, out_refs, sems):
        send_sems, recv_sems = sems
        x, y, c = _my_pos()
        return [pltpu.make_async_remote_copy(
            src_ref=in_refs[t].at[pl.ds(2 * k + (1 - c), 1)], dst_ref=out_refs[t].at[pl.ds(k, 1)],
            send_sem=send_sems.at[t * N_CHIP + k], recv_sem=recv_sems.at[t * N_CHIP + k],
            device_id=(x, y, 1 - c), device_id_type=MESH) for t in range(n) for k in range(N_CHIP)]

    def start(in_refs, out_refs, sems):
        for cp in copies(in_refs, out_refs, sems):
            cp.start()

    def finish(in_refs, out_refs, sems):
        cps = copies(in_refs, out_refs, sems)
        for cp in cps:
            cp.wait_recv()
        for cp in cps:
            cp.wait_send()

    return _Job(arrs, outs, (N_CHIP * n, N_CHIP * n), start, finish)


class _Job:
    def __init__(self, arrs, outs, n_sems, start, finish):
        self.arrs, self.outs, self.n_sems, self.start, self.finish = arrs, outs, n_sems, start, finish


def _gather_job(stacked, layers):
    n = len(stacked)
    outs = [jax.ShapeDtypeStruct((N_DEV,) + a.shape[1:], a.dtype) for a in stacked]

    def parts(in_refs, out_refs, sems):
        send_sems, recv_sems, local_sems = sems
        x, y, c = _my_pos()
        me, sibling = (x, y, c), (x, y, 1 - c)
        chips = [(1 - x, y), (x, 1 - y), (1 - x, 1 - y)]
        src = lambda t: in_refs[t].at[pl.ds(layers[t], 1)]

        def slot(t, pos):
            return out_refs[t].at[pl.ds(4 * pos[0] + 2 * pos[1] + pos[2], 1)]

        def copy(t, k, block, to, from_input=False):
            return pltpu.make_async_remote_copy(
                src_ref=src(t) if from_input else slot(t, block), dst_ref=slot(t, block),
                send_sem=send_sems.at[t * 7 + k], recv_sem=recv_sems.at[t * 7 + k],
                device_id=to, device_id_type=MESH)

        mine = [pltpu.make_async_copy(src(t), slot(t, me), local_sems.at[t]) for t in range(n)]
        first = []
        for t in range(n):
            first.append(copy(t, 0, me, sibling, True))
            first += [copy(t, 1 + j, me, (*chip, c), True) for j, chip in enumerate(chips)]
        return me, sibling, chips, copy, mine, first

    def start(in_refs, out_refs, sems):
        _, _, _, _, mine, first = parts(in_refs, out_refs, sems)
        for cp in mine + first:
            cp.start()

    def finish(in_refs, out_refs, sems):
        me, sibling, chips, copy, mine, first = parts(in_refs, out_refs, sems)
        passed = []
        for j, chip in enumerate(chips):
            for t in range(n):
                copy(t, 1 + j, (*chip, me[2]), me).wait_recv()
                fwd = copy(t, 4 + j, (*chip, me[2]), sibling)
                fwd.start()
                passed.append(fwd)
        for t in range(n):
            copy(t, 0, sibling, me).wait_recv()
            for j, chip in enumerate(chips):
                copy(t, 4 + j, (*chip, 1 - me[2]), me).wait_recv()
        for cp in first + passed:
            cp.wait_send()
        for cp in mine:
            cp.wait()

    return _Job(stacked, outs, (7 * n, 7 * n, n), start, finish)


def _chip_job(arrs):
    n = len(arrs)
    outs = [jax.ShapeDtypeStruct(a.shape, a.dtype) for a in arrs]

    def parts(in_refs, out_refs, sems):
        send_sems, recv_sems, local_sems = sems
        x, y, c = _my_pos()
        chip = 2 * x + y
        mine, sends, recvs = [], [], []
        for t in range(n):
            mine.append(pltpu.make_async_copy(in_refs[t].at[pl.ds(chip, 1)], out_refs[t].at[pl.ds(chip, 1)],
                                              local_sems.at[t]))
            for m in range(1, N_CHIP):
                px, py = x ^ (m >> 1), y ^ (m & 1)
                peer = 2 * px + py
                sends.append(pltpu.make_async_remote_copy(
                    src_ref=in_refs[t].at[pl.ds(peer, 1)], dst_ref=out_refs[t].at[pl.ds(chip, 1)],
                    send_sem=send_sems.at[t * 3 + m - 1], recv_sem=recv_sems.at[t * 3 + m - 1],
                    device_id=(px, py, c), device_id_type=MESH))
                recvs.append(pltpu.make_async_remote_copy(
                    src_ref=in_refs[t].at[pl.ds(peer, 1)], dst_ref=out_refs[t].at[pl.ds(peer, 1)],
                    send_sem=send_sems.at[t * 3 + m - 1], recv_sem=recv_sems.at[t * 3 + m - 1],
                    device_id=(x, y, c), device_id_type=MESH))
        return mine, sends, recvs

    def start(in_refs, out_refs, sems):
        mine, sends, _ = parts(in_refs, out_refs, sems)
        for cp in mine + sends:
            cp.start()

    def finish(in_refs, out_refs, sems):
        mine, sends, recvs = parts(in_refs, out_refs, sems)
        for cp in recvs:
            cp.wait_recv()
        for cp in sends:
            cp.wait_send()
        for cp in mine:
            cp.wait()

    return _Job(arrs, outs, (3 * n, 3 * n, n), start, finish)


def _run_job(job, name):
    n = len(job.arrs)

    def body(*refs):
        ins, outs, sems = refs[:n], refs[n:n + len(job.outs)], refs[n + len(job.outs):]
        job.start(ins, outs, sems)
        job.finish(ins, outs, sems)

    return _comm_call(body, name, job.arrs, job.outs, job.n_sems)


def _carry(job, body, grid, in_specs, out_specs, out_shape, scratch_shapes, args):
    out_specs = list(out_specs) if isinstance(out_specs, (list, tuple)) else [out_specs]
    out_shape = list(out_shape) if isinstance(out_shape, (list, tuple)) else [out_shape]
    n_ci, n_co, n_cs = len(in_specs), len(out_specs), len(scratch_shapes)
    n_ji, n_jo = len(job.arrs), len(job.outs)
    any_spec = pl.BlockSpec(memory_space=pl.ANY)
    total = int(np.prod(grid))

    def wrapped(*refs):
        cin, jin = refs[:n_ci], refs[n_ci:n_ci + n_ji]
        o0 = n_ci + n_ji
        cout, jout = refs[o0:o0 + n_co], refs[o0 + n_co:o0 + n_co + n_jo]
        s0 = o0 + n_co + n_jo
        cscr, jsems = refs[s0:s0 + n_cs], refs[s0 + n_cs:]
        step = pl.program_id(0)
        for d in range(1, len(grid)):
            step = step * grid[d] + pl.program_id(d)

        @pl.when(step == 0)
        def _():
            job.start(jin, jout, jsems)

        body(*cin, *cout, *cscr)

        @pl.when(step == total - 1)
        def _():
            job.finish(jin, jout, jsems)

    return (wrapped, list(in_specs) + [any_spec] * n_ji, out_specs + [any_spec] * n_jo,
            out_shape + list(job.outs),
            list(scratch_shapes) + [pltpu.SemaphoreType.DMA((s,)) for s in job.n_sems],
            list(args) + list(job.arrs), n_co)


def _pair_add(g, r, name):
    _, R, C = g.shape
    cpad = -(-C // 128) * 128
    rt = _pick(R, max(16, (1 << 20) // (2 * cpad)), 16)
    cidx = lax.axis_index("c").astype(jnp.int32).reshape(1)

    def body(c_ref, g_ref, r_ref, o_ref):
        o_ref[...] = (g_ref[...].astype(F32) + r_ref[...].astype(F32)).astype(o_ref.dtype)

    return pl.pallas_call(
        body, name=name, out_shape=jax.ShapeDtypeStruct((N_CHIP, R, C), g.dtype),
        grid_spec=pltpu.PrefetchScalarGridSpec(
            num_scalar_prefetch=1, grid=(N_CHIP, R // rt),
            in_specs=[pl.BlockSpec((None, rt, C), lambda k, i, c_ref: (2 * k + c_ref[0], i, 0)),
                      pl.BlockSpec((None, rt, C), lambda k, i, c_ref: (k, i, 0))],
            out_specs=pl.BlockSpec((None, rt, C), lambda k, i, c_ref: (k, i, 0))),
        compiler_params=_cparams(("parallel", "parallel")),
    )(cidx, g, r)


def _pair_adds(chunks, sib, tag):
    return [_pair_add(g, r, "rs_pair_add" + tag) for g, r in zip(chunks, sib)]


def _mm(a, b, mode, out_dtype=F32, name="mm", bias=None, add=None, b_pre=(), b_shard=False,
        a_half=False, b_half=False, out_shard=False, job=None):
    npre = len(b_pre)
    bshape = b.shape[npre:]
    if mode == "nn":
        M, K = a.shape
        if b_shard:
            K2, N = bshape[1], N_DEV * bshape[2]
        else:
            K2, N = bshape
    elif mode == "nt":
        M, K = (a.shape[1], 2 * a.shape[2]) if a_half else a.shape
        if b_shard:
            N, K2 = bshape[1], N_DEV * bshape[2]
        else:
            N, K2 = bshape
    else:
        K, M = a.shape
        K2, N = (b.shape[1], 2 * b.shape[2]) if b_half else bshape
    assert K == K2, (a.shape, b.shape, mode)
    tm = _pick(M, 1100, 16) if mode != "tn" else _pick(M, 1024, 128)
    tn = _pick(N, 1024, 128)
    tk = _pick(K, 2816 if mode == "nt" else 2176, 128)
    if b_shard and mode == "nn":
        tn = bshape[2]
    sps = 1
    if b_shard and mode == "nt":
        ns = bshape[2]
        sps = 2 if ns % 128 == 0 and (not a_half or (a.shape[2] // ns) % 2 == 0) else 1
        tk = sps * ns
    if out_shard:
        tn = N // N_DEV
    nk = K // tk
    none_pre = (None,) * npre
    if mode == "nn":
        a_spec = pl.BlockSpec((tm, tk), lambda i, j, k: (i, k))
        if b_shard:
            b_spec = pl.BlockSpec(none_pre + (None, tk, tn), lambda i, j, k: b_pre + (j, k, 0))
        else:
            b_spec = pl.BlockSpec(none_pre + (tk, tn), lambda i, j, k: b_pre + (k, j))
        dot = _dot
    elif mode == "nt":
        if a_half:
            nkh = a.shape[2] // tk
            a_spec = pl.BlockSpec((None, tm, tk), lambda i, j, k: (k // nkh, i, k % nkh))
        else:
            a_spec = pl.BlockSpec((tm, tk), lambda i, j, k: (i, k))
        if b_shard:
            b_spec = pl.BlockSpec(none_pre + (sps, tn, tk // sps), lambda i, j, k: b_pre + (k, j, 0))
        else:
            b_spec = pl.BlockSpec(none_pre + (tn, tk), lambda i, j, k: b_pre + (j, k))
        dot = _dot_nt
        if b_shard:
            def dot(a_blk, b_blk):
                ns_ = tk // sps
                p = _dot_nt(a_blk[:, :ns_], b_blk[0])
                for s in range(1, sps):
                    p = p + _dot_nt(a_blk[:, s * ns_:(s + 1) * ns_], b_blk[s])
                return p
    else:
        a_spec = pl.BlockSpec((tk, tm), lambda i, j, k: (k, i))
        if b_half:
            nnh = b.shape[2] // tn
            b_spec = pl.BlockSpec((None, tk, tn), lambda i, j, k: (j // nnh, k, j % nnh))
        else:
            b_spec = pl.BlockSpec(none_pre + (tk, tn), lambda i, j, k: b_pre + (k, j))
        dot = _dot_tn
    in_specs = [a_spec, b_spec]
    args = [a, b]
    if bias is not None:
        in_specs.append(pl.BlockSpec((1, tn), lambda i, j, k: (0, j)))
        args.append(bias)
    if add is not None:
        in_specs.append(pl.BlockSpec((tm, tn), lambda i, j, k: (i, j)))
        args.append(add)
    n_in = len(args)
    if out_shard:
        o_spec = pl.BlockSpec((None, tm, tn), lambda i, j, k: (j, i, 0))
        o_shape = jax.ShapeDtypeStruct((N_DEV, M, tn), out_dtype)
    else:
        o_spec = pl.BlockSpec((tm, tn), lambda i, j, k: (i, j))
        o_shape = jax.ShapeDtypeStruct((M, N), out_dtype)

    def body(*refs):
        a_ref, b_ref = refs[0], refs[1]
        bias_ref = refs[2] if bias is not None else None
        add_ref = refs[n_in - 1] if add is not None else None
        o_ref = refs[n_in]
        p = dot(a_ref[...].astype(BF16), b_ref[...].astype(BF16))

        def finish(acc):
            if bias_ref is not None:
                acc = acc + bias_ref[...]
            if add_ref is not None:
                acc = acc + add_ref[...]
            o_ref[...] = acc.astype(o_ref.dtype)

        if nk == 1:
            finish(p)
        else:
            acc_ref = refs[-1]
            k = pl.program_id(2)

            @pl.when(k == 0)
            def _():
                acc_ref[...] = p

            @pl.when(k > 0)
            def _():
                acc_ref[...] += p

            @pl.when(k == nk - 1)
            def _():
                finish(acc_ref[...])

    grid = (M // tm, N // tn, nk)
    scratch = [pltpu.VMEM((tm, tn), F32)] if nk > 1 else []
    if job is None:
        return pl.pallas_call(
            body, name=name, grid=grid, in_specs=in_specs, out_specs=o_spec, out_shape=o_shape,
            scratch_shapes=scratch, compiler_params=_cparams(("parallel", "parallel", "arbitrary")),
        )(*args)
    return _call_carrying(job, body, name, grid, in_specs, o_spec, o_shape, scratch, args)


def _call_carrying(job, body, name, grid, in_specs, out_specs, out_shape, scratch, args):
    body, in_specs, out_specs, out_shape, scratch, args, n_co = _carry(
        job, body, grid, in_specs, out_specs, out_shape, scratch, args)
    res = pl.pallas_call(
        body, name=name, grid=grid, in_specs=in_specs, out_specs=out_specs, out_shape=out_shape,
        scratch_shapes=scratch, compiler_params=_cparams(("arbitrary",) * len(grid)),
    )(*args)
    own = res[0] if n_co == 1 else list(res[:n_co])
    return own, list(res[n_co:])


def _proj_layout(D):
    DK, DP, R2 = D // 2, D // 2, 2 * GATE_RANK
    return [("q", 0, DK, 4 * D), ("k", DK, DK, 4 * D + DK), ("v", 2 * DK, D, 0), ("g", 2 * DK + D, D, D),
            ("a", 2 * DK + 2 * D, R2, None), ("p", 2 * DK + 2 * D + R2, DP, 5 * D),
            ("bg", 2 * DK + 2 * D + R2 + DP, 2 * D, 2 * D)]


RELAYOUT_ROWS = 64


def _w_in_operands(g, name):
    _, D, n = g.shape
    segs = _proj_layout(D)
    tr = RELAYOUT_ROWS

    def body(g_ref, main_ref, alr_ref):
        shard = [g_ref[j].astype(F32) for j in range(N_DEV)]

        def columns(a, b):
            parts = []
            for j in range(a // n, (b - 1) // n + 1):
                parts.append(shard[j][:, max(a, j * n) - j * n:min(b, (j + 1) * n) - j * n])
            return parts[0] if len(parts) == 1 else jnp.concatenate(parts, axis=1)

        for _, start, width, dst in segs:
            cols = columns(start, start + width)
            if dst is None:
                cols = jnp.concatenate([cols, jnp.zeros((tr, ALR_PAD - width), F32)], axis=1)
                alr_ref[...] = cols.astype(BF16)
            else:
                main_ref[:, dst:dst + width] = cols.astype(BF16)

    return pl.pallas_call(
        body, name=name, grid=(D // tr,),
        in_specs=[pl.BlockSpec((N_DEV, tr, n), lambda i: (0, i, 0))],
        out_specs=[pl.BlockSpec((tr, 11 * D // 2), lambda i: (i, 0)), pl.BlockSpec((tr, ALR_PAD), lambda i: (i, 0))],
        out_shape=[jax.ShapeDtypeStruct((D, 11 * D // 2), BF16), jax.ShapeDtypeStruct((D, ALR_PAD), BF16)],
        compiler_params=_cparams(("parallel",)),
    )(g)


def _w_in_chunks(g_main, g_alr, n, name):
    D = g_main.shape[0]
    segs = _proj_layout(D)
    tr = RELAYOUT_ROWS

    def body(main_ref, alr_ref, o_ref):
        main = main_ref[...].astype(F32)
        alr = alr_ref[...].astype(F32)
        for j in range(N_DEV):
            a, b = j * n, (j + 1) * n
            parts = []
            for _, start, width, dst in segs:
                lo, hi = max(a, start), min(b, start + width)
                if lo >= hi:
                    continue
                src = alr if dst is None else main
                off = 0 if dst is None else dst
                parts.append(src[:, off + lo - start:off + hi - start])
            o_ref[j] = (parts[0] if len(parts) == 1 else jnp.concatenate(parts, axis=1)).astype(BF16)

    return pl.pallas_call(
        body, name=name, grid=(D // tr,),
        in_specs=[pl.BlockSpec((tr, 11 * D // 2), lambda i: (i, 0)), pl.BlockSpec((tr, ALR_PAD), lambda i: (i, 0))],
        out_specs=pl.BlockSpec((N_DEV, tr, n), lambda i: (0, i, 0)),
        out_shape=jax.ShapeDtypeStruct((N_DEV, D, n), BF16),
        compiler_params=_cparams(("parallel",)),
    )(g_main, g_alr)


def _ffn_in_fwd(h2, w_fi, l, name, job=None):
    T, D = h2.shape
    n = w_fi.shape[3]
    nh = N_DEV // 2
    dff = nh * n
    tm = _pick(T, 600, 16)

    def body(a_ref, bg_ref, bu_ref, ff_ref, s_ref):
        a = a_ref[...]
        g = _dot(a, bg_ref[...])
        u = _dot(a, bu_ref[...])
        ff_ref[0] = g
        ff_ref[1] = u
        s_ref[...] = _swiglu_f(g, u).astype(BF16)

    grid = (T // tm, nh)
    in_specs = [pl.BlockSpec((tm, D), lambda i, j: (i, 0)),
                pl.BlockSpec((None, None, D, n), lambda i, j: (l, j, 0, 0)),
                pl.BlockSpec((None, None, D, n), lambda i, j: (l, nh + j, 0, 0))]
    out_specs = [pl.BlockSpec((2, tm, n), lambda i, j: (0, i, j)), pl.BlockSpec((tm, n), lambda i, j: (i, j))]
    out_shape = [jax.ShapeDtypeStruct((2, T, dff), F32), jax.ShapeDtypeStruct((T, dff), BF16)]
    args = (h2, w_fi, w_fi)
    if job is None:
        return pl.pallas_call(
            body, name=name, grid=grid, in_specs=in_specs, out_specs=out_specs, out_shape=out_shape,
            compiler_params=_cparams(("parallel", "parallel")),
        )(*args)
    return _call_carrying(job, body, name, grid, in_specs, out_specs, out_shape, [], args)


def _ffn_out_dx(dffn, w_fo, l, ff, name, job=None):
    T, D = dffn.shape
    dff = ff.shape[2]
    tm = _pick(T, 600, 16)
    tw = _pick(dff, 1408, 128)

    def body(a_ref, b_ref, ff_ref, o_ref):
        ds = _dot_nt(a_ref[...], b_ref[...])
        _, vjp = jax.vjp(_swiglu_f, ff_ref[0], ff_ref[1])
        dg, du = vjp(ds)
        o_ref[0] = dg.astype(BF16)
        o_ref[1] = du.astype(BF16)

    grid = (T // tm, dff // tw)
    in_specs = [pl.BlockSpec((tm, D), lambda i, j: (i, 0)),
                pl.BlockSpec((None, tw, D), lambda i, j: (l, j, 0)),
                pl.BlockSpec((2, tm, tw), lambda i, j: (0, i, j))]
    out_specs = pl.BlockSpec((2, tm, tw), lambda i, j: (0, i, j))
    out_shape = jax.ShapeDtypeStruct((2, T, dff), BF16)
    args = (dffn, w_fo, ff)
    if job is None:
        return pl.pallas_call(
            body, name=name, grid=grid, in_specs=in_specs, out_specs=out_specs, out_shape=out_shape,
            compiler_params=_cparams(("parallel", "parallel")),
        )(*args)
    return _call_carrying(job, body, name, grid, in_specs, out_specs, out_shape, [], args)


def _row(rb, w, col=0):
    return pl.BlockSpec((rb, w), lambda i: (i, col))


def _modspec(d, sec):
    return pl.BlockSpec((None, 1, d), lambda i: (jnp.minimum(i, 1), 0, sec))


def _vec(w):
    return pl.BlockSpec((1, w), lambda i: (0, 0))


def _acc2(w):
    return pl.BlockSpec((None, 1, w), lambda i: (jnp.minimum(i, 1), 0, 0))


def _accum(ref, val):
    i = pl.program_id(0)

    @pl.when(i <= 1)
    def _():
        ref[...] = val

    @pl.when(i > 1)
    def _():
        ref[...] += val


def _acc_shape(w):
    return jax.ShapeDtypeStruct((2, 1, w), F32)


def _mod_f(x, sc, sh):
    return x * (1.0 + sc) + sh


def _mod_fwd(xa, mod, sec_sc, sec_sh, rb, name):
    T, D = xa.shape

    def body(x_ref, sc_ref, sh_ref, h_ref):
        h_ref[...] = _mod_f(x_ref[...], sc_ref[...], sh_ref[...]).astype(BF16)

    return pl.pallas_call(
        body, name=name, grid=(T // rb,),
        in_specs=[_row(rb, D), _modspec(D, sec_sc), _modspec(D, sec_sh)],
        out_specs=_row(rb, D), out_shape=jax.ShapeDtypeStruct((T, D), BF16),
        compiler_params=_cparams(("parallel",)),
    )(xa, mod, mod)


def _mod_bwd(dxa, dh, xa, mod, sec_sc, sec_sh, rb, name):
    T, D = xa.shape

    def body(dxa_ref, dh_ref, x_ref, sc_ref, sh_ref, dx_ref, dsc_ref, dsh_ref):
        _, vjp = jax.vjp(_mod_f, x_ref[...], sc_ref[...], sh_ref[...])
        dx, dsc, dsh = vjp(dh_ref[...])
        dx_ref[...] = dxa_ref[...] + dx
        _accum(dsc_ref, dsc)
        _accum(dsh_ref, dsh)

    return pl.pallas_call(
        body, name=name, grid=(T // rb,),
        in_specs=[_row(rb, D), _row(rb, D), _row(rb, D), _modspec(D, sec_sc), _modspec(D, sec_sh)],
        out_specs=[_row(rb, D), _acc2(D), _acc2(D)],
        out_shape=[jax.ShapeDtypeStruct((T, D), F32), _acc_shape(D), _acc_shape(D)],
        compiler_params=_cparams(("arbitrary",)),
    )(dxa, dh, xa, mod, mod)


def _ln_f(alpha, x, mix, gt, gain, bias):
    z = alpha * x + gt * mix
    mu = jnp.mean(z, axis=-1, keepdims=True)
    zc = z - mu
    var = jnp.mean(zc * zc, axis=-1, keepdims=True)
    return zc * lax.rsqrt(var + LN_EPS) * gain + bias


def _unit_fwd(alpha, x, mix, mod, sec_gt, gain, bias, next_mod, rb, name):
    T, D = x.shape
    has_mod = next_mod is not None

    def body(*refs):
        if has_mod:
            x_ref, mix_ref, gt_ref, g_ref, b_ref, sc_ref, sh_ref, xo_ref, h_ref = refs
        else:
            x_ref, mix_ref, gt_ref, g_ref, b_ref, xo_ref = refs
        xo = _ln_f(alpha, x_ref[...], mix_ref[...], gt_ref[...], g_ref[...], b_ref[...])
        xo_ref[...] = xo
        if has_mod:
            h_ref[...] = _mod_f(xo, sc_ref[...], sh_ref[...]).astype(BF16)

    in_specs = [_row(rb, D), _row(rb, D), _modspec(D, sec_gt), _vec(D), _vec(D)]
    args = [x, mix, mod, gain, bias]
    out_specs = [_row(rb, D)]
    out_shape = [jax.ShapeDtypeStruct((T, D), F32)]
    if has_mod:
        nm, s_sc, s_sh = next_mod
        in_specs += [_modspec(D, s_sc), _modspec(D, s_sh)]
        args += [nm, nm]
        out_specs.append(_row(rb, D))
        out_shape.append(jax.ShapeDtypeStruct((T, D), BF16))
    res = pl.pallas_call(
        body, name=name, grid=(T // rb,), in_specs=in_specs, out_specs=out_specs, out_shape=out_shape,
        compiler_params=_cparams(("parallel",)),
    )(*args)
    return (res[0], res[1]) if has_mod else (res[0], None)


def _unit_bwd(alpha, dxo, dh, x, mix, mod, sec_gt, gain, bias, next_mod, rb, name, job=None):
    T, D = x.shape
    has_mod = next_mod is not None

    def body(*refs):
        if has_mod:
            (dxo_ref, dh_ref, x_ref, mix_ref, gt_ref, g_ref, b_ref, sc_ref, sh_ref,
             dx_ref, dmix_ref, dgt_ref, dg_ref, db_ref, dsc_ref, dsh_ref) = refs
        else:
            (dxo_ref, x_ref, mix_ref, gt_ref, g_ref, b_ref,
             dx_ref, dmix_ref, dgt_ref, dg_ref, db_ref) = refs
        xo, vjp = jax.vjp(functools.partial(_ln_f, alpha), x_ref[...], mix_ref[...], gt_ref[...],
                          g_ref[...], b_ref[...])
        dxo_t = dxo_ref[...]
        if has_mod:
            _, vjp_m = jax.vjp(_mod_f, xo, sc_ref[...], sh_ref[...])
            dxo_m, dsc, dsh = vjp_m(dh_ref[...])
            dxo_t = dxo_t + dxo_m
            _accum(dsc_ref, dsc)
            _accum(dsh_ref, dsh)
        dx, dmix, dgt, dg, db = vjp(dxo_t)
        dx_ref[...] = dx
        dmix_ref[...] = dmix.astype(BF16)
        _accum(dgt_ref, dgt)
        _accum(dg_ref, dg)
        _accum(db_ref, db)

    in_specs = [_row(rb, D)]
    args = [dxo]
    if has_mod:
        in_specs.append(_row(rb, D))
        args.append(dh)
    in_specs += [_row(rb, D), _row(rb, D), _modspec(D, sec_gt), _vec(D), _vec(D)]
    args += [x, mix, mod, gain, bias]
    out_specs = [_row(rb, D), _row(rb, D), _acc2(D), _acc2(D), _acc2(D)]
    out_shape = [jax.ShapeDtypeStruct((T, D), F32), jax.ShapeDtypeStruct((T, D), BF16),
                 _acc_shape(D), _acc_shape(D), _acc_shape(D)]
    if has_mod:
        nm, s_sc, s_sh = next_mod
        in_specs += [_modspec(D, s_sc), _modspec(D, s_sh)]
        args += [nm, nm]
        out_specs += [_acc2(D), _acc2(D)]
        out_shape += [_acc_shape(D), _acc_shape(D)]
    if job is None:
        res = pl.pallas_call(
            body, name=name, grid=(T // rb,), in_specs=in_specs, out_specs=out_specs, out_shape=out_shape,
            compiler_params=_cparams(("arbitrary",)),
        )(*args)
        job_res = None
    else:
        res, job_res = _call_carrying(job, body, name, (T // rb,), in_specs, out_specs, out_shape, [], args)
    res = list(res) if has_mod else list(res) + [None, None]
    return res if job is None else (res, job_res)


def _log_sigmoid(z):
    return jnp.minimum(z, 0.0) - jnp.log(1.0 + jnp.exp(-jnp.abs(z)))


def _decay_fwd(alr, wdu, bdu, rb, name):
    T = alr.shape[0]
    W = wdu.shape[1]

    def body(a_ref, w_ref, b_ref, la_ref):
        z = _dot(a_ref[...].astype(BF16), w_ref[...]) + b_ref[...]
        la_ref[...] = _log_sigmoid(z) * (1.0 / GATE_NORM)

    return pl.pallas_call(
        body, name=name, grid=(T // rb,),
        in_specs=[_row(rb, ALR_PAD), pl.BlockSpec((ALR_PAD, W), lambda i: (0, 0)), _vec(W)],
        out_specs=_row(rb, W), out_shape=jax.ShapeDtypeStruct((T, W), F32),
        compiler_params=_cparams(("parallel",)),
    )(alr, wdu, bdu)


def _decay_bwd(dla_f, dla_b, alr, wdu, bdu, rb, name):
    T = alr.shape[0]
    W = wdu.shape[1]
    DK = W // 2

    def body(df_ref, db_ref, a_ref, w_ref, b_ref, dalr_ref, gw_ref, gb_ref):
        i = pl.program_id(0)
        ab = a_ref[...].astype(BF16)
        z = _dot(ab, w_ref[...]) + b_ref[...]
        dla = jnp.concatenate([df_ref[...], db_ref[...]], axis=1)
        dz = dla * _sig(-z) * (1.0 / GATE_NORM)
        dzb = dz.astype(BF16)
        dalr_ref[...] = _dot_nt(dzb, w_ref[...]).astype(BF16)
        gw = _dot_tn(ab, dzb)
        gb = jnp.sum(dz, axis=0, keepdims=True)

        @pl.when(i == 0)
        def _():
            gw_ref[...] = gw
            gb_ref[...] = gb

        @pl.when(i > 0)
        def _():
            gw_ref[...] += gw
            gb_ref[...] += gb

    return pl.pallas_call(
        body, name=name, grid=(T // rb,),
        in_specs=[_row(rb, DK), _row(rb, DK), _row(rb, ALR_PAD), pl.BlockSpec((ALR_PAD, W), lambda i: (0, 0)), _vec(W)],
        out_specs=[_row(rb, ALR_PAD), pl.BlockSpec((ALR_PAD, W), lambda i: (0, 0)), _vec(W)],
        out_shape=[jax.ShapeDtypeStruct((T, ALR_PAD), BF16), jax.ShapeDtypeStruct((ALR_PAD, W), F32),
                   jax.ShapeDtypeStruct((1, W), F32)],
        compiler_params=_cparams(("arbitrary",)),
    )(dla_f, dla_b, alr, wdu, bdu)


def _tri(rev, ncb):
    m = np.tril(np.ones((CHUNK, CHUNK), np.float32))
    return jnp.asarray(np.kron(np.eye(ncb, dtype=np.float32), m.T if rev else m), BF16)


def _gla_block_common(q_ref, k_ref, v_ref, la_ref, tri_ref, ck, cv, rev, scale_q, ncb):
    mid = CHUNK // 2 if rev else CHUNK // 2 - 1
    last_i = 0 if rev else CHUNK - 1
    rb = ncb * CHUNK
    hk = ck.stop - ck.start
    q = q_ref[:, ck] * scale_q
    k = k_ref[:, ck]
    v = v_ref[:, cv]
    cum = _dot3(tri_ref[...], la_ref[:, ck])
    per_chunk = lambda i: jnp.concatenate(
        [jnp.broadcast_to(cum[c * CHUNK + i:c * CHUNK + i + 1, :], (CHUNK, hk)) for c in range(ncb)], axis=0)
    ref, last = per_chunk(mid), per_chunk(last_i)
    e_q = jnp.exp(cum - ref)
    e_k = jnp.exp(ref - cum)
    e_c = jnp.exp(cum)
    e_s = jnp.exp(last - cum)
    e_l = [jnp.exp(cum[c * CHUNK + last_i:c * CHUNK + last_i + 1, :]) for c in range(ncb)]
    ri = lax.broadcasted_iota(jnp.int32, (rb, rb), 0)
    ci = lax.broadcasted_iota(jnp.int32, (rb, rb), 1)
    mask = (ri // CHUNK == ci // CHUNK) & ((ci >= ri) if rev else (ci <= ri))
    return q, k, v, e_q, e_k, e_c, e_s, e_l, mask, last_i


GLA_HEADS_PER_STEP = 1


def _gla_specs(rb, hk, hv, D, rbmap, rev):
    hp = GLA_HEADS_PER_STEP
    q_col0 = 4 * D // (hp * hk)
    k_col0 = q_col0 + N_HEADS // hp
    la_col0 = N_HEADS // hp if rev else 0
    return [
        pl.BlockSpec((rb, hp * hk), lambda h, i: (rbmap(i), q_col0 + h)),
        pl.BlockSpec((rb, hp * hk), lambda h, i: (rbmap(i), k_col0 + h)),
        pl.BlockSpec((rb, hp * hv), lambda h, i: (rbmap(i), h)),
        pl.BlockSpec((rb, hp * hk), lambda h, i: (rbmap(i), la_col0 + h)),
        pl.BlockSpec((rb, rb), lambda h, i: (0, 0)),
    ]


def _gla_call(job, body, name, grid, in_specs, out_specs, out_shape, scratch, args):
    if job is None:
        return pl.pallas_call(
            body, name=name, grid=grid, in_specs=in_specs, out_specs=out_specs, out_shape=out_shape,
            scratch_shapes=scratch, compiler_params=_cparams(("parallel", "arbitrary")),
        )(*args)
    return _call_carrying(job, body, name, grid, in_specs, out_specs, out_shape, scratch, args)


def _gla_fwd(proj, la, rev, rb, D, name, job=None):
    T = proj.shape[0]
    nb = T // rb
    ncb = rb // CHUNK
    hp = GLA_HEADS_PER_STEP
    hk, hv = D // 2 // N_HEADS, D // N_HEADS
    scale_q = float(hk) ** -0.5
    rbmap = (lambda i: jnp.where(i == 0, 0, nb - i)) if rev else (lambda i: i)

    def body(q_ref, k_ref, v_ref, la_ref, tri_ref, o_ref, s_ref, st_ref):
        @pl.when(pl.program_id(1) == 0)
        def _():
            st_ref[...] = jnp.zeros_like(st_ref)

        order = range(ncb - 1, -1, -1) if rev else range(ncb)
        for hh in range(hp):
            ck, cv = slice(hh * hk, (hh + 1) * hk), slice(hh * hv, (hh + 1) * hv)
            q, k, v, e_q, e_k, e_c, e_s, e_l, mask, _ = _gla_block_common(
                q_ref, k_ref, v_ref, la_ref, tri_ref, ck, cv, rev, scale_q, ncb)
            vb = v.astype(BF16)
            a = jnp.where(mask, _dot_nt((q * e_q).astype(BF16), (k * e_k).astype(BF16)), 0.0)
            o_intra = _dot(a.astype(BF16), vb)
            qc = (q * e_c).astype(BF16)
            ks = (k * e_s).astype(BF16)
            st = st_ref[hh]
            for cc in order:
                rows = slice(cc * CHUNK, (cc + 1) * CHUNK)
                s_ref[hh, cc] = st
                o_ref[rows, cv] = o_intra[rows] + _dot_nt(qc[rows], st.astype(BF16))
                st = st * e_l[cc] + _dot_tn(vb[rows], ks[rows])
            st_ref[hh] = st

    return _gla_call(
        job, body, name, (N_HEADS // hp, nb), _gla_specs(rb, hk, hv, D, rbmap, rev),
        [pl.BlockSpec((rb, hp * hv), lambda h, i: (rbmap(i), h)),
         pl.BlockSpec((hp, ncb, hv, hk), lambda h, i: (h, rbmap(i), 0, 0))],
        [jax.ShapeDtypeStruct((T, D), F32), jax.ShapeDtypeStruct((N_HEADS, T // CHUNK, hv, hk), F32)],
        [pltpu.VMEM((hp, hv, hk), F32)], (proj, proj, proj, la, _tri(rev, ncb)))


def _gla_bwd(proj, la, do, states, rev, rb, D, prev, name, job=None):
    T = proj.shape[0]
    nb = T // rb
    ncb = rb // CHUNK
    hp = GLA_HEADS_PER_STEP
    hk, hv = D // 2 // N_HEADS, D // N_HEADS
    DK = D // 2
    scale_q = float(hk) ** -0.5
    if rev:
        rbmap = lambda i: jnp.where(i == nb - 1, 0, i + 1)
    else:
        rbmap = lambda i: nb - 1 - i
    has_prev = prev is not None
    out_dt = BF16 if has_prev else F32

    def body(*refs):
        q_ref, k_ref, v_ref, la_ref, tri_ref, trit_ref, do_ref, s_ref = refs[:8]
        n_in = 11 if has_prev else 8
        pq_ref, pk_ref, pv_ref = refs[8:11] if has_prev else (None, None, None)
        dq_ref, dk_ref, dv_ref, dla_ref, ds_ref = refs[n_in:]

        @pl.when(pl.program_id(1) == 0)
        def _():
            ds_ref[...] = jnp.zeros_like(ds_ref)

        order = range(ncb) if rev else range(ncb - 1, -1, -1)
        for hh in range(hp):
            ck, cv = slice(hh * hk, (hh + 1) * hk), slice(hh * hv, (hh + 1) * hv)
            q, k, v, e_q, e_k, e_c, e_s, e_l, mask, last_i = _gla_block_common(
                q_ref, k_ref, v_ref, la_ref, tri_ref, ck, cv, rev, scale_q, ncb)
            vb = v.astype(BF16)
            qi = (q * e_q).astype(BF16)
            ki = (k * e_k).astype(BF16)
            qc = (q * e_c).astype(BF16)
            ks = (k * e_s).astype(BF16)
            a = jnp.where(mask, _dot_nt(qi, ki), 0.0).astype(BF16)
            dob = do_ref[:, cv].astype(BF16)
            da = jnp.where(mask, _dot_nt(dob, vb), 0.0).astype(BF16)
            dv_intra = _dot_tn(a, dob)
            dq_intra = _dot(da, ki) * e_q
            dk_intra = _dot_tn(da, qi) * e_k
            rowi = lax.broadcasted_iota(jnp.int32, (CHUNK, hk), 0)
            dst = ds_ref[hh]
            for cc in order:
                rows = slice(cc * CHUNK, (cc + 1) * CHUNK)
                st0 = s_ref[hh, cc]
                dstb = dst.astype(BF16)
                dv = dv_intra[rows] + _dot_nt(ks[rows], dstb)
                dk_inter = _dot(vb[rows], dstb) * e_s[rows]
                dq_s = dq_intra[rows] + _dot(dob[rows], st0.astype(BF16)) * e_c[rows]
                dk = dk_intra[rows] + dk_inter
                extra = (jnp.sum(k[rows] * dk_inter, axis=0, keepdims=True)
                         + e_l[cc] * jnp.sum(dst * st0, axis=0, keepdims=True))
                dla_ref[rows, ck] = q[rows] * dq_s - k[rows] * dk + jnp.where(rowi == last_i, extra, 0.0)
                dq = dq_s * scale_q
                if has_prev:
                    dq = dq + pq_ref[rows, ck]
                    dk = dk + pk_ref[rows, ck]
                    dv = dv + pv_ref[rows, cv]
                dq_ref[rows, ck] = dq.astype(out_dt)
                dk_ref[rows, ck] = dk.astype(out_dt)
                dv_ref[rows, cv] = dv.astype(out_dt)
                dst = dst * e_l[cc] + _dot_tn(dob[rows], qc[rows])
            ds_ref[hh] = dst
            dla_ref[:, ck] = _dot3(trit_ref[...], dla_ref[:, ck])

    in_specs = _gla_specs(rb, hk, hv, D, rbmap, rev)
    in_specs += [pl.BlockSpec((rb, rb), lambda h, i: (0, 0)),
                 pl.BlockSpec((rb, hp * hv), lambda h, i: (rbmap(i), h)),
                 pl.BlockSpec((hp, ncb, hv, hk), lambda h, i: (h, rbmap(i), 0, 0))]
    args = [proj, proj, proj, la, _tri(rev, ncb), _tri(not rev, ncb), do, states]
    hk_spec = pl.BlockSpec((rb, hp * hk), lambda h, i: (rbmap(i), h))
    hv_spec = pl.BlockSpec((rb, hp * hv), lambda h, i: (rbmap(i), h))
    if has_prev:
        in_specs += [hk_spec, hk_spec, hv_spec]
        args += list(prev)
    return _gla_call(
        job, body, name, (N_HEADS // hp, nb), in_specs, [hk_spec, hk_spec, hv_spec, hk_spec],
        [jax.ShapeDtypeStruct((T, DK), out_dt), jax.ShapeDtypeStruct((T, DK), out_dt),
         jax.ShapeDtypeStruct((T, D), out_dt), jax.ShapeDtypeStruct((T, DK), F32)],
        [pltpu.VMEM((hp, hv, hk), F32)], args)


def _glaout_f(of, ob, g, gain):
    o = of + ob
    n = o * lax.rsqrt(jnp.mean(o * o, axis=-1, keepdims=True) + RMS_EPS)
    return n * gain * _silu(g)


def _glaout_fwd(o_f, o_b, proj, gain, rb, name):
    T, D = o_f.shape
    hv = D // N_HEADS

    def body(of_ref, ob_ref, g_ref, gn_ref, u_ref):
        for h in range(N_HEADS):
            cs = slice(h * hv, (h + 1) * hv)
            u_ref[:, cs] = _glaout_f(of_ref[:, cs], ob_ref[:, cs], g_ref[:, cs], gn_ref[:, cs]).astype(BF16)

    return pl.pallas_call(
        body, name=name, grid=(T // rb,),
        in_specs=[_row(rb, D), _row(rb, D), _row(rb, D, 1), _vec(D)],
        out_specs=_row(rb, D), out_shape=jax.ShapeDtypeStruct((T, D), BF16),
        compiler_params=_cparams(("parallel",)),
    )(o_f, o_b, proj, gain)


def _glaout_bwd(du, o_f, o_b, proj, gain, rb, name):
    T, D = o_f.shape
    hv = D // N_HEADS

    def body(du_ref, of_ref, ob_ref, g_ref, gn_ref, do_ref, dg_ref, dgn_ref, tmp_ref):
        for h in range(N_HEADS):
            cs = slice(h * hv, (h + 1) * hv)
            _, vjp = jax.vjp(_glaout_f, of_ref[:, cs], ob_ref[:, cs], g_ref[:, cs], gn_ref[:, cs])
            d_of, _, dg, dgn = vjp(du_ref[:, cs])
            do_ref[:, cs] = d_of
            dg_ref[:, cs] = dg.astype(BF16)
            tmp_ref[:, cs] = dgn
        _accum(dgn_ref, tmp_ref[...])

    return pl.pallas_call(
        body, name=name, grid=(T // rb,),
        in_specs=[_row(rb, D), _row(rb, D), _row(rb, D), _row(rb, D, 1), _vec(D)],
        out_specs=[_row(rb, D), _row(rb, D), _acc2(D)],
        out_shape=[jax.ShapeDtypeStruct((T, D), F32), jax.ShapeDtypeStruct((T, D), BF16), _acc_shape(D)],
        scratch_shapes=[pltpu.VMEM((1, D), F32)],
        compiler_params=_cparams(("arbitrary",)),
    )(du, o_f, o_b, proj, gain)


def _merge_f(bg1, bg2, yg, yp):
    return _sig(bg1) * yg + _sig(bg2) * yp


def _merge_fwd(proj, y_gla, y_pool, rb, name):
    T, D = y_gla.shape

    def body(b1_ref, b2_ref, yg_ref, yp_ref, m_ref):
        m_ref[...] = _merge_f(b1_ref[...], b2_ref[...], yg_ref[...], yp_ref[...]).astype(BF16)

    return pl.pallas_call(
        body, name=name, grid=(T // rb,),
        in_specs=[_row(rb, D, 2), _row(rb, D, 3), _row(rb, D), _row(rb, D)],
        out_specs=_row(rb, D), out_shape=jax.ShapeDtypeStruct((T, D), BF16),
        compiler_params=_cparams(("parallel",)),
    )(proj, proj, y_gla, y_pool)


def _merge_bwd(dm, proj, y_gla, y_pool, rb, name):
    T, D = y_gla.shape

    def body(dm_ref, b1_ref, b2_ref, yg_ref, yp_ref, d1_ref, d2_ref, dyg_ref, dyp_ref):
        _, vjp = jax.vjp(_merge_f, b1_ref[...], b2_ref[...], yg_ref[...], yp_ref[...])
        d1, d2, dyg, dyp = vjp(dm_ref[...])
        d1_ref[...] = d1.astype(BF16)
        d2_ref[...] = d2.astype(BF16)
        dyg_ref[...] = dyg.astype(BF16)
        dyp_ref[...] = dyp.astype(BF16)

    return pl.pallas_call(
        body, name=name, grid=(T // rb,),
        in_specs=[_row(rb, D), _row(rb, D, 2), _row(rb, D, 3), _row(rb, D), _row(rb, D)],
        out_specs=[_row(rb, D)] * 4, out_shape=[jax.ShapeDtypeStruct((T, D), BF16)] * 4,
        compiler_params=_cparams(("parallel",)),
    )(dm, proj, proj, y_gla, y_pool)


def _swiglu_f(gate, up):
    return _silu(gate) * up


def _pool_consts(ctx_len, seq):
    rows = seq // GRID_W
    reps = POOL_TB // GRID_W
    mw, bc, cw, ch, cc = [], [], [], [], []
    for w in POOL_WINDOWS:
        lo, hi = w // 2, w - w // 2 - 1

        def band(n):
            i = np.arange(n)[:, None]
            j = np.arange(n)[None, :]
            return ((j - i >= -lo) & (j - i <= hi)).astype(np.float32)

        def count(n):
            i = np.arange(n)
            return (np.minimum(i + hi + 1, n) - np.maximum(i - lo, 0)).astype(np.float32)

        mw.append(np.kron(np.eye(reps, dtype=np.float32), band(GRID_W)))
        bc.append(band(ctx_len))
        cw.append(np.tile(count(GRID_W), reps)[:, None])
        ch.append(np.repeat(count(rows), GRID_W)[:, None])
        cc.append(count(ctx_len)[:, None])
    mw, bc = np.stack(mw), np.stack(bc)
    return dict(
        mw=jnp.asarray(mw, BF16), mwt=jnp.asarray(mw.transpose(0, 2, 1), BF16),
        bc=jnp.asarray(bc, BF16), bct=jnp.asarray(bc.transpose(0, 2, 1), BF16),
        cw=jnp.asarray(np.stack(cw)), ch=jnp.asarray(np.stack(ch)), cc=jnp.asarray(np.stack(cc)))


def _gspec(*shape):
    nd = len(shape)
    return pl.BlockSpec((None,) + tuple(shape), lambda g: (g,) + (0,) * nd)


def _pool_fwd(proj, pc, wg, scale, ctx_len, D, name):
    T = proj.shape[0]
    seq = T - ctx_len
    dp = D // 2
    pg = dp // len(POOL_WINDOWS)
    nblk = seq // POOL_TB
    padt = POOL_PAD_ROWS * GRID_W
    p_col0 = 5 * D // pg

    def body(p_ref, mw_ref, bc_ref, cw_ref, ch_ref, cc_ref, wg_ref, sc_ref, pd_ref, y0_ref, r_ref, pad_ref):
        g = pl.program_id(0)

        def tail(rows, mean, x):
            pdb = (mean - x).astype(BF16)
            y0 = _dot(pdb, wg_ref[...])
            pd_ref[rows, :] = pdb
            y0_ref[rows, :] = y0
            r_ref[rows, :] = (y0 * sc_ref[...]).astype(BF16)

        xc = p_ref[0:ctx_len, :]
        tail(slice(0, ctx_len), _dot2(bc_ref[...], xc) / cc_ref[...], xc)

        pad_ref[0:padt, :] = jnp.zeros((padt, pg), F32)
        pad_ref[padt + seq:, :] = jnp.zeros((padt, pg), F32)

        def wpass(b, carry):
            rows = pl.ds(pl.multiple_of(ctx_len + b * POOL_TB, CHUNK), POOL_TB)
            dst = pl.ds(pl.multiple_of(padt + b * POOL_TB, CHUNK), POOL_TB)
            pad_ref[dst, :] = _dot2(mw_ref[...], p_ref[rows, :]) / cw_ref[...]
            return carry

        lax.fori_loop(0, nblk, wpass, 0)

        for gi, w in enumerate(POOL_WINDOWS):
            lo, hi = w // 2, w - w // 2 - 1

            @pl.when(g == gi)
            def _():
                def hpass(b, carry):
                    acc = jnp.zeros((POOL_TB, pg), F32)
                    for d in range(-lo, hi + 1):
                        src = pl.ds(pl.multiple_of(padt + b * POOL_TB + d * GRID_W, CHUNK), POOL_TB)
                        acc = acc + pad_ref[src, :]
                    mean = acc / ch_ref[pl.ds(pl.multiple_of(b * POOL_TB, CHUNK), POOL_TB), :]
                    rows = pl.ds(pl.multiple_of(ctx_len + b * POOL_TB, CHUNK), POOL_TB)
                    tail(rows, mean, p_ref[rows, :])
                    return carry

                lax.fori_loop(0, nblk, hpass, 0)

    col = lambda g: (0, g)
    return pl.pallas_call(
        body, name=name, grid=(len(POOL_WINDOWS),),
        in_specs=[pl.BlockSpec((T, pg), lambda g: (0, p_col0 + g)),
                  _gspec(POOL_TB, POOL_TB), _gspec(ctx_len, ctx_len), _gspec(POOL_TB, 1), _gspec(seq, 1),
                  _gspec(ctx_len, 1), _gspec(pg, pg), pl.BlockSpec((1, pg), col)],
        out_specs=[pl.BlockSpec((T, pg), col)] * 3,
        out_shape=[jax.ShapeDtypeStruct((T, dp), BF16), jax.ShapeDtypeStruct((T, dp), F32),
                   jax.ShapeDtypeStruct((T, dp), BF16)],
        scratch_shapes=[pltpu.VMEM((seq + 2 * padt, pg), F32)],
        compiler_params=_cparams(("arbitrary",)),
    )(proj, pc["mw"], pc["bc"], pc["cw"], pc["ch"], pc["cc"], wg, scale)


def _pool_bwd(dr, y0, pd, pc, wg, scale, ctx_len, D, name):
    T = dr.shape[0]
    seq = T - ctx_len
    dp = D // 2
    ng = len(POOL_WINDOWS)
    pg = dp // ng
    nblk = seq // POOL_TB
    padt = POOL_PAD_ROWS * GRID_W

    def body(dr_ref, y0_ref, pd_ref, mwt_ref, bct_ref, cw_ref, ch_ref, cc_ref, wg_ref, sc_ref,
             dp_ref, dsc_ref, gwg_ref, pad_ref, dpd_ref):
        g = pl.program_id(0)
        dsc_ref[...] = jnp.zeros_like(dsc_ref)
        gwg_ref[...] = jnp.zeros_like(gwg_ref)

        def head(rows):
            drv = dr_ref[rows, :]
            dsc_ref[...] += jnp.sum(drv * y0_ref[rows, :], axis=0, keepdims=True)
            dy0 = (drv * sc_ref[...]).astype(BF16)
            gwg_ref[...] += _dot_tn(pd_ref[rows, :], dy0)
            return _dot_nt(dy0, wg_ref[...])

        crow = slice(0, ctx_len)
        dpd_c = head(crow)
        dp_ref[crow, :] = (_dot2(bct_ref[...], dpd_c / cc_ref[...]) - dpd_c).astype(BF16)

        pad_ref[0:padt, :] = jnp.zeros((padt, pg), F32)
        pad_ref[padt + seq:, :] = jnp.zeros((padt, pg), F32)

        def first(b, carry):
            rows = pl.ds(pl.multiple_of(ctx_len + b * POOL_TB, CHUNK), POOL_TB)
            lrows = pl.ds(pl.multiple_of(b * POOL_TB, CHUNK), POOL_TB)
            dst = pl.ds(pl.multiple_of(padt + b * POOL_TB, CHUNK), POOL_TB)
            dpd = head(rows)
            dpd_ref[lrows, :] = dpd
            pad_ref[dst, :] = dpd / ch_ref[lrows, :]
            return carry

        lax.fori_loop(0, nblk, first, 0)

        for gi, w in enumerate(POOL_WINDOWS):
            lo, hi = w // 2, w - w // 2 - 1

            @pl.when(g == gi)
            def _():
                def second(b, carry):
                    acc = jnp.zeros((POOL_TB, pg), F32)
                    for d in range(-hi, lo + 1):
                        src = pl.ds(pl.multiple_of(padt + b * POOL_TB + d * GRID_W, CHUNK), POOL_TB)
                        acc = acc + pad_ref[src, :]
                    rows = pl.ds(pl.multiple_of(ctx_len + b * POOL_TB, CHUNK), POOL_TB)
                    lrows = pl.ds(pl.multiple_of(b * POOL_TB, CHUNK), POOL_TB)
                    dx = _dot2(mwt_ref[...], acc / cw_ref[...]) - dpd_ref[lrows, :]
                    dp_ref[rows, :] = dx.astype(BF16)
                    return carry

                lax.fori_loop(0, nblk, second, 0)

    col = lambda g: (0, g)
    return pl.pallas_call(
        body, name=name, grid=(ng,),
        in_specs=[pl.BlockSpec((T, pg), col), pl.BlockSpec((T, pg), col), pl.BlockSpec((T, pg), col),
                  _gspec(POOL_TB, POOL_TB), _gspec(ctx_len, ctx_len), _gspec(POOL_TB, 1), _gspec(seq, 1),
                  _gspec(ctx_len, 1), _gspec(pg, pg), pl.BlockSpec((1, pg), col)],
        out_specs=[pl.BlockSpec((T, pg), col), pl.BlockSpec((1, pg), col), _gspec(pg, pg)],
        out_shape=[jax.ShapeDtypeStruct((T, dp), BF16), jax.ShapeDtypeStruct((1, dp), F32),
                   jax.ShapeDtypeStruct((ng, pg, pg), F32)],
        scratch_shapes=[pltpu.VMEM((seq + 2 * padt, pg), F32), pltpu.VMEM((seq, pg), F32)],
        compiler_params=_cparams(("arbitrary",)),
    )(dr, y0, pd, pc["mwt"], pc["bct"], pc["cw"], pc["ch"], pc["cc"], wg, scale)


def _loss_head(x2, target, rb, name):
    T, D = x2.shape

    def body(y_ref, t_ref, dy_ref, l_ref):
        i = pl.program_id(0)

        @pl.when(i == 0)
        def _():
            dy_ref[...] = jnp.zeros_like(dy_ref)
            l_ref[...] = jnp.zeros_like(l_ref)

        @pl.when(i > 0)
        def _():
            e = y_ref[...] - t_ref[...]
            dy_ref[...] = e * (1.0 / D)
            l_ref[...] += 0.5 * jnp.sum(jnp.mean(e * e, axis=-1, keepdims=True), axis=0, keepdims=True)

    return pl.pallas_call(
        body, name=name, grid=(T // rb,),
        in_specs=[_row(rb, D), pl.BlockSpec((rb, D), lambda i: (jnp.maximum(i - 1, 0), 0))],
        out_specs=[_row(rb, D), pl.BlockSpec((8, 128), lambda i: (0, 0))],
        out_shape=[jax.ShapeDtypeStruct((T, D), F32), jax.ShapeDtypeStruct((8, 128), F32)],
        compiler_params=_cparams(("arbitrary",)),
    )(x2, target)


def _sum_lead(x, name):
    S, R, C = x.shape

    def body(x_ref, o_ref):
        acc = x_ref[0]
        for s in range(1, S):
            acc = acc + x_ref[s]
        o_ref[...] = acc

    return pl.pallas_call(
        body, name=name, out_shape=jax.ShapeDtypeStruct((R, C), F32),
        compiler_params=_cparams(),
    )(x)


def _silu_rows(cond, name):
    def body(c_ref, o_ref):
        o_ref[...] = _silu(c_ref[...]).astype(BF16)

    return pl.pallas_call(body, name=name, out_shape=jax.ShapeDtypeStruct(cond.shape, BF16),
                          compiler_params=_cparams())(cond)


def _silu_grad(cond, ds, name):
    def body(c_ref, d_ref, o_ref):
        _, vjp = jax.vjp(_silu, c_ref[...])
        o_ref[...] = vjp(d_ref[...])[0]

    return pl.pallas_call(body, name=name, out_shape=jax.ShapeDtypeStruct(cond.shape, F32),
                          compiler_params=_cparams())(cond, ds)


def _adamw_math(g, w_ref, m_ref, v_ref, go_ref, d_ref, mo_ref, vo_ref):
    c1 = 1.0 / (1.0 - ADAM_B1 ** ADAM_STEP)
    c2 = 1.0 / (1.0 - ADAM_B2 ** ADAM_STEP)
    mn = ADAM_B1 * m_ref[...] + (1.0 - ADAM_B1) * g
    vn = ADAM_B2 * v_ref[...] + (1.0 - ADAM_B2) * (g * g)
    go_ref[...] = g
    mo_ref[...] = mn
    vo_ref[...] = vn
    d_ref[...] = -ADAM_LR * ((mn * c1) / (jnp.sqrt(vn * c2) + ADAM_EPS) + ADAM_WD * w_ref[...])


def _adamw(gs, w, m, v, name, job=None):
    S, R, C = gs.shape
    cpad = -(-C // 128) * 128
    rt = _pick(R, max(16, (1 << 20) // (4 * cpad)), 16)

    def body(g_ref, w_ref, m_ref, v_ref, go_ref, d_ref, mo_ref, vo_ref):
        g = g_ref[0].astype(F32)
        for s in range(1, S):
            g = g + g_ref[s].astype(F32)
        _adamw_math(g, w_ref, m_ref, v_ref, go_ref, d_ref, mo_ref, vo_ref)

    blk = pl.BlockSpec((rt, C), lambda i: (i, 0))
    in_specs = [pl.BlockSpec((S, rt, C), lambda i: (0, i, 0)), blk, blk, blk]
    out_shape = [jax.ShapeDtypeStruct((R, C), F32)] * 4
    if job is None:
        return pl.pallas_call(
            body, name=name, grid=(R // rt,), in_specs=in_specs, out_specs=[blk] * 4, out_shape=out_shape,
            compiler_params=_cparams(("parallel",)),
        )(gs, w, m, v)
    return _call_carrying(job, body, name, (R // rt,), in_specs, [blk] * 4, out_shape, [], (gs, w, m, v))


def _adamw_layer(gs, w, m, v, l, prev, name):
    S, R, C = gs.shape
    L = w.shape[0]
    cpad = -(-C // 128) * 128
    rt = _pick(R, max(16, (1 << 20) // (4 * cpad)), 16)

    def body(*refs):
        g_ref, w_ref, m_ref, v_ref = refs[:4]
        go_ref, d_ref, mo_ref, vo_ref = refs[-4:]
        g = g_ref[0].astype(F32)
        for s in range(1, S):
            g = g + g_ref[s].astype(F32)
        _adamw_math(g, w_ref, m_ref, v_ref, go_ref, d_ref, mo_ref, vo_ref)

    blk = pl.BlockSpec((None, rt, C), lambda i: (l, i, 0))
    in_specs = [pl.BlockSpec((S, rt, C), lambda i: (0, i, 0)), blk, blk, blk]
    args = [gs, w, m, v]
    aliases = {}
    if prev is not None:
        in_specs += [pl.BlockSpec(memory_space=pl.ANY)] * 4
        args += list(prev)
        aliases = {4 + q: q for q in range(4)}
    return pl.pallas_call(
        body, name=name, grid=(R // rt,), in_specs=in_specs,
        out_specs=[blk] * 4, out_shape=[jax.ShapeDtypeStruct((L, R, C), F32)] * 4,
        input_output_aliases=aliases,
        compiler_params=_cparams(("parallel",)),
    )(*args)


def _adamw_nd(gs, w, m, v, name, job=None):
    shp = w.shape
    if len(shp) == 1:
        r, c = 1, shp[0]
    else:
        r, c = int(np.prod(shp[:-1])), shp[-1]
    outs = _adamw(gs.reshape(gs.shape[0], r, c), w.reshape(r, c), m.reshape(r, c), v.reshape(r, c), name, job)
    if job is None:
        return [o.reshape(shp) for o in outs]
    return [o.reshape(shp) for o in outs[0]], outs[1]


def kernel(x, c, ctx, c_ctx, w_ada, b_ada, w_in, w_decay_up, b_decay_up, gla_norm_gain, w_pool_group, pool_scale, w_gla_out, w_pool_out, w_out, ln_mix_gain, ln_mix_bias, w_ffn_in, w_ffn_out, ln_ffn_gain, ln_ffn_bias, loss_target, m_c_ctx, m_w_ada, m_b_ada, m_w_in, m_w_decay_up, m_b_decay_up, m_gla_norm_gain, m_w_pool_group, m_pool_scale, m_w_gla_out, m_w_pool_out, m_w_out, m_ln_mix_gain, m_ln_mix_bias, m_w_ffn_in, m_w_ffn_out, m_ln_ffn_gain, m_ln_ffn_bias, v_c_ctx, v_w_ada, v_b_ada, v_w_in, v_w_decay_up, v_b_decay_up, v_gla_norm_gain, v_w_pool_group, v_pool_scale, v_w_gla_out, v_w_pool_out, v_w_out, v_ln_mix_gain, v_ln_mix_bias, v_w_ffn_in, v_w_ffn_out, v_ln_ffn_gain, v_ln_ffn_bias):
    L, D = w_ada.shape[0], w_ada.shape[1]
    seq, ctx_len = x.shape[1], ctx.shape[1]
    T = seq + ctx_len
    rb = ctx_len
    DK = D // 2
    DP = D // 2
    ng = len(POOL_WINDOWS)
    pg = DP // ng
    dff = w_ffn_out.shape[1] * N_DEV
    alpha = (2.0 * L) ** 0.25
    assert seq % rb == 0 and rb % CHUNK == 0 and seq % POOL_TB == 0 and ctx_len % 8 == 0
    xi, yi, ci = _my_pos()
    me = 4 * xi + 2 * yi + ci
    pc = _pool_consts(ctx_len, seq)

    shards = dict(w_in=w_in.astype(BF16), go=w_gla_out.astype(BF16), po=w_pool_out.astype(BF16),
                  out=w_out.astype(BF16), fi=w_ffn_in.astype(BF16), fo=w_ffn_out.astype(BF16),
                  pg=w_pool_group.astype(BF16).reshape(L, ng * pg // N_DEV, pg))
    wkeys = ("w_in", "go", "po", "out", "fi", "fo", "pg")

    def prepared(gw):
        W = {}
        if "w_in" in gw:
            W["main"], W["alr"] = _w_in_operands(gw["w_in"], "w_in_operands")
            W["pg"] = jnp.swapaxes(gw["pg"].reshape(N_DEV, ng, pg // N_DEV, pg), 0, 1).reshape(ng, pg, pg)
            W.update(go=gw["go"].reshape(D, D), po=gw["po"], out=gw["out"].reshape(D, D))
        if "fi" in gw:
            W["fi"] = gw["fi"][None]
        if "fo" in gw:
            W["fo"] = gw["fo"].reshape(1, dff, D)
        return W

    def gather_next(fn, names, l, nxt, own=(), cur=None):
        keys = [(k, l + 1) for k in names if l + 1 < L] + [(k, l) for k in own]
        if not keys:
            return fn(None)
        res, outs = fn(_gather_job([shards[k] for k, _ in keys], [ll for _, ll in keys]))
        for (k, ll), o in zip(keys, outs):
            (cur if ll == l else nxt)[k] = o
        return res

    first = ("w_in", "go", "po", "out", "pg")
    gathered = dict(zip(first, _run_job(_gather_job([shards[k] for k in first], [0] * len(first)), "ag_layer0")))

    dku = w_decay_up.shape[-1]
    small_in = jnp.concatenate([c.reshape(-1), w_decay_up.reshape(-1), b_decay_up.reshape(-1)])
    (small_all,) = _gather_flat([small_in], "ag_small")
    c_all = small_all[:, :D]
    n_wdu = L * 2 * GATE_RANK * dku
    wdu_all = small_all[:, D:D + n_wdu].reshape(N_DEV, L, 2, GATE_RANK, dku)
    wdu_full = jnp.transpose(wdu_all, (1, 2, 3, 0, 4)).reshape(L, 2, GATE_RANK, DK)
    bdu_all = small_all[:, D + n_wdu:].reshape(N_DEV, L, 2, dku)
    bdu_full = jnp.transpose(bdu_all, (1, 2, 0, 3)).reshape(L, 1, 2 * DK)
    wdu_bd = jnp.zeros((L, ALR_PAD, 2 * DK), F32)
    wdu_bd = wdu_bd.at[:, :GATE_RANK, :DK].set(wdu_full[:, 0])
    wdu_bd = wdu_bd.at[:, GATE_RANK:2 * GATE_RANK, DK:].set(wdu_full[:, 1]).astype(BF16)

    ncond = 16
    cond = jnp.concatenate([c_all, c_ctx.reshape(1, D), jnp.zeros((ncond - N_DEV - 1, D), F32)], axis=0)
    s_cond = _silu_rows(cond, "silu_cond")
    wsh = w_ada.shape[-1]
    b_ada_mine = lax.dynamic_slice_in_dim(b_ada, me * wsh, wsh, axis=1)
    mod_part = jnp.stack([_mm(s_cond, w_ada, "nn", F32, "mod_mm", bias=b_ada_mine[l:l + 1], b_pre=(l,))
                          for l in range(L)])
    (mod_all,) = _all_gather([mod_part], "ag_mod")
    mod_all = jnp.swapaxes(mod_all, 1, 2).reshape(L, ncond, N_MOD * D)
    mod_lat = lax.dynamic_slice_in_dim(mod_all, me, 1, axis=1)
    mods = jnp.concatenate([mod_all[:, N_DEV:N_DEV + 1], mod_lat], axis=1).reshape(L, 2, 1, N_MOD * D)
    SH_M, SC_M, GT_M, SH_F, SC_F, GT_F = range(N_MOD)

    xa = jnp.concatenate([ctx[0], x[0]], axis=0)
    vec = lambda a, l: a[l].reshape(1, -1)
    saved = []
    h = _mod_fwd(xa, mods[0], SC_M, SH_M, rb, "mod_fwd")
    weights = []
    for l in range(L):
        W = prepared(gathered)
        weights.append(W)
        gathered = {}
        late = {}
        proj = gather_next(lambda j: _mm(h, W["main"], "nn", F32, "mm_in", job=j), ["w_in"], l, gathered)
        alr = _mm(h, W["alr"], "nn", F32, "mm_alr")
        la = _decay_fwd(alr, wdu_bd[l], bdu_full[l], rb, "decay_fwd")
        o_f, s_f = gather_next(lambda j: _gla_fwd(proj, la, False, rb, D, "gla_fwd_f", job=j), ["go", "out"], l,
                               gathered, own=("fi",) if l == 0 else (), cur=late)
        o_b, s_b = gather_next(lambda j: _gla_fwd(proj, la, True, rb, D, "gla_fwd_b", job=j), ["po", "pg"], l,
                               gathered, own=("fo",) if l == 0 else (), cur=late)
        W.update(prepared(late))
        u = _glaout_fwd(o_f, o_b, proj, vec(gla_norm_gain, l), rb, "glaout_fwd")
        y_gla = _mm(u, W["go"], "nn", F32, "mm_go")
        pd, y0, r = _pool_fwd(proj, pc, W["pg"], vec(pool_scale, l), ctx_len, D, "pool_fwd")
        y_pool = _mm(r, W["po"], "nn", F32, "mm_po", b_shard=True)
        m_ = _merge_fwd(proj, y_gla, y_pool, rb, "merge_fwd")
        mix = _mm(m_, W["out"], "nn", F32, "mm_out")
        x1, h2 = _unit_fwd(alpha, xa, mix, mods[l], GT_M, vec(ln_mix_gain, l), vec(ln_mix_bias, l),
                           (mods[l], SC_F, SH_F), rb, "unit_mix_fwd")
        ff, s_ = gather_next(lambda j: _ffn_in_fwd(h2, W["fi"], 0, "mm_fi_swiglu", job=j), ["fi"], l, gathered)
        ffn = gather_next(lambda j: _mm(s_, W["fo"], "nn", F32, "mm_fo", b_pre=(0,), job=j), ["fo"], l, gathered)
        nxt = (mods[l + 1], SC_M, SH_M) if l + 1 < L else None
        x2, h_next = _unit_fwd(alpha, x1, ffn, mods[l], GT_F, vec(ln_ffn_gain, l), vec(ln_ffn_bias, l),
                               nxt, rb, "unit_ffn_fwd")
        saved.append(dict(xa=xa, h=h, proj=proj, alr=alr, la=la, o_f=o_f, o_b=o_b, s_f=s_f, s_b=s_b, u=u,
                          y_gla=y_gla, pd=pd, y0=y0, r=r, y_pool=y_pool, m=m_, mix=mix, x1=x1, h2=h2, ff=ff,
                          s=s_, ffn=ffn))
        xa, h = x2, h_next

    dxo, loss_part = _loss_head(xa, loss_target[0], rb, "loss_head")
    loss = lax.psum(loss_part[0, 0], ("x", "y", "c"))

    big_params = [("w_in", w_in, m_w_in, v_w_in), ("w_gla_out", w_gla_out, m_w_gla_out, v_w_gla_out),
                  ("w_pool_out", w_pool_out, m_w_pool_out, v_w_pool_out), ("w_out", w_out, m_w_out, v_w_out),
                  ("w_ffn_in", w_ffn_in, m_w_ffn_in, v_w_ffn_in), ("w_ffn_out", w_ffn_out, m_w_ffn_out, v_w_ffn_out),
                  ("w_pool_group", w_pool_group, m_w_pool_group, v_w_pool_group)]
    big_out = {nm: None for nm, _, _, _ in big_params}
    g_small = {k: [None] * L for k in ("gla_gain", "pool_scale", "mix_g", "mix_b", "ffn_g", "ffn_b", "wdu", "bdu")}
    dmods = [None] * L
    dh = None
    sum2 = lambda a: a[0] + a[1]
    rows8 = lambda g: g.reshape(N_DEV, g.shape[0] // N_DEV, g.shape[1])

    def apply_adamw(parts, layer):
        for (nm, w, m, v), gs in zip(big_params, parts):
            R, C = gs.shape[1], gs.shape[2]
            big_out[nm] = _adamw_layer(gs, w.reshape(L, R, C), m.reshape(L, R, C), v.reshape(L, R, C), layer,
                                       big_out[nm], "adamw_" + nm)

    LATE = (0, 1, 2, 3, 6)
    late_chunks = None
    arrived = {}

    def behind(fn, job, positions):
        if job is None:
            return fn(None)
        res, outs = fn(job)
        if positions is None:
            return res, outs
        arrived.update(zip(positions, outs))
        return res

    for l in range(L - 1, -1, -1):
        sv = saved[l]
        W = weights[l]
        nxt = (mods[l + 1], SC_M, SH_M) if l + 1 < L else None
        unit = lambda j: _unit_bwd(alpha, dxo, dh, sv["x1"], sv["ffn"], mods[l], GT_F, vec(ln_ffn_gain, l),
                                   vec(ln_ffn_bias, l), nxt, rb, "unit_ffn_bwd", job=j)
        late_pairs = None
        if late_chunks is None:
            res = unit(None)
        else:
            res, sib = behind(unit, _sibling_job(late_chunks), None)
            late_pairs = _pair_adds(late_chunks, sib, "_late")
        dx1, dffn, d_gtf, d_gf, d_bf, d_scm_n, d_shm_n = res
        if nxt is not None:
            dmods[l + 1]["sc_m"], dmods[l + 1]["sh_m"] = d_scm_n, d_shm_n
        dmods[l] = dict(gt_f=d_gtf)
        g_small["ffn_g"][l], g_small["ffn_b"][l] = sum2(d_gf), sum2(d_bf)
        dff_ = behind(lambda j: _ffn_out_dx(dffn, W["fo"], 0, sv["ff"], "mm_fo_dx_swiglu", job=j),
                      _chip_job(late_pairs[1:]) if late_pairs else None, LATE[1:])
        c_fo = rows8(_mm(sv["s"], dffn, "tn", BF16, "mm_fo_dw"))
        dh2 = behind(lambda j: _mm(dff_, W["fi"], "nt", F32, "mm_fi_dx", b_pre=(0,), b_shard=True, a_half=True,
                                   job=j), _chip_job(late_pairs[:1]) if late_pairs else None, LATE[:1])
        c_fi = _mm(sv["h2"], dff_, "tn", BF16, "mm_fi_dw", b_half=True, out_shard=True)
        if late_pairs:
            apply_adamw([arrived[i] for i in range(len(big_params))], l + 1)
            arrived = {}
        ffn_chunks = [c_fi, c_fo]
        res, sib = behind(lambda j: _unit_bwd(
            alpha, dx1, dh2, sv["xa"], sv["mix"], mods[l], GT_M, vec(ln_mix_gain, l), vec(ln_mix_bias, l),
            (mods[l], SC_F, SH_F), rb, "unit_mix_bwd", job=j), _sibling_job(ffn_chunks), None)
        dxa, dmix, d_gtm, d_gm, d_bm, d_scf, d_shf = res
        ffn_pairs = _pair_adds(ffn_chunks, sib, "_ffn")
        dmods[l].update(gt_m=d_gtm, sc_f=d_scf, sh_f=d_shf)
        g_small["mix_g"][l], g_small["mix_b"][l] = sum2(d_gm), sum2(d_bm)
        dm = _mm(dmix, W["out"], "nt", F32, "mm_out_dx")
        c_out = rows8(_mm(sv["m"], dmix, "tn", BF16, "mm_out_dw"))
        dbg1, dbg2, dyg, dyp = _merge_bwd(dm, sv["proj"], sv["y_gla"], sv["y_pool"], rb, "merge_bwd")
        dr = _mm(dyp, W["po"], "nt", F32, "mm_po_dx", b_shard=True)
        c_po = _mm(sv["r"], dyp, "tn", BF16, "mm_po_dw", out_shard=True)
        dp_, d_ps, g_pgl = _pool_bwd(dr, sv["y0"], sv["pd"], pc, W["pg"], vec(pool_scale, l), ctx_len, D, "pool_bwd")
        g_small["pool_scale"][l] = d_ps
        c_pg = jnp.swapaxes(g_pgl.astype(BF16).reshape(ng, N_DEV, pg // N_DEV, pg), 0, 1).reshape(N_DEV, -1, pg)
        du = _mm(dyg, W["go"], "nt", F32, "mm_go_dx")
        c_go = rows8(_mm(sv["u"], dyg, "tn", BF16, "mm_go_dw"))
        do, dg, d_gg = _glaout_bwd(du, sv["o_f"], sv["o_b"], sv["proj"], vec(gla_norm_gain, l), rb, "glaout_bwd")
        g_small["gla_gain"][l] = sum2(d_gg)
        dq_f, dk_f, dv_f, dla_f = behind(lambda j: _gla_bwd(
            sv["proj"], sv["la"], do, sv["s_f"], False, rb, D, None, "gla_bwd_f", job=j),
            _chip_job(ffn_pairs[:1]), (4,))
        dq, dk, dv, dla_b = behind(lambda j: _gla_bwd(
            sv["proj"], sv["la"], do, sv["s_b"], True, rb, D, (dq_f, dk_f, dv_f), "gla_bwd_b", job=j),
            _chip_job(ffn_pairs[1:]), (5,))
        dalr, g_wdu, g_bdu = _decay_bwd(dla_f, dla_b, sv["alr"], wdu_bd[l], bdu_full[l], rb, "decay_bwd")
        g_small["wdu"][l] = jnp.stack([g_wdu[:GATE_RANK, :DK], g_wdu[GATE_RANK:2 * GATE_RANK, DK:]])
        g_small["bdu"][l] = g_bdu.reshape(2, DK)
        dproj = jnp.concatenate([dv, dg, dbg1, dbg2, dq, dk, dp_], axis=1)
        dh_alr = _mm(dalr, W["alr"], "nt", F32, "mm_alr_dx")
        dh = _mm(dproj, W["main"], "nt", F32, "mm_in_dx", add=dh_alr)
        g_main = _mm(sv["h"], dproj, "tn", BF16, "mm_in_dw")
        g_alr = _mm(sv["h"], dalr, "tn", BF16, "mm_alr_dw")
        c_in = _w_in_chunks(g_main, g_alr, w_in.shape[2], "w_in_chunks")
        late_chunks = [c_in, c_go, c_po, c_out, c_pg]
        dxo = dxa
    sib = _run_job(_sibling_job(late_chunks), "rs_sibling_last")
    last_pairs = _pair_adds(late_chunks, sib, "_last")
    grad_xa, d_scm0, d_shm0 = _mod_bwd(dxo, dh, saved[0]["xa"], mods[0], SC_M, SH_M, rb, "mod_bwd")
    dmods[0]["sc_m"], dmods[0]["sh_m"] = d_scm0, d_shm0
    grad_x = grad_xa[ctx_len:].reshape(1, seq, D)

    order = ("sh_m", "sc_m", "gt_m", "sh_f", "sc_f", "gt_f")
    dmod = jnp.stack([jnp.concatenate([dmods[l][k] for k in order], axis=2) for l in range(L)])
    dmod = dmod.reshape(-1)
    sm = lambda k: jnp.stack([a.reshape(-1) for a in g_small[k]]).reshape(-1)
    small_keys = ("gla_gain", "pool_scale", "mix_g", "mix_b", "ffn_g", "ffn_b", "wdu", "bdu")
    small_part = jnp.concatenate([sm(k) for k in small_keys])
    small_g, dmod_g = _gather_flat([small_part, dmod], "ag_small_grads")
    small_sum = _sum_lead(small_g.reshape(N_DEV, -1, 128), "sum_small").reshape(-1)
    off = 0
    rep = {}
    for k, n in zip(small_keys, (L * D, L * DP, L * D, L * D, L * D, L * D, L * 2 * GATE_RANK * DK, L * 2 * DK)):
        rep[k] = small_sum[off:off + n]
        off += n
    g_wdu_mine = lax.dynamic_slice_in_dim(rep["wdu"].reshape(L, 2, GATE_RANK, DK), me * dku, dku, axis=3)
    g_bdu_mine = lax.dynamic_slice_in_dim(rep["bdu"].reshape(L, 2, DK), me * dku, dku, axis=2)

    dmod_all = dmod_g.reshape(N_DEV, L, 2, N_MOD * D)
    dm_ctx = _sum_lead(dmod_all[:, :, 0].reshape(N_DEV, L, N_MOD * D), "sum_dmod_ctx")
    dm_rows = jnp.concatenate([jnp.swapaxes(dmod_all[:, :, 1], 0, 1), dm_ctx[:, None],
                               jnp.zeros((L, ncond - N_DEV - 1, N_MOD * D), F32)], axis=1)
    g_b_ada = _sum_lead(jnp.swapaxes(dm_rows, 0, 1), "sum_b_ada")
    dm_mine = lax.dynamic_slice_in_dim(dm_rows, me * wsh, wsh, axis=2).astype(BF16)
    g_w_ada = jnp.stack([_mm(s_cond, dm_mine[l], "tn", F32, "ada_dw") for l in range(L)])
    ds_part = _sum_lead(jnp.stack([_mm(dm_mine[l], w_ada, "nt", F32, "ada_dx", b_pre=(l,)) for l in range(L)]),
                        "sum_ds")
    (ds_all,) = _gather_flat([ds_part[N_DEV]], "ag_ds")
    ds_ctx = _sum_lead(ds_all.reshape(N_DEV, 1, D), "sum_ds_ctx")
    g_c_ctx = _silu_grad(c_ctx.reshape(1, D), ds_ctx, "silu_grad").reshape(D)

    one = lambda g: g[None]
    small_table = {
        "c_ctx": (one(g_c_ctx), c_ctx, m_c_ctx, v_c_ctx),
        "w_ada": (one(g_w_ada), w_ada, m_w_ada, v_w_ada),
        "b_ada": (one(g_b_ada), b_ada, m_b_ada, v_b_ada),
        "w_decay_up": (one(g_wdu_mine), w_decay_up, m_w_decay_up, v_w_decay_up),
        "b_decay_up": (one(g_bdu_mine), b_decay_up, m_b_decay_up, v_b_decay_up),
        "gla_norm_gain": (one(rep["gla_gain"].reshape(L, D)), gla_norm_gain, m_gla_norm_gain, v_gla_norm_gain),
        "pool_scale": (one(rep["pool_scale"].reshape(L, DP)), pool_scale, m_pool_scale, v_pool_scale),
        "ln_mix_gain": (one(rep["mix_g"].reshape(L, D)), ln_mix_gain, m_ln_mix_gain, v_ln_mix_gain),
        "ln_mix_bias": (one(rep["mix_b"].reshape(L, D)), ln_mix_bias, m_ln_mix_bias, v_ln_mix_bias),
        "ln_ffn_gain": (one(rep["ffn_g"].reshape(L, D)), ln_ffn_gain, m_ln_ffn_gain, v_ln_ffn_gain),
        "ln_ffn_bias": (one(rep["ffn_b"].reshape(L, D)), ln_ffn_bias, m_ln_ffn_bias, v_ln_ffn_bias),
    }
    big_shapes = {nm: w.shape for nm, w, _, _ in big_params}
    names = ("c_ctx", "w_ada", "b_ada", "w_in", "w_decay_up", "b_decay_up", "gla_norm_gain", "w_pool_group",
             "pool_scale", "w_gla_out", "w_pool_out", "w_out", "ln_mix_gain", "ln_mix_bias", "w_ffn_in", "w_ffn_out",
             "ln_ffn_gain", "ln_ffn_bias")
    ada_res, last_parts = _adamw_nd(*small_table["w_ada"], "adamw_w_ada", job=_chip_job(last_pairs))
    arrived.update(zip(LATE, last_parts))
    apply_adamw([arrived[i] for i in range(len(big_params))], 0)
    grads, deltas, new_m, new_v = [], [], [], []
    for nm in names:
        if nm == "w_ada":
            res = ada_res
        elif nm in small_table:
            res = _adamw_nd(*small_table[nm], "adamw_" + nm)
        else:
            res = [o.reshape(big_shapes[nm]) for o in big_out[nm]]
        for lst, o in zip((grads, deltas, new_m, new_v), res):
            lst.append(o)
    return (loss, grad_x, *grads, *deltas, *new_m, *new_v)
```

```python
import functools
import math

import numpy as np
import jax
import jax.numpy as jnp
from jax import lax
from jax.experimental import pallas as pl
from jax.experimental.pallas import tpu as pltpu

F32 = jnp.float32
BF16 = jnp.bfloat16

N_DEV = 8
N_HEADS = 4
GATE_RANK = 16
GATE_NORM = 16.0
CHUNK = 64
GRID_W = 64
POOL_WINDOWS = (2, 4, 8, 16)
N_MOD = 6
LN_EPS = 1e-5
RMS_EPS = 1e-6
ALR_PAD = 128
POOL_TB = 256
POOL_PAD_ROWS = 8
ADAM_LR = 0.001
ADAM_B1 = 0.9
ADAM_B2 = 0.999
ADAM_EPS = 1e-08
ADAM_WD = 0.01
ADAM_STEP = 10
VMEM_LIMIT = 56 * 1024 * 1024
MESH = pl.DeviceIdType.MESH


def _cparams(sem=None):
    return pltpu.CompilerParams(dimension_semantics=sem, vmem_limit_bytes=VMEM_LIMIT)


def _pick(dim, cap, mult):
    best = None
    for d in range(mult, min(dim, cap) + 1, mult):
        if dim % d == 0:
            best = d
    return best if best is not None else dim


def _sig(x):
    return 1.0 / (1.0 + jnp.exp(-x))


def _silu(x):
    return x * _sig(x)


def _dot(a, b):
    return lax.dot_general(a, b, (((1,), (0,)), ((), ())), preferred_element_type=F32)


def _dot_nt(a, b):
    return lax.dot_general(a, b, (((1,), (1,)), ((), ())), preferred_element_type=F32)


def _dot_tn(a, b):
    return lax.dot_general(a, b, (((0,), (0,)), ((), ())), preferred_element_type=F32)


def _split2(x):
    hi = x.astype(BF16)
    lo = (x - hi.astype(F32)).astype(BF16)
    return hi, lo


def _dot2(m_b, x):
    hi, lo = _split2(x)
    return _dot(m_b, hi) + _dot(m_b, lo)


def _dot3(m_b, x):
    h1 = x.astype(BF16)
    r1 = x - h1.astype(F32)
    h2 = r1.astype(BF16)
    h3 = (r1 - h2.astype(F32)).astype(BF16)
    return _dot(m_b, h1) + _dot(m_b, h2) + _dot(m_b, h3)


def _my_pos():
    return lax.axis_index("x"), lax.axis_index("y"), lax.axis_index("c")


def _all_gather(arrs, name):
    n = len(arrs)
    srcs = [a.reshape((a.shape[0], 1) + a.shape[1:]) for a in arrs]
    outs = [jax.ShapeDtypeStruct((a.shape[0], N_DEV) + a.shape[1:], a.dtype) for a in arrs]

    def body(*refs):
        in_refs, out_refs = refs[:n], refs[n:2 * n]
        send_sems, recv_sems, local_sems = refs[2 * n:]
        x, y, c = _my_pos()
        me, sibling = (x, y, c), (x, y, 1 - c)
        chips = [(1 - x, y), (x, 1 - y), (1 - x, 1 - y)]

        def slot(t, pos):
            return out_refs[t].at[:, pl.ds(4 * pos[0] + 2 * pos[1] + pos[2], 1)]

        def copy(t, k, block, to, src=None):
            return pltpu.make_async_remote_copy(
                src_ref=slot(t, block) if src is None else src, dst_ref=slot(t, block),
                send_sem=send_sems.at[t * 7 + k], recv_sem=recv_sems.at[t * 7 + k],
                device_id=to, device_id_type=MESH)

        mine = [pltpu.make_async_copy(in_refs[t], slot(t, me), local_sems.at[t]) for t in range(n)]
        for cp in mine:
            cp.start()
        first = []
        for t in range(n):
            first.append(copy(t, 0, me, sibling, src=in_refs[t]))
            first += [copy(t, 1 + j, me, (*chip, c), src=in_refs[t]) for j, chip in enumerate(chips)]
        for cp in first:
            cp.start()
        passed = []
        for j, chip in enumerate(chips):
            for t in range(n):
                copy(t, 1 + j, (*chip, c), me).wait_recv()
                fwd = copy(t, 4 + j, (*chip, c), sibling)
                fwd.start()
                passed.append(fwd)
        for t in range(n):
            copy(t, 0, sibling, me).wait_recv()
            for j, chip in enumerate(chips):
                copy(t, 4 + j, (*chip, 1 - c), me).wait_recv()
        for cp in first + passed:
            cp.wait_send()
        for cp in mine:
            cp.wait()

    any_spec = pl.BlockSpec(memory_space=pl.ANY)
    res = pl.pallas_call(
        body, name=name, out_shape=outs,
        in_specs=[any_spec] * n, out_specs=[any_spec] * n,
        scratch_shapes=[pltpu.SemaphoreType.DMA((7 * n,)), pltpu.SemaphoreType.DMA((7 * n,)),
                        pltpu.SemaphoreType.DMA((n,))],
        compiler_params=pltpu.CompilerParams(has_side_effects=True),
    )(*srcs)
    return list(res)


def _gather_flat(vecs, name):
    padded = []
    for v in vecs:
        n = v.shape[0]
        padded.append(jnp.pad(v, (0, -n % 128)).reshape(1, -1, 128))
    res = _all_gather(padded, name)
    return [r.reshape(N_DEV, -1)[:, :v.shape[0]] for r, v in zip(res, vecs)]


N_CHIP = 4


def _comm_call(body, name, arrs, outs, n_sems):
    any_spec = pl.BlockSpec(memory_space=pl.ANY)
    n = len(arrs)
    res = pl.pallas_call(
        body, name=name, out_shape=outs,
        in_specs=[any_spec] * n, out_specs=[any_spec] * len(outs),
        scratch_shapes=[pltpu.SemaphoreType.DMA((s,)) for s in n_sems],
        compiler_params=pltpu.CompilerParams(has_side_effects=True),
    )(*arrs)
    return list(res)


def _sibling_job(arrs):
    n = len(arrs)
    outs = [jax.ShapeDtypeStruct((N_CHIP,) + a.shape[1:], a.dtype) for a in arrs]

    def copies(in_refs, out_refs, sems):
        send_sems, recv_sems = sems
        x, y, c = _my_pos()
        return [pltpu.make_async_remote_copy(
            src_ref=in_refs[t].at[pl.ds(2 * k + (1 - c), 1)], dst_ref=out_refs[t].at[pl.ds(k, 1)],
            send_sem=send_sems.at[t * N_CHIP + k], recv_sem=recv_sems.at[t * N_CHIP + k],
            device_id=(x, y, 1 - c), device_id_type=MESH) for t in range(n) for k in range(N_CHIP)]

    def start(in_refs, out_refs, sems):
        for cp in copies(in_refs, out_refs, sems):
            cp.start()

    def finish(in_refs, out_refs, sems):
        cps = copies(in_refs, out_refs, sems)
        for cp in cps:
            cp.wait_recv()
        for cp in cps:
            cp.wait_send()

    return _Job(arrs, outs, (N_CHIP * n, N_CHIP * n), start, finish)


class _Job:
    def __init__(self, arrs, outs, n_sems, start, finish):
        self.arrs, self.outs, self.n_sems, self.start, self.finish = arrs, outs, n_sems, start, finish


def _gather_job(stacked, layers):
    n = len(stacked)
    outs = [jax.ShapeDtypeStruct((N_DEV,) + a.shape[1:], a.dtype) for a in stacked]

    def parts(in_refs, out_refs, sems):
        send_sems, recv_sems, local_sems = sems
        x, y, c = _my_pos()
        me, sibling = (x, y, c), (x, y, 1 - c)
        chips = [(1 - x, y), (x, 1 - y), (1 - x, 1 - y)]
        src = lambda t: in_refs[t].at[pl.ds(layers[t], 1)]

        def slot(t, pos):
            return out_refs[t].at[pl.ds(4 * pos[0] + 2 * pos[1] + pos[2], 1)]

        def copy(t, k, block, to, from_input=False):
            return pltpu.make_async_remote_copy(
                src_ref=src(t) if from_input else slot(t, block), dst_ref=slot(t, block),
                send_sem=send_sems.at[t * 7 + k], recv_sem=recv_sems.at[t * 7 + k],
                device_id=to, device_id_type=MESH)

        mine = [pltpu.make_async_copy(src(t), slot(t, me), local_sems.at[t]) for t in range(n)]
        first = []
        for t in range(n):
            first.append(copy(t, 0, me, sibling, True))
            first += [copy(t, 1 + j, me, (*chip, c), True) for j, chip in enumerate(chips)]
        return me, sibling, chips, copy, mine, first

    def start(in_refs, out_refs, sems):
        _, _, _, _, mine, first = parts(in_refs, out_refs, sems)
        for cp in mine + first:
            cp.start()

    def finish(in_refs, out_refs, sems):
        me, sibling, chips, copy, mine, first = parts(in_refs, out_refs, sems)
        passed = []
        for j, chip in enumerate(chips):
            for t in range(n):
                copy(t, 1 + j, (*chip, me[2]), me).wait_recv()
                fwd = copy(t, 4 + j, (*chip, me[2]), sibling)
                fwd.start()
                passed.append(fwd)
        for t in range(n):
            copy(t, 0, sibling, me).wait_recv()
            for j, chip in enumerate(chips):
                copy(t, 4 + j, (*chip, 1 - me[2]), me).wait_recv()
        for cp in first + passed:
            cp.wait_send()
        for cp in mine:
            cp.wait()

    return _Job(stacked, outs, (7 * n, 7 * n, n), start, finish)


def _chip_job(arrs):
    n = len(arrs)
    outs = [jax.ShapeDtypeStruct(a.shape, a.dtype) for a in arrs]

    def parts(in_refs, out_refs, sems):
        send_sems, recv_sems, local_sems = sems
        x, y, c = _my_pos()
        chip = 2 * x + y
        mine, sends, recvs = [], [], []
        for t in range(n):
            mine.append(pltpu.make_async_copy(in_refs[t].at[pl.ds(chip, 1)], out_refs[t].at[pl.ds(chip, 1)],
                                              local_sems.at[t]))
            for m in range(1, N_CHIP):
                px, py = x ^ (m >> 1), y ^ (m & 1)
                peer = 2 * px + py
                sends.append(pltpu.make_async_remote_copy(
                    src_ref=in_refs[t].at[pl.ds(peer, 1)], dst_ref=out_refs[t].at[pl.ds(chip, 1)],
                    send_sem=send_sems.at[t * 3 + m - 1], recv_sem=recv_sems.at[t * 3 + m - 1],
                    device_id=(px, py, c), device_id_type=MESH))
                recvs.append(pltpu.make_async_remote_copy(
                    src_ref=in_refs[t].at[pl.ds(peer, 1)], dst_ref=out_refs[t].at[pl.ds(peer, 1)],
                    send_sem=send_sems.at[t * 3 + m - 1], recv_sem=recv_sems.at[t * 3 + m - 1],
                    device_id=(x, y, c), device_id_type=MESH))
        return mine, sends, recvs

    def start(in_refs, out_refs, sems):
        mine, sends, _ = parts(in_refs, out_refs, sems)
        for cp in mine + sends:
            cp.start()

    def finish(in_refs, out_refs, sems):
        mine, sends, recvs = parts(in_refs, out_refs, sems)
        for cp in recvs:
            cp.wait_recv()
        for cp in sends:
            cp.wait_send()
        for cp in mine:
            cp.wait()

    return _Job(arrs, outs, (3 * n, 3 * n, n), start, finish)


def _run_job(job, name):
    n = len(job.arrs)

    def body(*refs):
        ins, outs, sems = refs[:n], refs[n:n + len(job.outs)], refs[n + len(job.outs):]
        job.start(ins, outs, sems)
        job.finish(ins, outs, sems)

    return _comm_call(body, name, job.arrs, job.outs, job.n_sems)


def _carry(job, body, grid, in_specs, out_specs, out_shape, scratch_shapes, args):
    out_specs = list(out_specs) if isinstance(out_specs, (list, tuple)) else [out_specs]
    out_shape = list(out_shape) if isinstance(out_shape, (list, tuple)) else [out_shape]
    n_ci, n_co, n_cs = len(in_specs), len(out_specs), len(scratch_shapes)
    n_ji, n_jo = len(job.arrs), len(job.outs)
    any_spec = pl.BlockSpec(memory_space=pl.ANY)
    total = int(np.prod(grid))

    def wrapped(*refs):
        cin, jin = refs[:n_ci], refs[n_ci:n_ci + n_ji]
        o0 = n_ci + n_ji
        cout, jout = refs[o0:o0 + n_co], refs[o0 + n_co:o0 + n_co + n_jo]
        s0 = o0 + n_co + n_jo
        cscr, jsems = refs[s0:s0 + n_cs], refs[s0 + n_cs:]
        step = pl.program_id(0)
        for d in range(1, len(grid)):
            step = step * grid[d] + pl.program_id(d)

        @pl.when(step == 0)
        def _():
            job.start(jin, jout, jsems)

        body(*cin, *cout, *cscr)

        @pl.when(step == total - 1)
        def _():
            job.finish(jin, jout, jsems)

    return (wrapped, list(in_specs) + [any_spec] * n_ji, out_specs + [any_spec] * n_jo,
            out_shape + list(job.outs),
            list(scratch_shapes) + [pltpu.SemaphoreType.DMA((s,)) for s in job.n_sems],
            list(args) + list(job.arrs), n_co)


def _pair_add(g, r, name):
    _, R, C = g.shape
    cpad = -(-C // 128) * 128
    rt = _pick(R, max(16, (1 << 20) // (2 * cpad)), 16)
    cidx = lax.axis_index("c").astype(jnp.int32).reshape(1)

    def body(c_ref, g_ref, r_ref, o_ref):
        o_ref[...] = (g_ref[...].astype(F32) + r_ref[...].astype(F32)).astype(o_ref.dtype)

    return pl.pallas_call(
        body, name=name, out_shape=jax.ShapeDtypeStruct((N_CHIP, R, C), g.dtype),
        grid_spec=pltpu.PrefetchScalarGridSpec(
            num_scalar_prefetch=1, grid=(N_CHIP, R // rt),
            in_specs=[pl.BlockSpec((None, rt, C), lambda k, i, c_ref: (2 * k + c_ref[0], i, 0)),
                      pl.BlockSpec((None, rt, C), lambda k, i, c_ref: (k, i, 0))],
            out_specs=pl.BlockSpec((None, rt, C), lambda k, i, c_ref: (k, i, 0))),
        compiler_params=_cparams(("parallel", "parallel")),
    )(cidx, g, r)


def _pair_adds(chunks, sib, tag):
    return [_pair_add(g, r, "rs_pair_add" + tag) for g, r in zip(chunks, sib)]


def _mm(a, b, mode, out_dtype=F32, name="mm", bias=None, add=None, b_pre=(), b_shard=False,
        a_half=False, b_half=False, out_shard=False, job=None):
    npre = len(b_pre)
    bshape = b.shape[npre:]
    if mode == "nn":
        M, K = a.shape
        if b_shard:
            K2, N = bshape[1], N_DEV * bshape[2]
        else:
            K2, N = bshape
    elif mode == "nt":
        M, K = (a.shape[1], 2 * a.shape[2]) if a_half else a.shape
        if b_shard:
            N, K2 = bshape[1], N_DEV * bshape[2]
        else:
            N, K2 = bshape
    else:
        K, M = a.shape
        K2, N = (b.shape[1], 2 * b.shape[2]) if b_half else bshape
    assert K == K2, (a.shape, b.shape, mode)
    tm = _pick(M, 1100, 16) if mode != "tn" else _pick(M, 1024, 128)
    tn = _pick(N, 1024, 128)
    tk = _pick(K, 2816 if mode == "nt" else 2176, 128)
    if b_shard and mode == "nn":
        tn = bshape[2]
    sps = 1
    if b_shard and mode == "nt":
        ns = bshape[2]
        sps = 2 if ns % 128 == 0 and (not a_half or (a.shape[2] // ns) % 2 == 0) else 1
        tk = sps * ns
    if out_shard:
        tn = N // N_DEV
    nk = K // tk
    none_pre = (None,) * npre
    if mode == "nn":
        a_spec = pl.BlockSpec((tm, tk), lambda i, j, k: (i, k))
        if b_shard:
            b_spec = pl.BlockSpec(none_pre + (None, tk, tn), lambda i, j, k: b_pre + (j, k, 0))
        else:
            b_spec = pl.BlockSpec(none_pre + (tk, tn), lambda i, j, k: b_pre + (k, j))
        dot = _dot
    elif mode == "nt":
        if a_half:
            nkh = a.shape[2] // tk
            a_spec = pl.BlockSpec((None, tm, tk), lambda i, j, k: (k // nkh, i, k % nkh))
        else:
            a_spec = pl.BlockSpec((tm, tk), lambda i, j, k: (i, k))
        if b_shard:
            b_spec = pl.BlockSpec(none_pre + (sps, tn, tk // sps), lambda i, j, k: b_pre + (k, j, 0))
        else:
            b_spec = pl.BlockSpec(none_pre + (tn, tk), lambda i, j, k: b_pre + (j, k))
        dot = _dot_nt
        if b_shard:
            def dot(a_blk, b_blk):
                ns_ = tk // sps
                p = _dot_nt(a_blk[:, :ns_], b_blk[0])
                for s in range(1, sps):
                    p = p + _dot_nt(a_blk[:, s * ns_:(s + 1) * ns_], b_blk[s])
                return p
    else:
        a_spec = pl.BlockSpec((tk, tm), lambda i, j, k: (k, i))
        if b_half:
            nnh = b.shape[2] // tn
            b_spec = pl.BlockSpec((None, tk, tn), lambda i, j, k: (j // nnh, k, j % nnh))
        else:
            b_spec = pl.BlockSpec(none_pre + (tk, tn), lambda i, j, k: b_pre + (k, j))
        dot = _dot_tn
    in_specs = [a_spec, b_spec]
    args = [a, b]
    if bias is not None:
        in_specs.append(pl.BlockSpec((1, tn), lambda i, j, k: (0, j)))
        args.append(bias)
    if add is not None:
        in_specs.append(pl.BlockSpec((tm, tn), lambda i, j, k: (i, j)))
        args.append(add)
    n_in = len(args)
    if out_shard:
        o_spec = pl.BlockSpec((None, tm, tn), lambda i, j, k: (j, i, 0))
        o_shape = jax.ShapeDtypeStruct((N_DEV, M, tn), out_dtype)
    else:
        o_spec = pl.BlockSpec((tm, tn), lambda i, j, k: (i, j))
        o_shape = jax.ShapeDtypeStruct((M, N), out_dtype)

    def body(*refs):
        a_ref, b_ref = refs[0], refs[1]
        bias_ref = refs[2] if bias is not None else None
        add_ref = refs[n_in - 1] if add is not None else None
        o_ref = refs[n_in]
        p = dot(a_ref[...].astype(BF16), b_ref[...].astype(BF16))

        def finish(acc):
            if bias_ref is not None:
                acc = acc + bias_ref[...]
            if add_ref is not None:
                acc = acc + add_ref[...]
            o_ref[...] = acc.astype(o_ref.dtype)

        if nk == 1:
            finish(p)
        else:
            acc_ref = refs[-1]
            k = pl.program_id(2)

            @pl.when(k == 0)
            def _():
                acc_ref[...] = p

            @pl.when(k > 0)
            def _():
                acc_ref[...] += p

            @pl.when(k == nk - 1)
            def _():
                finish(acc_ref[...])

    grid = (M // tm, N // tn, nk)
    scratch = [pltpu.VMEM((tm, tn), F32)] if nk > 1 else []
    if job is None:
        return pl.pallas_call(
            body, name=name, grid=grid, in_specs=in_specs, out_specs=o_spec, out_shape=o_shape,
            scratch_shapes=scratch, compiler_params=_cparams(("parallel", "parallel", "arbitrary")),
        )(*args)
    return _call_carrying(job, body, name, grid, in_specs, o_spec, o_shape, scratch, args)


def _call_carrying(job, body, name, grid, in_specs, out_specs, out_shape, scratch, args):
    body, in_specs, out_specs, out_shape, scratch, args, n_co = _carry(
        job, body, grid, in_specs, out_specs, out_shape, scratch, args)
    res = pl.pallas_call(
        body, name=name, grid=grid, in_specs=in_specs, out_specs=out_specs, out_shape=out_shape,
        scratch_shapes=scratch, compiler_params=_cparams(("arbitrary",) * len(grid)),
    )(*args)
    own = res[0] if n_co == 1 else list(res[:n_co])
    return own, list(res[n_co:])


def _proj_layout(D):
    DK, DP, R2 = D // 2, D // 2, 2 * GATE_RANK
    return [("q", 0, DK, 4 * D), ("k", DK, DK, 4 * D + DK), ("v", 2 * DK, D, 0), ("g", 2 * DK + D, D, D),
            ("a", 2 * DK + 2 * D, R2, None), ("p", 2 * DK + 2 * D + R2, DP, 5 * D),
            ("bg", 2 * DK + 2 * D + R2 + DP, 2 * D, 2 * D)]


RELAYOUT_ROWS = 64


def _w_in_operands(g, name):
    _, D, n = g.shape
    segs = _proj_layout(D)
    tr = RELAYOUT_ROWS

    def body(g_ref, main_ref, alr_ref):
        shard = [g_ref[j].astype(F32) for j in range(N_DEV)]

        def columns(a, b):
            parts = []
            for j in range(a // n, (b - 1) // n + 1):
                parts.append(shard[j][:, max(a, j * n) - j * n:min(b, (j + 1) * n) - j * n])
            return parts[0] if len(parts) == 1 else jnp.concatenate(parts, axis=1)

        for _, start, width, dst in segs:
            cols = columns(start, start + width)
            if dst is None:
                cols = jnp.concatenate([cols, jnp.zeros((tr, ALR_PAD - width), F32)], axis=1)
                alr_ref[...] = cols.astype(BF16)
            else:
                main_ref[:, dst:dst + width] = cols.astype(BF16)

    return pl.pallas_call(
        body, name=name, grid=(D // tr,),
        in_specs=[pl.BlockSpec((N_DEV, tr, n), lambda i: (0, i, 0))],
        out_specs=[pl.BlockSpec((tr, 11 * D // 2), lambda i: (i, 0)), pl.BlockSpec((tr, ALR_PAD), lambda i: (i, 0))],
        out_shape=[jax.ShapeDtypeStruct((D, 11 * D // 2), BF16), jax.ShapeDtypeStruct((D, ALR_PAD), BF16)],
        compiler_params=_cparams(("parallel",)),
    )(g)


def _w_in_chunks(g_main, g_alr, n, name):
    D = g_main.shape[0]
    segs = _proj_layout(D)
    tr = RELAYOUT_ROWS

    def body(main_ref, alr_ref, o_ref):
        main = main_ref[...].astype(F32)
        alr = alr_ref[...].astype(F32)
        for j in range(N_DEV):
            a, b = j * n, (j + 1) * n
            parts = []
            for _, start, width, dst in segs:
                lo, hi = max(a, start), min(b, start + width)
                if lo >= hi:
                    continue
                src = alr if dst is None else main
                off = 0 if dst is None else dst
                parts.append(src[:, off + lo - start:off + hi - start])
            o_ref[j] = (parts[0] if len(parts) == 1 else jnp.concatenate(parts, axis=1)).astype(BF16)

    return pl.pallas_call(
        body, name=name, grid=(D // tr,),
        in_specs=[pl.BlockSpec((tr, 11 * D // 2), lambda i: (i, 0)), pl.BlockSpec((tr, ALR_PAD), lambda i: (i, 0))],
        out_specs=pl.BlockSpec((N_DEV, tr, n), lambda i: (0, i, 0)),
        out_shape=jax.ShapeDtypeStruct((N_DEV, D, n), BF16),
        compiler_params=_cparams(("parallel",)),
    )(g_main, g_alr)


def _ffn_in_fwd(h2, w_fi, l, name, job=None):
    T, D = h2.shape
    n = w_fi.shape[3]
    nh = N_DEV // 2
    dff = nh * n
    tm = _pick(T, 600, 16)

    def body(a_ref, bg_ref, bu_ref, ff_ref, s_ref):
        a = a_ref[...]
        g = _dot(a, bg_ref[...])
        u = _dot(a, bu_ref[...])
        ff_ref[0] = g.astype(BF16)
        ff_ref[1] = u.astype(BF16)
        s_ref[...] = _swiglu_f(g, u).astype(BF16)

    grid = (T // tm, nh)
    in_specs = [pl.BlockSpec((tm, D), lambda i, j: (i, 0)),
                pl.BlockSpec((None, None, D, n), lambda i, j: (l, j, 0, 0)),
                pl.BlockSpec((None, None, D, n), lambda i, j: (l, nh + j, 0, 0))]
    out_specs = [pl.BlockSpec((2, tm, n), lambda i, j: (0, i, j)), pl.BlockSpec((tm, n), lambda i, j: (i, j))]
    out_shape = [jax.ShapeDtypeStruct((2, T, dff), BF16), jax.ShapeDtypeStruct((T, dff), BF16)]
    args = (h2, w_fi, w_fi)
    if job is None:
        return pl.pallas_call(
            body, name=name, grid=grid, in_specs=in_specs, out_specs=out_specs, out_shape=out_shape,
            compiler_params=_cparams(("parallel", "parallel")),
        )(*args)
    return _call_carrying(job, body, name, grid, in_specs, out_specs, out_shape, [], args)


def _ffn_out_dx(dffn, w_fo, l, ff, name, job=None):
    T, D = dffn.shape
    dff = ff.shape[2]
    tm = _pick(T, 600, 16)
    tw = _pick(dff, 1408, 128)

    def body(a_ref, b_ref, ff_ref, o_ref):
        ds = _dot_nt(a_ref[...], b_ref[...])
        _, vjp = jax.vjp(_swiglu_f, ff_ref[0].astype(F32), ff_ref[1].astype(F32))
        dg, du = vjp(ds)
        o_ref[0] = dg.astype(BF16)
        o_ref[1] = du.astype(BF16)

    grid = (T // tm, dff // tw)
    in_specs = [pl.BlockSpec((tm, D), lambda i, j: (i, 0)),
                pl.BlockSpec((None, tw, D), lambda i, j: (l, j, 0)),
                pl.BlockSpec((2, tm, tw), lambda i, j: (0, i, j))]
    out_specs = pl.BlockSpec((2, tm, tw), lambda i, j: (0, i, j))
    out_shape = jax.ShapeDtypeStruct((2, T, dff), BF16)
    args = (dffn, w_fo, ff)
    if job is None:
        return pl.pallas_call(
            body, name=name, grid=grid, in_specs=in_specs, out_specs=out_specs, out_shape=out_shape,
            compiler_params=_cparams(("parallel", "parallel")),
        )(*args)
    return _call_carrying(job, body, name, grid, in_specs, out_specs, out_shape, [], args)


def _row(rb, w, col=0):
    return pl.BlockSpec((rb, w), lambda i: (i, col))


def _modspec(d, sec):
    return pl.BlockSpec((None, 1, d), lambda i: (jnp.minimum(i, 1), 0, sec))


def _vec(w):
    return pl.BlockSpec((1, w), lambda i: (0, 0))


def _acc2(w):
    return pl.BlockSpec((None, 1, w), lambda i: (jnp.minimum(i, 1), 0, 0))


def _accum(ref, val):
    i = pl.program_id(0)

    @pl.when(i <= 1)
    def _():
        ref[...] = val

    @pl.when(i > 1)
    def _():
        ref[...] += val


def _acc_shape(w):
    return jax.ShapeDtypeStruct((2, 1, w), F32)


def _mod_f(x, sc, sh):
    return x * (1.0 + sc) + sh


def _mod_fwd(xa, mod, sec_sc, sec_sh, rb, name):
    T, D = xa.shape

    def body(x_ref, sc_ref, sh_ref, h_ref):
        h_ref[...] = _mod_f(x_ref[...], sc_ref[...], sh_ref[...]).astype(BF16)

    return pl.pallas_call(
        body, name=name, grid=(T // rb,),
        in_specs=[_row(rb, D), _modspec(D, sec_sc), _modspec(D, sec_sh)],
        out_specs=_row(rb, D), out_shape=jax.ShapeDtypeStruct((T, D), BF16),
        compiler_params=_cparams(("parallel",)),
    )(xa, mod, mod)


def _mod_bwd(dxa, dh, xa, mod, sec_sc, sec_sh, rb, name):
    T, D = xa.shape

    def body(dxa_ref, dh_ref, x_ref, sc_ref, sh_ref, dx_ref, dsc_ref, dsh_ref):
        _, vjp = jax.vjp(_mod_f, x_ref[...], sc_ref[...], sh_ref[...])
        dx, dsc, dsh = vjp(dh_ref[...])
        dx_ref[...] = dxa_ref[...] + dx
        _accum(dsc_ref, dsc)
        _accum(dsh_ref, dsh)

    return pl.pallas_call(
        body, name=name, grid=(T // rb,),
        in_specs=[_row(rb, D), _row(rb, D), _row(rb, D), _modspec(D, sec_sc), _modspec(D, sec_sh)],
        out_specs=[_row(rb, D), _acc2(D), _acc2(D)],
        out_shape=[jax.ShapeDtypeStruct((T, D), F32), _acc_shape(D), _acc_shape(D)],
        compiler_params=_cparams(("arbitrary",)),
    )(dxa, dh, xa, mod, mod)


def _ln_f(alpha, x, mix, gt, gain, bias):
    z = alpha * x + gt * mix
    mu = jnp.mean(z, axis=-1, keepdims=True)
    zc = z - mu
    var = jnp.mean(zc * zc, axis=-1, keepdims=True)
    return zc * lax.rsqrt(var + LN_EPS) * gain + bias


def _unit_fwd(alpha, x, mix, mod, sec_gt, gain, bias, next_mod, rb, name):
    T, D = x.shape
    has_mod = next_mod is not None

    def body(*refs):
        if has_mod:
            x_ref, mix_ref, gt_ref, g_ref, b_ref, sc_ref, sh_ref, xo_ref, h_ref = refs
        else:
            x_ref, mix_ref, gt_ref, g_ref, b_ref, xo_ref = refs
        xo = _ln_f(alpha, x_ref[...], mix_ref[...], gt_ref[...], g_ref[...], b_ref[...])
        xo_ref[...] = xo
        if has_mod:
            h_ref[...] = _mod_f(xo, sc_ref[...], sh_ref[...]).astype(BF16)

    in_specs = [_row(rb, D), _row(rb, D), _modspec(D, sec_gt), _vec(D), _vec(D)]
    args = [x, mix, mod, gain, bias]
    out_specs = [_row(rb, D)]
    out_shape = [jax.ShapeDtypeStruct((T, D), F32)]
    if has_mod:
        nm, s_sc, s_sh = next_mod
        in_specs += [_modspec(D, s_sc), _modspec(D, s_sh)]
        args += [nm, nm]
        out_specs.append(_row(rb, D))
        out_shape.append(jax.ShapeDtypeStruct((T, D), BF16))
    res = pl.pallas_call(
        body, name=name, grid=(T // rb,), in_specs=in_specs, out_specs=out_specs, out_shape=out_shape,
        compiler_params=_cparams(("parallel",)),
    )(*args)
    return (res[0], res[1]) if has_mod else (res[0], None)


def _unit_bwd(alpha, dxo, dh, x, mix, mod, sec_gt, gain, bias, next_mod, rb, name, job=None):
    T, D = x.shape
    has_mod = next_mod is not None

    def body(*refs):
        if has_mod:
            (dxo_ref, dh_ref, x_ref, mix_ref, gt_ref, g_ref, b_ref, sc_ref, sh_ref,
             dx_ref, dmix_ref, dgt_ref, dg_ref, db_ref, dsc_ref, dsh_ref) = refs
        else:
            (dxo_ref, x_ref, mix_ref, gt_ref, g_ref, b_ref,
             dx_ref, dmix_ref, dgt_ref, dg_ref, db_ref) = refs
        xo, vjp = jax.vjp(functools.partial(_ln_f, alpha), x_ref[...], mix_ref[...], gt_ref[...],
                          g_ref[...], b_ref[...])
        dxo_t = dxo_ref[...]
        if has_mod:
            _, vjp_m = jax.vjp(_mod_f, xo, sc_ref[...], sh_ref[...])
            dxo_m, dsc, dsh = vjp_m(dh_ref[...])
            dxo_t = dxo_t + dxo_m
            _accum(dsc_ref, dsc)
            _accum(dsh_ref, dsh)
        dx, dmix, dgt, dg, db = vjp(dxo_t)
        dx_ref[...] = dx
        dmix_ref[...] = dmix.astype(BF16)
        _accum(dgt_ref, dgt)
        _accum(dg_ref, dg)
        _accum(db_ref, db)

    in_specs = [_row(rb, D)]
    args = [dxo]
    if has_mod:
        in_specs.append(_row(rb, D))
        args.append(dh)
    in_specs += [_row(rb, D), _row(rb, D), _modspec(D, sec_gt), _vec(D), _vec(D)]
    args += [x, mix, mod, gain, bias]
    out_specs = [_row(rb, D), _row(rb, D), _acc2(D), _acc2(D), _acc2(D)]
    out_shape = [jax.ShapeDtypeStruct((T, D), F32), jax.ShapeDtypeStruct((T, D), BF16),
                 _acc_shape(D), _acc_shape(D), _acc_shape(D)]
    if has_mod:
        nm, s_sc, s_sh = next_mod
        in_specs += [_modspec(D, s_sc), _modspec(D, s_sh)]
        args += [nm, nm]
        out_specs += [_acc2(D), _acc2(D)]
        out_shape += [_acc_shape(D), _acc_shape(D)]
    if job is None:
        res = pl.pallas_call(
            body, name=name, grid=(T // rb,), in_specs=in_specs, out_specs=out_specs, out_shape=out_shape,
            compiler_params=_cparams(("arbitrary",)),
        )(*args)
        job_res = None
    else:
        res, job_res = _call_carrying(job, body, name, (T // rb,), in_specs, out_specs, out_shape, [], args)
    res = list(res) if has_mod else list(res) + [None, None]
    return res if job is None else (res, job_res)


def _log_sigmoid(z):
    return jnp.minimum(z, 0.0) - jnp.log(1.0 + jnp.exp(-jnp.abs(z)))


def _decay_fwd(alr, wdu, bdu, rb, name):
    T = alr.shape[0]
    W = wdu.shape[1]

    def body(a_ref, w_ref, b_ref, la_ref):
        z = _dot(a_ref[...].astype(BF16), w_ref[...]) + b_ref[...]
        la_ref[...] = _log_sigmoid(z) * (1.0 / GATE_NORM)

    return pl.pallas_call(
        body, name=name, grid=(T // rb,),
        in_specs=[_row(rb, ALR_PAD), pl.BlockSpec((ALR_PAD, W), lambda i: (0, 0)), _vec(W)],
        out_specs=_row(rb, W), out_shape=jax.ShapeDtypeStruct((T, W), F32),
        compiler_params=_cparams(("parallel",)),
    )(alr, wdu, bdu)


def _decay_bwd(dla_f, dla_b, alr, wdu, bdu, rb, name):
    T = alr.shape[0]
    W = wdu.shape[1]
    DK = W // 2

    def body(df_ref, db_ref, a_ref, w_ref, b_ref, dalr_ref, gw_ref, gb_ref):
        i = pl.program_id(0)
        ab = a_ref[...].astype(BF16)
        z = _dot(ab, w_ref[...]) + b_ref[...]
        dla = jnp.concatenate([df_ref[...], db_ref[...]], axis=1)
        dz = dla * _sig(-z) * (1.0 / GATE_NORM)
        dzb = dz.astype(BF16)
        dalr_ref[...] = _dot_nt(dzb, w_ref[...]).astype(BF16)
        gw = _dot_tn(ab, dzb)
        gb = jnp.sum(dz, axis=0, keepdims=True)

        @pl.when(i == 0)
        def _():
            gw_ref[...] = gw
            gb_ref[...] = gb

        @pl.when(i > 0)
        def _():
            gw_ref[...] += gw
            gb_ref[...] += gb

    return pl.pallas_call(
        body, name=name, grid=(T // rb,),
        in_specs=[_row(rb, DK), _row(rb, DK), _row(rb, ALR_PAD), pl.BlockSpec((ALR_PAD, W), lambda i: (0, 0)), _vec(W)],
        out_specs=[_row(rb, ALR_PAD), pl.BlockSpec((ALR_PAD, W), lambda i: (0, 0)), _vec(W)],
        out_shape=[jax.ShapeDtypeStruct((T, ALR_PAD), BF16), jax.ShapeDtypeStruct((ALR_PAD, W), F32),
                   jax.ShapeDtypeStruct((1, W), F32)],
        compiler_params=_cparams(("arbitrary",)),
    )(dla_f, dla_b, alr, wdu, bdu)


def _tri(rev, ncb):
    m = np.tril(np.ones((CHUNK, CHUNK), np.float32))
    return jnp.asarray(np.kron(np.eye(ncb, dtype=np.float32), m.T if rev else m), BF16)


def _gla_block_common(q_ref, k_ref, v_ref, la_ref, tri_ref, ck, cv, rev, scale_q, ncb):
    mid = CHUNK // 2 if rev else CHUNK // 2 - 1
    last_i = 0 if rev else CHUNK - 1
    rb = ncb * CHUNK
    hk = ck.stop - ck.start
    q = q_ref[:, ck].astype(F32) * scale_q
    k = k_ref[:, ck].astype(F32)
    v = v_ref[:, cv].astype(F32)
    cum = _dot3(tri_ref[...], la_ref[:, ck])
    per_chunk = lambda i: jnp.concatenate(
        [jnp.broadcast_to(cum[c * CHUNK + i:c * CHUNK + i + 1, :], (CHUNK, hk)) for c in range(ncb)], axis=0)
    ref, last = per_chunk(mid), per_chunk(last_i)
    e_q = jnp.exp(cum - ref)
    e_k = jnp.exp(ref - cum)
    e_c = jnp.exp(cum)
    e_s = jnp.exp(last - cum)
    e_l = [jnp.exp(cum[c * CHUNK + last_i:c * CHUNK + last_i + 1, :]) for c in range(ncb)]
    ri = lax.broadcasted_iota(jnp.int32, (rb, rb), 0)
    ci = lax.broadcasted_iota(jnp.int32, (rb, rb), 1)
    mask = (ri // CHUNK == ci // CHUNK) & ((ci >= ri) if rev else (ci <= ri))
    return q, k, v, e_q, e_k, e_c, e_s, e_l, mask, last_i


GLA_HEADS_PER_STEP = 1


def _gla_specs(rb, hk, hv, D, rbmap, rev):
    hp = GLA_HEADS_PER_STEP
    q_col0 = 4 * D // (hp * hk)
    k_col0 = q_col0 + N_HEADS // hp
    la_col0 = N_HEADS // hp if rev else 0
    return [
        pl.BlockSpec((rb, hp * hk), lambda h, i: (rbmap(i), q_col0 + h)),
        pl.BlockSpec((rb, hp * hk), lambda h, i: (rbmap(i), k_col0 + h)),
        pl.BlockSpec((rb, hp * hv), lambda h, i: (rbmap(i), h)),
        pl.BlockSpec((rb, hp * hk), lambda h, i: (rbmap(i), la_col0 + h)),
        pl.BlockSpec((rb, rb), lambda h, i: (0, 0)),
    ]


def _gla_call(job, body, name, grid, in_specs, out_specs, out_shape, scratch, args):
    if job is None:
        return pl.pallas_call(
            body, name=name, grid=grid, in_specs=in_specs, out_specs=out_specs, out_shape=out_shape,
            scratch_shapes=scratch, compiler_params=_cparams(("parallel", "arbitrary")),
        )(*args)
    return _call_carrying(job, body, name, grid, in_specs, out_specs, out_shape, scratch, args)


def _gla_fwd(proj, la, rev, rb, D, name, job=None):
    T = proj.shape[0]
    nb = T // rb
    ncb = rb // CHUNK
    hp = GLA_HEADS_PER_STEP
    hk, hv = D // 2 // N_HEADS, D // N_HEADS
    scale_q = float(hk) ** -0.5
    rbmap = (lambda i: jnp.where(i == 0, 0, nb - i)) if rev else (lambda i: i)

    def body(q_ref, k_ref, v_ref, la_ref, tri_ref, o_ref, s_ref, st_ref):
        @pl.when(pl.program_id(1) == 0)
        def _():
            st_ref[...] = jnp.zeros_like(st_ref)

        order = range(ncb - 1, -1, -1) if rev else range(ncb)
        for hh in range(hp):
            ck, cv = slice(hh * hk, (hh + 1) * hk), slice(hh * hv, (hh + 1) * hv)
            q, k, v, e_q, e_k, e_c, e_s, e_l, mask, _ = _gla_block_common(
                q_ref, k_ref, v_ref, la_ref, tri_ref, ck, cv, rev, scale_q, ncb)
            vb = v.astype(BF16)
            a = jnp.where(mask, _dot_nt((q * e_q).astype(BF16), (k * e_k).astype(BF16)), 0.0)
            o_intra = _dot(a.astype(BF16), vb)
            qc = (q * e_c).astype(BF16)
            ks = (k * e_s).astype(BF16)
            st = st_ref[hh]
            for cc in order:
                rows = slice(cc * CHUNK, (cc + 1) * CHUNK)
                s_ref[hh, cc] = st
                o_ref[rows, cv] = o_intra[rows] + _dot_nt(qc[rows], st.astype(BF16))
                st = st * e_l[cc] + _dot_tn(vb[rows], ks[rows])
            st_ref[hh] = st

    return _gla_call(
        job, body, name, (N_HEADS // hp, nb), _gla_specs(rb, hk, hv, D, rbmap, rev),
        [pl.BlockSpec((rb, hp * hv), lambda h, i: (rbmap(i), h)),
         pl.BlockSpec((hp, ncb, hv, hk), lambda h, i: (h, rbmap(i), 0, 0))],
        [jax.ShapeDtypeStruct((T, D), F32), jax.ShapeDtypeStruct((N_HEADS, T // CHUNK, hv, hk), F32)],
        [pltpu.VMEM((hp, hv, hk), F32)], (proj, proj, proj, la, _tri(rev, ncb)))


def _gla_bwd(proj, la, do, states, rev, rb, D, prev, name, job=None):
    T = proj.shape[0]
    nb = T // rb
    ncb = rb // CHUNK
    hp = GLA_HEADS_PER_STEP
    hk, hv = D // 2 // N_HEADS, D // N_HEADS
    DK = D // 2
    scale_q = float(hk) ** -0.5
    if rev:
        rbmap = lambda i: jnp.where(i == nb - 1, 0, i + 1)
    else:
        rbmap = lambda i: nb - 1 - i
    has_prev = prev is not None
    out_dt = BF16 if has_prev else F32

    def body(*refs):
        q_ref, k_ref, v_ref, la_ref, tri_ref, trit_ref, do_ref, s_ref = refs[:8]
        n_in = 11 if has_prev else 8
        pq_ref, pk_ref, pv_ref = refs[8:11] if has_prev else (None, None, None)
        dq_ref, dk_ref, dv_ref, dla_ref, ds_ref = refs[n_in:]

        @pl.when(pl.program_id(1) == 0)
        def _():
            ds_ref[...] = jnp.zeros_like(ds_ref)

        order = range(ncb) if rev else range(ncb - 1, -1, -1)
        for hh in range(hp):
            ck, cv = slice(hh * hk, (hh + 1) * hk), slice(hh * hv, (hh + 1) * hv)
            q, k, v, e_q, e_k, e_c, e_s, e_l, mask, last_i = _gla_block_common(
                q_ref, k_ref, v_ref, la_ref, tri_ref, ck, cv, rev, scale_q, ncb)
            vb = v.astype(BF16)
            qi = (q * e_q).astype(BF16)
            ki = (k * e_k).astype(BF16)
            qc = (q * e_c).astype(BF16)
            ks = (k * e_s).astype(BF16)
            a = jnp.where(mask, _dot_nt(qi, ki), 0.0).astype(BF16)
            dob = do_ref[:, cv].astype(BF16)
            da = jnp.where(mask, _dot_nt(dob, vb), 0.0).astype(BF16)
            dv_intra = _dot_tn(a, dob)
            dq_intra = _dot(da, ki) * e_q
            dk_intra = _dot_tn(da, qi) * e_k
            rowi = lax.broadcasted_iota(jnp.int32, (CHUNK, hk), 0)
            dst = ds_ref[hh]
            for cc in order:
                rows = slice(cc * CHUNK, (cc + 1) * CHUNK)
                st0 = s_ref[hh, cc]
                dstb = dst.astype(BF16)
                dv = dv_intra[rows] + _dot_nt(ks[rows], dstb)
                dk_inter = _dot(vb[rows], dstb) * e_s[rows]
                dq_s = dq_intra[rows] + _dot(dob[rows], st0.astype(BF16)) * e_c[rows]
                dk = dk_intra[rows] + dk_inter
                extra = (jnp.sum(k[rows] * dk_inter, axis=0, keepdims=True)
                         + e_l[cc] * jnp.sum(dst * st0, axis=0, keepdims=True))
                dla_ref[rows, ck] = q[rows] * dq_s - k[rows] * dk + jnp.where(rowi == last_i, extra, 0.0)
                dq = dq_s * scale_q
                if has_prev:
                    dq = dq + pq_ref[rows, ck]
                    dk = dk + pk_ref[rows, ck]
                    dv = dv + pv_ref[rows, cv]
                dq_ref[rows, ck] = dq.astype(out_dt)
                dk_ref[rows, ck] = dk.astype(out_dt)
                dv_ref[rows, cv] = dv.astype(out_dt)
                dst = dst * e_l[cc] + _dot_tn(dob[rows], qc[rows])
            ds_ref[hh] = dst
            dla_ref[:, ck] = _dot3(trit_ref[...], dla_ref[:, ck])

    in_specs = _gla_specs(rb, hk, hv, D, rbmap, rev)
    in_specs += [pl.BlockSpec((rb, rb), lambda h, i: (0, 0)),
                 pl.BlockSpec((rb, hp * hv), lambda h, i: (rbmap(i), h)),
                 pl.BlockSpec((hp, ncb, hv, hk), lambda h, i: (h, rbmap(i), 0, 0))]
    args = [proj, proj, proj, la, _tri(rev, ncb), _tri(not rev, ncb), do, states]
    hk_spec = pl.BlockSpec((rb, hp * hk), lambda h, i: (rbmap(i), h))
    hv_spec = pl.BlockSpec((rb, hp * hv), lambda h, i: (rbmap(i), h))
    if has_prev:
        in_specs += [hk_spec, hk_spec, hv_spec]
        args += list(prev)
    return _gla_call(
        job, body, name, (N_HEADS // hp, nb), in_specs, [hk_spec, hk_spec, hv_spec, hk_spec],
        [jax.ShapeDtypeStruct((T, DK), out_dt), jax.ShapeDtypeStruct((T, DK), out_dt),
         jax.ShapeDtypeStruct((T, D), out_dt), jax.ShapeDtypeStruct((T, DK), F32)],
        [pltpu.VMEM((hp, hv, hk), F32)], args)


def _glaout_f(of, ob, g, gain):
    o = of + ob
    n = o * lax.rsqrt(jnp.mean(o * o, axis=-1, keepdims=True) + RMS_EPS)
    return n * gain * _silu(g)


def _glaout_fwd(o_f, o_b, proj, gain, rb, name):
    T, D = o_f.shape
    hv = D // N_HEADS

    def body(of_ref, ob_ref, g_ref, gn_ref, u_ref):
        for h in range(N_HEADS):
            cs = slice(h * hv, (h + 1) * hv)
            u_ref[:, cs] = _glaout_f(of_ref[:, cs], ob_ref[:, cs], g_ref[:, cs].astype(F32),
                                     gn_ref[:, cs]).astype(BF16)

    return pl.pallas_call(
        body, name=name, grid=(T // rb,),
        in_specs=[_row(rb, D), _row(rb, D), _row(rb, D, 1), _vec(D)],
        out_specs=_row(rb, D), out_shape=jax.ShapeDtypeStruct((T, D), BF16),
        compiler_params=_cparams(("parallel",)),
    )(o_f, o_b, proj, gain)


def _glaout_bwd(du, o_f, o_b, proj, gain, rb, name):
    T, D = o_f.shape
    hv = D // N_HEADS

    def body(du_ref, of_ref, ob_ref, g_ref, gn_ref, do_ref, dg_ref, dgn_ref, tmp_ref):
        for h in range(N_HEADS):
            cs = slice(h * hv, (h + 1) * hv)
            _, vjp = jax.vjp(_glaout_f, of_ref[:, cs], ob_ref[:, cs], g_ref[:, cs].astype(F32), gn_ref[:, cs])
            d_of, _, dg, dgn = vjp(du_ref[:, cs])
            do_ref[:, cs] = d_of
            dg_ref[:, cs] = dg.astype(BF16)
            tmp_ref[:, cs] = dgn
        _accum(dgn_ref, tmp_ref[...])

    return pl.pallas_call(
        body, name=name, grid=(T // rb,),
        in_specs=[_row(rb, D), _row(rb, D), _row(rb, D), _row(rb, D, 1), _vec(D)],
        out_specs=[_row(rb, D), _row(rb, D), _acc2(D)],
        out_shape=[jax.ShapeDtypeStruct((T, D), F32), jax.ShapeDtypeStruct((T, D), BF16), _acc_shape(D)],
        scratch_shapes=[pltpu.VMEM((1, D), F32)],
        compiler_params=_cparams(("arbitrary",)),
    )(du, o_f, o_b, proj, gain)


def _merge_f(bg1, bg2, yg, yp):
    return _sig(bg1) * yg + _sig(bg2) * yp


def _merge_fwd(proj, y_gla, y_pool, rb, name):
    T, D = y_gla.shape

    def body(b1_ref, b2_ref, yg_ref, yp_ref, m_ref):
        m_ref[...] = _merge_f(b1_ref[...].astype(F32), b2_ref[...].astype(F32), yg_ref[...],
                              yp_ref[...]).astype(BF16)

    return pl.pallas_call(
        body, name=name, grid=(T // rb,),
        in_specs=[_row(rb, D, 2), _row(rb, D, 3), _row(rb, D), _row(rb, D)],
        out_specs=_row(rb, D), out_shape=jax.ShapeDtypeStruct((T, D), BF16),
        compiler_params=_cparams(("parallel",)),
    )(proj, proj, y_gla, y_pool)


def _merge_bwd(dm, proj, y_gla, y_pool, rb, name):
    T, D = y_gla.shape

    def body(dm_ref, b1_ref, b2_ref, yg_ref, yp_ref, d1_ref, d2_ref, dyg_ref, dyp_ref):
        _, vjp = jax.vjp(_merge_f, b1_ref[...].astype(F32), b2_ref[...].astype(F32), yg_ref[...], yp_ref[...])
        d1, d2, dyg, dyp = vjp(dm_ref[...])
        d1_ref[...] = d1.astype(BF16)
        d2_ref[...] = d2.astype(BF16)
        dyg_ref[...] = dyg.astype(BF16)
        dyp_ref[...] = dyp.astype(BF16)

    return pl.pallas_call(
        body, name=name, grid=(T // rb,),
        in_specs=[_row(rb, D), _row(rb, D, 2), _row(rb, D, 3), _row(rb, D), _row(rb, D)],
        out_specs=[_row(rb, D)] * 4, out_shape=[jax.ShapeDtypeStruct((T, D), BF16)] * 4,
        compiler_params=_cparams(("parallel",)),
    )(dm, proj, proj, y_gla, y_pool)


def _swiglu_f(gate, up):
    return _silu(gate) * up


def _pool_consts(ctx_len, seq):
    rows = seq // GRID_W
    reps = POOL_TB // GRID_W
    mw, bc, cw, ch, cc = [], [], [], [], []
    for w in POOL_WINDOWS:
        lo, hi = w // 2, w - w // 2 - 1

        def band(n):
            i = np.arange(n)[:, None]
            j = np.arange(n)[None, :]
            return ((j - i >= -lo) & (j - i <= hi)).astype(np.float32)

        def count(n):
            i = np.arange(n)
            return (np.minimum(i + hi + 1, n) - np.maximum(i - lo, 0)).astype(np.float32)

        mw.append(np.kron(np.eye(reps, dtype=np.float32), band(GRID_W)))
        bc.append(band(ctx_len))
        cw.append(np.tile(count(GRID_W), reps)[:, None])
        ch.append(np.repeat(count(rows), GRID_W)[:, None])
        cc.append(count(ctx_len)[:, None])
    mw, bc = np.stack(mw), np.stack(bc)
    return dict(
        mw=jnp.asarray(mw, BF16), mwt=jnp.asarray(mw.transpose(0, 2, 1), BF16),
        bc=jnp.asarray(bc, BF16), bct=jnp.asarray(bc.transpose(0, 2, 1), BF16),
        cw=jnp.asarray(np.stack(cw)), ch=jnp.asarray(np.stack(ch)), cc=jnp.asarray(np.stack(cc)))


def _gspec(*shape):
    nd = len(shape)
    return pl.BlockSpec((None,) + tuple(shape), lambda g: (g,) + (0,) * nd)


def _pool_fwd(proj, pc, wg, scale, ctx_len, D, name):
    T = proj.shape[0]
    seq = T - ctx_len
    dp = D // 2
    pg = dp // len(POOL_WINDOWS)
    nblk = seq // POOL_TB
    padt = POOL_PAD_ROWS * GRID_W
    p_col0 = 5 * D // pg

    def body(p_ref, mw_ref, bc_ref, cw_ref, ch_ref, cc_ref, wg_ref, sc_ref, pd_ref, y0_ref, r_ref, pad_ref):
        g = pl.program_id(0)

        def tail(rows, mean, x):
            pdb = (mean - x).astype(BF16)
            y0 = _dot(pdb, wg_ref[...])
            pd_ref[rows, :] = pdb
            y0_ref[rows, :] = y0
            r_ref[rows, :] = (y0 * sc_ref[...]).astype(BF16)

        xc = p_ref[0:ctx_len, :].astype(F32)
        tail(slice(0, ctx_len), _dot2(bc_ref[...], xc) / cc_ref[...], xc)

        pad_ref[0:padt, :] = jnp.zeros((padt, pg), F32)
        pad_ref[padt + seq:, :] = jnp.zeros((padt, pg), F32)

        def wpass(b, carry):
            rows = pl.ds(pl.multiple_of(ctx_len + b * POOL_TB, CHUNK), POOL_TB)
            dst = pl.ds(pl.multiple_of(padt + b * POOL_TB, CHUNK), POOL_TB)
            pad_ref[dst, :] = _dot2(mw_ref[...], p_ref[rows, :].astype(F32)) / cw_ref[...]
            return carry

        lax.fori_loop(0, nblk, wpass, 0)

        for gi, w in enumerate(POOL_WINDOWS):
            lo, hi = w // 2, w - w // 2 - 1

            @pl.when(g == gi)
            def _():
                def hpass(b, carry):
                    acc = jnp.zeros((POOL_TB, pg), F32)
                    for d in range(-lo, hi + 1):
                        src = pl.ds(pl.multiple_of(padt + b * POOL_TB + d * GRID_W, CHUNK), POOL_TB)
                        acc = acc + pad_ref[src, :]
                    mean = acc / ch_ref[pl.ds(pl.multiple_of(b * POOL_TB, CHUNK), POOL_TB), :]
                    rows = pl.ds(pl.multiple_of(ctx_len + b * POOL_TB, CHUNK), POOL_TB)
                    tail(rows, mean, p_ref[rows, :].astype(F32))
                    return carry

                lax.fori_loop(0, nblk, hpass, 0)

    col = lambda g: (0, g)
    return pl.pallas_call(
        body, name=name, grid=(len(POOL_WINDOWS),),
        in_specs=[pl.BlockSpec((T, pg), lambda g: (0, p_col0 + g)),
                  _gspec(POOL_TB, POOL_TB), _gspec(ctx_len, ctx_len), _gspec(POOL_TB, 1), _gspec(seq, 1),
                  _gspec(ctx_len, 1), _gspec(pg, pg), pl.BlockSpec((1, pg), col)],
        out_specs=[pl.BlockSpec((T, pg), col)] * 3,
        out_shape=[jax.ShapeDtypeStruct((T, dp), BF16), jax.ShapeDtypeStruct((T, dp), F32),
                   jax.ShapeDtypeStruct((T, dp), BF16)],
        scratch_shapes=[pltpu.VMEM((seq + 2 * padt, pg), F32)],
        compiler_params=_cparams(("arbitrary",)),
    )(proj, pc["mw"], pc["bc"], pc["cw"], pc["ch"], pc["cc"], wg, scale)


def _pool_bwd(dr, y0, pd, pc, wg, scale, ctx_len, D, name):
    T = dr.shape[0]
    seq = T - ctx_len
    dp = D // 2
    ng = len(POOL_WINDOWS)
    pg = dp // ng
    nblk = seq // POOL_TB
    padt = POOL_PAD_ROWS * GRID_W

    def body(dr_ref, y0_ref, pd_ref, mwt_ref, bct_ref, cw_ref, ch_ref, cc_ref, wg_ref, sc_ref,
             dp_ref, dsc_ref, gwg_ref, pad_ref, dpd_ref):
        g = pl.program_id(0)
        dsc_ref[...] = jnp.zeros_like(dsc_ref)
        gwg_ref[...] = jnp.zeros_like(gwg_ref)

        def head(rows):
            drv = dr_ref[rows, :]
            dsc_ref[...] += jnp.sum(drv * y0_ref[rows, :], axis=0, keepdims=True)
            dy0 = (drv * sc_ref[...]).astype(BF16)
            gwg_ref[...] += _dot_tn(pd_ref[rows, :], dy0)
            return _dot_nt(dy0, wg_ref[...])

        crow = slice(0, ctx_len)
        dpd_c = head(crow)
        dp_ref[crow, :] = (_dot2(bct_ref[...], dpd_c / cc_ref[...]) - dpd_c).astype(BF16)

        pad_ref[0:padt, :] = jnp.zeros((padt, pg), F32)
        pad_ref[padt + seq:, :] = jnp.zeros((padt, pg), F32)

        def first(b, carry):
            rows = pl.ds(pl.multiple_of(ctx_len + b * POOL_TB, CHUNK), POOL_TB)
            lrows = pl.ds(pl.multiple_of(b * POOL_TB, CHUNK), POOL_TB)
            dst = pl.ds(pl.multiple_of(padt + b * POOL_TB, CHUNK), POOL_TB)
            dpd = head(rows)
            dpd_ref[lrows, :] = dpd
            pad_ref[dst, :] = dpd / ch_ref[lrows, :]
            return carry

        lax.fori_loop(0, nblk, first, 0)

        for gi, w in enumerate(POOL_WINDOWS):
            lo, hi = w // 2, w - w // 2 - 1

            @pl.when(g == gi)
            def _():
                def second(b, carry):
                    acc = jnp.zeros((POOL_TB, pg), F32)
                    for d in range(-hi, lo + 1):
                        src = pl.ds(pl.multiple_of(padt + b * POOL_TB + d * GRID_W, CHUNK), POOL_TB)
                        acc = acc + pad_ref[src, :]
                    rows = pl.ds(pl.multiple_of(ctx_len + b * POOL_TB, CHUNK), POOL_TB)
                    lrows = pl.ds(pl.multiple_of(b * POOL_TB, CHUNK), POOL_TB)
                    dx = _dot2(mwt_ref[...], acc / cw_ref[...]) - dpd_ref[lrows, :]
                    dp_ref[rows, :] = dx.astype(BF16)
                    return carry

                lax.fori_loop(0, nblk, second, 0)

    col = lambda g: (0, g)
    return pl.pallas_call(
        body, name=name, grid=(ng,),
        in_specs=[pl.BlockSpec((T, pg), col), pl.BlockSpec((T, pg), col), pl.BlockSpec((T, pg), col),
                  _gspec(POOL_TB, POOL_TB), _gspec(ctx_len, ctx_len), _gspec(POOL_TB, 1), _gspec(seq, 1),
                  _gspec(ctx_len, 1), _gspec(pg, pg), pl.BlockSpec((1, pg), col)],
        out_specs=[pl.BlockSpec((T, pg), col), pl.BlockSpec((1, pg), col), _gspec(pg, pg)],
        out_shape=[jax.ShapeDtypeStruct((T, dp), BF16), jax.ShapeDtypeStruct((1, dp), F32),
                   jax.ShapeDtypeStruct((ng, pg, pg), F32)],
        scratch_shapes=[pltpu.VMEM((seq + 2 * padt, pg), F32), pltpu.VMEM((seq, pg), F32)],
        compiler_params=_cparams(("arbitrary",)),
    )(dr, y0, pd, pc["mwt"], pc["bct"], pc["cw"], pc["ch"], pc["cc"], wg, scale)


def _loss_head(x2, target, rb, name):
    T, D = x2.shape

    def body(y_ref, t_ref, dy_ref, l_ref):
        i = pl.program_id(0)

        @pl.when(i == 0)
        def _():
            dy_ref[...] = jnp.zeros_like(dy_ref)
            l_ref[...] = jnp.zeros_like(l_ref)

        @pl.when(i > 0)
        def _():
            e = y_ref[...] - t_ref[...]
            dy_ref[...] = e * (1.0 / D)
            l_ref[...] += 0.5 * jnp.sum(jnp.mean(e * e, axis=-1, keepdims=True), axis=0, keepdims=True)

    return pl.pallas_call(
        body, name=name, grid=(T // rb,),
        in_specs=[_row(rb, D), pl.BlockSpec((rb, D), lambda i: (jnp.maximum(i - 1, 0), 0))],
        out_specs=[_row(rb, D), pl.BlockSpec((8, 128), lambda i: (0, 0))],
        out_shape=[jax.ShapeDtypeStruct((T, D), F32), jax.ShapeDtypeStruct((8, 128), F32)],
        compiler_params=_cparams(("arbitrary",)),
    )(x2, target)


def _sum_lead(x, name):
    S, R, C = x.shape

    def body(x_ref, o_ref):
        acc = x_ref[0]
        for s in range(1, S):
            acc = acc + x_ref[s]
        o_ref[...] = acc

    return pl.pallas_call(
        body, name=name, out_shape=jax.ShapeDtypeStruct((R, C), F32),
        compiler_params=_cparams(),
    )(x)


def _silu_rows(cond, name):
    def body(c_ref, o_ref):
        o_ref[...] = _silu(c_ref[...]).astype(BF16)

    return pl.pallas_call(body, name=name, out_shape=jax.ShapeDtypeStruct(cond.shape, BF16),
                          compiler_params=_cparams())(cond)


def _silu_grad(cond, ds, name):
    def body(c_ref, d_ref, o_ref):
        _, vjp = jax.vjp(_silu, c_ref[...])
        o_ref[...] = vjp(d_ref[...])[0]

    return pl.pallas_call(body, name=name, out_shape=jax.ShapeDtypeStruct(cond.shape, F32),
                          compiler_params=_cparams())(cond, ds)


def _adamw_math(g, w_ref, m_ref, v_ref, go_ref, d_ref, mo_ref, vo_ref):
    c1 = 1.0 / (1.0 - ADAM_B1 ** ADAM_STEP)
    c2 = 1.0 / (1.0 - ADAM_B2 ** ADAM_STEP)
    mn = ADAM_B1 * m_ref[...] + (1.0 - ADAM_B1) * g
    vn = ADAM_B2 * v_ref[...] + (1.0 - ADAM_B2) * (g * g)
    go_ref[...] = g
    mo_ref[...] = mn
    vo_ref[...] = vn
    d_ref[...] = -ADAM_LR * ((mn * c1) / (jnp.sqrt(vn * c2) + ADAM_EPS) + ADAM_WD * w_ref[...])


def _adamw(gs, w, m, v, name, job=None):
    S, R, C = gs.shape
    cpad = -(-C // 128) * 128
    rt = _pick(R, max(16, (1 << 20) // (4 * cpad)), 16)

    def body(g_ref, w_ref, m_ref, v_ref, go_ref, d_ref, mo_ref, vo_ref):
        g = g_ref[0].astype(F32)
        for s in range(1, S):
            g = g + g_ref[s].astype(F32)
        _adamw_math(g, w_ref, m_ref, v_ref, go_ref, d_ref, mo_ref, vo_ref)

    blk = pl.BlockSpec((rt, C), lambda i: (i, 0))
    in_specs = [pl.BlockSpec((S, rt, C), lambda i: (0, i, 0)), blk, blk, blk]
    out_shape = [jax.ShapeDtypeStruct((R, C), F32)] * 4
    if job is None:
        return pl.pallas_call(
            body, name=name, grid=(R // rt,), in_specs=in_specs, out_specs=[blk] * 4, out_shape=out_shape,
            compiler_params=_cparams(("parallel",)),
        )(gs, w, m, v)
    return _call_carrying(job, body, name, (R // rt,), in_specs, [blk] * 4, out_shape, [], (gs, w, m, v))


def _adamw_layer(gs, w, m, v, l, prev, name):
    S, R, C = gs.shape
    L = w.shape[0]
    cpad = -(-C // 128) * 128
    rt = _pick(R, max(16, (1 << 20) // (4 * cpad)), 16)

    def body(*refs):
        g_ref, w_ref, m_ref, v_ref = refs[:4]
        go_ref, d_ref, mo_ref, vo_ref = refs[-4:]
        g = g_ref[0].astype(F32)
        for s in range(1, S):
            g = g + g_ref[s].astype(F32)
        _adamw_math(g, w_ref, m_ref, v_ref, go_ref, d_ref, mo_ref, vo_ref)

    blk = pl.BlockSpec((None, rt, C), lambda i: (l, i, 0))
    in_specs = [pl.BlockSpec((S, rt, C), lambda i: (0, i, 0)), blk, blk, blk]
    args = [gs, w, m, v]
    aliases = {}
    if prev is not None:
        in_specs += [pl.BlockSpec(memory_space=pl.ANY)] * 4
        args += list(prev)
        aliases = {4 + q: q for q in range(4)}
    return pl.pallas_call(
        body, name=name, grid=(R // rt,), in_specs=in_specs,
        out_specs=[blk] * 4, out_shape=[jax.ShapeDtypeStruct((L, R, C), F32)] * 4,
        input_output_aliases=aliases,
        compiler_params=_cparams(("parallel",)),
    )(*args)


def _adamw_nd(gs, w, m, v, name, job=None):
    shp = w.shape
    if len(shp) == 1:
        r, c = 1, shp[0]
    else:
        r, c = int(np.prod(shp[:-1])), shp[-1]
    outs = _adamw(gs.reshape(gs.shape[0], r, c), w.reshape(r, c), m.reshape(r, c), v.reshape(r, c), name, job)
    if job is None:
        return [o.reshape(shp) for o in outs]
    return [o.reshape(shp) for o in outs[0]], outs[1]


def kernel(x, c, ctx, c_ctx, w_ada, b_ada, w_in, w_decay_up, b_decay_up, gla_norm_gain, w_pool_group, pool_scale, w_gla_out, w_pool_out, w_out, ln_mix_gain, ln_mix_bias, w_ffn_in, w_ffn_out, ln_ffn_gain, ln_ffn_bias, loss_target, m_c_ctx, m_w_ada, m_b_ada, m_w_in, m_w_decay_up, m_b_decay_up, m_gla_norm_gain, m_w_pool_group, m_pool_scale, m_w_gla_out, m_w_pool_out, m_w_out, m_ln_mix_gain, m_ln_mix_bias, m_w_ffn_in, m_w_ffn_out, m_ln_ffn_gain, m_ln_ffn_bias, v_c_ctx, v_w_ada, v_b_ada, v_w_in, v_w_decay_up, v_b_decay_up, v_gla_norm_gain, v_w_pool_group, v_pool_scale, v_w_gla_out, v_w_pool_out, v_w_out, v_ln_mix_gain, v_ln_mix_bias, v_w_ffn_in, v_w_ffn_out, v_ln_ffn_gain, v_ln_ffn_bias):
    L, D = w_ada.shape[0], w_ada.shape[1]
    seq, ctx_len = x.shape[1], ctx.shape[1]
    T = seq + ctx_len
    rb = ctx_len
    DK = D // 2
    DP = D // 2
    ng = len(POOL_WINDOWS)
    pg = DP // ng
    dff = w_ffn_out.shape[1] * N_DEV
    alpha = (2.0 * L) ** 0.25
    assert seq % rb == 0 and rb % CHUNK == 0 and seq % POOL_TB == 0 and ctx_len % 8 == 0
    xi, yi, ci = _my_pos()
    me = 4 * xi + 2 * yi + ci
    pc = _pool_consts(ctx_len, seq)

    shards = dict(w_in=w_in.astype(BF16), go=w_gla_out.astype(BF16), po=w_pool_out.astype(BF16),
                  out=w_out.astype(BF16), fi=w_ffn_in.astype(BF16), fo=w_ffn_out.astype(BF16),
                  pg=w_pool_group.astype(BF16).reshape(L, ng * pg // N_DEV, pg))
    wkeys = ("w_in", "go", "po", "out", "fi", "fo", "pg")

    def prepared(gw):
        W = {}
        if "w_in" in gw:
            W["main"], W["alr"] = _w_in_operands(gw["w_in"], "w_in_operands")
            W["pg"] = jnp.swapaxes(gw["pg"].reshape(N_DEV, ng, pg // N_DEV, pg), 0, 1).reshape(ng, pg, pg)
            W.update(go=gw["go"].reshape(D, D), po=gw["po"], out=gw["out"].reshape(D, D))
        if "fi" in gw:
            W["fi"] = gw["fi"][None]
        if "fo" in gw:
            W["fo"] = gw["fo"].reshape(1, dff, D)
        return W

    def gather_next(fn, names, l, nxt, own=(), cur=None):
        keys = [(k, l + 1) for k in names if l + 1 < L] + [(k, l) for k in own]
        if not keys:
            return fn(None)
        res, outs = fn(_gather_job([shards[k] for k, _ in keys], [ll for _, ll in keys]))
        for (k, ll), o in zip(keys, outs):
            (cur if ll == l else nxt)[k] = o
        return res

    first = ("w_in", "go", "po", "out", "pg")
    gathered = dict(zip(first, _run_job(_gather_job([shards[k] for k in first], [0] * len(first)), "ag_layer0")))

    dku = w_decay_up.shape[-1]
    small_in = jnp.concatenate([c.reshape(-1), w_decay_up.reshape(-1), b_decay_up.reshape(-1)])
    (small_all,) = _gather_flat([small_in], "ag_small")
    c_all = small_all[:, :D]
    n_wdu = L * 2 * GATE_RANK * dku
    wdu_all = small_all[:, D:D + n_wdu].reshape(N_DEV, L, 2, GATE_RANK, dku)
    wdu_full = jnp.transpose(wdu_all, (1, 2, 3, 0, 4)).reshape(L, 2, GATE_RANK, DK)
    bdu_all = small_all[:, D + n_wdu:].reshape(N_DEV, L, 2, dku)
    bdu_full = jnp.transpose(bdu_all, (1, 2, 0, 3)).reshape(L, 1, 2 * DK)
    wdu_bd = jnp.zeros((L, ALR_PAD, 2 * DK), F32)
    wdu_bd = wdu_bd.at[:, :GATE_RANK, :DK].set(wdu_full[:, 0])
    wdu_bd = wdu_bd.at[:, GATE_RANK:2 * GATE_RANK, DK:].set(wdu_full[:, 1]).astype(BF16)

    ncond = 16
    cond = jnp.concatenate([c_all, c_ctx.reshape(1, D), jnp.zeros((ncond - N_DEV - 1, D), F32)], axis=0)
    s_cond = _silu_rows(cond, "silu_cond")
    wsh = w_ada.shape[-1]
    b_ada_mine = lax.dynamic_slice_in_dim(b_ada, me * wsh, wsh, axis=1)
    mod_part = jnp.stack([_mm(s_cond, w_ada, "nn", F32, "mod_mm", bias=b_ada_mine[l:l + 1], b_pre=(l,))
                          for l in range(L)])
    (mod_all,) = _all_gather([mod_part], "ag_mod")
    mod_all = jnp.swapaxes(mod_all, 1, 2).reshape(L, ncond, N_MOD * D)
    mod_lat = lax.dynamic_slice_in_dim(mod_all, me, 1, axis=1)
    mods = jnp.concatenate([mod_all[:, N_DEV:N_DEV + 1], mod_lat], axis=1).reshape(L, 2, 1, N_MOD * D)
    SH_M, SC_M, GT_M, SH_F, SC_F, GT_F = range(N_MOD)

    xa = jnp.concatenate([ctx[0], x[0]], axis=0)
    vec = lambda a, l: a[l].reshape(1, -1)
    saved = []
    h = _mod_fwd(xa, mods[0], SC_M, SH_M, rb, "mod_fwd")
    weights = []
    for l in range(L):
        W = prepared(gathered)
        weights.append(W)
        gathered = {}
        late = {}
        proj = gather_next(lambda j: _mm(h, W["main"], "nn", BF16, "mm_in", job=j), ["w_in"], l, gathered)
        alr = _mm(h, W["alr"], "nn", F32, "mm_alr")
        la = _decay_fwd(alr, wdu_bd[l], bdu_full[l], rb, "decay_fwd")
        o_f, s_f = gather_next(lambda j: _gla_fwd(proj, la, False, rb, D, "gla_fwd_f", job=j), ["go", "out"], l,
                               gathered, own=("fi",) if l == 0 else (), cur=late)
        o_b, s_b = gather_next(lambda j: _gla_fwd(proj, la, True, rb, D, "gla_fwd_b", job=j), ["po", "pg"], l,
                               gathered, own=("fo",) if l == 0 else (), cur=late)
        W.update(prepared(late))
        u = _glaout_fwd(o_f, o_b, proj, vec(gla_norm_gain, l), rb, "glaout_fwd")
        y_gla = _mm(u, W["go"], "nn", F32, "mm_go")
        pd, y0, r = _pool_fwd(proj, pc, W["pg"], vec(pool_scale, l), ctx_len, D, "pool_fwd")
        y_pool = _mm(r, W["po"], "nn", F32, "mm_po", b_shard=True)
        m_ = _merge_fwd(proj, y_gla, y_pool, rb, "merge_fwd")
        mix = _mm(m_, W["out"], "nn", F32, "mm_out")
        x1, h2 = _unit_fwd(alpha, xa, mix, mods[l], GT_M, vec(ln_mix_gain, l), vec(ln_mix_bias, l),
                           (mods[l], SC_F, SH_F), rb, "unit_mix_fwd")
        ff, s_ = gather_next(lambda j: _ffn_in_fwd(h2, W["fi"], 0, "mm_fi_swiglu", job=j), ["fi"], l, gathered)
        ffn = gather_next(lambda j: _mm(s_, W["fo"], "nn", F32, "mm_fo", b_pre=(0,), job=j), ["fo"], l, gathered)
        nxt = (mods[l + 1], SC_M, SH_M) if l + 1 < L else None
        x2, h_next = _unit_fwd(alpha, x1, ffn, mods[l], GT_F, vec(ln_ffn_gain, l), vec(ln_ffn_bias, l),
                               nxt, rb, "unit_ffn_fwd")
        saved.append(dict(xa=xa, h=h, proj=proj, alr=alr, la=la, o_f=o_f, o_b=o_b, s_f=s_f, s_b=s_b, u=u,
                          y_gla=y_gla, pd=pd, y0=y0, r=r, y_pool=y_pool, m=m_, mix=mix, x1=x1, h2=h2, ff=ff,
                          s=s_, ffn=ffn))
        xa, h = x2, h_next

    dxo, loss_part = _loss_head(xa, loss_target[0], rb, "loss_head")
    loss = lax.psum(loss_part[0, 0], ("x", "y", "c"))

    big_params = [("w_in", w_in, m_w_in, v_w_in), ("w_gla_out", w_gla_out, m_w_gla_out, v_w_gla_out),
                  ("w_pool_out", w_pool_out, m_w_pool_out, v_w_pool_out), ("w_out", w_out, m_w_out, v_w_out),
                  ("w_ffn_in", w_ffn_in, m_w_ffn_in, v_w_ffn_in), ("w_ffn_out", w_ffn_out, m_w_ffn_out, v_w_ffn_out),
                  ("w_pool_group", w_pool_group, m_w_pool_group, v_w_pool_group)]
    big_out = {nm: None for nm, _, _, _ in big_params}
    g_small = {k: [None] * L for k in ("gla_gain", "pool_scale", "mix_g", "mix_b", "ffn_g", "ffn_b", "wdu", "bdu")}
    dmods = [None] * L
    dh = None
    sum2 = lambda a: a[0] + a[1]
    rows8 = lambda g: g.reshape(N_DEV, g.shape[0] // N_DEV, g.shape[1])

    def apply_adamw(parts, layer):
        for (nm, w, m, v), gs in zip(big_params, parts):
            R, C = gs.shape[1], gs.shape[2]
            big_out[nm] = _adamw_layer(gs, w.reshape(L, R, C), m.reshape(L, R, C), v.reshape(L, R, C), layer,
                                       big_out[nm], "adamw_" + nm)

    LATE = (0, 1, 2, 3, 6)
    late_chunks = None
    arrived = {}

    def behind(fn, job, positions):
        if job is None:
            return fn(None)
        res, outs = fn(job)
        if positions is None:
            return res, outs
        arrived.update(zip(positions, outs))
        return res

    for l in range(L - 1, -1, -1):
        sv = saved[l]
        W = weights[l]
        nxt = (mods[l + 1], SC_M, SH_M) if l + 1 < L else None
        unit = lambda j: _unit_bwd(alpha, dxo, dh, sv["x1"], sv["ffn"], mods[l], GT_F, vec(ln_ffn_gain, l),
                                   vec(ln_ffn_bias, l), nxt, rb, "unit_ffn_bwd", job=j)
        late_pairs = None
        if late_chunks is None:
            res = unit(None)
        else:
            res, sib = behind(unit, _sibling_job(late_chunks), None)
            late_pairs = _pair_adds(late_chunks, sib, "_late")
        dx1, dffn, d_gtf, d_gf, d_bf, d_scm_n, d_shm_n = res
        if nxt is not None:
            dmods[l + 1]["sc_m"], dmods[l + 1]["sh_m"] = d_scm_n, d_shm_n
        dmods[l] = dict(gt_f=d_gtf)
        g_small["ffn_g"][l], g_small["ffn_b"][l] = sum2(d_gf), sum2(d_bf)
        dff_ = behind(lambda j: _ffn_out_dx(dffn, W["fo"], 0, sv["ff"], "mm_fo_dx_swiglu", job=j),
                      _chip_job(late_pairs[1:]) if late_pairs else None, LATE[1:])
        c_fo = rows8(_mm(sv["s"], dffn, "tn", BF16, "mm_fo_dw"))
        dh2 = behind(lambda j: _mm(dff_, W["fi"], "nt", F32, "mm_fi_dx", b_pre=(0,), b_shard=True, a_half=True,
                                   job=j), _chip_job(late_pairs[:1]) if late_pairs else None, LATE[:1])
        c_fi = _mm(sv["h2"], dff_, "tn", BF16, "mm_fi_dw", b_half=True, out_shard=True)
        if late_pairs:
            apply_adamw([arrived[i] for i in range(len(big_params))], l + 1)
            arrived = {}
        ffn_chunks = [c_fi, c_fo]
        res, sib = behind(lambda j: _unit_bwd(
            alpha, dx1, dh2, sv["xa"], sv["mix"], mods[l], GT_M, vec(ln_mix_gain, l), vec(ln_mix_bias, l),
            (mods[l], SC_F, SH_F), rb, "unit_mix_bwd", job=j), _sibling_job(ffn_chunks), None)
        dxa, dmix, d_gtm, d_gm, d_bm, d_scf, d_shf = res
        ffn_pairs = _pair_adds(ffn_chunks, sib, "_ffn")
        dmods[l].update(gt_m=d_gtm, sc_f=d_scf, sh_f=d_shf)
        g_small["mix_g"][l], g_small["mix_b"][l] = sum2(d_gm), sum2(d_bm)
        dm = _mm(dmix, W["out"], "nt", F32, "mm_out_dx")
        c_out = rows8(_mm(sv["m"], dmix, "tn", BF16, "mm_out_dw"))
        dbg1, dbg2, dyg, dyp = _merge_bwd(dm, sv["proj"], sv["y_gla"], sv["y_pool"], rb, "merge_bwd")
        dr = _mm(dyp, W["po"], "nt", F32, "mm_po_dx", b_shard=True)
        c_po = _mm(sv["r"], dyp, "tn", BF16, "mm_po_dw", out_shard=True)
        dp_, d_ps, g_pgl = _pool_bwd(dr, sv["y0"], sv["pd"], pc, W["pg"], vec(pool_scale, l), ctx_len, D, "pool_bwd")
        g_small["pool_scale"][l] = d_ps
        c_pg = jnp.swapaxes(g_pgl.astype(BF16).reshape(ng, N_DEV, pg // N_DEV, pg), 0, 1).reshape(N_DEV, -1, pg)
        du = _mm(dyg, W["go"], "nt", F32, "mm_go_dx")
        c_go = rows8(_mm(sv["u"], dyg, "tn", BF16, "mm_go_dw"))
        do, dg, d_gg = _glaout_bwd(du, sv["o_f"], sv["o_b"], sv["proj"], vec(gla_norm_gain, l), rb, "glaout_bwd")
        g_small["gla_gain"][l] = sum2(d_gg)
        dq_f, dk_f, dv_f, dla_f = behind(lambda j: _gla_bwd(
            sv["proj"], sv["la"], do, sv["s_f"], False, rb, D, None, "gla_bwd_f", job=j),
            _chip_job(ffn_pairs[:1]), (4,))
        dq, dk, dv, dla_b = behind(lambda j: _gla_bwd(
            sv["proj"], sv["la"], do, sv["s_b"], True, rb, D, (dq_f, dk_f, dv_f), "gla_bwd_b", job=j),
            _chip_job(ffn_pairs[1:]), (5,))
        dalr, g_wdu, g_bdu = _decay_bwd(dla_f, dla_b, sv["alr"], wdu_bd[l], bdu_full[l], rb, "decay_bwd")
        g_small["wdu"][l] = jnp.stack([g_wdu[:GATE_RANK, :DK], g_wdu[GATE_RANK:2 * GATE_RANK, DK:]])
        g_small["bdu"][l] = g_bdu.reshape(2, DK)
        dproj = jnp.concatenate([dv, dg, dbg1, dbg2, dq, dk, dp_], axis=1)
        dh_alr = _mm(dalr, W["alr"], "nt", F32, "mm_alr_dx")
        dh = _mm(dproj, W["main"], "nt", F32, "mm_in_dx", add=dh_alr)
        g_main = _mm(sv["h"], dproj, "tn", BF16, "mm_in_dw")
        g_alr = _mm(sv["h"], dalr, "tn", BF16, "mm_alr_dw")
        c_in = _w_in_chunks(g_main, g_alr, w_in.shape[2], "w_in_chunks")
        late_chunks = [c_in, c_go, c_po, c_out, c_pg]
        dxo = dxa
    sib = _run_job(_sibling_job(late_chunks), "rs_sibling_last")
    last_pairs = _pair_adds(late_chunks, sib, "_last")
    grad_xa, d_scm0, d_shm0 = _mod_bwd(dxo, dh, saved[0]["xa"], mods[0], SC_M, SH_M, rb, "mod_bwd")
    dmods[0]["sc_m"], dmods[0]["sh_m"] = d_scm0, d_shm0
    grad_x = grad_xa[ctx_len:].reshape(1, seq, D)

    order = ("sh_m", "sc_m", "gt_m", "sh_f", "sc_f", "gt_f")
    dmod = jnp.stack([jnp.concatenate([dmods[l][k] for k in order], axis=2) for l in range(L)])
    dmod = dmod.reshape(-1)
    sm = lambda k: jnp.stack([a.reshape(-1) for a in g_small[k]]).reshape(-1)
    small_keys = ("gla_gain", "pool_scale", "mix_g", "mix_b", "ffn_g", "ffn_b", "wdu", "bdu")
    small_part = jnp.concatenate([sm(k) for k in small_keys])
    small_g, dmod_g = _gather_flat([small_part, dmod], "ag_small_grads")
    small_sum = _sum_lead(small_g.reshape(N_DEV, -1, 128), "sum_small").reshape(-1)
    off = 0
    rep = {}
    for k, n in zip(small_keys, (L * D, L * DP, L * D, L * D, L * D, L * D, L * 2 * GATE_RANK * DK, L * 2 * DK)):
        rep[k] = small_sum[off:off + n]
        off += n
    g_wdu_mine = lax.dynamic_slice_in_dim(rep["wdu"].reshape(L, 2, GATE_RANK, DK), me * dku, dku, axis=3)
    g_bdu_mine = lax.dynamic_slice_in_dim(rep["bdu"].reshape(L, 2, DK), me * dku, dku, axis=2)

    dmod_all = dmod_g.reshape(N_DEV, L, 2, N_MOD * D)
    dm_ctx = _sum_lead(dmod_all[:, :, 0].reshape(N_DEV, L, N_MOD * D), "sum_dmod_ctx")
    dm_rows = jnp.concatenate([jnp.swapaxes(dmod_all[:, :, 1], 0, 1), dm_ctx[:, None],
                               jnp.zeros((L, ncond - N_DEV - 1, N_MOD * D), F32)], axis=1)
    g_b_ada = _sum_lead(jnp.swapaxes(dm_rows, 0, 1), "sum_b_ada")
    dm_mine = lax.dynamic_slice_in_dim(dm_rows, me * wsh, wsh, axis=2).astype(BF16)
    g_w_ada = jnp.stack([_mm(s_cond, dm_mine[l], "tn", F32, "ada_dw") for l in range(L)])
    ds_part = _sum_lead(jnp.stack([_mm(dm_mine[l], w_ada, "nt", F32, "ada_dx", b_pre=(l,)) for l in range(L)]),
                        "sum_ds")
    (ds_all,) = _gather_flat([ds_part[N_DEV]], "ag_ds")
    ds_ctx = _sum_lead(ds_all.reshape(N_DEV, 1, D), "sum_ds_ctx")
    g_c_ctx = _silu_grad(c_ctx.reshape(1, D), ds_ctx, "silu_grad").reshape(D)

    one = lambda g: g[None]
    small_table = {
        "c_ctx": (one(g_c_ctx), c_ctx, m_c_ctx, v_c_ctx),
        "w_ada": (one(g_w_ada), w_ada, m_w_ada, v_w_ada),
        "b_ada": (one(g_b_ada), b_ada, m_b_ada, v_b_ada),
        "w_decay_up": (one(g_wdu_mine), w_decay_up, m_w_decay_up, v_w_decay_up),
        "b_decay_up": (one(g_bdu_mine), b_decay_up, m_b_decay_up, v_b_decay_up),
        "gla_norm_gain": (one(rep["gla_gain"].reshape(L, D)), gla_norm_gain, m_gla_norm_gain, v_gla_norm_gain),
        "pool_scale": (one(rep["pool_scale"].reshape(L, DP)), pool_scale, m_pool_scale, v_pool_scale),
        "ln_mix_gain": (one(rep["mix_g"].reshape(L, D)), ln_mix_gain, m_ln_mix_gain, v_ln_mix_gain),
        "ln_mix_bias": (one(rep["mix_b"].reshape(L, D)), ln_mix_bias, m_ln_mix_bias, v_ln_mix_bias),
        "ln_ffn_gain": (one(rep["ffn_g"].reshape(L, D)), ln_ffn_gain, m_ln_ffn_gain, v_ln_ffn_gain),
        "ln_ffn_bias": (one(rep["ffn_b"].reshape(L, D)), ln_ffn_bias, m_ln_ffn_bias, v_ln_ffn_bias),
    }
    big_shapes = {nm: w.shape for nm, w, _, _ in big_params}
    names = ("c_ctx", "w_ada", "b_ada", "w_in", "w_decay_up", "b_decay_up", "gla_norm_gain", "w_pool_group",
             "pool_scale", "w_gla_out", "w_pool_out", "w_out", "ln_mix_gain", "ln_mix_bias", "w_ffn_in", "w_ffn_out",
             "ln_ffn_gain", "ln_ffn_bias")
    ada_res, last_parts = _adamw_nd(*small_table["w_ada"], "adamw_w_ada", job=_chip_job(last_pairs))
    arrived.update(zip(LATE, last_parts))
    apply_adamw([arrived[i] for i in range(len(big_params))], 0)
    grads, deltas, new_m, new_v = [], [], [], []
    for nm in names:
        if nm == "w_ada":
            res = ada_res
        elif nm in small_table:
            res = _adamw_nd(*small_table[nm], "adamw_" + nm)
        else:
            res = [o.reshape(big_shapes[nm]) for o in big_out[nm]]
        for lst, o in zip((grads, deltas, new_m, new_v), res):
            lst.append(o)
    return (loss, grad_x, *grads, *deltas, *new_m, *new_v)
```

```python
import functools
import math

import numpy as np
import jax
import jax.numpy as jnp
from jax import lax
from jax.experimental import pallas as pl
from jax.experimental.pallas import tpu as pltpu

F32 = jnp.float32
BF16 = jnp.bfloat16

N_DEV = 8
N_HEADS = 4
GATE_RANK = 16
GATE_NORM = 16.0
CHUNK = 64
GRID_W = 64
POOL_WINDOWS = (2, 4, 8, 16)
N_MOD = 6
LN_EPS = 1e-5
RMS_EPS = 1e-6
ALR_PAD = 128
POOL_TB = 256
POOL_PAD_ROWS = 8
ADAM_LR = 0.001
ADAM_B1 = 0.9
ADAM_B2 = 0.999
ADAM_EPS = 1e-08
ADAM_WD = 0.01
ADAM_STEP = 10
VMEM_LIMIT = 56 * 1024 * 1024
MESH = pl.DeviceIdType.MESH


def _cparams(sem=None):
    return pltpu.CompilerParams(dimension_semantics=sem, vmem_limit_bytes=VMEM_LIMIT)


def _pick(dim, cap, mult):
    best = None
    for d in range(mult, min(dim, cap) + 1, mult):
        if dim % d == 0:
            best = d
    return best if best is not None else dim


def _sig(x):
    return 1.0 / (1.0 + jnp.exp(-x))


def _silu(x):
    return x * _sig(x)


def _dot(a, b):
    return lax.dot_general(a, b, (((1,), (0,)), ((), ())), preferred_element_type=F32)


def _dot_nt(a, b):
    return lax.dot_general(a, b, (((1,), (1,)), ((), ())), preferred_element_type=F32)


def _dot_tn(a, b):
    return lax.dot_general(a, b, (((0,), (0,)), ((), ())), preferred_element_type=F32)


def _split2(x):
    hi = x.astype(BF16)
    lo = (x - hi.astype(F32)).astype(BF16)
    return hi, lo


def _dot2(m_b, x):
    hi, lo = _split2(x)
    return _dot(m_b, hi) + _dot(m_b, lo)


def _dot3(m_b, x):
    h1 = x.astype(BF16)
    r1 = x - h1.astype(F32)
    h2 = r1.astype(BF16)
    h3 = (r1 - h2.astype(F32)).astype(BF16)
    return _dot(m_b, h1) + _dot(m_b, h2) + _dot(m_b, h3)


def _my_pos():
    return lax.axis_index("x"), lax.axis_index("y"), lax.axis_index("c")


def _all_gather(arrs, name):
    n = len(arrs)
    srcs = [a.reshape((a.shape[0], 1) + a.shape[1:]) for a in arrs]
    outs = [jax.ShapeDtypeStruct((a.shape[0], N_DEV) + a.shape[1:], a.dtype) for a in arrs]

    def body(*refs):
        in_refs, out_refs = refs[:n], refs[n:2 * n]
        send_sems, recv_sems, local_sems = refs[2 * n:]
        x, y, c = _my_pos()
        me, sibling = (x, y, c), (x, y, 1 - c)
        chips = [(1 - x, y), (x, 1 - y), (1 - x, 1 - y)]

        def slot(t, pos):
            return out_refs[t].at[:, pl.ds(4 * pos[0] + 2 * pos[1] + pos[2], 1)]

        def copy(t, k, block, to, src=None):
            return pltpu.make_async_remote_copy(
                src_ref=slot(t, block) if src is None else src, dst_ref=slot(t, block),
                send_sem=send_sems.at[t * 7 + k], recv_sem=recv_sems.at[t * 7 + k],
                device_id=to, device_id_type=MESH)

        mine = [pltpu.make_async_copy(in_refs[t], slot(t, me), local_sems.at[t]) for t in range(n)]
        for cp in mine:
            cp.start()
        first = []
        for t in range(n):
            first.append(copy(t, 0, me, sibling, src=in_refs[t]))
            first += [copy(t, 1 + j, me, (*chip, c), src=in_refs[t]) for j, chip in enumerate(chips)]
        for cp in first:
            cp.start()
        passed = []
        for j, chip in enumerate(chips):
            for t in range(n):
                copy(t, 1 + j, (*chip, c), me).wait_recv()
                fwd = copy(t, 4 + j, (*chip, c), sibling)
                fwd.start()
                passed.append(fwd)
        for t in range(n):
            copy(t, 0, sibling, me).wait_recv()
            for j, chip in enumerate(chips):
                copy(t, 4 + j, (*chip, 1 - c), me).wait_recv()
        for cp in first + passed:
            cp.wait_send()
        for cp in mine:
            cp.wait()

    any_spec = pl.BlockSpec(memory_space=pl.ANY)
    res = pl.pallas_call(
        body, name=name, out_shape=outs,
        in_specs=[any_spec] * n, out_specs=[any_spec] * n,
        scratch_shapes=[pltpu.SemaphoreType.DMA((7 * n,)), pltpu.SemaphoreType.DMA((7 * n,)),
                        pltpu.SemaphoreType.DMA((n,))],
        compiler_params=pltpu.CompilerParams(has_side_effects=True),
    )(*srcs)
    return list(res)


def _gather_flat(vecs, name):
    padded = []
    for v in vecs:
        n = v.shape[0]
        padded.append(jnp.pad(v, (0, -n % 128)).reshape(1, -1, 128))
    res = _all_gather(padded, name)
    return [r.reshape(N_DEV, -1)[:, :v.shape[0]] for r, v in zip(res, vecs)]


N_CHIP = 4


def _comm_call(body, name, arrs, outs, n_sems):
    any_spec = pl.BlockSpec(memory_space=pl.ANY)
    n = len(arrs)
    res = pl.pallas_call(
        body, name=name, out_shape=outs,
        in_specs=[any_spec] * n, out_specs=[any_spec] * len(outs),
        scratch_shapes=[pltpu.SemaphoreType.DMA((s,)) for s in n_sems],
        compiler_params=pltpu.CompilerParams(has_side_effects=True),
    )(*arrs)
    return list(res)


def _sibling_job(arrs):
    n = len(arrs)
    outs = [jax.ShapeDtypeStruct((N_CHIP,) + a.shape[1:], a.dtype) for a in arrs]

    def copies(in_refs, out_refs, sems):
        send_sems, recv_sems = sems
        x, y, c = _my_pos()
        return [pltpu.make_async_remote_copy(
            src_ref=in_refs[t].at[pl.ds(2 * k + (1 - c), 1)], dst_ref=out_refs[t].at[pl.ds(k, 1)],
            send_sem=send_sems.at[t * N_CHIP + k], recv_sem=recv_sems.at[t * N_CHIP + k],
            device_id=(x, y, 1 - c), device_id_type=MESH) for t in range(n) for k in range(N_CHIP)]

    def start(in_refs, out_refs, sems):
        for cp in copies(in_refs, out_refs, sems):
            cp.start()

    def finish(in_refs, out_refs, sems):
        cps = copies(in_refs, out_refs, sems)
        for cp in cps:
            cp.wait_recv()
        for cp in cps:
            cp.wait_send()

    return _Job(arrs, outs, (N_CHIP * n, N_CHIP * n), start, finish)


class _Job:
    def __init__(self, arrs, outs, n_sems, start, finish):
        self.arrs, self.outs, self.n_sems, self.start, self.finish = arrs, outs, n_sems, start, finish


def _gather_job(stacked, layers):
    n = len(stacked)
    outs = [jax.ShapeDtypeStruct((N_DEV,) + a.shape[1:], a.dtype) for a in stacked]

    def parts(in_refs, out_refs, sems):
        send_sems, recv_sems, local_sems = sems
        x, y, c = _my_pos()
        me, sibling = (x, y, c), (x, y, 1 - c)
        chips = [(1 - x, y), (x, 1 - y), (1 - x, 1 - y)]
        src = lambda t: in_refs[t].at[pl.ds(layers[t], 1)]

        def slot(t, pos):
            return out_refs[t].at[pl.ds(4 * pos[0] + 2 * pos[1] + pos[2], 1)]

        def copy(t, k, block, to, from_input=False):
            return pltpu.make_async_remote_copy(
                src_ref=src(t) if from_input else slot(t, block), dst_ref=slot(t, block),
                send_sem=send_sems.at[t * 7 + k], recv_sem=recv_sems.at[t * 7 + k],
                device_id=to, device_id_type=MESH)

        mine = [pltpu.make_async_copy(src(t), slot(t, me), local_sems.at[t]) for t in range(n)]
        first = []
        for t in range(n):
            first.append(copy(t, 0, me, sibling, True))
            first += [copy(t, 1 + j, me, (*chip, c), True) for j, chip in enumerate(chips)]
        return me, sibling, chips, copy, mine, first

    def start(in_refs, out_refs, sems):
        _, _, _, _, mine, first = parts(in_refs, out_refs, sems)
        for cp in mine + first:
            cp.start()

    def finish(in_refs, out_refs, sems):
        me, sibling, chips, copy, mine, first = parts(in_refs, out_refs, sems)
        passed = []
        for j, chip in enumerate(chips):
            for t in range(n):
                copy(t, 1 + j, (*chip, me[2]), me).wait_recv()
                fwd = copy(t, 4 + j, (*chip, me[2]), sibling)
                fwd.start()
                passed.append(fwd)
        for t in range(n):
            copy(t, 0, sibling, me).wait_recv()
            for j, chip in enumerate(chips):
                copy(t, 4 + j, (*chip, 1 - me[2]), me).wait_recv()
        for cp in first + passed:
            cp.wait_send()
        for cp in mine:
            cp.wait()

    return _Job(stacked, outs, (7 * n, 7 * n, n), start, finish)


def _chip_job(arrs):
    n = len(arrs)
    outs = [jax.ShapeDtypeStruct(a.shape, a.dtype) for a in arrs]

    def parts(in_refs, out_refs, sems):
        send_sems, recv_sems, local_sems = sems
        x, y, c = _my_pos()
        chip = 2 * x + y
        mine, sends, recvs = [], [], []
        for t in range(n):
            mine.append(pltpu.make_async_copy(in_refs[t].at[pl.ds(chip, 1)], out_refs[t].at[pl.ds(chip, 1)],
                                              local_sems.at[t]))
            for m in range(1, N_CHIP):
                px, py = x ^ (m >> 1), y ^ (m & 1)
                peer = 2 * px + py
                sends.append(pltpu.make_async_remote_copy(
                    src_ref=in_refs[t].at[pl.ds(peer, 1)], dst_ref=out_refs[t].at[pl.ds(chip, 1)],
                    send_sem=send_sems.at[t * 3 + m - 1], recv_sem=recv_sems.at[t * 3 + m - 1],
                    device_id=(px, py, c), device_id_type=MESH))
                recvs.append(pltpu.make_async_remote_copy(
                    src_ref=in_refs[t].at[pl.ds(peer, 1)], dst_ref=out_refs[t].at[pl.ds(peer, 1)],
                    send_sem=send_sems.at[t * 3 + m - 1], recv_sem=recv_sems.at[t * 3 + m - 1],
                    device_id=(x, y, c), device_id_type=MESH))
        return mine, sends, recvs

    def start(in_refs, out_refs, sems):
        mine, sends, _ = parts(in_refs, out_refs, sems)
        for cp in mine + sends:
            cp.start()

    def finish(in_refs, out_refs, sems):
        mine, sends, recvs = parts(in_refs, out_refs, sems)
        for cp in recvs:
            cp.wait_recv()
        for cp in sends:
            cp.wait_send()
        for cp in mine:
            cp.wait()

    return _Job(arrs, outs, (3 * n, 3 * n, n), start, finish)


def _run_job(job, name):
    n = len(job.arrs)

    def body(*refs):
        ins, outs, sems = refs[:n], refs[n:n + len(job.outs)], refs[n + len(job.outs):]
        job.start(ins, outs, sems)
        job.finish(ins, outs, sems)

    return _comm_call(body, name, job.arrs, job.outs, job.n_sems)


def _carry(job, body, grid, in_specs, out_specs, out_shape, scratch_shapes, args):
    out_specs = list(out_specs) if isinstance(out_specs, (list, tuple)) else [out_specs]
    out_shape = list(out_shape) if isinstance(out_shape, (list, tuple)) else [out_shape]
    n_ci, n_co, n_cs = len(in_specs), len(out_specs), len(scratch_shapes)
    n_ji, n_jo = len(job.arrs), len(job.outs)
    any_spec = pl.BlockSpec(memory_space=pl.ANY)
    total = int(np.prod(grid))

    def wrapped(*refs):
        cin, jin = refs[:n_ci], refs[n_ci:n_ci + n_ji]
        o0 = n_ci + n_ji
        cout, jout = refs[o0:o0 + n_co], refs[o0 + n_co:o0 + n_co + n_jo]
        s0 = o0 + n_co + n_jo
        cscr, jsems = refs[s0:s0 + n_cs], refs[s0 + n_cs:]
        step = pl.program_id(0)
        for d in range(1, len(grid)):
            step = step * grid[d] + pl.program_id(d)

        @pl.when(step == 0)
        def _():
            job.start(jin, jout, jsems)

        body(*cin, *cout, *cscr)

        @pl.when(step == total - 1)
        def _():
            job.finish(jin, jout, jsems)

    return (wrapped, list(in_specs) + [any_spec] * n_ji, out_specs + [any_spec] * n_jo,
            out_shape + list(job.outs),
            list(scratch_shapes) + [pltpu.SemaphoreType.DMA((s,)) for s in job.n_sems],
            list(args) + list(job.arrs), n_co)


def _pair_add(g, r, name):
    _, R, C = g.shape
    cpad = -(-C // 128) * 128
    rt = _pick(R, max(16, (1 << 20) // (2 * cpad)), 16)
    cidx = lax.axis_index("c").astype(jnp.int32).reshape(1)

    def body(c_ref, g_ref, r_ref, o_ref):
        o_ref[...] = (g_ref[...].astype(F32) + r_ref[...].astype(F32)).astype(o_ref.dtype)

    return pl.pallas_call(
        body, name=name, out_shape=jax.ShapeDtypeStruct((N_CHIP, R, C), g.dtype),
        grid_spec=pltpu.PrefetchScalarGridSpec(
            num_scalar_prefetch=1, grid=(N_CHIP, R // rt),
            in_specs=[pl.BlockSpec((None, rt, C), lambda k, i, c_ref: (2 * k + c_ref[0], i, 0)),
                      pl.BlockSpec((None, rt, C), lambda k, i, c_ref: (k, i, 0))],
            out_specs=pl.BlockSpec((None, rt, C), lambda k, i, c_ref: (k, i, 0))),
        compiler_params=_cparams(("parallel", "parallel")),
    )(cidx, g, r)


def _pair_adds(chunks, sib, tag):
    return [_pair_add(g, r, "rs_pair_add" + tag) for g, r in zip(chunks, sib)]


def _mm(a, b, mode, out_dtype=F32, name="mm", bias=None, add=None, b_pre=(), b_shard=False,
        a_half=False, b_half=False, out_shard=False, job=None):
    npre = len(b_pre)
    bshape = b.shape[npre:]
    if mode == "nn":
        M, K = a.shape
        if b_shard:
            K2, N = bshape[1], N_DEV * bshape[2]
        else:
            K2, N = bshape
    elif mode == "nt":
        M, K = (a.shape[1], 2 * a.shape[2]) if a_half else a.shape
        if b_shard:
            N, K2 = bshape[1], N_DEV * bshape[2]
        else:
            N, K2 = bshape
    else:
        K, M = a.shape
        K2, N = (b.shape[1], 2 * b.shape[2]) if b_half else bshape
    assert K == K2, (a.shape, b.shape, mode)
    tm = _pick(M, 1100, 16) if mode != "tn" else _pick(M, 1024, 128)
    tn = _pick(N, 1024, 128)
    tk = _pick(K, 2816 if mode == "nt" else 2176, 128)
    if b_shard and mode == "nn":
        tn = bshape[2]
    sps = 1
    if b_shard and mode == "nt":
        ns = bshape[2]
        sps = 2 if ns % 128 == 0 and (not a_half or (a.shape[2] // ns) % 2 == 0) else 1
        tk = sps * ns
    if out_shard:
        tn = N // N_DEV
    nk = K // tk
    none_pre = (None,) * npre
    if mode == "nn":
        a_spec = pl.BlockSpec((tm, tk), lambda i, j, k: (i, k))
        if b_shard:
            b_spec = pl.BlockSpec(none_pre + (None, tk, tn), lambda i, j, k: b_pre + (j, k, 0))
        else:
            b_spec = pl.BlockSpec(none_pre + (tk, tn), lambda i, j, k: b_pre + (k, j))
        dot = _dot
    elif mode == "nt":
        if a_half:
            nkh = a.shape[2] // tk
            a_spec = pl.BlockSpec((None, tm, tk), lambda i, j, k: (k // nkh, i, k % nkh))
        else:
            a_spec = pl.BlockSpec((tm, tk), lambda i, j, k: (i, k))
        if b_shard:
            b_spec = pl.BlockSpec(none_pre + (sps, tn, tk // sps), lambda i, j, k: b_pre + (k, j, 0))
        else:
            b_spec = pl.BlockSpec(none_pre + (tn, tk), lambda i, j, k: b_pre + (j, k))
        dot = _dot_nt
        if b_shard:
            def dot(a_blk, b_blk):
                ns_ = tk // sps
                p = _dot_nt(a_blk[:, :ns_], b_blk[0])
                for s in range(1, sps):
                    p = p + _dot_nt(a_blk[:, s * ns_:(s + 1) * ns_], b_blk[s])
                return p
    else:
        a_spec = pl.BlockSpec((tk, tm), lambda i, j, k: (k, i))
        if b_half:
            nnh = b.shape[2] // tn
            b_spec = pl.BlockSpec((None, tk, tn), lambda i, j, k: (j // nnh, k, j % nnh))
        else:
            b_spec = pl.BlockSpec(none_pre + (tk, tn), lambda i, j, k: b_pre + (k, j))
        dot = _dot_tn
    in_specs = [a_spec, b_spec]
    args = [a, b]
    if bias is not None:
        in_specs.append(pl.BlockSpec((1, tn), lambda i, j, k: (0, j)))
        args.append(bias)
    if add is not None:
        in_specs.append(pl.BlockSpec((tm, tn), lambda i, j, k: (i, j)))
        args.append(add)
    n_in = len(args)
    if out_shard:
        o_spec = pl.BlockSpec((None, tm, tn), lambda i, j, k: (j, i, 0))
        o_shape = jax.ShapeDtypeStruct((N_DEV, M, tn), out_dtype)
    else:
        o_spec = pl.BlockSpec((tm, tn), lambda i, j, k: (i, j))
        o_shape = jax.ShapeDtypeStruct((M, N), out_dtype)

    def body(*refs):
        a_ref, b_ref = refs[0], refs[1]
        bias_ref = refs[2] if bias is not None else None
        add_ref = refs[n_in - 1] if add is not None else None
        o_ref = refs[n_in]
        p = dot(a_ref[...].astype(BF16), b_ref[...].astype(BF16))

        def finish(acc):
            if bias_ref is not None:
                acc = acc + bias_ref[...]
            if add_ref is not None:
                acc = acc + add_ref[...]
            o_ref[...] = acc.astype(o_ref.dtype)

        if nk == 1:
            finish(p)
        else:
            acc_ref = refs[-1]
            k = pl.program_id(2)

            @pl.when(k == 0)
            def _():
                acc_ref[...] = p

            @pl.when(k > 0)
            def _():
                acc_ref[...] += p

            @pl.when(k == nk - 1)
            def _():
                finish(acc_ref[...])

    grid = (M // tm, N // tn, nk)
    scratch = [pltpu.VMEM((tm, tn), F32)] if nk > 1 else []
    if job is None:
        return pl.pallas_call(
            body, name=name, grid=grid, in_specs=in_specs, out_specs=o_spec, out_shape=o_shape,
            scratch_shapes=scratch, compiler_params=_cparams(("parallel", "parallel", "arbitrary")),
        )(*args)
    return _call_carrying(job, body, name, grid, in_specs, o_spec, o_shape, scratch, args)


def _call_carrying(job, body, name, grid, in_specs, out_specs, out_shape, scratch, args):
    body, in_specs, out_specs, out_shape, scratch, args, n_co = _carry(
        job, body, grid, in_specs, out_specs, out_shape, scratch, args)
    res = pl.pallas_call(
        body, name=name, grid=grid, in_specs=in_specs, out_specs=out_specs, out_shape=out_shape,
        scratch_shapes=scratch, compiler_params=_cparams(("arbitrary",) * len(grid)),
    )(*args)
    own = res[0] if n_co == 1 else list(res[:n_co])
    return own, list(res[n_co:])


def _proj_layout(D):
    DK, DP, R2 = D // 2, D // 2, 2 * GATE_RANK
    return [("q", 0, DK, 4 * D), ("k", DK, DK, 4 * D + DK), ("v", 2 * DK, D, 0), ("g", 2 * DK + D, D, D),
            ("a", 2 * DK + 2 * D, R2, None), ("p", 2 * DK + 2 * D + R2, DP, 5 * D),
            ("bg", 2 * DK + 2 * D + R2 + DP, 2 * D, 2 * D)]


RELAYOUT_ROWS = 64


def _w_in_operands(g, name):
    _, D, n = g.shape
    segs = _proj_layout(D)
    tr = RELAYOUT_ROWS

    def body(g_ref, main_ref, alr_ref):
        shard = [g_ref[j].astype(F32) for j in range(N_DEV)]

        def columns(a, b):
            parts = []
            for j in range(a // n, (b - 1) // n + 1):
                parts.append(shard[j][:, max(a, j * n) - j * n:min(b, (j + 1) * n) - j * n])
            return parts[0] if len(parts) == 1 else jnp.concatenate(parts, axis=1)

        for _, start, width, dst in segs:
            cols = columns(start, start + width)
            if dst is None:
                cols = jnp.concatenate([cols, jnp.zeros((tr, ALR_PAD - width), F32)], axis=1)
                alr_ref[...] = cols.astype(BF16)
            else:
                main_ref[:, dst:dst + width] = cols.astype(BF16)

    return pl.pallas_call(
        body, name=name, grid=(D // tr,),
        in_specs=[pl.BlockSpec((N_DEV, tr, n), lambda i: (0, i, 0))],
        out_specs=[pl.BlockSpec((tr, 11 * D // 2), lambda i: (i, 0)), pl.BlockSpec((tr, ALR_PAD), lambda i: (i, 0))],
        out_shape=[jax.ShapeDtypeStruct((D, 11 * D // 2), BF16), jax.ShapeDtypeStruct((D, ALR_PAD), BF16)],
        compiler_params=_cparams(("parallel",)),
    )(g)


def _w_in_chunks(g_main, g_alr, n, name):
    D = g_main.shape[0]
    segs = _proj_layout(D)
    tr = RELAYOUT_ROWS

    def body(main_ref, alr_ref, o_ref):
        main = main_ref[...].astype(F32)
        alr = alr_ref[...].astype(F32)
        for j in range(N_DEV):
            a, b = j * n, (j + 1) * n
            parts = []
            for _, start, width, dst in segs:
                lo, hi = max(a, start), min(b, start + width)
                if lo >= hi:
                    continue
                src = alr if dst is None else main
                off = 0 if dst is None else dst
                parts.append(src[:, off + lo - start:off + hi - start])
            o_ref[j] = (parts[0] if len(parts) == 1 else jnp.concatenate(parts, axis=1)).astype(BF16)

    return pl.pallas_call(
        body, name=name, grid=(D // tr,),
        in_specs=[pl.BlockSpec((tr, 11 * D // 2), lambda i: (i, 0)), pl.BlockSpec((tr, ALR_PAD), lambda i: (i, 0))],
        out_specs=pl.BlockSpec((N_DEV, tr, n), lambda i: (0, i, 0)),
        out_shape=jax.ShapeDtypeStruct((N_DEV, D, n), BF16),
        compiler_params=_cparams(("parallel",)),
    )(g_main, g_alr)


def _ffn_in_fwd(h2, w_fi, l, name, job=None):
    T, D = h2.shape
    n = w_fi.shape[3]
    nh = N_DEV // 2
    dff = nh * n
    tm = _pick(T, 600, 16)

    def body(a_ref, bg_ref, bu_ref, ff_ref, s_ref):
        a = a_ref[...]
        g = _dot(a, bg_ref[...])
        u = _dot(a, bu_ref[...])
        ff_ref[0] = g.astype(BF16)
        ff_ref[1] = u.astype(BF16)
        s_ref[...] = _swiglu_f(g, u).astype(BF16)

    grid = (T // tm, nh)
    in_specs = [pl.BlockSpec((tm, D), lambda i, j: (i, 0)),
                pl.BlockSpec((None, None, D, n), lambda i, j: (l, j, 0, 0)),
                pl.BlockSpec((None, None, D, n), lambda i, j: (l, nh + j, 0, 0))]
    out_specs = [pl.BlockSpec((2, tm, n), lambda i, j: (0, i, j)), pl.BlockSpec((tm, n), lambda i, j: (i, j))]
    out_shape = [jax.ShapeDtypeStruct((2, T, dff), BF16), jax.ShapeDtypeStruct((T, dff), BF16)]
    args = (h2, w_fi, w_fi)
    if job is None:
        return pl.pallas_call(
            body, name=name, grid=grid, in_specs=in_specs, out_specs=out_specs, out_shape=out_shape,
            compiler_params=_cparams(("parallel", "parallel")),
        )(*args)
    return _call_carrying(job, body, name, grid, in_specs, out_specs, out_shape, [], args)


def _ffn_out_dx(dffn, w_fo, l, ff, name, job=None):
    T, D = dffn.shape
    dff = ff.shape[2]
    tm = _pick(T, 600, 16)
    tw = _pick(dff, 1408, 128)

    def body(a_ref, b_ref, ff_ref, o_ref):
        ds = _dot_nt(a_ref[...], b_ref[...])
        _, vjp = jax.vjp(_swiglu_f, ff_ref[0].astype(F32), ff_ref[1].astype(F32))
        dg, du = vjp(ds)
        o_ref[0] = dg.astype(BF16)
        o_ref[1] = du.astype(BF16)

    grid = (T // tm, dff // tw)
    in_specs = [pl.BlockSpec((tm, D), lambda i, j: (i, 0)),
                pl.BlockSpec((None, tw, D), lambda i, j: (l, j, 0)),
                pl.BlockSpec((2, tm, tw), lambda i, j: (0, i, j))]
    out_specs = pl.BlockSpec((2, tm, tw), lambda i, j: (0, i, j))
    out_shape = jax.ShapeDtypeStruct((2, T, dff), BF16)
    args = (dffn, w_fo, ff)
    if job is None:
        return pl.pallas_call(
            body, name=name, grid=grid, in_specs=in_specs, out_specs=out_specs, out_shape=out_shape,
            compiler_params=_cparams(("parallel", "parallel")),
        )(*args)
    return _call_carrying(job, body, name, grid, in_specs, out_specs, out_shape, [], args)


def _row(rb, w, col=0):
    return pl.BlockSpec((rb, w), lambda i: (i, col))


def _modspec(d, sec):
    return pl.BlockSpec((None, 1, d), lambda i: (jnp.minimum(i, 1), 0, sec))


def _vec(w):
    return pl.BlockSpec((1, w), lambda i: (0, 0))


def _acc2(w):
    return pl.BlockSpec((None, 1, w), lambda i: (jnp.minimum(i, 1), 0, 0))


def _accum(ref, val):
    i = pl.program_id(0)

    @pl.when(i <= 1)
    def _():
        ref[...] = val

    @pl.when(i > 1)
    def _():
        ref[...] += val


def _acc_shape(w):
    return jax.ShapeDtypeStruct((2, 1, w), F32)


def _mod_f(x, sc, sh):
    return x * (1.0 + sc) + sh


def _mod_fwd(xa, mod, sec_sc, sec_sh, rb, name):
    T, D = xa.shape

    def body(x_ref, sc_ref, sh_ref, h_ref):
        h_ref[...] = _mod_f(x_ref[...], sc_ref[...], sh_ref[...]).astype(BF16)

    return pl.pallas_call(
        body, name=name, grid=(T // rb,),
        in_specs=[_row(rb, D), _modspec(D, sec_sc), _modspec(D, sec_sh)],
        out_specs=_row(rb, D), out_shape=jax.ShapeDtypeStruct((T, D), BF16),
        compiler_params=_cparams(("parallel",)),
    )(xa, mod, mod)


def _mod_bwd(dxa, dh, xa, mod, sec_sc, sec_sh, rb, name):
    T, D = xa.shape

    def body(dxa_ref, dh_ref, x_ref, sc_ref, sh_ref, dx_ref, dsc_ref, dsh_ref):
        _, vjp = jax.vjp(_mod_f, x_ref[...], sc_ref[...], sh_ref[...])
        dx, dsc, dsh = vjp(dh_ref[...])
        dx_ref[...] = dxa_ref[...] + dx
        _accum(dsc_ref, dsc)
        _accum(dsh_ref, dsh)

    return pl.pallas_call(
        body, name=name, grid=(T // rb,),
        in_specs=[_row(rb, D), _row(rb, D), _row(rb, D), _modspec(D, sec_sc), _modspec(D, sec_sh)],
        out_specs=[_row(rb, D), _acc2(D), _acc2(D)],
        out_shape=[jax.ShapeDtypeStruct((T, D), F32), _acc_shape(D), _acc_shape(D)],
        compiler_params=_cparams(("arbitrary",)),
    )(dxa, dh, xa, mod, mod)


def _ln_f(alpha, x, mix, gt, gain, bias):
    z = alpha * x + gt * mix
    mu = jnp.mean(z, axis=-1, keepdims=True)
    zc = z - mu
    var = jnp.mean(zc * zc, axis=-1, keepdims=True)
    return zc * lax.rsqrt(var + LN_EPS) * gain + bias


def _unit_fwd(alpha, x, mix, mod, sec_gt, gain, bias, next_mod, rb, name):
    T, D = x.shape
    has_mod = next_mod is not None

    def body(*refs):
        if has_mod:
            x_ref, mix_ref, gt_ref, g_ref, b_ref, sc_ref, sh_ref, xo_ref, h_ref = refs
        else:
            x_ref, mix_ref, gt_ref, g_ref, b_ref, xo_ref = refs
        xo = _ln_f(alpha, x_ref[...], mix_ref[...], gt_ref[...], g_ref[...], b_ref[...])
        xo_ref[...] = xo
        if has_mod:
            h_ref[...] = _mod_f(xo, sc_ref[...], sh_ref[...]).astype(BF16)

    in_specs = [_row(rb, D), _row(rb, D), _modspec(D, sec_gt), _vec(D), _vec(D)]
    args = [x, mix, mod, gain, bias]
    out_specs = [_row(rb, D)]
    out_shape = [jax.ShapeDtypeStruct((T, D), F32)]
    if has_mod:
        nm, s_sc, s_sh = next_mod
        in_specs += [_modspec(D, s_sc), _modspec(D, s_sh)]
        args += [nm, nm]
        out_specs.append(_row(rb, D))
        out_shape.append(jax.ShapeDtypeStruct((T, D), BF16))
    res = pl.pallas_call(
        body, name=name, grid=(T // rb,), in_specs=in_specs, out_specs=out_specs, out_shape=out_shape,
        compiler_params=_cparams(("parallel",)),
    )(*args)
    return (res[0], res[1]) if has_mod else (res[0], None)


def _unit_bwd(alpha, dxo, dh, x, mix, mod, sec_gt, gain, bias, next_mod, rb, name, job=None):
    T, D = x.shape
    has_mod = next_mod is not None

    def body(*refs):
        if has_mod:
            (dxo_ref, dh_ref, x_ref, mix_ref, gt_ref, g_ref, b_ref, sc_ref, sh_ref,
             dx_ref, dmix_ref, dgt_ref, dg_ref, db_ref, dsc_ref, dsh_ref) = refs
        else:
            (dxo_ref, x_ref, mix_ref, gt_ref, g_ref, b_ref,
             dx_ref, dmix_ref, dgt_ref, dg_ref, db_ref) = refs
        xo, vjp = jax.vjp(functools.partial(_ln_f, alpha), x_ref[...], mix_ref[...], gt_ref[...],
                          g_ref[...], b_ref[...])
        dxo_t = dxo_ref[...]
        if has_mod:
            _, vjp_m = jax.vjp(_mod_f, xo, sc_ref[...], sh_ref[...])
            dxo_m, dsc, dsh = vjp_m(dh_ref[...])
            dxo_t = dxo_t + dxo_m
            _accum(dsc_ref, dsc)
            _accum(dsh_ref, dsh)
        dx, dmix, dgt, dg, db = vjp(dxo_t)
        dx_ref[...] = dx
        dmix_ref[...] = dmix.astype(BF16)
        _accum(dgt_ref, dgt)
        _accum(dg_ref, dg)
        _accum(db_ref, db)

    in_specs = [_row(rb, D)]
    args = [dxo]
    if has_mod:
        in_specs.append(_row(rb, D))
        args.append(dh)
    in_specs += [_row(rb, D), _row(rb, D), _modspec(D, sec_gt), _vec(D), _vec(D)]
    args += [x, mix, mod, gain, bias]
    out_specs = [_row(rb, D), _row(rb, D), _acc2(D), _acc2(D), _acc2(D)]
    out_shape = [jax.ShapeDtypeStruct((T, D), F32), jax.ShapeDtypeStruct((T, D), BF16),
                 _acc_shape(D), _acc_shape(D), _acc_shape(D)]
    if has_mod:
        nm, s_sc, s_sh = next_mod
        in_specs += [_modspec(D, s_sc), _modspec(D, s_sh)]
        args += [nm, nm]
        out_specs += [_acc2(D), _acc2(D)]
        out_shape += [_acc_shape(D), _acc_shape(D)]
    if job is None:
        res = pl.pallas_call(
            body, name=name, grid=(T // rb,), in_specs=in_specs, out_specs=out_specs, out_shape=out_shape,
            compiler_params=_cparams(("arbitrary",)),
        )(*args)
        job_res = None
    else:
        res, job_res = _call_carrying(job, body, name, (T // rb,), in_specs, out_specs, out_shape, [], args)
    res = list(res) if has_mod else list(res) + [None, None]
    return res if job is None else (res, job_res)


def _log_sigmoid(z):
    return jnp.minimum(z, 0.0) - jnp.log(1.0 + jnp.exp(-jnp.abs(z)))


def _decay_fwd(alr, wdu, bdu, rb, name):
    T = alr.shape[0]
    W = wdu.shape[1]

    def body(a_ref, w_ref, b_ref, la_ref):
        z = _dot(a_ref[...].astype(BF16), w_ref[...]) + b_ref[...]
        la_ref[...] = _log_sigmoid(z) * (1.0 / GATE_NORM)

    return pl.pallas_call(
        body, name=name, grid=(T // rb,),
        in_specs=[_row(rb, ALR_PAD), pl.BlockSpec((ALR_PAD, W), lambda i: (0, 0)), _vec(W)],
        out_specs=_row(rb, W), out_shape=jax.ShapeDtypeStruct((T, W), F32),
        compiler_params=_cparams(("parallel",)),
    )(alr, wdu, bdu)


def _decay_bwd(dla_f, dla_b, alr, wdu, bdu, rb, name):
    T = alr.shape[0]
    W = wdu.shape[1]
    DK = W // 2

    def body(df_ref, db_ref, a_ref, w_ref, b_ref, dalr_ref, gw_ref, gb_ref):
        i = pl.program_id(0)
        ab = a_ref[...].astype(BF16)
        z = _dot(ab, w_ref[...]) + b_ref[...]
        dla = jnp.concatenate([df_ref[...], db_ref[...]], axis=1)
        dz = dla * _sig(-z) * (1.0 / GATE_NORM)
        dzb = dz.astype(BF16)
        dalr_ref[...] = _dot_nt(dzb, w_ref[...]).astype(BF16)
        gw = _dot_tn(ab, dzb)
        gb = jnp.sum(dz, axis=0, keepdims=True)

        @pl.when(i == 0)
        def _():
            gw_ref[...] = gw
            gb_ref[...] = gb

        @pl.when(i > 0)
        def _():
            gw_ref[...] += gw
            gb_ref[...] += gb

    return pl.pallas_call(
        body, name=name, grid=(T // rb,),
        in_specs=[_row(rb, DK), _row(rb, DK), _row(rb, ALR_PAD), pl.BlockSpec((ALR_PAD, W), lambda i: (0, 0)), _vec(W)],
        out_specs=[_row(rb, ALR_PAD), pl.BlockSpec((ALR_PAD, W), lambda i: (0, 0)), _vec(W)],
        out_shape=[jax.ShapeDtypeStruct((T, ALR_PAD), BF16), jax.ShapeDtypeStruct((ALR_PAD, W), F32),
                   jax.ShapeDtypeStruct((1, W), F32)],
        compiler_params=_cparams(("arbitrary",)),
    )(dla_f, dla_b, alr, wdu, bdu)


def _tri(rev, ncb):
    m = np.tril(np.ones((CHUNK, CHUNK), np.float32))
    return jnp.asarray(np.kron(np.eye(ncb, dtype=np.float32), m.T if rev else m), BF16)


def _gla_block_common(q_ref, k_ref, v_ref, la_ref, tri_ref, ck, cv, rev, scale_q, ncb):
    mid = CHUNK // 2 if rev else CHUNK // 2 - 1
    last_i = 0 if rev else CHUNK - 1
    rb = ncb * CHUNK
    hk = ck.stop - ck.start
    q = q_ref[:, ck].astype(F32) * scale_q
    k = k_ref[:, ck].astype(F32)
    v = v_ref[:, cv].astype(F32)
    cum = _dot3(tri_ref[...], la_ref[:, ck])
    per_chunk = lambda i: jnp.concatenate(
        [jnp.broadcast_to(cum[c * CHUNK + i:c * CHUNK + i + 1, :], (CHUNK, hk)) for c in range(ncb)], axis=0)
    ref, last = per_chunk(mid), per_chunk(last_i)
    e_q = jnp.exp(cum - ref)
    e_k = jnp.exp(ref - cum)
    e_c = jnp.exp(cum)
    e_s = jnp.exp(last - cum)
    e_l = [jnp.exp(cum[c * CHUNK + last_i:c * CHUNK + last_i + 1, :]) for c in range(ncb)]
    ri = lax.broadcasted_iota(jnp.int32, (rb, rb), 0)
    ci = lax.broadcasted_iota(jnp.int32, (rb, rb), 1)
    mask = (ri // CHUNK == ci // CHUNK) & ((ci >= ri) if rev else (ci <= ri))
    return q, k, v, e_q, e_k, e_c, e_s, e_l, mask, last_i


GLA_HEADS_PER_STEP = 1


def _gla_specs(rb, hk, hv, D, rbmap, rev):
    hp = GLA_HEADS_PER_STEP
    q_col0 = 4 * D // (hp * hk)
    k_col0 = q_col0 + N_HEADS // hp
    la_col0 = N_HEADS // hp if rev else 0
    return [
        pl.BlockSpec((rb, hp * hk), lambda h, i: (rbmap(i), q_col0 + h)),
        pl.BlockSpec((rb, hp * hk), lambda h, i: (rbmap(i), k_col0 + h)),
        pl.BlockSpec((rb, hp * hv), lambda h, i: (rbmap(i), h)),
        pl.BlockSpec((rb, hp * hk), lambda h, i: (rbmap(i), la_col0 + h)),
        pl.BlockSpec((rb, rb), lambda h, i: (0, 0)),
    ]


def _gla_call(job, body, name, grid, in_specs, out_specs, out_shape, scratch, args):
    if job is None:
        return pl.pallas_call(
            body, name=name, grid=grid, in_specs=in_specs, out_specs=out_specs, out_shape=out_shape,
            scratch_shapes=scratch, compiler_params=_cparams(("parallel", "arbitrary")),
        )(*args)
    return _call_carrying(job, body, name, grid, in_specs, out_specs, out_shape, scratch, args)


def _gla_fwd(proj, la, rev, rb, D, name, job=None):
    T = proj.shape[0]
    nb = T // rb
    ncb = rb // CHUNK
    hp = GLA_HEADS_PER_STEP
    hk, hv = D // 2 // N_HEADS, D // N_HEADS
    scale_q = float(hk) ** -0.5
    rbmap = (lambda i: jnp.where(i == 0, 0, nb - i)) if rev else (lambda i: i)

    def body(q_ref, k_ref, v_ref, la_ref, tri_ref, o_ref, s_ref, st_ref):
        @pl.when(pl.program_id(1) == 0)
        def _():
            st_ref[...] = jnp.zeros_like(st_ref)

        order = range(ncb - 1, -1, -1) if rev else range(ncb)
        for hh in range(hp):
            ck, cv = slice(hh * hk, (hh + 1) * hk), slice(hh * hv, (hh + 1) * hv)
            q, k, v, e_q, e_k, e_c, e_s, e_l, mask, _ = _gla_block_common(
                q_ref, k_ref, v_ref, la_ref, tri_ref, ck, cv, rev, scale_q, ncb)
            vb = v.astype(BF16)
            a = jnp.where(mask, _dot_nt((q * e_q).astype(BF16), (k * e_k).astype(BF16)), 0.0)
            o_intra = _dot(a.astype(BF16), vb)
            qc = (q * e_c).astype(BF16)
            ks = (k * e_s).astype(BF16)
            st = st_ref[hh]
            for cc in order:
                rows = slice(cc * CHUNK, (cc + 1) * CHUNK)
                s_ref[hh, cc] = st
                o_ref[rows, cv] = o_intra[rows] + _dot_nt(qc[rows], st.astype(BF16))
                st = st * e_l[cc] + _dot_tn(vb[rows], ks[rows])
            st_ref[hh] = st

    return _gla_call(
        job, body, name, (N_HEADS // hp, nb), _gla_specs(rb, hk, hv, D, rbmap, rev),
        [pl.BlockSpec((rb, hp * hv), lambda h, i: (rbmap(i), h)),
         pl.BlockSpec((hp, ncb, hv, hk), lambda h, i: (h, rbmap(i), 0, 0))],
        [jax.ShapeDtypeStruct((T, D), F32), jax.ShapeDtypeStruct((N_HEADS, T // CHUNK, hv, hk), F32)],
        [pltpu.VMEM((hp, hv, hk), F32)], (proj, proj, proj, la, _tri(rev, ncb)))


def _gla_bwd(proj, la, do, states, rev, rb, D, prev, name, job=None):
    T = proj.shape[0]
    nb = T // rb
    ncb = rb // CHUNK
    hp = GLA_HEADS_PER_STEP
    hk, hv = D // 2 // N_HEADS, D // N_HEADS
    DK = D // 2
    scale_q = float(hk) ** -0.5
    if rev:
        rbmap = lambda i: jnp.where(i == nb - 1, 0, i + 1)
    else:
        rbmap = lambda i: nb - 1 - i
    has_prev = prev is not None
    out_dt = BF16 if has_prev else F32

    def body(*refs):
        q_ref, k_ref, v_ref, la_ref, tri_ref, trit_ref, do_ref, s_ref = refs[:8]
        n_in = 11 if has_prev else 8
        pq_ref, pk_ref, pv_ref = refs[8:11] if has_prev else (None, None, None)
        dq_ref, dk_ref, dv_ref, dla_ref, ds_ref = refs[n_in:]

        @pl.when(pl.program_id(1) == 0)
        def _():
            ds_ref[...] = jnp.zeros_like(ds_ref)

        order = range(ncb) if rev else range(ncb - 1, -1, -1)
        for hh in range(hp):
            ck, cv = slice(hh * hk, (hh + 1) * hk), slice(hh * hv, (hh + 1) * hv)
            q, k, v, e_q, e_k, e_c, e_s, e_l, mask, last_i = _gla_block_common(
                q_ref, k_ref, v_ref, la_ref, tri_ref, ck, cv, rev, scale_q, ncb)
            vb = v.astype(BF16)
            qi = (q * e_q).astype(BF16)
            ki = (k * e_k).astype(BF16)
            qc = (q * e_c).astype(BF16)
            ks = (k * e_s).astype(BF16)
            a = jnp.where(mask, _dot_nt(qi, ki), 0.0).astype(BF16)
            dob = do_ref[:, cv].astype(BF16)
            da = jnp.where(mask, _dot_nt(dob, vb), 0.0).astype(BF16)
            dv_intra = _dot_tn(a, dob)
            dq_intra = _dot(da, ki) * e_q
            dk_intra = _dot_tn(da, qi) * e_k
            rowi = lax.broadcasted_iota(jnp.int32, (CHUNK, hk), 0)
            dst = ds_ref[hh]
            for cc in order:
                rows = slice(cc * CHUNK, (cc + 1) * CHUNK)
                st0 = s_ref[hh, cc]
                dstb = dst.astype(BF16)
                dv = dv_intra[rows] + _dot_nt(ks[rows], dstb)
                dk_inter = _dot(vb[rows], dstb) * e_s[rows]
                dq_s = dq_intra[rows] + _dot(dob[rows], st0.astype(BF16)) * e_c[rows]
                dk = dk_intra[rows] + dk_inter
                extra = (jnp.sum(k[rows] * dk_inter, axis=0, keepdims=True)
                         + e_l[cc] * jnp.sum(dst * st0, axis=0, keepdims=True))
                dla_ref[rows, ck] = q[rows] * dq_s - k[rows] * dk + jnp.where(rowi == last_i, extra, 0.0)
                dq = dq_s * scale_q
                if has_prev:
                    dq = dq + pq_ref[rows, ck]
                    dk = dk + pk_ref[rows, ck]
                    dv = dv + pv_ref[rows, cv]
                dq_ref[rows, ck] = dq.astype(out_dt)
                dk_ref[rows, ck] = dk.astype(out_dt)
                dv_ref[rows, cv] = dv.astype(out_dt)
                dst = dst * e_l[cc] + _dot_tn(dob[rows], qc[rows])
            ds_ref[hh] = dst
            dla_ref[:, ck] = _dot3(trit_ref[...], dla_ref[:, ck])

    in_specs = _gla_specs(rb, hk, hv, D, rbmap, rev)
    in_specs += [pl.BlockSpec((rb, rb), lambda h, i: (0, 0)),
                 pl.BlockSpec((rb, hp * hv), lambda h, i: (rbmap(i), h)),
                 pl.BlockSpec((hp, ncb, hv, hk), lambda h, i: (h, rbmap(i), 0, 0))]
    args = [proj, proj, proj, la, _tri(rev, ncb), _tri(not rev, ncb), do, states]
    hk_spec = pl.BlockSpec((rb, hp * hk), lambda h, i: (rbmap(i), h))
    hv_spec = pl.BlockSpec((rb, hp * hv), lambda h, i: (rbmap(i), h))
    if has_prev:
        in_specs += [hk_spec, hk_spec, hv_spec]
        args += list(prev)
    return _gla_call(
        job, body, name, (N_HEADS // hp, nb), in_specs, [hk_spec, hk_spec, hv_spec, hk_spec],
        [jax.ShapeDtypeStruct((T, DK), out_dt), jax.ShapeDtypeStruct((T, DK), out_dt),
         jax.ShapeDtypeStruct((T, D), out_dt), jax.ShapeDtypeStruct((T, DK), F32)],
        [pltpu.VMEM((hp, hv, hk), F32)], args)


def _glaout_f(of, ob, g, gain):
    o = of + ob
    n = o * lax.rsqrt(jnp.mean(o * o, axis=-1, keepdims=True) + RMS_EPS)
    return n * gain * _silu(g)


def _glaout_fwd(o_f, o_b, proj, gain, rb, name):
    T, D = o_f.shape
    hv = D // N_HEADS

    def body(of_ref, ob_ref, g_ref, gn_ref, u_ref):
        for h in range(N_HEADS):
            cs = slice(h * hv, (h + 1) * hv)
            u_ref[:, cs] = _glaout_f(of_ref[:, cs], ob_ref[:, cs], g_ref[:, cs].astype(F32),
                                     gn_ref[:, cs]).astype(BF16)

    return pl.pallas_call(
        body, name=name, grid=(T // rb,),
        in_specs=[_row(rb, D), _row(rb, D), _row(rb, D, 1), _vec(D)],
        out_specs=_row(rb, D), out_shape=jax.ShapeDtypeStruct((T, D), BF16),
        compiler_params=_cparams(("parallel",)),
    )(o_f, o_b, proj, gain)


def _glaout_bwd(du, o_f, o_b, proj, gain, rb, name):
    T, D = o_f.shape
    hv = D // N_HEADS

    def body(du_ref, of_ref, ob_ref, g_ref, gn_ref, do_ref, dg_ref, dgn_ref, tmp_ref):
        for h in range(N_HEADS):
            cs = slice(h * hv, (h + 1) * hv)
            _, vjp = jax.vjp(_glaout_f, of_ref[:, cs], ob_ref[:, cs], g_ref[:, cs].astype(F32), gn_ref[:, cs])
            d_of, _, dg, dgn = vjp(du_ref[:, cs])
            do_ref[:, cs] = d_of
            dg_ref[:, cs] = dg.astype(BF16)
            tmp_ref[:, cs] = dgn
        _accum(dgn_ref, tmp_ref[...])

    return pl.pallas_call(
        body, name=name, grid=(T // rb,),
        in_specs=[_row(rb, D), _row(rb, D), _row(rb, D), _row(rb, D, 1), _vec(D)],
        out_specs=[_row(rb, D), _row(rb, D), _acc2(D)],
        out_shape=[jax.ShapeDtypeStruct((T, D), F32), jax.ShapeDtypeStruct((T, D), BF16), _acc_shape(D)],
        scratch_shapes=[pltpu.VMEM((1, D), F32)],
        compiler_params=_cparams(("arbitrary",)),
    )(du, o_f, o_b, proj, gain)


def _merge_f(bg1, bg2, yg, yp):
    return _sig(bg1) * yg + _sig(bg2) * yp


def _merge_fwd(proj, y_gla, y_pool, rb, name):
    T, D = y_gla.shape

    def body(b1_ref, b2_ref, yg_ref, yp_ref, m_ref):
        m_ref[...] = _merge_f(b1_ref[...].astype(F32), b2_ref[...].astype(F32), yg_ref[...].astype(F32),
                              yp_ref[...].astype(F32)).astype(BF16)

    return pl.pallas_call(
        body, name=name, grid=(T // rb,),
        in_specs=[_row(rb, D, 2), _row(rb, D, 3), _row(rb, D), _row(rb, D)],
        out_specs=_row(rb, D), out_shape=jax.ShapeDtypeStruct((T, D), BF16),
        compiler_params=_cparams(("parallel",)),
    )(proj, proj, y_gla, y_pool)


def _merge_bwd(dm, proj, y_gla, y_pool, rb, name):
    T, D = y_gla.shape

    def body(dm_ref, b1_ref, b2_ref, yg_ref, yp_ref, d1_ref, d2_ref, dyg_ref, dyp_ref):
        _, vjp = jax.vjp(_merge_f, b1_ref[...].astype(F32), b2_ref[...].astype(F32), yg_ref[...].astype(F32),
                         yp_ref[...].astype(F32))
        d1, d2, dyg, dyp = vjp(dm_ref[...])
        d1_ref[...] = d1.astype(BF16)
        d2_ref[...] = d2.astype(BF16)
        dyg_ref[...] = dyg.astype(BF16)
        dyp_ref[...] = dyp.astype(BF16)

    return pl.pallas_call(
        body, name=name, grid=(T // rb,),
        in_specs=[_row(rb, D), _row(rb, D, 2), _row(rb, D, 3), _row(rb, D), _row(rb, D)],
        out_specs=[_row(rb, D)] * 4, out_shape=[jax.ShapeDtypeStruct((T, D), BF16)] * 4,
        compiler_params=_cparams(("parallel",)),
    )(dm, proj, proj, y_gla, y_pool)


def _swiglu_f(gate, up):
    return _silu(gate) * up


def _pool_consts(ctx_len, seq):
    rows = seq // GRID_W
    reps = POOL_TB // GRID_W
    mw, bc, cw, ch, cc = [], [], [], [], []
    for w in POOL_WINDOWS:
        lo, hi = w // 2, w - w // 2 - 1

        def band(n):
            i = np.arange(n)[:, None]
            j = np.arange(n)[None, :]
            return ((j - i >= -lo) & (j - i <= hi)).astype(np.float32)

        def count(n):
            i = np.arange(n)
            return (np.minimum(i + hi + 1, n) - np.maximum(i - lo, 0)).astype(np.float32)

        mw.append(np.kron(np.eye(reps, dtype=np.float32), band(GRID_W)))
        bc.append(band(ctx_len))
        cw.append(np.tile(count(GRID_W), reps)[:, None])
        ch.append(np.repeat(count(rows), GRID_W)[:, None])
        cc.append(count(ctx_len)[:, None])
    mw, bc = np.stack(mw), np.stack(bc)
    return dict(
        mw=jnp.asarray(mw, BF16), mwt=jnp.asarray(mw.transpose(0, 2, 1), BF16),
        bc=jnp.asarray(bc, BF16), bct=jnp.asarray(bc.transpose(0, 2, 1), BF16),
        cw=jnp.asarray(np.stack(cw)), ch=jnp.asarray(np.stack(ch)), cc=jnp.asarray(np.stack(cc)))


def _gspec(*shape):
    nd = len(shape)
    return pl.BlockSpec((None,) + tuple(shape), lambda g: (g,) + (0,) * nd)


def _pool_fwd(proj, pc, wg, scale, ctx_len, D, name):
    T = proj.shape[0]
    seq = T - ctx_len
    dp = D // 2
    pg = dp // len(POOL_WINDOWS)
    nblk = seq // POOL_TB
    padt = POOL_PAD_ROWS * GRID_W
    p_col0 = 5 * D // pg

    def body(p_ref, mw_ref, bc_ref, cw_ref, ch_ref, cc_ref, wg_ref, sc_ref, pd_ref, y0_ref, r_ref, pad_ref):
        g = pl.program_id(0)

        def tail(rows, mean, x):
            pdb = (mean - x).astype(BF16)
            y0 = _dot(pdb, wg_ref[...])
            pd_ref[rows, :] = pdb
            y0_ref[rows, :] = y0
            r_ref[rows, :] = (y0 * sc_ref[...]).astype(BF16)

        xc = p_ref[0:ctx_len, :].astype(F32)
        tail(slice(0, ctx_len), _dot2(bc_ref[...], xc) / cc_ref[...], xc)

        pad_ref[0:padt, :] = jnp.zeros((padt, pg), F32)
        pad_ref[padt + seq:, :] = jnp.zeros((padt, pg), F32)

        def wpass(b, carry):
            rows = pl.ds(pl.multiple_of(ctx_len + b * POOL_TB, CHUNK), POOL_TB)
            dst = pl.ds(pl.multiple_of(padt + b * POOL_TB, CHUNK), POOL_TB)
            pad_ref[dst, :] = _dot2(mw_ref[...], p_ref[rows, :].astype(F32)) / cw_ref[...]
            return carry

        lax.fori_loop(0, nblk, wpass, 0)

        for gi, w in enumerate(POOL_WINDOWS):
            lo, hi = w // 2, w - w // 2 - 1

            @pl.when(g == gi)
            def _():
                def hpass(b, carry):
                    acc = jnp.zeros((POOL_TB, pg), F32)
                    for d in range(-lo, hi + 1):
                        src = pl.ds(pl.multiple_of(padt + b * POOL_TB + d * GRID_W, CHUNK), POOL_TB)
                        acc = acc + pad_ref[src, :]
                    mean = acc / ch_ref[pl.ds(pl.multiple_of(b * POOL_TB, CHUNK), POOL_TB), :]
                    rows = pl.ds(pl.multiple_of(ctx_len + b * POOL_TB, CHUNK), POOL_TB)
                    tail(rows, mean, p_ref[rows, :].astype(F32))
                    return carry

                lax.fori_loop(0, nblk, hpass, 0)

    col = lambda g: (0, g)
    return pl.pallas_call(
        body, name=name, grid=(len(POOL_WINDOWS),),
        in_specs=[pl.BlockSpec((T, pg), lambda g: (0, p_col0 + g)),
                  _gspec(POOL_TB, POOL_TB), _gspec(ctx_len, ctx_len), _gspec(POOL_TB, 1), _gspec(seq, 1),
                  _gspec(ctx_len, 1), _gspec(pg, pg), pl.BlockSpec((1, pg), col)],
        out_specs=[pl.BlockSpec((T, pg), col)] * 3,
        out_shape=[jax.ShapeDtypeStruct((T, dp), BF16), jax.ShapeDtypeStruct((T, dp), F32),
                   jax.ShapeDtypeStruct((T, dp), BF16)],
        scratch_shapes=[pltpu.VMEM((seq + 2 * padt, pg), F32)],
        compiler_params=_cparams(("arbitrary",)),
    )(proj, pc["mw"], pc["bc"], pc["cw"], pc["ch"], pc["cc"], wg, scale)


def _pool_bwd(dr, y0, pd, pc, wg, scale, ctx_len, D, name):
    T = dr.shape[0]
    seq = T - ctx_len
    dp = D // 2
    ng = len(POOL_WINDOWS)
    pg = dp // ng
    nblk = seq // POOL_TB
    padt = POOL_PAD_ROWS * GRID_W

    def body(dr_ref, y0_ref, pd_ref, mwt_ref, bct_ref, cw_ref, ch_ref, cc_ref, wg_ref, sc_ref,
             dp_ref, dsc_ref, gwg_ref, pad_ref, dpd_ref):
        g = pl.program_id(0)
        dsc_ref[...] = jnp.zeros_like(dsc_ref)
        gwg_ref[...] = jnp.zeros_like(gwg_ref)

        def head(rows):
            drv = dr_ref[rows, :]
            dsc_ref[...] += jnp.sum(drv * y0_ref[rows, :], axis=0, keepdims=True)
            dy0 = (drv * sc_ref[...]).astype(BF16)
            gwg_ref[...] += _dot_tn(pd_ref[rows, :], dy0)
            return _dot_nt(dy0, wg_ref[...])

        crow = slice(0, ctx_len)
        dpd_c = head(crow)
        dp_ref[crow, :] = (_dot2(bct_ref[...], dpd_c / cc_ref[...]) - dpd_c).astype(BF16)

        pad_ref[0:padt, :] = jnp.zeros((padt, pg), F32)
        pad_ref[padt + seq:, :] = jnp.zeros((padt, pg), F32)

        def first(b, carry):
            rows = pl.ds(pl.multiple_of(ctx_len + b * POOL_TB, CHUNK), POOL_TB)
            lrows = pl.ds(pl.multiple_of(b * POOL_TB, CHUNK), POOL_TB)
            dst = pl.ds(pl.multiple_of(padt + b * POOL_TB, CHUNK), POOL_TB)
            dpd = head(rows)
            dpd_ref[lrows, :] = dpd
            pad_ref[dst, :] = dpd / ch_ref[lrows, :]
            return carry

        lax.fori_loop(0, nblk, first, 0)

        for gi, w in enumerate(POOL_WINDOWS):
            lo, hi = w // 2, w - w // 2 - 1

            @pl.when(g == gi)
            def _():
                def second(b, carry):
                    acc = jnp.zeros((POOL_TB, pg), F32)
                    for d in range(-hi, lo + 1):
                        src = pl.ds(pl.multiple_of(padt + b * POOL_TB + d * GRID_W, CHUNK), POOL_TB)
                        acc = acc + pad_ref[src, :]
                    rows = pl.ds(pl.multiple_of(ctx_len + b * POOL_TB, CHUNK), POOL_TB)
                    lrows = pl.ds(pl.multiple_of(b * POOL_TB, CHUNK), POOL_TB)
                    dx = _dot2(mwt_ref[...], acc / cw_ref[...]) - dpd_ref[lrows, :]
                    dp_ref[rows, :] = dx.astype(BF16)
                    return carry

                lax.fori_loop(0, nblk, second, 0)

    col = lambda g: (0, g)
    return pl.pallas_call(
        body, name=name, grid=(ng,),
        in_specs=[pl.BlockSpec((T, pg), col), pl.BlockSpec((T, pg), col), pl.BlockSpec((T, pg), col),
                  _gspec(POOL_TB, POOL_TB), _gspec(ctx_len, ctx_len), _gspec(POOL_TB, 1), _gspec(seq, 1),
                  _gspec(ctx_len, 1), _gspec(pg, pg), pl.BlockSpec((1, pg), col)],
        out_specs=[pl.BlockSpec((T, pg), col), pl.BlockSpec((1, pg), col), _gspec(pg, pg)],
        out_shape=[jax.ShapeDtypeStruct((T, dp), BF16), jax.ShapeDtypeStruct((1, dp), F32),
                   jax.ShapeDtypeStruct((ng, pg, pg), F32)],
        scratch_shapes=[pltpu.VMEM((seq + 2 * padt, pg), F32), pltpu.VMEM((seq, pg), F32)],
        compiler_params=_cparams(("arbitrary",)),
    )(dr, y0, pd, pc["mwt"], pc["bct"], pc["cw"], pc["ch"], pc["cc"], wg, scale)


def _loss_head(x2, target, rb, name):
    T, D = x2.shape

    def body(y_ref, t_ref, dy_ref, l_ref):
        i = pl.program_id(0)

        @pl.when(i == 0)
        def _():
            dy_ref[...] = jnp.zeros_like(dy_ref)
            l_ref[...] = jnp.zeros_like(l_ref)

        @pl.when(i > 0)
        def _():
            e = y_ref[...] - t_ref[...]
            dy_ref[...] = e * (1.0 / D)
            l_ref[...] += 0.5 * jnp.sum(jnp.mean(e * e, axis=-1, keepdims=True), axis=0, keepdims=True)

    return pl.pallas_call(
        body, name=name, grid=(T // rb,),
        in_specs=[_row(rb, D), pl.BlockSpec((rb, D), lambda i: (jnp.maximum(i - 1, 0), 0))],
        out_specs=[_row(rb, D), pl.BlockSpec((8, 128), lambda i: (0, 0))],
        out_shape=[jax.ShapeDtypeStruct((T, D), F32), jax.ShapeDtypeStruct((8, 128), F32)],
        compiler_params=_cparams(("arbitrary",)),
    )(x2, target)


def _sum_lead(x, name):
    S, R, C = x.shape

    def body(x_ref, o_ref):
        acc = x_ref[0]
        for s in range(1, S):
            acc = acc + x_ref[s]
        o_ref[...] = acc

    return pl.pallas_call(
        body, name=name, out_shape=jax.ShapeDtypeStruct((R, C), F32),
        compiler_params=_cparams(),
    )(x)


def _silu_rows(cond, name):
    def body(c_ref, o_ref):
        o_ref[...] = _silu(c_ref[...]).astype(BF16)

    return pl.pallas_call(body, name=name, out_shape=jax.ShapeDtypeStruct(cond.shape, BF16),
                          compiler_params=_cparams())(cond)


def _silu_grad(cond, ds, name):
    def body(c_ref, d_ref, o_ref):
        _, vjp = jax.vjp(_silu, c_ref[...])
        o_ref[...] = vjp(d_ref[...])[0]

    return pl.pallas_call(body, name=name, out_shape=jax.ShapeDtypeStruct(cond.shape, F32),
                          compiler_params=_cparams())(cond, ds)


def _adamw_math(g, w_ref, m_ref, v_ref, go_ref, d_ref, mo_ref, vo_ref):
    c1 = 1.0 / (1.0 - ADAM_B1 ** ADAM_STEP)
    c2 = 1.0 / (1.0 - ADAM_B2 ** ADAM_STEP)
    mn = ADAM_B1 * m_ref[...] + (1.0 - ADAM_B1) * g
    vn = ADAM_B2 * v_ref[...] + (1.0 - ADAM_B2) * (g * g)
    go_ref[...] = g
    mo_ref[...] = mn
    vo_ref[...] = vn
    d_ref[...] = -ADAM_LR * ((mn * c1) / (jnp.sqrt(vn * c2) + ADAM_EPS) + ADAM_WD * w_ref[...])


def _adamw(gs, w, m, v, name, job=None):
    S, R, C = gs.shape
    cpad = -(-C // 128) * 128
    rt = _pick(R, max(16, (1 << 20) // (4 * cpad)), 16)

    def body(g_ref, w_ref, m_ref, v_ref, go_ref, d_ref, mo_ref, vo_ref):
        g = g_ref[0].astype(F32)
        for s in range(1, S):
            g = g + g_ref[s].astype(F32)
        _adamw_math(g, w_ref, m_ref, v_ref, go_ref, d_ref, mo_ref, vo_ref)

    blk = pl.BlockSpec((rt, C), lambda i: (i, 0))
    in_specs = [pl.BlockSpec((S, rt, C), lambda i: (0, i, 0)), blk, blk, blk]
    out_shape = [jax.ShapeDtypeStruct((R, C), F32)] * 4
    if job is None:
        return pl.pallas_call(
            body, name=name, grid=(R // rt,), in_specs=in_specs, out_specs=[blk] * 4, out_shape=out_shape,
            compiler_params=_cparams(("parallel",)),
        )(gs, w, m, v)
    return _call_carrying(job, body, name, (R // rt,), in_specs, [blk] * 4, out_shape, [], (gs, w, m, v))


def _adamw_layer(gs, w, m, v, l, prev, name):
    S, R, C = gs.shape
    L = w.shape[0]
    cpad = -(-C // 128) * 128
    rt = _pick(R, max(16, (1 << 20) // (4 * cpad)), 16)

    def body(*refs):
        g_ref, w_ref, m_ref, v_ref = refs[:4]
        go_ref, d_ref, mo_ref, vo_ref = refs[-4:]
        g = g_ref[0].astype(F32)
        for s in range(1, S):
            g = g + g_ref[s].astype(F32)
        _adamw_math(g, w_ref, m_ref, v_ref, go_ref, d_ref, mo_ref, vo_ref)

    blk = pl.BlockSpec((None, rt, C), lambda i: (l, i, 0))
    in_specs = [pl.BlockSpec((S, rt, C), lambda i: (0, i, 0)), blk, blk, blk]
    args = [gs, w, m, v]
    aliases = {}
    if prev is not None:
        in_specs += [pl.BlockSpec(memory_space=pl.ANY)] * 4
        args += list(prev)
        aliases = {4 + q: q for q in range(4)}
    return pl.pallas_call(
        body, name=name, grid=(R // rt,), in_specs=in_specs,
        out_specs=[blk] * 4, out_shape=[jax.ShapeDtypeStruct((L, R, C), F32)] * 4,
        input_output_aliases=aliases,
        compiler_params=_cparams(("parallel",)),
    )(*args)


def _adamw_nd(gs, w, m, v, name, job=None):
    shp = w.shape
    if len(shp) == 1:
        r, c = 1, shp[0]
    else:
        r, c = int(np.prod(shp[:-1])), shp[-1]
    outs = _adamw(gs.reshape(gs.shape[0], r, c), w.reshape(r, c), m.reshape(r, c), v.reshape(r, c), name, job)
    if job is None:
        return [o.reshape(shp) for o in outs]
    return [o.reshape(shp) for o in outs[0]], outs[1]


def kernel(x, c, ctx, c_ctx, w_ada, b_ada, w_in, w_decay_up, b_decay_up, gla_norm_gain, w_pool_group, pool_scale, w_gla_out, w_pool_out, w_out, ln_mix_gain, ln_mix_bias, w_ffn_in, w_ffn_out, ln_ffn_gain, ln_ffn_bias, loss_target, m_c_ctx, m_w_ada, m_b_ada, m_w_in, m_w_decay_up, m_b_decay_up, m_gla_norm_gain, m_w_pool_group, m_pool_scale, m_w_gla_out, m_w_pool_out, m_w_out, m_ln_mix_gain, m_ln_mix_bias, m_w_ffn_in, m_w_ffn_out, m_ln_ffn_gain, m_ln_ffn_bias, v_c_ctx, v_w_ada, v_b_ada, v_w_in, v_w_decay_up, v_b_decay_up, v_gla_norm_gain, v_w_pool_group, v_pool_scale, v_w_gla_out, v_w_pool_out, v_w_out, v_ln_mix_gain, v_ln_mix_bias, v_w_ffn_in, v_w_ffn_out, v_ln_ffn_gain, v_ln_ffn_bias):
    L, D = w_ada.shape[0], w_ada.shape[1]
    seq, ctx_len = x.shape[1], ctx.shape[1]
    T = seq + ctx_len
    rb = ctx_len
    DK = D // 2
    DP = D // 2
    ng = len(POOL_WINDOWS)
    pg = DP // ng
    dff = w_ffn_out.shape[1] * N_DEV
    alpha = (2.0 * L) ** 0.25
    assert seq % rb == 0 and rb % CHUNK == 0 and seq % POOL_TB == 0 and ctx_len % 8 == 0
    xi, yi, ci = _my_pos()
    me = 4 * xi + 2 * yi + ci
    pc = _pool_consts(ctx_len, seq)

    shards = dict(w_in=w_in.astype(BF16), go=w_gla_out.astype(BF16), po=w_pool_out.astype(BF16),
                  out=w_out.astype(BF16), fi=w_ffn_in.astype(BF16), fo=w_ffn_out.astype(BF16),
                  pg=w_pool_group.astype(BF16).reshape(L, ng * pg // N_DEV, pg))
    wkeys = ("w_in", "go", "po", "out", "fi", "fo", "pg")

    def prepared(gw):
        W = {}
        if "w_in" in gw:
            W["main"], W["alr"] = _w_in_operands(gw["w_in"], "w_in_operands")
            W["pg"] = jnp.swapaxes(gw["pg"].reshape(N_DEV, ng, pg // N_DEV, pg), 0, 1).reshape(ng, pg, pg)
            W.update(go=gw["go"].reshape(D, D), po=gw["po"], out=gw["out"].reshape(D, D))
        if "fi" in gw:
            W["fi"] = gw["fi"][None]
        if "fo" in gw:
            W["fo"] = gw["fo"].reshape(1, dff, D)
        return W

    def gather_next(fn, names, l, nxt, own=(), cur=None):
        keys = [(k, l + 1) for k in names if l + 1 < L] + [(k, l) for k in own]
        if not keys:
            return fn(None)
        res, outs = fn(_gather_job([shards[k] for k, _ in keys], [ll for _, ll in keys]))
        for (k, ll), o in zip(keys, outs):
            (cur if ll == l else nxt)[k] = o
        return res

    first = ("w_in", "go", "po", "out", "pg")
    gathered = dict(zip(first, _run_job(_gather_job([shards[k] for k in first], [0] * len(first)), "ag_layer0")))

    dku = w_decay_up.shape[-1]
    small_in = jnp.concatenate([c.reshape(-1), w_decay_up.reshape(-1), b_decay_up.reshape(-1)])
    (small_all,) = _gather_flat([small_in], "ag_small")
    c_all = small_all[:, :D]
    n_wdu = L * 2 * GATE_RANK * dku
    wdu_all = small_all[:, D:D + n_wdu].reshape(N_DEV, L, 2, GATE_RANK, dku)
    wdu_full = jnp.transpose(wdu_all, (1, 2, 3, 0, 4)).reshape(L, 2, GATE_RANK, DK)
    bdu_all = small_all[:, D + n_wdu:].reshape(N_DEV, L, 2, dku)
    bdu_full = jnp.transpose(bdu_all, (1, 2, 0, 3)).reshape(L, 1, 2 * DK)
    wdu_bd = jnp.zeros((L, ALR_PAD, 2 * DK), F32)
    wdu_bd = wdu_bd.at[:, :GATE_RANK, :DK].set(wdu_full[:, 0])
    wdu_bd = wdu_bd.at[:, GATE_RANK:2 * GATE_RANK, DK:].set(wdu_full[:, 1]).astype(BF16)

    ncond = 16
    cond = jnp.concatenate([c_all, c_ctx.reshape(1, D), jnp.zeros((ncond - N_DEV - 1, D), F32)], axis=0)
    s_cond = _silu_rows(cond, "silu_cond")
    wsh = w_ada.shape[-1]
    b_ada_mine = lax.dynamic_slice_in_dim(b_ada, me * wsh, wsh, axis=1)
    mod_part = jnp.stack([_mm(s_cond, w_ada, "nn", F32, "mod_mm", bias=b_ada_mine[l:l + 1], b_pre=(l,))
                          for l in range(L)])
    (mod_all,) = _all_gather([mod_part], "ag_mod")
    mod_all = jnp.swapaxes(mod_all, 1, 2).reshape(L, ncond, N_MOD * D)
    mod_lat = lax.dynamic_slice_in_dim(mod_all, me, 1, axis=1)
    mods = jnp.concatenate([mod_all[:, N_DEV:N_DEV + 1], mod_lat], axis=1).reshape(L, 2, 1, N_MOD * D)
    SH_M, SC_M, GT_M, SH_F, SC_F, GT_F = range(N_MOD)

    xa = jnp.concatenate([ctx[0], x[0]], axis=0)
    vec = lambda a, l: a[l].reshape(1, -1)
    saved = []
    h = _mod_fwd(xa, mods[0], SC_M, SH_M, rb, "mod_fwd")
    weights = []
    for l in range(L):
        W = prepared(gathered)
        weights.append(W)
        gathered = {}
        late = {}
        proj = gather_next(lambda j: _mm(h, W["main"], "nn", BF16, "mm_in", job=j), ["w_in"], l, gathered)
        alr = _mm(h, W["alr"], "nn", F32, "mm_alr")
        la = _decay_fwd(alr, wdu_bd[l], bdu_full[l], rb, "decay_fwd")
        o_f, s_f = gather_next(lambda j: _gla_fwd(proj, la, False, rb, D, "gla_fwd_f", job=j), ["go", "out"], l,
                               gathered, own=("fi",) if l == 0 else (), cur=late)
        o_b, s_b = gather_next(lambda j: _gla_fwd(proj, la, True, rb, D, "gla_fwd_b", job=j), ["po", "pg"], l,
                               gathered, own=("fo",) if l == 0 else (), cur=late)
        W.update(prepared(late))
        u = _glaout_fwd(o_f, o_b, proj, vec(gla_norm_gain, l), rb, "glaout_fwd")
        y_gla = _mm(u, W["go"], "nn", BF16, "mm_go")
        pd, y0, r = _pool_fwd(proj, pc, W["pg"], vec(pool_scale, l), ctx_len, D, "pool_fwd")
        y_pool = _mm(r, W["po"], "nn", BF16, "mm_po", b_shard=True)
        m_ = _merge_fwd(proj, y_gla, y_pool, rb, "merge_fwd")
        mix = _mm(m_, W["out"], "nn", F32, "mm_out")
        x1, h2 = _unit_fwd(alpha, xa, mix, mods[l], GT_M, vec(ln_mix_gain, l), vec(ln_mix_bias, l),
                           (mods[l], SC_F, SH_F), rb, "unit_mix_fwd")
        ff, s_ = gather_next(lambda j: _ffn_in_fwd(h2, W["fi"], 0, "mm_fi_swiglu", job=j), ["fi"], l, gathered)
        ffn = gather_next(lambda j: _mm(s_, W["fo"], "nn", F32, "mm_fo", b_pre=(0,), job=j), ["fo"], l, gathered)
        nxt = (mods[l + 1], SC_M, SH_M) if l + 1 < L else None
        x2, h_next = _unit_fwd(alpha, x1, ffn, mods[l], GT_F, vec(ln_ffn_gain, l), vec(ln_ffn_bias, l),
                               nxt, rb, "unit_ffn_fwd")
        saved.append(dict(xa=xa, h=h, proj=proj, alr=alr, la=la, o_f=o_f, o_b=o_b, s_f=s_f, s_b=s_b, u=u,
                          y_gla=y_gla, pd=pd, y0=y0, r=r, y_pool=y_pool, m=m_, mix=mix, x1=x1, h2=h2, ff=ff,
                          s=s_, ffn=ffn))
        xa, h = x2, h_next

    dxo, loss_part = _loss_head(xa, loss_target[0], rb, "loss_head")
    loss = lax.psum(loss_part[0, 0], ("x", "y", "c"))

    big_params = [("w_in", w_in, m_w_in, v_w_in), ("w_gla_out", w_gla_out, m_w_gla_out, v_w_gla_out),
                  ("w_pool_out", w_pool_out, m_w_pool_out, v_w_pool_out), ("w_out", w_out, m_w_out, v_w_out),
                  ("w_ffn_in", w_ffn_in, m_w_ffn_in, v_w_ffn_in), ("w_ffn_out", w_ffn_out, m_w_ffn_out, v_w_ffn_out),
                  ("w_pool_group", w_pool_group, m_w_pool_group, v_w_pool_group)]
    big_out = {nm: None for nm, _, _, _ in big_params}
    g_small = {k: [None] * L for k in ("gla_gain", "pool_scale", "mix_g", "mix_b", "ffn_g", "ffn_b", "wdu", "bdu")}
    dmods = [None] * L
    dh = None
    sum2 = lambda a: a[0] + a[1]
    rows8 = lambda g: g.reshape(N_DEV, g.shape[0] // N_DEV, g.shape[1])

    def apply_adamw(parts, layer):
        for (nm, w, m, v), gs in zip(big_params, parts):
            R, C = gs.shape[1], gs.shape[2]
            big_out[nm] = _adamw_layer(gs, w.reshape(L, R, C), m.reshape(L, R, C), v.reshape(L, R, C), layer,
                                       big_out[nm], "adamw_" + nm)

    LATE = (0, 1, 2, 3, 6)
    late_chunks = None
    arrived = {}

    def behind(fn, job, positions):
        if job is None:
            return fn(None)
        res, outs = fn(job)
        if positions is None:
            return res, outs
        arrived.update(zip(positions, outs))
        return res

    for l in range(L - 1, -1, -1):
        sv = saved[l]
        W = weights[l]
        nxt = (mods[l + 1], SC_M, SH_M) if l + 1 < L else None
        unit = lambda j: _unit_bwd(alpha, dxo, dh, sv["x1"], sv["ffn"], mods[l], GT_F, vec(ln_ffn_gain, l),
                                   vec(ln_ffn_bias, l), nxt, rb, "unit_ffn_bwd", job=j)
        late_pairs = None
        if late_chunks is None:
            res = unit(None)
        else:
            res, sib = behind(unit, _sibling_job(late_chunks), None)
            late_pairs = _pair_adds(late_chunks, sib, "_late")
        dx1, dffn, d_gtf, d_gf, d_bf, d_scm_n, d_shm_n = res
        if nxt is not None:
            dmods[l + 1]["sc_m"], dmods[l + 1]["sh_m"] = d_scm_n, d_shm_n
        dmods[l] = dict(gt_f=d_gtf)
        g_small["ffn_g"][l], g_small["ffn_b"][l] = sum2(d_gf), sum2(d_bf)
        dff_ = behind(lambda j: _ffn_out_dx(dffn, W["fo"], 0, sv["ff"], "mm_fo_dx_swiglu", job=j),
                      _chip_job(late_pairs[1:]) if late_pairs else None, LATE[1:])
        c_fo = rows8(_mm(sv["s"], dffn, "tn", BF16, "mm_fo_dw"))
        dh2 = behind(lambda j: _mm(dff_, W["fi"], "nt", F32, "mm_fi_dx", b_pre=(0,), b_shard=True, a_half=True,
                                   job=j), _chip_job(late_pairs[:1]) if late_pairs else None, LATE[:1])
        c_fi = _mm(sv["h2"], dff_, "tn", BF16, "mm_fi_dw", b_half=True, out_shard=True)
        if late_pairs:
            apply_adamw([arrived[i] for i in range(len(big_params))], l + 1)
            arrived = {}
        ffn_chunks = [c_fi, c_fo]
        res, sib = behind(lambda j: _unit_bwd(
            alpha, dx1, dh2, sv["xa"], sv["mix"], mods[l], GT_M, vec(ln_mix_gain, l), vec(ln_mix_bias, l),
            (mods[l], SC_F, SH_F), rb, "unit_mix_bwd", job=j), _sibling_job(ffn_chunks), None)
        dxa, dmix, d_gtm, d_gm, d_bm, d_scf, d_shf = res
        ffn_pairs = _pair_adds(ffn_chunks, sib, "_ffn")
        dmods[l].update(gt_m=d_gtm, sc_f=d_scf, sh_f=d_shf)
        g_small["mix_g"][l], g_small["mix_b"][l] = sum2(d_gm), sum2(d_bm)
        dm = _mm(dmix, W["out"], "nt", F32, "mm_out_dx")
        c_out = rows8(_mm(sv["m"], dmix, "tn", BF16, "mm_out_dw"))
        dbg1, dbg2, dyg, dyp = _merge_bwd(dm, sv["proj"], sv["y_gla"], sv["y_pool"], rb, "merge_bwd")
        dr = _mm(dyp, W["po"], "nt", F32, "mm_po_dx", b_shard=True)
        c_po = _mm(sv["r"], dyp, "tn", BF16, "mm_po_dw", out_shard=True)
        dp_, d_ps, g_pgl = _pool_bwd(dr, sv["y0"], sv["pd"], pc, W["pg"], vec(pool_scale, l), ctx_len, D, "pool_bwd")
        g_small["pool_scale"][l] = d_ps
        c_pg = jnp.swapaxes(g_pgl.astype(BF16).reshape(ng, N_DEV, pg // N_DEV, pg), 0, 1).reshape(N_DEV, -1, pg)
        du = _mm(dyg, W["go"], "nt", F32, "mm_go_dx")
        c_go = rows8(_mm(sv["u"], dyg, "tn", BF16, "mm_go_dw"))
        do, dg, d_gg = _glaout_bwd(du, sv["o_f"], sv["o_b"], sv["proj"], vec(gla_norm_gain, l), rb, "glaout_bwd")
        g_small["gla_gain"][l] = sum2(d_gg)
        dq_f, dk_f, dv_f, dla_f = behind(lambda j: _gla_bwd(
            sv["proj"], sv["la"], do, sv["s_f"], False, rb, D, None, "gla_bwd_f", job=j),
            _chip_job(ffn_pairs[:1]), (4,))
        dq, dk, dv, dla_b = behind(lambda j: _gla_bwd(
            sv["proj"], sv["la"], do, sv["s_b"], True, rb, D, (dq_f, dk_f, dv_f), "gla_bwd_b", job=j),
            _chip_job(ffn_pairs[1:]), (5,))
        dalr, g_wdu, g_bdu = _decay_bwd(dla_f, dla_b, sv["alr"], wdu_bd[l], bdu_full[l], rb, "decay_bwd")
        g_small["wdu"][l] = jnp.stack([g_wdu[:GATE_RANK, :DK], g_wdu[GATE_RANK:2 * GATE_RANK, DK:]])
        g_small["bdu"][l] = g_bdu.reshape(2, DK)
        dproj = jnp.concatenate([dv, dg, dbg1, dbg2, dq, dk, dp_], axis=1)
        dh_alr = _mm(dalr, W["alr"], "nt", F32, "mm_alr_dx")
        dh = _mm(dproj, W["main"], "nt", F32, "mm_in_dx", add=dh_alr)
        g_main = _mm(sv["h"], dproj, "tn", BF16, "mm_in_dw")
        g_alr = _mm(sv["h"], dalr, "tn", BF16, "mm_alr_dw")
        c_in = _w_in_chunks(g_main, g_alr, w_in.shape[2], "w_in_chunks")
        late_chunks = [c_in, c_go, c_po, c_out, c_pg]
        dxo = dxa
    sib = _run_job(_sibling_job(late_chunks), "rs_sibling_last")
    last_pairs = _pair_adds(late_chunks, sib, "_last")
    grad_xa, d_scm0, d_shm0 = _mod_bwd(dxo, dh, saved[0]["xa"], mods[0], SC_M, SH_M, rb, "mod_bwd")
    dmods[0]["sc_m"], dmods[0]["sh_m"] = d_scm0, d_shm0
    grad_x = grad_xa[ctx_len:].reshape(1, seq, D)

    order = ("sh_m", "sc_m", "gt_m", "sh_f", "sc_f", "gt_f")
    dmod = jnp.stack([jnp.concatenate([dmods[l][k] for k in order], axis=2) for l in range(L)])
    dmod = dmod.reshape(-1)
    sm = lambda k: jnp.stack([a.reshape(-1) for a in g_small[k]]).reshape(-1)
    small_keys = ("gla_gain", "pool_scale", "mix_g", "mix_b", "ffn_g", "ffn_b", "wdu", "bdu")
    small_part = jnp.concatenate([sm(k) for k in small_keys])
    small_g, dmod_g = _gather_flat([small_part, dmod], "ag_small_grads")
    small_sum = _sum_lead(small_g.reshape(N_DEV, -1, 128), "sum_small").reshape(-1)
    off = 0
    rep = {}
    for k, n in zip(small_keys, (L * D, L * DP, L * D, L * D, L * D, L * D, L * 2 * GATE_RANK * DK, L * 2 * DK)):
        rep[k] = small_sum[off:off + n]
        off += n
    g_wdu_mine = lax.dynamic_slice_in_dim(rep["wdu"].reshape(L, 2, GATE_RANK, DK), me * dku, dku, axis=3)
    g_bdu_mine = lax.dynamic_slice_in_dim(rep["bdu"].reshape(L, 2, DK), me * dku, dku, axis=2)

    dmod_all = dmod_g.reshape(N_DEV, L, 2, N_MOD * D)
    dm_ctx = _sum_lead(dmod_all[:, :, 0].reshape(N_DEV, L, N_MOD * D), "sum_dmod_ctx")
    dm_rows = jnp.concatenate([jnp.swapaxes(dmod_all[:, :, 1], 0, 1), dm_ctx[:, None],
                               jnp.zeros((L, ncond - N_DEV - 1, N_MOD * D), F32)], axis=1)
    g_b_ada = _sum_lead(jnp.swapaxes(dm_rows, 0, 1), "sum_b_ada")
    dm_mine = lax.dynamic_slice_in_dim(dm_rows, me * wsh, wsh, axis=2).astype(BF16)
    g_w_ada = jnp.stack([_mm(s_cond, dm_mine[l], "tn", F32, "ada_dw") for l in range(L)])
    ds_part = _sum_lead(jnp.stack([_mm(dm_mine[l], w_ada, "nt", F32, "ada_dx", b_pre=(l,)) for l in range(L)]),
                        "sum_ds")
    (ds_all,) = _gather_flat([ds_part[N_DEV]], "ag_ds")
    ds_ctx = _sum_lead(ds_all.reshape(N_DEV, 1, D), "sum_ds_ctx")
    g_c_ctx = _silu_grad(c_ctx.reshape(1, D), ds_ctx, "silu_grad").reshape(D)

    one = lambda g: g[None]
    small_table = {
        "c_ctx": (one(g_c_ctx), c_ctx, m_c_ctx, v_c_ctx),
        "w_ada": (one(g_w_ada), w_ada, m_w_ada, v_w_ada),
        "b_ada": (one(g_b_ada), b_ada, m_b_ada, v_b_ada),
        "w_decay_up": (one(g_wdu_mine), w_decay_up, m_w_decay_up, v_w_decay_up),
        "b_decay_up": (one(g_bdu_mine), b_decay_up, m_b_decay_up, v_b_decay_up),
        "gla_norm_gain": (one(rep["gla_gain"].reshape(L, D)), gla_norm_gain, m_gla_norm_gain, v_gla_norm_gain),
        "pool_scale": (one(rep["pool_scale"].reshape(L, DP)), pool_scale, m_pool_scale, v_pool_scale),
        "ln_mix_gain": (one(rep["mix_g"].reshape(L, D)), ln_mix_gain, m_ln_mix_gain, v_ln_mix_gain),
        "ln_mix_bias": (one(rep["mix_b"].reshape(L, D)), ln_mix_bias, m_ln_mix_bias, v_ln_mix_bias),
        "ln_ffn_gain": (one(rep["ffn_g"].reshape(L, D)), ln_ffn_gain, m_ln_ffn_gain, v_ln_ffn_gain),
        "ln_ffn_bias": (one(rep["ffn_b"].reshape(L, D)), ln_ffn_bias, m_ln_ffn_bias, v_ln_ffn_bias),
    }
    big_shapes = {nm: w.shape for nm, w, _, _ in big_params}
    names = ("c_ctx", "w_ada", "b_ada", "w_in", "w_decay_up", "b_decay_up", "gla_norm_gain", "w_pool_group",
             "pool_scale", "w_gla_out", "w_pool_out", "w_out", "ln_mix_gain", "ln_mix_bias", "w_ffn_in", "w_ffn_out",
             "ln_ffn_gain", "ln_ffn_bias")
    ada_res, last_parts = _adamw_nd(*small_table["w_ada"], "adamw_w_ada", job=_chip_job(last_pairs))
    arrived.update(zip(LATE, last_parts))
    apply_adamw([arrived[i] for i in range(len(big_params))], 0)
    grads, deltas, new_m, new_v = [], [], [], []
    for nm in names:
        if nm == "w_ada":
            res = ada_res
        elif nm in small_table:
            res = _adamw_nd(*small_table[nm], "adamw_" + nm)
        else:
            res = [o.reshape(big_shapes[nm]) for o in big_out[nm]]
        for lst, o in zip((grads, deltas, new_m, new_v), res):
            lst.append(o)
    return (loss, grad_x, *grads, *deltas, *new_m, *new_v)
```
